```python
import math
import jax
import jax.numpy as jnp
from jax import lax
import numpy as np

D_MODEL = 2048
BATCH = 4
SEQ = 2048
DEPTH = 2

CHUNK = 64
D_MIX = D_MODEL
N_MIXERS = 4
GROUP_W = D_MIX // N_MIXERS
NORM_EPS = 1e-6

GLA_HEADS = 4
GLA_DV = GROUP_W // GLA_HEADS
GLA_DK = GLA_DV // 2
GLA_GATE_RANK = 16
GLA_GATE_NORMALIZER = 16.0

RWKV_HEAD = 64
RWKV_HEADS = GROUP_W // RWKV_HEAD
RWKV_W_RANK = 64
RWKV_A_RANK = 64
RWKV_V_RANK = 32
RWKV_LN_EPS = 64e-5
RWKV_DECAY_OFFSET = 0.5

SSD_HEADDIM = 64
SSD_HEADS = GROUP_W // SSD_HEADDIM
SSD_GROUPS = 2
SSD_STATE = 128
SSD_CONV = 4
SSD_XBC = GROUP_W + 2 * SSD_GROUPS * SSD_STATE

MLSTM_HEADS = 4
MLSTM_HEAD = GROUP_W // MLSTM_HEADS
MLSTM_CONV = 4

kernel_name = 'hybrid_gla_rwkv7_ssd_mlstm_heads'


def rwkv_shift_width(layer):
    return 3 * GROUP_W + RWKV_W_RANK + RWKV_A_RANK + (RWKV_V_RANK if layer > 0 else 0)


def in_layout(layer):
    return (('gla_q', GLA_HEADS * GLA_DK), ('gla_k', GLA_HEADS * GLA_DK), ('gla_v', GROUP_W),
            ('gla_gk', GLA_GATE_RANK), ('gla_z', GROUP_W),
            ('rwkv_shift', rwkv_shift_width(layer)), ('rwkv_z', GROUP_W),
            ('ssd_xbc', SSD_XBC), ('ssd_dt', SSD_HEADS), ('ssd_z', GROUP_W),
            ('mlstm_qk', 2 * GROUP_W), ('mlstm_v', GROUP_W), ('mlstm_i', MLSTM_HEADS),
            ('mlstm_f', MLSTM_HEADS), ('mlstm_o', GROUP_W), ('mlstm_z', GROUP_W))


def split_cols(proj, layout):
    idx = np.cumsum([w for _, w in layout])[:-1].tolist()
    parts = jnp.split(proj, idx, axis=-1)
    return {name: part for (name, _), part in zip(layout, parts)}


def rms_norm(x, g, eps=NORM_EPS):
    xf = x.astype(jnp.float32)
    y = xf * lax.rsqrt(jnp.mean(xf * xf, axis=-1, keepdims=True) + eps)
    return (y * g.astype(jnp.float32)).astype(x.dtype)


def head_rms_norm(y, g, eps=NORM_EPS):
    y = y * lax.rsqrt(jnp.mean(y * y, axis=-1, keepdims=True) + eps)
    return y * g.astype(jnp.float32).reshape(y.shape[-2:])


def head_layer_norm(y, g, eps):
    mu = jnp.mean(y, axis=-1, keepdims=True)
    yc = y - mu
    y = yc * lax.rsqrt(jnp.mean(yc * yc, axis=-1, keepdims=True) + eps)
    return y * g.astype(jnp.float32).reshape(y.shape[-2:])


def causal_depthwise_conv(x, w, b):
    k = w.shape[0]
    y = lax.conv_general_dilated(x, w[:, None, :].astype(x.dtype), window_strides=(1,),
                                 padding=[(k - 1, 0)], dimension_numbers=('NWC', 'WIO', 'NWC'),
                                 feature_group_count=x.shape[-1])
    return y + b.astype(x.dtype)


def token_shift(x):
    return jnp.pad(x, ((0, 0), (1, 0), (0, 0)))[:, :-1]


def causal_mask():
    return jnp.tril(jnp.ones((CHUNK, CHUNK), dtype=bool))


def chunk_state_scan(decay, inc):
    def step(state, xs):
        d, u = xs
        return d * state + u, state
    _, prev = lax.scan(step, jnp.zeros_like(inc[0]), (decay, inc))
    return prev


def gla_mixer(q, k, v, gk_low, z, gk_w2, gk_b, norm_g):
    f32 = jnp.float32
    B, S, _ = q.shape
    nc = S // CHUNK
    q = q.astype(f32).reshape(B, nc, CHUNK, GLA_HEADS, GLA_DK) * GLA_DK ** -0.5
    k = k.astype(f32).reshape(B, nc, CHUNK, GLA_HEADS, GLA_DK)
    v = v.astype(f32).reshape(B, nc, CHUNK, GLA_HEADS, GLA_DV)
    gk = gk_low.astype(f32) @ gk_w2.astype(f32) + gk_b.astype(f32)
    log_a = jax.nn.log_sigmoid(gk).reshape(B, nc, CHUNK, GLA_HEADS, GLA_DK) / GLA_GATE_NORMALIZER
    cum = jnp.cumsum(log_a, axis=2)
    last = cum[:, :, -1:]
    qg = q * jnp.exp(cum)
    kg = k * jnp.exp(-cum)
    kd = k * jnp.exp(last - cum)
    att = jnp.where(causal_mask(), jnp.einsum('bnlhd,bnshd->bnhls', qg, kg), 0.0)
    o = jnp.einsum('bnhls,bnshv->bnlhv', att, v)
    inc = jnp.einsum('bnshd,bnshv->nbhdv', kd, v)
    decay = jnp.exp(last[:, :, 0]).transpose(1, 0, 2, 3)[..., None]
    s_prev = chunk_state_scan(decay, inc)
    o = o + jnp.einsum('bnlhd,nbhdv->bnlhv', qg, s_prev)
    o = head_rms_norm(o.reshape(B, S, GLA_HEADS, GLA_DV), norm_g)
    return o.reshape(B, S, GROUP_W) * jax.nn.silu(z.astype(f32))


def rwkv7_mixer(feats, z, v_first, mu, w0, w2, a0, a2, k_k, k_a, r_k, ln_g, ln_b, v0, v2):
    f32 = jnp.float32
    B, S, _ = feats.shape
    f = feats.astype(f32)
    f = f + mu.astype(f32) * (token_shift(f) - f)
    widths = [GROUP_W, GROUP_W, GROUP_W, RWKV_W_RANK, RWKV_A_RANK] + ([] if v0 is None else [RWKV_V_RANK])
    parts = jnp.split(f, np.cumsum(widths)[:-1].tolist(), axis=-1)
    r, k, v, wl, al = parts[:5]
    w_log = -jax.nn.softplus(-(w0 + jnp.tanh(wl) @ w2)) - RWKV_DECAY_OFFSET
    decay = jnp.exp(-jnp.exp(w_log))
    a = jax.nn.sigmoid(a0 + al @ a2)
    if v0 is None:
        v_first = v
    else:
        v = v + (v_first - v) * jax.nn.sigmoid(v0 + parts[5] @ v2)
    hs = lambda t: t.reshape(B, S, RWKV_HEADS, RWKV_HEAD)
    kk = hs(k * k_k)
    kk = kk / jnp.maximum(jnp.sqrt(jnp.sum(kk * kk, axis=-1, keepdims=True)), 1e-12)
    k = k * (1.0 + (a - 1.0) * k_a)
    r, k, v, decay, a = hs(r), hs(k), hs(v), hs(decay), hs(a)
    b_vec = kk * a

    def step(state, xs):
        r_t, k_t, v_t, w_t, kk_t, b_t = xs
        sa = jnp.einsum('bhij,bhj->bhi', state, -kk_t)
        state = (state * w_t[:, :, None, :] + sa[..., None] * b_t[:, :, None, :]
                 + v_t[..., None] * k_t[:, :, None, :])
        return state, jnp.einsum('bhij,bhj->bhi', state, r_t)

    xs = tuple(t.transpose(1, 0, 2, 3) for t in (r, k, v, decay, kk, b_vec))
    init = jnp.zeros((B, RWKV_HEADS, RWKV_HEAD, RWKV_HEAD), f32)
    _, y = lax.scan(step, init, xs)
    y = y.transpose(1, 0, 2, 3)
    y = head_layer_norm(y, ln_g, RWKV_LN_EPS) + ln_b.astype(f32).reshape(RWKV_HEADS, RWKV_HEAD)
    bonus = jnp.sum(r * k * r_k.astype(f32), axis=-1, keepdims=True) * v
    y = (y + bonus).reshape(B, S, GROUP_W) * jax.nn.silu(z.astype(f32))
    return y, v_first


def ssd_mixer(xbc, dt_raw, z, conv_w, conv_b, dt_bias, a_log, d_skip, norm_g):
    f32 = jnp.float32
    B, S, _ = xbc.shape
    nc = S // CHUNK
    E = SSD_HEADS // SSD_GROUPS
    xbc = jax.nn.silu(causal_depthwise_conv(xbc.astype(f32), conv_w.astype(f32), conv_b.astype(f32)))
    x, bm, cm = jnp.split(xbc, [GROUP_W, GROUP_W + SSD_GROUPS * SSD_STATE], axis=-1)
    dt = jax.nn.softplus(dt_raw.astype(f32) + dt_bias.astype(f32))
    a = -jnp.exp(a_log.astype(f32))
    x = x.reshape(B, S, SSD_HEADS, SSD_HEADDIM)
    xdt = (x * dt[..., None]).reshape(B, nc, CHUNK, SSD_GROUPS, E, SSD_HEADDIM)
    bm = bm.reshape(B, nc, CHUNK, SSD_GROUPS, SSD_STATE)
    cm = cm.reshape(B, nc, CHUNK, SSD_GROUPS, SSD_STATE)
    cum = jnp.cumsum((dt * a).reshape(B, nc, CHUNK, SSD_GROUPS, E), axis=2)
    cum_t = cum.transpose(0, 1, 3, 4, 2)
    seg = cum_t[..., :, None] - cum_t[..., None, :]
    L = jnp.exp(jnp.where(causal_mask(), seg, -jnp.inf))
    cb = jnp.einsum('bnlgs,bnmgs->bnglm', cm, bm)
    y = jnp.einsum('bnglm,bngelm,bnmgep->bnlgep', cb, L, xdt)
    last = cum[:, :, -1:]
    inc = jnp.einsum('bnmgs,bnmge,bnmgep->nbgeps', bm, jnp.exp(last - cum), xdt)
    decay = jnp.exp(last[:, :, 0]).transpose(1, 0, 2, 3)[..., None, None]
    s_prev = chunk_state_scan(decay, inc)
    y = y + jnp.einsum('bnlgs,nbgeps,bnlge->bnlgep', cm, s_prev, jnp.exp(cum))
    y = y.reshape(B, S, SSD_HEADS, SSD_HEADDIM) + d_skip.astype(f32)[:, None] * x
    y = y.reshape(B, S, GROUP_W) * jax.nn.silu(z.astype(f32))
    return rms_norm(y, norm_g)


def mlstm_mixer(qk, v, i_pre, f_pre, o_pre, z, conv_w, conv_b, ig_b, fg_b, norm_g):
    f32 = jnp.float32
    B, S, _ = qk.shape
    nc = S // CHUNK
    H, Dh = MLSTM_HEADS, MLSTM_HEAD
    qk = jax.nn.silu(causal_depthwise_conv(qk.astype(f32), conv_w.astype(f32), conv_b.astype(f32)))
    q, k = jnp.split(qk, 2, axis=-1)
    q = q.reshape(B, nc, CHUNK, H, Dh)
    k = k.reshape(B, nc, CHUNK, H, Dh) * Dh ** -0.5
    v = v.astype(f32).reshape(B, nc, CHUNK, H, Dh)
    log_i = (i_pre.astype(f32) + ig_b.astype(f32)).reshape(B, nc, CHUNK, H)
    log_f = jax.nn.log_sigmoid(f_pre.astype(f32) + fg_b.astype(f32)).reshape(B, nc, CHUNK, H)
    cum = jnp.cumsum(log_f, axis=2)
    last = cum[:, :, -1]
    g = last[:, :, None] - cum + log_i
    g_max = jnp.max(g, axis=2)
    w_end = jnp.exp(g - g_max[:, :, None])
    c_loc = jnp.einsum('bnsh,bnshd,bnshe->nbhde', w_end, k, v)
    n_loc = jnp.einsum('bnsh,bnshd->nbhd', w_end, k)

    def step(carry, xs):
        c, n, m = carry
        lf, gm, cl, nl = xs
        m_new = jnp.maximum(lf + m, gm)
        a_old = jnp.exp(lf + m - m_new)
        a_new = jnp.exp(gm - m_new)
        c_new = a_old[..., None, None] * c + a_new[..., None, None] * cl
        n_new = a_old[..., None] * n + a_new[..., None] * nl
        return (c_new, n_new, m_new), (c, n, m)

    init = (jnp.zeros((B, H, Dh, Dh), f32), jnp.zeros((B, H, Dh), f32), jnp.zeros((B, H), f32))
    _, (c_prev, n_prev, m_prev) = lax.scan(
        step, init, (last.transpose(1, 0, 2), g_max.transpose(1, 0, 2), c_loc, n_loc))
    cum_t = cum.transpose(0, 1, 3, 2)
    log_d = cum_t[..., :, None] - cum_t[..., None, :] + log_i.transpose(0, 1, 3, 2)[..., None, :]
    log_d = jnp.where(causal_mask(), log_d, -jnp.inf)
    m_inter = cum_t + m_prev.transpose(1, 0, 2)[..., None]
    m_l = jnp.maximum(m_inter, jnp.max(log_d, axis=-1))
    wqk = jnp.einsum('bnlhd,bnshd->bnhls', q, k) * jnp.exp(log_d - m_l[..., None])
    w_inter = jnp.exp(m_inter - m_l)
    num = (jnp.einsum('bnhls,bnshe->bnlhe', wqk, v)
           + w_inter.transpose(0, 1, 3, 2)[..., None] * jnp.einsum('bnlhd,nbhde->bnlhe', q, c_prev))
    den = jnp.sum(wqk, axis=-1) + w_inter * jnp.einsum('bnlhd,nbhd->bnhl', q, n_prev)
    den = jnp.maximum(jnp.abs(den), jnp.exp(-m_l))
    h = num / den.transpose(0, 1, 3, 2)[..., None]
    h = h.reshape(B, S, H, Dh) * jax.nn.sigmoid(o_pre.astype(f32)).reshape(B, S, H, Dh)
    h = head_layer_norm(h, norm_g, NORM_EPS)
    return h.reshape(B, S, GROUP_W) * jax.nn.silu(z.astype(f32))


def hybrid_layer(x, v_first, layer, p):
    h = rms_norm(x, p['norm_g'])
    c = split_cols(h @ p['w_in'], in_layout(layer))
    y_gla = gla_mixer(c['gla_q'], c['gla_k'], c['gla_v'], c['gla_gk'], c['gla_z'],
                      p['gla_gk_w2'], p['gla_gk_b'], p['gla_norm_g'])
    y_rwkv, v_first = rwkv7_mixer(c['rwkv_shift'], c['rwkv_z'], v_first, p['rwkv_mu'],
                                  p['rwkv_w0'], p['rwkv_w2'], p['rwkv_a0'], p['rwkv_a2'],
                                  p['rwkv_k_k'], p['rwkv_k_a'], p['rwkv_r_k'],
                                  p['rwkv_ln_g'], p['rwkv_ln_b'], p['rwkv_v0'], p['rwkv_v2'])
    y_ssd = ssd_mixer(c['ssd_xbc'], c['ssd_dt'], c['ssd_z'], p['ssd_conv_w'], p['ssd_conv_b'],
                      p['ssd_dt_bias'], p['ssd_a_log'], p['ssd_d'], p['ssd_norm_g'])
    y_mlstm = mlstm_mixer(c['mlstm_qk'], c['mlstm_v'], c['mlstm_i'], c['mlstm_f'], c['mlstm_o'],
                          c['mlstm_z'], p['mlstm_conv_w'], p['mlstm_conv_b'], p['mlstm_ig_b'],
                          p['mlstm_fg_b'], p['mlstm_norm_g'])
    y = jnp.concatenate([y_gla, y_rwkv, y_ssd, y_mlstm], axis=-1).astype(x.dtype)
    return x + y @ p['w_out'], v_first


def setup_inputs(seed: int = 0) -> dict:
    key = jax.random.key(seed)
    keys = iter(jax.random.split(key, 128))
    normal = lambda shape, scale: scale * jax.random.normal(next(keys), shape, jnp.float32)
    gain = lambda n: 1.0 + normal((n,), 0.02)
    inputs = {'x': normal((BATCH, SEQ, D_MODEL), 1.0)}
    for l in range(DEPTH):
        n_in = sum(w for _, w in in_layout(l))
        p = {}
        p['norm_g'] = gain(D_MODEL)
        p['w_in'] = normal((D_MODEL, n_in), D_MODEL ** -0.5)
        p['w_out'] = normal((D_MIX, D_MODEL), 0.5 * D_MIX ** -0.5)
        p['gla_gk_w2'] = normal((GLA_GATE_RANK, GLA_HEADS * GLA_DK), GLA_GATE_RANK ** -0.5)
        p['gla_gk_b'] = normal((GLA_HEADS * GLA_DK,), 0.1)
        p['gla_norm_g'] = gain(GROUP_W)
        p['rwkv_mu'] = jax.random.uniform(next(keys), (rwkv_shift_width(l),), jnp.float32)
        p['rwkv_w0'] = normal((GROUP_W,), 0.5)
        p['rwkv_w2'] = normal((RWKV_W_RANK, GROUP_W), 0.1)
        p['rwkv_a0'] = normal((GROUP_W,), 0.1)
        p['rwkv_a2'] = normal((RWKV_A_RANK, GROUP_W), 0.1)
        if l > 0:
            p['rwkv_v0'] = normal((GROUP_W,), 0.1)
            p['rwkv_v2'] = normal((RWKV_V_RANK, GROUP_W), 0.1)
        p['rwkv_k_k'] = 0.85 + normal((GROUP_W,), 0.02)
        p['rwkv_k_a'] = 1.0 + normal((GROUP_W,), 0.02)
        p['rwkv_r_k'] = normal((RWKV_HEADS, RWKV_HEAD), 0.1)
        p['rwkv_ln_g'] = gain(GROUP_W)
        p['rwkv_ln_b'] = normal((GROUP_W,), 0.02)
        p['ssd_conv_w'] = normal((SSD_CONV, SSD_XBC), SSD_CONV ** -0.5)
        p['ssd_conv_b'] = normal((SSD_XBC,), 0.02)
        dt0 = jnp.exp(jax.random.uniform(next(keys), (SSD_HEADS,), jnp.float32,
                                         math.log(1e-3), math.log(1e-1)))
        p['ssd_dt_bias'] = dt0 + jnp.log(-jnp.expm1(-dt0))
        p['ssd_a_log'] = jnp.log(jax.random.uniform(next(keys), (SSD_HEADS,), jnp.float32, 1.0, 16.0))
        p['ssd_d'] = 1.0 + normal((SSD_HEADS,), 0.1)
        p['ssd_norm_g'] = gain(GROUP_W)
        p['mlstm_conv_w'] = normal((MLSTM_CONV, 2 * GROUP_W), MLSTM_CONV ** -0.5)
        p['mlstm_conv_b'] = normal((2 * GROUP_W,), 0.02)
        p['mlstm_ig_b'] = normal((MLSTM_HEADS,), 0.1)
        p['mlstm_fg_b'] = jnp.linspace(3.0, 6.0, MLSTM_HEADS, dtype=jnp.float32) + normal((MLSTM_HEADS,), 0.1)
        p['mlstm_norm_g'] = gain(GROUP_W)
        for name, val in p.items():
            inputs[name + '_' + str(l)] = val
    inputs['final_norm_g'] = gain(D_MODEL)
    return inputs


def reference(x,
              norm_g_0, w_in_0, w_out_0, gla_gk_w2_0, gla_gk_b_0, gla_norm_g_0,
              rwkv_mu_0, rwkv_w0_0, rwkv_w2_0, rwkv_a0_0, rwkv_a2_0,
              rwkv_k_k_0, rwkv_k_a_0, rwkv_r_k_0, rwkv_ln_g_0, rwkv_ln_b_0,
              ssd_conv_w_0, ssd_conv_b_0, ssd_dt_bias_0, ssd_a_log_0, ssd_d_0, ssd_norm_g_0,
              mlstm_conv_w_0, mlstm_conv_b_0, mlstm_ig_b_0, mlstm_fg_b_0, mlstm_norm_g_0,
              norm_g_1, w_in_1, w_out_1, gla_gk_w2_1, gla_gk_b_1, gla_norm_g_1,
              rwkv_mu_1, rwkv_w0_1, rwkv_w2_1, rwkv_a0_1, rwkv_a2_1, rwkv_v0_1, rwkv_v2_1,
              rwkv_k_k_1, rwkv_k_a_1, rwkv_r_k_1, rwkv_ln_g_1, rwkv_ln_b_1,
              ssd_conv_w_1, ssd_conv_b_1, ssd_dt_bias_1, ssd_a_log_1, ssd_d_1, ssd_norm_g_1,
              mlstm_conv_w_1, mlstm_conv_b_1, mlstm_ig_b_1, mlstm_fg_b_1, mlstm_norm_g_1,
              final_norm_g):
    layers = [
        dict(norm_g=norm_g_0, w_in=w_in_0, w_out=w_out_0, gla_gk_w2=gla_gk_w2_0, gla_gk_b=gla_gk_b_0,
             gla_norm_g=gla_norm_g_0, rwkv_mu=rwkv_mu_0, rwkv_w0=rwkv_w0_0, rwkv_w2=rwkv_w2_0,
             rwkv_a0=rwkv_a0_0, rwkv_a2=rwkv_a2_0, rwkv_v0=None, rwkv_v2=None,
             rwkv_k_k=rwkv_k_k_0, rwkv_k_a=rwkv_k_a_0, rwkv_r_k=rwkv_r_k_0, rwkv_ln_g=rwkv_ln_g_0,
             rwkv_ln_b=rwkv_ln_b_0, ssd_conv_w=ssd_conv_w_0, ssd_conv_b=ssd_conv_b_0,
             ssd_dt_bias=ssd_dt_bias_0, ssd_a_log=ssd_a_log_0, ssd_d=ssd_d_0, ssd_norm_g=ssd_norm_g_0,
             mlstm_conv_w=mlstm_conv_w_0, mlstm_conv_b=mlstm_conv_b_0, mlstm_ig_b=mlstm_ig_b_0,
             mlstm_fg_b=mlstm_fg_b_0, mlstm_norm_g=mlstm_norm_g_0),
        dict(norm_g=norm_g_1, w_in=w_in_1, w_out=w_out_1, gla_gk_w2=gla_gk_w2_1, gla_gk_b=gla_gk_b_1,
             gla_norm_g=gla_norm_g_1, rwkv_mu=rwkv_mu_1, rwkv_w0=rwkv_w0_1, rwkv_w2=rwkv_w2_1,
             rwkv_a0=rwkv_a0_1, rwkv_a2=rwkv_a2_1, rwkv_v0=rwkv_v0_1, rwkv_v2=rwkv_v2_1,
             rwkv_k_k=rwkv_k_k_1, rwkv_k_a=rwkv_k_a_1, rwkv_r_k=rwkv_r_k_1, rwkv_ln_g=rwkv_ln_g_1,
             rwkv_ln_b=rwkv_ln_b_1, ssd_conv_w=ssd_conv_w_1, ssd_conv_b=ssd_conv_b_1,
             ssd_dt_bias=ssd_dt_bias_1, ssd_a_log=ssd_a_log_1, ssd_d=ssd_d_1, ssd_norm_g=ssd_norm_g_1,
             mlstm_conv_w=mlstm_conv_w_1, mlstm_conv_b=mlstm_conv_b_1, mlstm_ig_b=mlstm_ig_b_1,
             mlstm_fg_b=mlstm_fg_b_1, mlstm_norm_g=mlstm_norm_g_1),
    ]
    v_first = None
    for layer in range(DEPTH):
        x, v_first = hybrid_layer(x, v_first, layer, layers[layer])
    return rms_norm(x, final_norm_g)
```

```python
import functools

import numpy as np
import jax
import jax.numpy as jnp
from jax import lax
from jax.experimental import pallas as pl
from jax.experimental.pallas import tpu as pltpu

F32 = jnp.float32
BF16 = jnp.bfloat16

D_MODEL = 2048
CHUNK = 64
GROUP_W = 512
NORM_EPS = 1e-6
LANES = 128
CARRY_ROWS = 8

GLA_HEADS, GLA_DK, GLA_DV, GLA_RANK = 4, 64, 128, 16
GLA_GATE_NORMALIZER = 16.0
RWKV_HEAD, RWKV_W_RANK, RWKV_A_RANK, RWKV_V_RANK = 64, 64, 64, 32
RWKV_LN_EPS = 64e-5
RWKV_DECAY_OFFSET = 0.5
SSD_HEADS, SSD_HEADDIM, SSD_STATE, SSD_CONV = 8, 64, 128, 4
SSD_XBC = 1024
MLSTM_HEADS, MLSTM_HEAD, MLSTM_CONV = 4, 128, 4

ROW_TILE_IN = 256
ROW_TILE_OUT = 512
VMEM_LIMIT_PROJ = 48 * 2**20


def _bf(x):
    return x.astype(BF16)


def _mm(a, b):
    return jnp.dot(_bf(a), _bf(b), preferred_element_type=F32)


def _mm_nt(a, b):
    return lax.dot_general(_bf(a), _bf(b), (((1,), (1,)), ((), ())), preferred_element_type=F32)


def _mm_tn(a, b):
    return lax.dot_general(_bf(a), _bf(b), (((0,), (0,)), ((), ())), preferred_element_type=F32)


def _split3(x):
    hi = _bf(x)
    r1 = x - hi.astype(F32)
    mid = _bf(r1)
    lo = _bf(r1 - mid.astype(F32))
    return hi, mid, lo


def _mm_sel_rhs(x, sel):
    hi, mid, lo = _split3(x)
    d = lambda a: jnp.dot(a, sel, preferred_element_type=F32)
    return d(hi) + d(mid) + d(lo)


def _mm_sel_lhs(sel, x):
    hi, mid, lo = _split3(x)
    d = lambda a: jnp.dot(sel, a, preferred_element_type=F32)
    return d(hi) + d(mid) + d(lo)


def _sigmoid(x):
    return 1.0 / (1.0 + jnp.exp(-x))


def _silu(x):
    return x * _sigmoid(x)


def _softplus(x):
    return jnp.maximum(x, 0.0) + jnp.log1p(jnp.exp(-jnp.abs(x)))


def _log_sigmoid(x):
    return -_softplus(-x)


def _lane_half_masks():
    lane = lax.broadcasted_iota(jnp.int32, (1, LANES), 1)
    lo = (lane < LANES // 2).astype(F32)
    return lo, 1.0 - lo


def _causal(n):
    r = lax.broadcasted_iota(jnp.int32, (n, n), 0)
    c = lax.broadcasted_iota(jnp.int32, (n, n), 1)
    return c <= r


def _shifted_rows(buf_ref, cur, offsets):
    buf_ref[CARRY_ROWS:CARRY_ROWS + CHUNK, :] = cur
    outs = [buf_ref[CARRY_ROWS - off:CARRY_ROWS - off + CHUNK, :] for off in offsets]
    tail = buf_ref[CHUNK:CHUNK + CARRY_ROWS, :]
    buf_ref[0:CARRY_ROWS, :] = tail
    return outs


def _in_proj_kernel(*refs, has_t):
    if has_t:
        x_ref, g_ref, w_ref, wt_ref, o_ref, ot_ref = refs
    else:
        x_ref, g_ref, w_ref, o_ref = refs
    x = x_ref[...]
    h = x * lax.rsqrt(jnp.mean(x * x, axis=-1, keepdims=True) + NORM_EPS) * g_ref[...]
    hb = _bf(h)
    o_ref[...] = jnp.dot(hb, w_ref[...], preferred_element_type=F32)
    if has_t:
        ot_ref[...] = lax.dot_general(wt_ref[...], hb, (((1,), (1,)), ((), ())),
                                      preferred_element_type=F32)


def _in_proj(x2, g, w, wt=None):
    T = x2.shape[0]
    n = w.shape[1]
    tm = ROW_TILE_IN
    in_specs = [pl.BlockSpec((tm, D_MODEL), lambda i: (i, 0)),
                pl.BlockSpec((1, D_MODEL), lambda i: (0, 0)),
                pl.BlockSpec((D_MODEL, n), lambda i: (0, 0))]
    out_shape = [jax.ShapeDtypeStruct((T, n), F32)]
    out_specs = [pl.BlockSpec((tm, n), lambda i: (i, 0))]
    args = [x2, g, w]
    if wt is not None:
        r = wt.shape[0]
        in_specs.append(pl.BlockSpec((r, D_MODEL), lambda i: (0, 0)))
        out_shape.append(jax.ShapeDtypeStruct((r, T), F32))
        out_specs.append(pl.BlockSpec((r, tm), lambda i: (0, i)))
        args.append(wt)
    res = pl.pallas_call(
        functools.partial(_in_proj_kernel, has_t=wt is not None),
        grid=(T // tm,), in_specs=in_specs, out_specs=out_specs, out_shape=out_shape,
        compiler_params=pltpu.CompilerParams(dimension_semantics=("arbitrary",),
                                             vmem_limit_bytes=VMEM_LIMIT_PROJ),
        name="in_proj")(*args)
    return res if wt is not None else res[0]


def _out_proj_kernel(*refs, final):
    if final:
        x_ref, y0, y1, y2, y3, w_ref, g_ref, o_ref = refs
    else:
        x_ref, y0, y1, y2, y3, w_ref, o_ref = refs
    acc = x_ref[...]
    for gi, y in enumerate((y0, y1, y2, y3)):
        acc = acc + jnp.dot(_bf(y[...]), w_ref[gi * GROUP_W:(gi + 1) * GROUP_W, :],
                            preferred_element_type=F32)
    if final:
        acc = acc * lax.rsqrt(jnp.mean(acc * acc, axis=-1, keepdims=True) + NORM_EPS) * g_ref[...]
    o_ref[...] = acc


def _out_proj(x2, ys, w, g_final=None):
    T = x2.shape[0]
    tm = ROW_TILE_OUT
    final = g_final is not None
    in_specs = [pl.BlockSpec((tm, D_MODEL), lambda i: (i, 0))]
    in_specs += [pl.BlockSpec((tm, GROUP_W), lambda i: (i, 0)) for _ in range(4)]
    in_specs += [pl.BlockSpec((D_MODEL, D_MODEL), lambda i: (0, 0))]
    args = [x2, *ys, w]
    if final:
        in_specs.append(pl.BlockSpec((1, D_MODEL), lambda i: (0, 0)))
        args.append(g_final)
    return pl.pallas_call(
        functools.partial(_out_proj_kernel, final=final),
        grid=(T // tm,), in_specs=in_specs,
        out_specs=pl.BlockSpec((tm, D_MODEL), lambda i: (i, 0)),
        out_shape=jax.ShapeDtypeStruct((T, D_MODEL), F32),
        compiler_params=pltpu.CompilerParams(dimension_semantics=("arbitrary",),
                                             vmem_limit_bytes=VMEM_LIMIT_PROJ),
        name="out_proj")(*args)


GLA_SLAB = 1664


def _gla_kernel(slab_ref, w2_ref, gkb_ref, ng_ref, tri_ref, o_ref, st_ref):
    @pl.when(pl.program_id(1) == 0)
    def _():
        st_ref[...] = jnp.zeros_like(st_ref)

    q = slab_ref[0, :, 0:256] * (GLA_DK ** -0.5)
    k = slab_ref[0, :, 256:512]
    gkl = slab_ref[0, :, 1536:1664]
    gk = _mm(gkl, w2_ref[...]) + gkb_ref[...]
    log_a = _log_sigmoid(gk) / GLA_GATE_NORMALIZER
    cum = _mm_sel_lhs(tri_ref[...], log_a)
    last = cum[CHUNK - 1:CHUNK, :]
    qg = q * jnp.exp(cum)
    kg = k * jnp.exp(-cum)
    kd = k * jnp.exp(last - cum)
    dec = jnp.exp(last)
    causal = _causal(CHUNK)
    masks = _lane_half_masks()
    for h in range(GLA_HEADS):
        p, j = divmod(h, 2)
        ls = slice(p * LANES, (p + 1) * LANES)
        qm = qg[:, ls] * masks[j]
        att = jnp.where(causal, _mm_nt(qm, kg[:, ls]), 0.0)
        v_h = slab_ref[0, :, 512 + h * GLA_DV:512 + (h + 1) * GLA_DV]
        st = st_ref[h]
        o = _mm(att, v_h) + _mm_nt(qm, st)
        st_ref[h] = st * dec[:, ls] + _mm_tn(v_h, kd[:, ls] * masks[j])
        o = o * lax.rsqrt(jnp.mean(o * o, axis=-1, keepdims=True) + NORM_EPS)
        o = o * ng_ref[:, h * GLA_DV:(h + 1) * GLA_DV]
        z_h = slab_ref[0, :, 1024 + h * GLA_DV:1024 + (h + 1) * GLA_DV]
        o_ref[0, :, h * GLA_DV:(h + 1) * GLA_DV] = o * _silu(z_h)


def _gla(slab, w2p, gkb, ng, tri):
    B, S, _ = slab.shape
    const = lambda shape: pl.BlockSpec(shape, lambda b, c: (0,) * len(shape))
    return pl.pallas_call(
        _gla_kernel, grid=(B, S // CHUNK),
        in_specs=[pl.BlockSpec((1, CHUNK, GLA_SLAB), lambda b, c: (b, c, 0)),
                  const((LANES, 256)), const((1, 256)), const((1, GROUP_W)), const((CHUNK, CHUNK))],
        out_specs=pl.BlockSpec((1, CHUNK, GROUP_W), lambda b, c: (b, c, 0)),
        out_shape=jax.ShapeDtypeStruct((B, S, GROUP_W), F32),
        scratch_shapes=[pltpu.VMEM((GLA_HEADS, GLA_DV, LANES), F32)],
        compiler_params=pltpu.CompilerParams(dimension_semantics=("arbitrary", "arbitrary")),
        name="gla")(slab, w2p, gkb, ng, tri)


SSD_SLAB = 1664


def _ssd_kernel(slab_ref, rows_ref, cw_ref, cb_ref, dtb_ref, a_ref, dtb_col_ref, a_col_ref,
                dskip_ref, ng_ref, tri_ref, triu_ref, ex_ref, o_ref, buf_ref, st_ref):
    @pl.when(pl.program_id(1) == 0)
    def _():
        st_ref[...] = jnp.zeros_like(st_ref)
        buf_ref[0:CARRY_ROWS, :] = jnp.zeros((CARRY_ROWS, SSD_XBC), F32)

    taps = _shifted_rows(buf_ref, slab_ref[0, :, 0:SSD_XBC], (3, 2, 1, 0))
    xbc = cb_ref[...]
    for j in range(SSD_CONV):
        xbc = xbc + taps[j] * cw_ref[j:j + 1, :]
    xbc = _silu(xbc)
    x = xbc[:, 0:512]

    dt_col = _softplus(slab_ref[0, :, 1536:1664] + dtb_ref[...])
    cum_col = _mm_sel_lhs(tri_ref[...], dt_col * a_ref[...])
    dt_row = _softplus(rows_ref[0] + dtb_col_ref[...])
    cum_row = _mm_sel_rhs(dt_row * a_col_ref[...], triu_ref[...])
    dt_b = _mm_sel_rhs(dt_col, ex_ref[...])
    cum_b = _mm_sel_rhs(cum_col, ex_ref[...])
    last_b = cum_b[CHUNK - 1:CHUNK, :]
    xdt = x * dt_b
    xw = xdt * jnp.exp(last_b - cum_b)
    ecum = jnp.exp(cum_b)
    dec = jnp.exp(last_b)
    causal = _causal(CHUNK)
    masks = _lane_half_masks()
    for p in range(SSD_HEADS // 2):
        g = p // 2
        ls = slice(p * LANES, (p + 1) * LANES)
        bm = xbc[:, 512 + g * SSD_STATE:512 + (g + 1) * SSD_STATE]
        cm = xbc[:, 768 + g * SSD_STATE:768 + (g + 1) * SSD_STATE]
        cbm = _mm_nt(cm, bm)
        st = st_ref[p]
        y = _mm(cm, st) * ecum[:, ls]
        for j in range(2):
            h = 2 * p + j
            seg = cum_col[:, h:h + 1] - cum_row[h:h + 1, :]
            lmat = jnp.exp(jnp.where(causal, seg, -jnp.inf))
            y = y + _mm(cbm * lmat, xdt[:, ls] * masks[j])
        st_ref[p] = st * dec[:, ls] + _mm_tn(bm, xw[:, ls])
        y = y + dskip_ref[:, ls] * x[:, ls]
        o_ref[0, :, ls] = y * _silu(slab_ref[0, :, 1024 + p * LANES:1024 + (p + 1) * LANES])
    y = o_ref[0]
    o_ref[0] = y * lax.rsqrt(jnp.mean(y * y, axis=-1, keepdims=True) + NORM_EPS) * ng_ref[...]


def _ssd(slab, rows, cw, cb, dtb, a, dtb_col, a_col, dskip, ng, tri, triu, ex):
    B, S, _ = slab.shape
    nc = S // CHUNK
    const = lambda shape: pl.BlockSpec(shape, lambda b, c: (0,) * len(shape))
    return pl.pallas_call(
        _ssd_kernel, grid=(B, nc),
        in_specs=[pl.BlockSpec((1, CHUNK, SSD_SLAB), lambda b, c: (b, c, 0)),
                  pl.BlockSpec((1, SSD_HEADS, CHUNK), lambda b, c: (b * nc + c, 0, 0)),
                  const((SSD_CONV, SSD_XBC)), const((1, SSD_XBC)), const((1, LANES)), const((1, LANES)),
                  const((SSD_HEADS, 1)), const((SSD_HEADS, 1)), const((1, GROUP_W)), const((1, GROUP_W)),
                  const((CHUNK, CHUNK)), const((CHUNK, CHUNK)), const((LANES, GROUP_W))],
        out_specs=pl.BlockSpec((1, CHUNK, GROUP_W), lambda b, c: (b, c, 0)),
        out_shape=jax.ShapeDtypeStruct((B, S, GROUP_W), F32),
        scratch_shapes=[pltpu.VMEM((CARRY_ROWS + CHUNK, SSD_XBC), F32),
                        pltpu.VMEM((SSD_HEADS // 2, SSD_STATE, LANES), F32)],
        compiler_params=pltpu.CompilerParams(dimension_semantics=("arbitrary", "arbitrary")),
        name="ssd")(slab, rows, cw, cb, dtb, a, dtb_col, a_col, dskip, ng, tri, triu, ex)


MLSTM_SLAB = 2816


def _mlstm_kernel(slab_ref, rows_ref, cw_ref, cb_ref, igb_ref, fgb_ref, gb_col_ref, ng_ref,
                  tri_ref, triu_ref, o_ref, buf_ref, c_ref, nm_ref):
    @pl.when(pl.program_id(1) == 0)
    def _():
        c_ref[...] = jnp.zeros_like(c_ref)
        nm_ref[...] = jnp.zeros_like(nm_ref)
        buf_ref[0:CARRY_ROWS, :] = jnp.zeros((CARRY_ROWS, 2 * GROUP_W), F32)

    taps = _shifted_rows(buf_ref, slab_ref[0, :, 0:2 * GROUP_W], (3, 2, 1, 0))
    qk = cb_ref[...]
    for j in range(MLSTM_CONV):
        qk = qk + taps[j] * cw_ref[j:j + 1, :]
    qk = _silu(qk)

    logi_col = slab_ref[0, :, 2560:2688] + igb_ref[...]
    logf_col = _log_sigmoid(slab_ref[0, :, 2688:2816] + fgb_ref[...])
    cum_col = _mm_sel_lhs(tri_ref[...], logf_col)
    pre_row = rows_ref[0] + gb_col_ref[...]
    logi_row = pre_row[0:MLSTM_HEADS, :]
    logf_row = _log_sigmoid(pre_row[MLSTM_HEADS:2 * MLSTM_HEADS, :])
    cum_row = _mm_sel_rhs(logf_row, triu_ref[...])
    causal = _causal(CHUNK)
    for h in range(MLSTM_HEADS):
        ls = slice(h * MLSTM_HEAD, (h + 1) * MLSTM_HEAD)
        q = qk[:, ls]
        k = qk[:, GROUP_W + h * MLSTM_HEAD:GROUP_W + (h + 1) * MLSTM_HEAD] * (MLSTM_HEAD ** -0.5)
        v = slab_ref[0, :, 1024 + h * MLSTM_HEAD:1024 + (h + 1) * MLSTM_HEAD]
        ci = cum_col[:, h:h + 1]
        li = logi_col[:, h:h + 1]
        cr = cum_row[h:h + 1, :]
        lir = logi_row[h:h + 1, :]
        last = cr[:, CHUNK - 1:CHUNK]
        c_prev = c_ref[h]
        n_prev = nm_ref[h, 0:1, :]
        m_prev = nm_ref[h, 1:2, 0:1]

        g = last - ci + li
        g_max = jnp.max(g, axis=0, keepdims=True)
        kw = k * jnp.exp(g - g_max)
        c_loc = _mm_tn(kw, v)
        n_loc = jnp.sum(kw, axis=0, keepdims=True)

        log_d = jnp.where(causal, ci - cr + lir, -jnp.inf)
        m_inter = ci + m_prev
        m_l = jnp.maximum(m_inter, jnp.max(log_d, axis=-1, keepdims=True))
        wqk = _mm_nt(q, k) * jnp.exp(log_d - m_l)
        w_inter = jnp.exp(m_inter - m_l)
        num = _mm(wqk, v) + w_inter * _mm(q, c_prev)
        den = jnp.sum(wqk, axis=-1, keepdims=True) + w_inter * jnp.sum(q * n_prev, axis=-1, keepdims=True)
        den = jnp.maximum(jnp.abs(den), jnp.exp(-m_l))
        hh = num / den * _sigmoid(slab_ref[0, :, 1536 + h * MLSTM_HEAD:1536 + (h + 1) * MLSTM_HEAD])
        mu = jnp.mean(hh, axis=-1, keepdims=True)
        yc = hh - mu
        hh = yc * lax.rsqrt(jnp.mean(yc * yc, axis=-1, keepdims=True) + NORM_EPS) * ng_ref[:, ls]
        o_ref[0, :, ls] = hh * _silu(slab_ref[0, :, 2048 + h * MLSTM_HEAD:2048 + (h + 1) * MLSTM_HEAD])

        m_new = jnp.maximum(last + m_prev, g_max)
        a_old = jnp.exp(last + m_prev - m_new)
        a_new = jnp.exp(g_max - m_new)
        c_ref[h] = a_old * c_prev + a_new * c_loc
        nm_ref[h, 0:1, :] = a_old * n_prev + a_new * n_loc
        nm_ref[h, 1:2, :] = jnp.broadcast_to(m_new, (1, MLSTM_HEAD))


def _mlstm(slab, rows, cw, cb, igb, fgb, gb_col, ng, tri, triu):
    B, S, _ = slab.shape
    nc = S // CHUNK
    const = lambda shape: pl.BlockSpec(shape, lambda b, c: (0,) * len(shape))
    return pl.pallas_call(
        _mlstm_kernel, grid=(B, nc),
        in_specs=[pl.BlockSpec((1, CHUNK, MLSTM_SLAB), lambda b, c: (b, c, 0)),
                  pl.BlockSpec((1, 2 * MLSTM_HEADS, CHUNK), lambda b, c: (b * nc + c, 0, 0)),
                  const((MLSTM_CONV, 2 * GROUP_W)), const((1, 2 * GROUP_W)), const((1, LANES)),
                  const((1, LANES)), const((2 * MLSTM_HEADS, 1)), const((1, GROUP_W)),
                  const((CHUNK, CHUNK)), const((CHUNK, CHUNK))],
        out_specs=pl.BlockSpec((1, CHUNK, GROUP_W), lambda b, c: (b, c, 0)),
        out_shape=jax.ShapeDtypeStruct((B, S, GROUP_W), F32),
        scratch_shapes=[pltpu.VMEM((CARRY_ROWS + CHUNK, 2 * GROUP_W), F32),
                        pltpu.VMEM((MLSTM_HEADS, MLSTM_HEAD, MLSTM_HEAD), F32),
                        pltpu.VMEM((MLSTM_HEADS, CARRY_ROWS, MLSTM_HEAD), F32)],
        compiler_params=pltpu.CompilerParams(dimension_semantics=("arbitrary", "arbitrary")),
        name="mlstm")(slab, rows, cw, cb, igb, fgb, gb_col, ng, tri, triu)


def _rwkv_slab_width(layer):
    return 2176 if layer == 0 else 2304


def _rwkv_shift_cols(layer):
    return 1664 if layer == 0 else 1792


def _rwkv_kernel(*refs, layer):
    if layer == 0:
        (slab_ref, mu_ref, w0_ref, a0_ref, w2a2_ref, kk_ref, ka_ref, rk_ref, lng_ref, lnb_ref,
         tri_ref, bd_ref, strict_ref, incl_ref, lvl_ref, o_ref, vf_out_ref, buf_ref, st_ref) = refs
    else:
        (slab_ref, vf_ref, mu_ref, w0_ref, a0_ref, w2a2_ref, v0_ref, v2_ref, kk_ref, ka_ref, rk_ref,
         lng_ref, lnb_ref, tri_ref, bd_ref, strict_ref, incl_ref, lvl_ref, o_ref, buf_ref, st_ref) = refs
    ws = _rwkv_shift_cols(layer)

    @pl.when(pl.program_id(1) == 0)
    def _():
        st_ref[...] = jnp.zeros_like(st_ref)
        buf_ref[0:CARRY_ROWS, :] = jnp.zeros((CARRY_ROWS, ws), F32)

    f = slab_ref[0, :, 0:ws]
    (prev,) = _shifted_rows(buf_ref, f, (1,))
    f = f + mu_ref[...] * (prev - f)
    r = f[:, 0:512]
    k = f[:, 512:1024]
    v = f[:, 1024:1536]
    lora = f[:, 1536:1664]
    lane = lax.broadcasted_iota(jnp.int32, (1, LANES), 1)
    lora = jnp.where(lane < RWKV_W_RANK, jnp.tanh(lora), lora)
    wa = _mm(lora, w2a2_ref[...])
    w_log = -_softplus(-(w0_ref[...] + wa[:, 0:512])) - RWKV_DECAY_OFFSET
    lw = -jnp.exp(w_log)
    a = _sigmoid(a0_ref[...] + wa[:, 512:1024])
    if layer == 0:
        vf_out_ref[0] = v
    else:
        mix = _sigmoid(v0_ref[...] + _mm(f[:, 1664:1792], v2_ref[...]))
        v = v + (vf_ref[0] - v) * mix
    kk = k * kk_ref[...]
    k = k * (1.0 + (a - 1.0) * ka_ref[...])
    bd = bd_ref[...]
    cum = _mm_sel_lhs(tri_ref[...], lw)
    last = cum[CHUNK - 1:CHUNK, :]
    e_pos = jnp.exp(cum)
    e_neg = jnp.exp(-cum)
    e_end = jnp.exp(last - cum)
    e_prev = jnp.exp(cum - lw)
    gam = jnp.exp(last)
    masks = _lane_half_masks()
    strict = strict_ref[...] > 0.5
    incl = incl_ref[...] > 0.5
    rows2 = lambda t: jnp.concatenate([t * masks[0], t * masks[1]], axis=0)
    eye = (lax.broadcasted_iota(jnp.int32, (LANES, LANES), 0)
           == lax.broadcasted_iota(jnp.int32, (LANES, LANES), 1)).astype(F32)
    for p in range(GROUP_W // LANES):
        ls = slice(p * LANES, (p + 1) * LANES)
        kk_p = kk[:, ls]
        ss = _mm_sel_rhs(kk_p * kk_p, bd)
        kk_p = kk_p / jnp.maximum(jnp.sqrt(ss), 1e-12)
        a_p, k_p, r_p, v_p = a[:, ls], k[:, ls], r[:, ls], v[:, ls]
        b_p = kk_p * a_p
        la = rows2(-kk_p * e_prev[:, ls])
        lr = rows2(r_p * e_pos[:, ls])
        rb = rows2(b_p * e_neg[:, ls])
        rk = rows2(k_p * e_neg[:, ls])
        bh = rows2(b_p * e_end[:, ls])
        kh = rows2(k_p * e_end[:, ls])
        vs = rows2(v_p)
        aa = _mm_nt(jnp.concatenate([la, lr], axis=0), jnp.concatenate([rb, rk], axis=0))
        nab = jnp.where(strict, aa[0:LANES, 0:LANES], 0.0)
        aak = jnp.where(strict, aa[0:LANES, LANES:2 * LANES], 0.0)
        arb = jnp.where(incl, aa[LANES:2 * LANES, 0:LANES], 0.0)
        ark = jnp.where(incl, aa[LANES:2 * LANES, LANES:2 * LANES], 0.0)
        t = eye + nab * lvl_ref[0]
        for lv in range(1, 6):
            t = t + _mm(t, _mm(nab * lvl_ref[lv], t))
        wst = _mm(t, la)
        u0 = _mm(t, _mm(aak, vs))
        y0 = _mm(ark, vs) + _mm(arb, u0)
        qt = lr + _mm(arb, wst)
        st = st_ref[p]
        ust = _mm_nt(wst, st) + u0
        yst = _mm_nt(qt, st) + y0
        st_ref[p] = st * gam[:, ls] + _mm_tn(jnp.concatenate([ust, vs], axis=0),
                                             jnp.concatenate([bh, kh], axis=0))
        y = yst[0:CHUNK] + yst[CHUNK:2 * CHUNK]
        inv = 1.0 / RWKV_HEAD
        mu_y = _mm_sel_rhs(y, bd) * inv
        yc = y - mu_y
        var = _mm_sel_rhs(yc * yc, bd) * inv
        yn = yc * lax.rsqrt(var + RWKV_LN_EPS) * lng_ref[:, ls] + lnb_ref[:, ls]
        bonus = _mm_sel_rhs(r_p * k_p * rk_ref[:, ls], bd) * v_p
        zoff = _rwkv_slab_width(layer) - GROUP_W
        z_p = slab_ref[0, :, zoff + p * LANES:zoff + (p + 1) * LANES]
        o_ref[0, :, ls] = (yn + bonus) * _silu(z_p)


def _rwkv(layer, slab, vf, mu, w0, a0, w2a2, v0, v2, kkw, ka, rk, lng, lnb, tri, bd, strict, incl, lvl):
    B, S, W = slab.shape
    ws = _rwkv_shift_cols(layer)
    const = lambda shape: pl.BlockSpec(shape, lambda b, c: (0,) * len(shape))
    tok = lambda w: pl.BlockSpec((1, CHUNK, w), lambda b, c: (b, c, 0))
    vecw = const((1, GROUP_W))
    in_specs = [tok(W)]
    args = [slab]
    if layer > 0:
        in_specs.append(tok(GROUP_W))
        args.append(vf)
    in_specs += [const((1, ws)), vecw, vecw, const((LANES, 2 * GROUP_W))]
    args += [mu, w0, a0, w2a2]
    if layer > 0:
        in_specs += [vecw, const((LANES, GROUP_W))]
        args += [v0, v2]
    in_specs += [vecw, vecw, vecw, vecw, vecw, const((CHUNK, CHUNK)), const((LANES, LANES)),
                 const((LANES, LANES)), const((LANES, LANES)), const((6, LANES, LANES))]
    args += [kkw, ka, rk, lng, lnb, tri, bd, strict, incl, lvl]
    out_shape = [jax.ShapeDtypeStruct((B, S, GROUP_W), F32)]
    out_specs = [tok(GROUP_W)]
    if layer == 0:
        out_shape.append(jax.ShapeDtypeStruct((B, S, GROUP_W), F32))
        out_specs.append(tok(GROUP_W))
    res = pl.pallas_call(
        functools.partial(_rwkv_kernel, layer=layer), grid=(B, S // CHUNK),
        in_specs=in_specs, out_specs=out_specs, out_shape=out_shape,
        scratch_shapes=[pltpu.VMEM((CARRY_ROWS + CHUNK, ws), F32),
                        pltpu.VMEM((GROUP_W // LANES, LANES, LANES), F32)],
        compiler_params=pltpu.CompilerParams(dimension_semantics=("arbitrary", "arbitrary")),
        name="rwkv")(*args)
    return (res[0], res[1]) if layer == 0 else (res[0], vf)


def _np_consts():
    i = np.arange(CHUNK)
    tri = (i[None, :] <= i[:, None]).astype(np.float32)
    t = np.arange(LANES)
    same = (t[:, None] // CHUNK) == (t[None, :] // CHUNK)
    strict = (same & (t[None, :] < t[:, None])).astype(np.float32)
    incl = (same & (t[None, :] <= t[:, None])).astype(np.float32)
    bd = same.astype(np.float32)
    lvl = np.stack([(((t[:, None] >> l) == (t[None, :] >> l))
                     & ((t[:, None] >> (l - 1)) != (t[None, :] >> (l - 1)))).astype(np.float32)
                    for l in range(1, 7)])
    ex = np.zeros((LANES, GROUP_W), np.float32)
    for h in range(SSD_HEADS):
        ex[h, h * SSD_HEADDIM:(h + 1) * SSD_HEADDIM] = 1.0
    return dict(tri=tri, triu=tri.T.copy(), strict=strict, incl=incl, bd=bd, lvl=lvl, ex=ex)


def _pad_cols(w, width):
    return jnp.pad(w, ((0, 0), (0, width - w.shape[1])))


def _pad_vec(v, width):
    v = v.reshape(1, -1)
    return jnp.pad(v, ((0, 0), (0, width - v.shape[1])))


def _layer(x2, v_first, layer, p, B, S, consts, g_final):
    w_in = p['w_in']
    sw = 3 * GROUP_W + RWKV_W_RANK + RWKV_A_RANK + (RWKV_V_RANK if layer > 0 else 0)
    widths = [256, 256, 512, 16, 512, sw, 512, SSD_XBC, SSD_HEADS, 512, 1024, 512, 4, 4, 512, 512]
    offs = np.concatenate([[0], np.cumsum(widths)]).tolist()
    names = ['gla_q', 'gla_k', 'gla_v', 'gla_gk', 'gla_z', 'rwkv_shift', 'rwkv_z', 'ssd_xbc', 'ssd_dt',
             'ssd_z', 'mlstm_qk', 'mlstm_v', 'mlstm_i', 'mlstm_f', 'mlstm_o', 'mlstm_z']
    col = {n: w_in[:, offs[i]:offs[i + 1]] for i, n in enumerate(names)}
    g = p['norm_g'].reshape(1, D_MODEL)
    tri, triu = consts['tri'], consts['triu']
    rows_of = lambda t: t.reshape(t.shape[0], (B * S) // CHUNK, CHUNK).transpose(1, 0, 2)

    w_gla = _bf(jnp.concatenate([col['gla_q'], col['gla_k'], col['gla_v'], col['gla_z'],
                                 _pad_cols(col['gla_gk'], LANES)], axis=1))
    slab = _in_proj(x2, g, w_gla).reshape(B, S, GLA_SLAB)
    w2p = _bf(jnp.pad(p['gla_gk_w2'], ((0, LANES - GLA_RANK), (0, 0))))
    y_gla = _gla(slab, w2p, p['gla_gk_b'].reshape(1, -1), p['gla_norm_g'].reshape(1, -1), _bf(tri))

    sh = col['rwkv_shift']
    mu = p['rwkv_mu']
    if layer == 0:
        w_rwkv = jnp.concatenate([sh, col['rwkv_z']], axis=1)
        mu_p = mu.reshape(1, -1)
    else:
        w_rwkv = jnp.concatenate([_pad_cols(sh, 1792), col['rwkv_z']], axis=1)
        mu_p = _pad_vec(mu, 1792)
    slab = _in_proj(x2, g, _bf(w_rwkv)).reshape(B, S, _rwkv_slab_width(layer))
    w2a2 = jnp.zeros((LANES, 2 * GROUP_W), F32)
    w2a2 = w2a2.at[0:RWKV_W_RANK, 0:GROUP_W].set(p['rwkv_w2'])
    w2a2 = w2a2.at[RWKV_W_RANK:, GROUP_W:].set(p['rwkv_a2'])
    vec = lambda t: t.reshape(1, GROUP_W)
    v0 = v2 = None
    if layer > 0:
        v0 = vec(p['rwkv_v0'])
        v2 = _bf(jnp.pad(p['rwkv_v2'], ((0, LANES - RWKV_V_RANK), (0, 0))))
    y_rwkv, v_first = _rwkv(layer, slab, v_first, mu_p, vec(p['rwkv_w0']), vec(p['rwkv_a0']), _bf(w2a2),
                            v0, v2, vec(p['rwkv_k_k']), vec(p['rwkv_k_a']), vec(p['rwkv_r_k']),
                            vec(p['rwkv_ln_g']), vec(p['rwkv_ln_b']), _bf(tri), _bf(consts['bd']),
                            consts['strict'], consts['incl'], consts['lvl'])

    w_ssd = _bf(jnp.concatenate([col['ssd_xbc'], col['ssd_z'], _pad_cols(col['ssd_dt'], LANES)], axis=1))
    slab, rows = _in_proj(x2, g, w_ssd, _bf(col['ssd_dt'].T))
    a_neg = -jnp.exp(p['ssd_a_log'])
    y_ssd = _ssd(slab.reshape(B, S, SSD_SLAB), rows_of(rows), p['ssd_conv_w'], p['ssd_conv_b'].reshape(1, -1),
                 _pad_vec(p['ssd_dt_bias'], LANES), _pad_vec(a_neg, LANES),
                 p['ssd_dt_bias'].reshape(-1, 1), a_neg.reshape(-1, 1),
                 jnp.repeat(p['ssd_d'], SSD_HEADDIM).reshape(1, -1), p['ssd_norm_g'].reshape(1, -1),
                 _bf(tri), _bf(triu), _bf(consts['ex']))

    w_ml = _bf(jnp.concatenate([col['mlstm_qk'], col['mlstm_v'], col['mlstm_o'], col['mlstm_z'],
                                _pad_cols(col['mlstm_i'], LANES), _pad_cols(col['mlstm_f'], LANES)], axis=1))
    wt_ml = _bf(jnp.concatenate([col['mlstm_i'], col['mlstm_f']], axis=1).T)
    slab, rows = _in_proj(x2, g, w_ml, wt_ml)
    gb_col = jnp.concatenate([p['mlstm_ig_b'], p['mlstm_fg_b']]).reshape(-1, 1)
    y_ml = _mlstm(slab.reshape(B, S, MLSTM_SLAB), rows_of(rows), p['mlstm_conv_w'],
                  p['mlstm_conv_b'].reshape(1, -1), _pad_vec(p['mlstm_ig_b'], LANES),
                  _pad_vec(p['mlstm_fg_b'], LANES), gb_col, p['mlstm_norm_g'].reshape(1, -1),
                  _bf(tri), _bf(triu))

    T = B * S
    ys = [y.reshape(T, GROUP_W) for y in (y_gla, y_rwkv, y_ssd, y_ml)]
    return _out_proj(x2, ys, _bf(p['w_out']), g_final), v_first


_PARAM_NAMES_0 = ['norm_g', 'w_in', 'w_out', 'gla_gk_w2', 'gla_gk_b', 'gla_norm_g', 'rwkv_mu', 'rwkv_w0',
                  'rwkv_w2', 'rwkv_a0', 'rwkv_a2', 'rwkv_k_k', 'rwkv_k_a', 'rwkv_r_k', 'rwkv_ln_g',
                  'rwkv_ln_b', 'ssd_conv_w', 'ssd_conv_b', 'ssd_dt_bias', 'ssd_a_log', 'ssd_d',
                  'ssd_norm_g', 'mlstm_conv_w', 'mlstm_conv_b', 'mlstm_ig_b', 'mlstm_fg_b', 'mlstm_norm_g']
_PARAM_NAMES_1 = (_PARAM_NAMES_0[:11] + ['rwkv_v0', 'rwkv_v2'] + _PARAM_NAMES_0[11:])


def kernel(x,
           norm_g_0, w_in_0, w_out_0, gla_gk_w2_0, gla_gk_b_0, gla_norm_g_0,
           rwkv_mu_0, rwkv_w0_0, rwkv_w2_0, rwkv_a0_0, rwkv_a2_0,
           rwkv_k_k_0, rwkv_k_a_0, rwkv_r_k_0, rwkv_ln_g_0, rwkv_ln_b_0,
           ssd_conv_w_0, ssd_conv_b_0, ssd_dt_bias_0, ssd_a_log_0, ssd_d_0, ssd_norm_g_0,
           mlstm_conv_w_0, mlstm_conv_b_0, mlstm_ig_b_0, mlstm_fg_b_0, mlstm_norm_g_0,
           norm_g_1, w_in_1, w_out_1, gla_gk_w2_1, gla_gk_b_1, gla_norm_g_1,
           rwkv_mu_1, rwkv_w0_1, rwkv_w2_1, rwkv_a0_1, rwkv_a2_1, rwkv_v0_1, rwkv_v2_1,
           rwkv_k_k_1, rwkv_k_a_1, rwkv_r_k_1, rwkv_ln_g_1, rwkv_ln_b_1,
           ssd_conv_w_1, ssd_conv_b_1, ssd_dt_bias_1, ssd_a_log_1, ssd_d_1, ssd_norm_g_1,
           mlstm_conv_w_1, mlstm_conv_b_1, mlstm_ig_b_1, mlstm_fg_b_1, mlstm_norm_g_1,
           final_norm_g):
    params = (norm_g_0, w_in_0, w_out_0, gla_gk_w2_0, gla_gk_b_0, gla_norm_g_0,
              rwkv_mu_0, rwkv_w0_0, rwkv_w2_0, rwkv_a0_0, rwkv_a2_0,
              rwkv_k_k_0, rwkv_k_a_0, rwkv_r_k_0, rwkv_ln_g_0, rwkv_ln_b_0,
              ssd_conv_w_0, ssd_conv_b_0, ssd_dt_bias_0, ssd_a_log_0, ssd_d_0, ssd_norm_g_0,
              mlstm_conv_w_0, mlstm_conv_b_0, mlstm_ig_b_0, mlstm_fg_b_0, mlstm_norm_g_0,
              norm_g_1, w_in_1, w_out_1, gla_gk_w2_1, gla_gk_b_1, gla_norm_g_1,
              rwkv_mu_1, rwkv_w0_1, rwkv_w2_1, rwkv_a0_1, rwkv_a2_1, rwkv_v0_1, rwkv_v2_1,
              rwkv_k_k_1, rwkv_k_a_1, rwkv_r_k_1, rwkv_ln_g_1, rwkv_ln_b_1,
              ssd_conv_w_1, ssd_conv_b_1, ssd_dt_bias_1, ssd_a_log_1, ssd_d_1, ssd_norm_g_1,
              mlstm_conv_w_1, mlstm_conv_b_1, mlstm_ig_b_1, mlstm_fg_b_1, mlstm_norm_g_1,
              final_norm_g)
    n0 = len(_PARAM_NAMES_0)
    n1 = len(_PARAM_NAMES_1)
    p0 = dict(zip(_PARAM_NAMES_0, params[:n0]))
    p1 = dict(zip(_PARAM_NAMES_1, params[n0:n0 + n1]))
    final_norm_g = params[n0 + n1]
    B, S, _ = x.shape
    consts = {k: jnp.asarray(v) for k, v in _np_consts().items()}
    x2 = x.reshape(B * S, D_MODEL)
    x2, v_first = _layer(x2, None, 0, p0, B, S, consts, None)
    x2, _ = _layer(x2, v_first, 1, p1, B, S, consts, final_norm_g.reshape(1, D_MODEL))
    return x2.reshape(B, S, D_MODEL)
```

```python
import functools

import numpy as np
import jax
import jax.numpy as jnp
from jax import lax
from jax.experimental import pallas as pl
from jax.experimental.pallas import tpu as pltpu

F32 = jnp.float32
BF16 = jnp.bfloat16

D_MODEL = 2048
CHUNK = 64
GROUP_W = 512
NORM_EPS = 1e-6
LANES = 128
CARRY_ROWS = 8

GLA_HEADS, GLA_DK, GLA_DV, GLA_RANK = 4, 64, 128, 16
GLA_GATE_NORMALIZER = 16.0
RWKV_HEAD, RWKV_W_RANK, RWKV_A_RANK, RWKV_V_RANK = 64, 64, 64, 32
RWKV_LN_EPS = 64e-5
RWKV_DECAY_OFFSET = 0.5
SSD_HEADS, SSD_HEADDIM, SSD_STATE, SSD_CONV = 8, 64, 128, 4
SSD_XBC = 1024
MLSTM_HEADS, MLSTM_HEAD, MLSTM_CONV = 4, 128, 4

ROW_TILE_IN = 256
ROW_TILE_OUT = 512
VMEM_LIMIT_PROJ = 48 * 2**20


def _bf(x):
    return x.astype(BF16)


def _mm(a, b):
    return jnp.dot(_bf(a), _bf(b), preferred_element_type=F32)


def _mm_nt(a, b):
    return lax.dot_general(_bf(a), _bf(b), (((1,), (1,)), ((), ())), preferred_element_type=F32)


def _mm_tn(a, b):
    return lax.dot_general(_bf(a), _bf(b), (((0,), (0,)), ((), ())), preferred_element_type=F32)


def _split3(x):
    hi = _bf(x)
    r1 = x - hi.astype(F32)
    mid = _bf(r1)
    lo = _bf(r1 - mid.astype(F32))
    return hi, mid, lo


def _mm_sel_rhs(x, sel):
    hi, mid, lo = _split3(x)
    d = lambda a: jnp.dot(a, sel, preferred_element_type=F32)
    return d(hi) + d(mid) + d(lo)


def _mm_sel_lhs(sel, x):
    hi, mid, lo = _split3(x)
    d = lambda a: jnp.dot(sel, a, preferred_element_type=F32)
    return d(hi) + d(mid) + d(lo)


def _rowsum(x):
    ones = jnp.ones((x.shape[-1], LANES), BF16)
    hi = _bf(x)
    lo = _bf(x - hi.astype(F32))
    return (jnp.dot(hi, ones, preferred_element_type=F32)
            + jnp.dot(lo, ones, preferred_element_type=F32))


def _sigmoid(x):
    return 1.0 / (1.0 + jnp.exp(-x))


def _silu(x):
    return x * _sigmoid(x)


def _softplus(x):
    return jnp.maximum(x, 0.0) + jnp.log1p(jnp.exp(-jnp.abs(x)))


def _log_sigmoid(x):
    return -_softplus(-x)


def _lane_half_masks():
    lane = lax.broadcasted_iota(jnp.int32, (1, LANES), 1)
    lo = (lane < LANES // 2).astype(F32)
    return lo, 1.0 - lo


def _causal(n):
    r = lax.broadcasted_iota(jnp.int32, (n, n), 0)
    c = lax.broadcasted_iota(jnp.int32, (n, n), 1)
    return c <= r


def _shifted_rows(buf_ref, cur, offsets):
    buf_ref[CARRY_ROWS:CARRY_ROWS + CHUNK, :] = cur
    outs = [buf_ref[CARRY_ROWS - off:CARRY_ROWS - off + CHUNK, :] for off in offsets]
    tail = buf_ref[CHUNK:CHUNK + CARRY_ROWS, :]
    buf_ref[0:CARRY_ROWS, :] = tail
    return outs


def _in_proj_kernel(*refs, has_t):
    if has_t:
        x_ref, g_ref, w_ref, wt_ref, o_ref, ot_ref = refs
    else:
        x_ref, g_ref, w_ref, o_ref = refs
    x = x_ref[...]
    h = x * lax.rsqrt(jnp.mean(x * x, axis=-1, keepdims=True) + NORM_EPS) * g_ref[...]
    hb = _bf(h)
    o_ref[...] = jnp.dot(hb, w_ref[...], preferred_element_type=F32)
    if has_t:
        ot_ref[...] = lax.dot_general(wt_ref[...], hb, (((1,), (1,)), ((), ())),
                                      preferred_element_type=F32)


def _in_proj(x2, g, w, wt=None):
    T = x2.shape[0]
    n = w.shape[1]
    tm = ROW_TILE_IN
    in_specs = [pl.BlockSpec((tm, D_MODEL), lambda i: (i, 0)),
                pl.BlockSpec((1, D_MODEL), lambda i: (0, 0)),
                pl.BlockSpec((D_MODEL, n), lambda i: (0, 0))]
    out_shape = [jax.ShapeDtypeStruct((T, n), F32)]
    out_specs = [pl.BlockSpec((tm, n), lambda i: (i, 0))]
    args = [x2, g, w]
    if wt is not None:
        r = wt.shape[0]
        in_specs.append(pl.BlockSpec((r, D_MODEL), lambda i: (0, 0)))
        out_shape.append(jax.ShapeDtypeStruct((r, T), F32))
        out_specs.append(pl.BlockSpec((r, tm), lambda i: (0, i)))
        args.append(wt)
    res = pl.pallas_call(
        functools.partial(_in_proj_kernel, has_t=wt is not None),
        grid=(T // tm,), in_specs=in_specs, out_specs=out_specs, out_shape=out_shape,
        compiler_params=pltpu.CompilerParams(dimension_semantics=("arbitrary",),
                                             vmem_limit_bytes=VMEM_LIMIT_PROJ),
        name="in_proj")(*args)
    return res if wt is not None else res[0]


def _out_proj_kernel(*refs, final):
    if final:
        x_ref, y0, y1, y2, y3, w_ref, g_ref, o_ref = refs
    else:
        x_ref, y0, y1, y2, y3, w_ref, o_ref = refs
    acc = x_ref[...]
    for gi, y in enumerate((y0, y1, y2, y3)):
        acc = acc + jnp.dot(_bf(y[...]), w_ref[gi * GROUP_W:(gi + 1) * GROUP_W, :],
                            preferred_element_type=F32)
    if final:
        acc = acc * lax.rsqrt(jnp.mean(acc * acc, axis=-1, keepdims=True) + NORM_EPS) * g_ref[...]
    o_ref[...] = acc


def _out_proj(x2, ys, w, g_final=None):
    T = x2.shape[0]
    tm = ROW_TILE_OUT
    final = g_final is not None
    in_specs = [pl.BlockSpec((tm, D_MODEL), lambda i: (i, 0))]
    in_specs += [pl.BlockSpec((tm, GROUP_W), lambda i: (i, 0)) for _ in range(4)]
    in_specs += [pl.BlockSpec((D_MODEL, D_MODEL), lambda i: (0, 0))]
    args = [x2, *ys, w]
    if final:
        in_specs.append(pl.BlockSpec((1, D_MODEL), lambda i: (0, 0)))
        args.append(g_final)
    return pl.pallas_call(
        functools.partial(_out_proj_kernel, final=final),
        grid=(T // tm,), in_specs=in_specs,
        out_specs=pl.BlockSpec((tm, D_MODEL), lambda i: (i, 0)),
        out_shape=jax.ShapeDtypeStruct((T, D_MODEL), F32),
        compiler_params=pltpu.CompilerParams(dimension_semantics=("arbitrary",),
                                             vmem_limit_bytes=VMEM_LIMIT_PROJ),
        name="out_proj")(*args)


GLA_SLAB = 1664


BATCH_BLOCK = 4


def _lockstep(gens):
    gens = list(gens)
    while gens:
        alive = []
        for g in gens:
            try:
                next(g)
                alive.append(g)
            except StopIteration:
                pass
        gens = alive


def _gla_kernel(slab_ref, w2_ref, gkb_ref, ng_ref, tri_ref, o_ref, st_ref, *, nb):
    @pl.when(pl.program_id(1) == 0)
    def _():
        st_ref[...] = jnp.zeros_like(st_ref)

    causal = _causal(CHUNK)
    masks = _lane_half_masks()

    def head(b, h, qg, kg, kd, dec):
        p, j = divmod(h, 2)
        ls = slice(p * LANES, (p + 1) * LANES)
        qm = qg[:, ls] * masks[j]
        att = jnp.where(causal, _mm_nt(qm, kg[:, ls]), 0.0)
        yield
        v_h = slab_ref[b, :, 512 + h * GLA_DV:512 + (h + 1) * GLA_DV]
        st = st_ref[b, h]
        o = _mm(att, v_h) + _mm_nt(qm, st)
        st_ref[b, h] = st * dec[:, ls] + _mm_tn(v_h, kd[:, ls] * masks[j])
        yield
        ms = jnp.mean(o * o, axis=-1, keepdims=True)
        yield
        o = o * lax.rsqrt(ms + NORM_EPS)
        o = o * ng_ref[:, h * GLA_DV:(h + 1) * GLA_DV]
        z_h = slab_ref[b, :, 1024 + h * GLA_DV:1024 + (h + 1) * GLA_DV]
        o_ref[b, :, h * GLA_DV:(h + 1) * GLA_DV] = o * _silu(z_h)

    gens = []
    for b in range(nb):
        q = slab_ref[b, :, 0:256] * (GLA_DK ** -0.5)
        k = slab_ref[b, :, 256:512]
        gk = _mm(slab_ref[b, :, 1536:1664], w2_ref[...]) + gkb_ref[...]
        log_a = _log_sigmoid(gk) / GLA_GATE_NORMALIZER
        cum = _mm_sel_lhs(tri_ref[...], log_a)
        last = cum[CHUNK - 1:CHUNK, :]
        qg = q * jnp.exp(cum)
        kg = k * jnp.exp(-cum)
        kd = k * jnp.exp(last - cum)
        dec = jnp.exp(last)
        gens += [head(b, h, qg, kg, kd, dec) for h in range(GLA_HEADS)]
    _lockstep(gens)


def _gla(slab, w2p, gkb, ng, tri):
    B, S, _ = slab.shape
    nb = BATCH_BLOCK
    const = lambda shape: pl.BlockSpec(shape, lambda b, c: (0,) * len(shape))
    return pl.pallas_call(
        functools.partial(_gla_kernel, nb=nb), grid=(B // nb, S // CHUNK),
        in_specs=[pl.BlockSpec((nb, CHUNK, GLA_SLAB), lambda b, c: (b, c, 0)),
                  const((LANES, 256)), const((1, 256)), const((1, GROUP_W)), const((CHUNK, CHUNK))],
        out_specs=pl.BlockSpec((nb, CHUNK, GROUP_W), lambda b, c: (b, c, 0)),
        out_shape=jax.ShapeDtypeStruct((B, S, GROUP_W), F32),
        scratch_shapes=[pltpu.VMEM((nb, GLA_HEADS, GLA_DV, LANES), F32)],
        compiler_params=pltpu.CompilerParams(dimension_semantics=("arbitrary", "arbitrary")),
        name="gla")(slab, w2p, gkb, ng, tri)


SSD_SLAB = 1664


def _ssd_kernel(slab_ref, rows_ref, cw_ref, cb_ref, dtb_ref, a_ref, dtb_col_ref, a_col_ref,
                dskip_ref, ng_ref, tri_ref, triu_ref, ex_ref, o_ref, buf_ref, st_ref, *, nb):
    @pl.when(pl.program_id(1) == 0)
    def _():
        st_ref[...] = jnp.zeros_like(st_ref)
        for b in range(nb):
            buf_ref[b, 0:CARRY_ROWS, :] = jnp.zeros((CARRY_ROWS, SSD_XBC), F32)

    causal = _causal(CHUNK)
    masks = _lane_half_masks()

    def pair(b, p, xbc, cum_col, cum_row, xdt, xw, ecum, dec):
        g = p // 2
        ls = slice(p * LANES, (p + 1) * LANES)
        bm = xbc[:, 512 + g * SSD_STATE:512 + (g + 1) * SSD_STATE]
        cm = xbc[:, 768 + g * SSD_STATE:768 + (g + 1) * SSD_STATE]
        cbm = _mm_nt(cm, bm)
        st = st_ref[b, p]
        y = _mm(cm, st) * ecum[:, ls]
        st_ref[b, p] = st * dec[:, ls] + _mm_tn(bm, xw[:, ls])
        yield
        for j in range(2):
            h = 2 * p + j
            seg = cum_col[:, h:h + 1] - cum_row[h:h + 1, :]
            lmat = jnp.exp(jnp.where(causal, seg, -jnp.inf))
            y = y + _mm(cbm * lmat, xdt[:, ls] * masks[j])
        yield
        y = y + dskip_ref[:, ls] * xbc[:, ls]
        o_ref[b, :, ls] = y * _silu(slab_ref[b, :, 1024 + p * LANES:1024 + (p + 1) * LANES])

    gens = []
    for b in range(nb):
        taps = _shifted_rows(buf_ref.at[b], slab_ref[b, :, 0:SSD_XBC], (3, 2, 1, 0))
        xbc = cb_ref[...]
        for j in range(SSD_CONV):
            xbc = xbc + taps[j] * cw_ref[j:j + 1, :]
        xbc = _silu(xbc)
        dt_col = _softplus(slab_ref[b, :, 1536:1664] + dtb_ref[...])
        cum_col = _mm_sel_lhs(tri_ref[...], dt_col * a_ref[...])
        dt_row = _softplus(rows_ref[b, 0] + dtb_col_ref[...])
        cum_row = _mm_sel_rhs(dt_row * a_col_ref[...], triu_ref[...])
        dt_b = _mm_sel_rhs(dt_col, ex_ref[...])
        cum_b = _mm_sel_rhs(cum_col, ex_ref[...])
        last_b = cum_b[CHUNK - 1:CHUNK, :]
        xdt = xbc[:, 0:512] * dt_b
        xw = xdt * jnp.exp(last_b - cum_b)
        ecum = jnp.exp(cum_b)
        dec = jnp.exp(last_b)
        gens += [pair(b, p, xbc, cum_col, cum_row, xdt, xw, ecum, dec) for p in range(SSD_HEADS // 2)]
    _lockstep(gens)
    for b in range(nb):
        y = o_ref[b]
        inv = lax.rsqrt(_rowsum(y * y) * (1.0 / GROUP_W) + NORM_EPS)
        for p in range(GROUP_W // LANES):
            ls = slice(p * LANES, (p + 1) * LANES)
            o_ref[b, :, ls] = y[:, ls] * inv * ng_ref[:, ls]


def _ssd(slab, rows, cw, cb, dtb, a, dtb_col, a_col, dskip, ng, tri, triu, ex):
    B, S, _ = slab.shape
    nb = BATCH_BLOCK
    const = lambda shape: pl.BlockSpec(shape, lambda b, c: (0,) * len(shape))
    return pl.pallas_call(
        functools.partial(_ssd_kernel, nb=nb), grid=(B // nb, S // CHUNK),
        in_specs=[pl.BlockSpec((nb, CHUNK, SSD_SLAB), lambda b, c: (b, c, 0)),
                  pl.BlockSpec((nb, 1, SSD_HEADS, CHUNK), lambda b, c: (b, c, 0, 0)),
                  const((SSD_CONV, SSD_XBC)), const((1, SSD_XBC)), const((1, LANES)), const((1, LANES)),
                  const((SSD_HEADS, 1)), const((SSD_HEADS, 1)), const((1, GROUP_W)), const((1, GROUP_W)),
                  const((CHUNK, CHUNK)), const((CHUNK, CHUNK)), const((LANES, GROUP_W))],
        out_specs=pl.BlockSpec((nb, CHUNK, GROUP_W), lambda b, c: (b, c, 0)),
        out_shape=jax.ShapeDtypeStruct((B, S, GROUP_W), F32),
        scratch_shapes=[pltpu.VMEM((nb, CARRY_ROWS + CHUNK, SSD_XBC), F32),
                        pltpu.VMEM((nb, SSD_HEADS // 2, SSD_STATE, LANES), F32)],
        compiler_params=pltpu.CompilerParams(dimension_semantics=("arbitrary", "arbitrary")),
        name="ssd")(slab, rows, cw, cb, dtb, a, dtb_col, a_col, dskip, ng, tri, triu, ex)


MLSTM_SLAB = 2816


def _mlstm_kernel(slab_ref, rows_ref, cw_ref, cb_ref, igb_ref, fgb_ref, gb_col_ref, ng_ref,
                  tri_ref, triu_ref, o_ref, buf_ref, c_ref, nm_ref, *, nb):
    @pl.when(pl.program_id(1) == 0)
    def _():
        c_ref[...] = jnp.zeros_like(c_ref)
        nm_ref[...] = jnp.zeros_like(nm_ref)
        for b in range(nb):
            buf_ref[b, 0:CARRY_ROWS, :] = jnp.zeros((CARRY_ROWS, 2 * GROUP_W), F32)

    causal = _causal(CHUNK)

    def head(b, h, qk, logi_col, cum_col, logi_row, cum_row):
        ls = slice(h * MLSTM_HEAD, (h + 1) * MLSTM_HEAD)
        q = qk[:, ls]
        k = qk[:, GROUP_W + h * MLSTM_HEAD:GROUP_W + (h + 1) * MLSTM_HEAD] * (MLSTM_HEAD ** -0.5)
        v = slab_ref[b, :, 1024 + h * MLSTM_HEAD:1024 + (h + 1) * MLSTM_HEAD]
        ci = cum_col[:, h:h + 1]
        li = logi_col[:, h:h + 1]
        cr = cum_row[h:h + 1, :]
        lir = logi_row[h:h + 1, :]
        last = cr[:, CHUNK - 1:CHUNK]
        c_prev = c_ref[b, h]
        n_prev = nm_ref[b, h, 0:1, :]
        m_prev = nm_ref[b, h, 1:2, 0:1]

        g = last - ci + li
        g_max = jnp.max(g, axis=0, keepdims=True)
        log_d = jnp.where(causal, ci - cr + lir, -jnp.inf)
        row_max = jnp.max(log_d, axis=-1, keepdims=True)
        qk_h = _mm_nt(q, k)
        qc = _mm(q, c_prev)
        qn = _rowsum(q * n_prev)
        yield
        kw = k * jnp.exp(g - g_max)
        c_loc = _mm_tn(kw, v)
        n_loc = jnp.sum(kw, axis=0, keepdims=True)
        m_new = jnp.maximum(last + m_prev, g_max)
        a_old = jnp.exp(last + m_prev - m_new)
        a_new = jnp.exp(g_max - m_new)
        c_ref[b, h] = a_old * c_prev + a_new * c_loc
        nm_ref[b, h, 0:1, :] = a_old * n_prev + a_new * n_loc
        nm_ref[b, h, 1:2, :] = jnp.broadcast_to(m_new, (1, MLSTM_HEAD))
        m_inter = ci + m_prev
        m_l = jnp.maximum(m_inter, row_max)
        wqk = qk_h * jnp.exp(log_d - m_l)
        w_inter = jnp.exp(m_inter - m_l)
        num = _mm(wqk, v) + w_inter * qc
        den = _rowsum(wqk) + w_inter * qn
        yield
        den = jnp.maximum(jnp.abs(den), jnp.exp(-m_l))
        hh = num / den * _sigmoid(slab_ref[b, :, 1536 + h * MLSTM_HEAD:1536 + (h + 1) * MLSTM_HEAD])
        mu = _rowsum(hh) * (1.0 / MLSTM_HEAD)
        yield
        yc = hh - mu
        var = _rowsum(yc * yc) * (1.0 / MLSTM_HEAD)
        yield
        hh = yc * lax.rsqrt(var + NORM_EPS) * ng_ref[:, ls]
        o_ref[b, :, ls] = hh * _silu(slab_ref[b, :, 2048 + h * MLSTM_HEAD:2048 + (h + 1) * MLSTM_HEAD])

    gens = []
    for b in range(nb):
        taps = _shifted_rows(buf_ref.at[b], slab_ref[b, :, 0:2 * GROUP_W], (3, 2, 1, 0))
        qk = cb_ref[...]
        for j in range(MLSTM_CONV):
            qk = qk + taps[j] * cw_ref[j:j + 1, :]
        qk = _silu(qk)
        logi_col = slab_ref[b, :, 2560:2688] + igb_ref[...]
        logf_col = _log_sigmoid(slab_ref[b, :, 2688:2816] + fgb_ref[...])
        cum_col = _mm_sel_lhs(tri_ref[...], logf_col)
        pre_row = rows_ref[b, 0] + gb_col_ref[...]
        logi_row = pre_row[0:MLSTM_HEADS, :]
        logf_row = _log_sigmoid(pre_row[MLSTM_HEADS:2 * MLSTM_HEADS, :])
        cum_row = _mm_sel_rhs(logf_row, triu_ref[...])
        gens += [head(b, h, qk, logi_col, cum_col, logi_row, cum_row) for h in range(MLSTM_HEADS)]
    _lockstep(gens)


def _mlstm(slab, rows, cw, cb, igb, fgb, gb_col, ng, tri, triu):
    B, S, _ = slab.shape
    nb = BATCH_BLOCK
    const = lambda shape: pl.BlockSpec(shape, lambda b, c: (0,) * len(shape))
    return pl.pallas_call(
        functools.partial(_mlstm_kernel, nb=nb), grid=(B // nb, S // CHUNK),
        in_specs=[pl.BlockSpec((nb, CHUNK, MLSTM_SLAB), lambda b, c: (b, c, 0)),
                  pl.BlockSpec((nb, 1, 2 * MLSTM_HEADS, CHUNK), lambda b, c: (b, c, 0, 0)),
                  const((MLSTM_CONV, 2 * GROUP_W)), const((1, 2 * GROUP_W)), const((1, LANES)),
                  const((1, LANES)), const((2 * MLSTM_HEADS, 1)), const((1, GROUP_W)),
                  const((CHUNK, CHUNK)), const((CHUNK, CHUNK))],
        out_specs=pl.BlockSpec((nb, CHUNK, GROUP_W), lambda b, c: (b, c, 0)),
        out_shape=jax.ShapeDtypeStruct((B, S, GROUP_W), F32),
        scratch_shapes=[pltpu.VMEM((nb, CARRY_ROWS + CHUNK, 2 * GROUP_W), F32),
                        pltpu.VMEM((nb, MLSTM_HEADS, MLSTM_HEAD, MLSTM_HEAD), F32),
                        pltpu.VMEM((nb, MLSTM_HEADS, CARRY_ROWS, MLSTM_HEAD), F32)],
        compiler_params=pltpu.CompilerParams(dimension_semantics=("arbitrary", "arbitrary")),
        name="mlstm")(slab, rows, cw, cb, igb, fgb, gb_col, ng, tri, triu)


def _rwkv_slab_width(layer):
    return 2176 if layer == 0 else 2304


def _rwkv_shift_cols(layer):
    return 1664 if layer == 0 else 1792


def _rwkv_kernel(*refs, layer, nb):
    if layer == 0:
        (slab_ref, mu_ref, w0_ref, a0_ref, w2a2_ref, kk_ref, ka_ref, rk_ref, lng_ref, lnb_ref,
         tri_ref, bd_ref, strict_ref, incl_ref, lvl_ref, o_ref, vf_out_ref, buf_ref, st_ref) = refs
    else:
        (slab_ref, vf_ref, mu_ref, w0_ref, a0_ref, w2a2_ref, v0_ref, v2_ref, kk_ref, ka_ref, rk_ref,
         lng_ref, lnb_ref, tri_ref, bd_ref, strict_ref, incl_ref, lvl_ref, o_ref, buf_ref, st_ref) = refs
    ws = _rwkv_shift_cols(layer)
    zoff = _rwkv_slab_width(layer) - GROUP_W
    npair = GROUP_W // LANES

    @pl.when(pl.program_id(1) == 0)
    def _():
        st_ref[...] = jnp.zeros_like(st_ref)
        for b in range(nb):
            buf_ref[b, 0:CARRY_ROWS, :] = jnp.zeros((CARRY_ROWS, ws), F32)

    lane = lax.broadcasted_iota(jnp.int32, (1, LANES), 1)
    masks = _lane_half_masks()
    bd = bd_ref[...]
    strict = strict_ref[...] > 0.5
    incl = incl_ref[...] > 0.5
    rows2 = lambda t: jnp.concatenate([t * masks[0], t * masks[1]], axis=0)
    eye = (lax.broadcasted_iota(jnp.int32, (LANES, LANES), 0)
           == lax.broadcasted_iota(jnp.int32, (LANES, LANES), 1)).astype(F32)

    inst = []
    for b in range(nb):
        f = slab_ref[b, :, 0:ws]
        (prev,) = _shifted_rows(buf_ref.at[b], f, (1,))
        f = f + mu_ref[...] * (prev - f)
        r = f[:, 0:512]
        k = f[:, 512:1024]
        v = f[:, 1024:1536]
        lora = f[:, 1536:1664]
        lora = jnp.where(lane < RWKV_W_RANK, jnp.tanh(lora), lora)
        wa = _mm(lora, w2a2_ref[...])
        w_log = -_softplus(-(w0_ref[...] + wa[:, 0:512])) - RWKV_DECAY_OFFSET
        lw = -jnp.exp(w_log)
        a = _sigmoid(a0_ref[...] + wa[:, 512:1024])
        if layer == 0:
            vf_out_ref[b] = v
        else:
            mix = _sigmoid(v0_ref[...] + _mm(f[:, 1664:1792], v2_ref[...]))
            v = v + (vf_ref[b] - v) * mix
        kk = k * kk_ref[...]
        k = k * (1.0 + (a - 1.0) * ka_ref[...])
        cum = _mm_sel_lhs(tri_ref[...], lw)
        last = cum[CHUNK - 1:CHUNK, :]
        e_pos = jnp.exp(cum)
        e_neg = jnp.exp(-cum)
        e_end = jnp.exp(last - cum)
        e_prev = jnp.exp(cum - lw)
        gam = jnp.exp(last)
        for p in range(npair):
            ls = slice(p * LANES, (p + 1) * LANES)
            kk_p = kk[:, ls]
            ss = _mm_sel_rhs(kk_p * kk_p, bd)
            kk_p = kk_p / jnp.maximum(jnp.sqrt(ss), 1e-12)
            k_p, r_p, v_p = k[:, ls], r[:, ls], v[:, ls]
            b_p = kk_p * a[:, ls]
            inst.append(dict(
                b=b, p=p, ls=ls, r=r_p, k=k_p, v=v_p, gam=gam[:, ls],
                la=rows2(-kk_p * e_prev[:, ls]), lr=rows2(r_p * e_pos[:, ls]),
                rb=rows2(b_p * e_neg[:, ls]), rk=rows2(k_p * e_neg[:, ls]),
                bh=rows2(b_p * e_end[:, ls]), kh=rows2(k_p * e_end[:, ls]), vs=rows2(v_p)))

    for d in inst:
        aa = _mm_nt(jnp.concatenate([d['la'], d['lr']], axis=0),
                    jnp.concatenate([d['rb'], d['rk']], axis=0))
        d['nab'] = jnp.where(strict, aa[0:LANES, 0:LANES], 0.0)
        d['aak'] = jnp.where(strict, aa[0:LANES, LANES:2 * LANES], 0.0)
        d['arb'] = jnp.where(incl, aa[LANES:2 * LANES, 0:LANES], 0.0)
        d['ark'] = jnp.where(incl, aa[LANES:2 * LANES, LANES:2 * LANES], 0.0)
        d['t'] = eye + d['nab'] * lvl_ref[0]
        d['x'] = _mm(d['aak'], d['vs'])
    for lv in range(1, 6):
        for d in inst:
            d['nt'] = _mm(d['nab'] * lvl_ref[lv], d['t'])
        for d in inst:
            d['t'] = d['t'] + _mm(d['t'], d['nt'])
    for d in inst:
        d['wu'] = _mm(d['t'], jnp.concatenate([d['la'], d['x']], axis=1))
    for d in inst:
        qy = _mm(d['arb'], d['wu'])
        d['qt'] = d['lr'] + qy[:, 0:LANES]
        d['y0'] = _mm(d['ark'], d['vs']) + qy[:, LANES:2 * LANES]
    for d in inst:
        st = st_ref[d['b'], d['p']]
        uy = _mm_nt(jnp.concatenate([d['wu'][:, 0:LANES], d['qt']], axis=0), st)
        ust = uy[0:LANES] + d['wu'][:, LANES:2 * LANES]
        d['yst'] = uy[LANES:2 * LANES] + d['y0']
        st_ref[d['b'], d['p']] = st * d['gam'] + _mm_tn(jnp.concatenate([ust, d['vs']], axis=0),
                                                        jnp.concatenate([d['bh'], d['kh']], axis=0))
    inv = 1.0 / RWKV_HEAD
    for d in inst:
        ls = d['ls']
        y = d['yst'][0:CHUNK] + d['yst'][CHUNK:2 * CHUNK]
        mu_y = _mm_sel_rhs(y, bd) * inv
        yc = y - mu_y
        var = _mm_sel_rhs(yc * yc, bd) * inv
        yn = yc * lax.rsqrt(var + RWKV_LN_EPS) * lng_ref[:, ls] + lnb_ref[:, ls]
        bonus = _mm_sel_rhs(d['r'] * d['k'] * rk_ref[:, ls], bd) * d['v']
        z_p = slab_ref[d['b'], :, zoff + d['p'] * LANES:zoff + (d['p'] + 1) * LANES]
        o_ref[d['b'], :, ls] = (yn + bonus) * _silu(z_p)


def _rwkv(layer, slab, vf, mu, w0, a0, w2a2, v0, v2, kkw, ka, rk, lng, lnb, tri, bd, strict, incl, lvl):
    B, S, W = slab.shape
    nb = BATCH_BLOCK
    ws = _rwkv_shift_cols(layer)
    const = lambda shape: pl.BlockSpec(shape, lambda b, c: (0,) * len(shape))
    tok = lambda w: pl.BlockSpec((nb, CHUNK, w), lambda b, c: (b, c, 0))
    vecw = const((1, GROUP_W))
    in_specs = [tok(W)]
    args = [slab]
    if layer > 0:
        in_specs.append(tok(GROUP_W))
        args.append(vf)
    in_specs += [const((1, ws)), vecw, vecw, const((LANES, 2 * GROUP_W))]
    args += [mu, w0, a0, w2a2]
    if layer > 0:
        in_specs += [vecw, const((LANES, GROUP_W))]
        args += [v0, v2]
    in_specs += [vecw, vecw, vecw, vecw, vecw, const((CHUNK, CHUNK)), const((LANES, LANES)),
                 const((LANES, LANES)), const((LANES, LANES)), const((6, LANES, LANES))]
    args += [kkw, ka, rk, lng, lnb, tri, bd, strict, incl, lvl]
    out_shape = [jax.ShapeDtypeStruct((B, S, GROUP_W), F32)]
    out_specs = [tok(GROUP_W)]
    if layer == 0:
        out_shape.append(jax.ShapeDtypeStruct((B, S, GROUP_W), F32))
        out_specs.append(tok(GROUP_W))
    res = pl.pallas_call(
        functools.partial(_rwkv_kernel, layer=layer, nb=nb), grid=(B // nb, S // CHUNK),
        in_specs=in_specs, out_specs=out_specs, out_shape=out_shape,
        scratch_shapes=[pltpu.VMEM((nb, CARRY_ROWS + CHUNK, ws), F32),
                        pltpu.VMEM((nb, GROUP_W // LANES, LANES, LANES), F32)],
        compiler_params=pltpu.CompilerParams(dimension_semantics=("arbitrary", "arbitrary"),
                                             vmem_limit_bytes=VMEM_LIMIT_PROJ),
        name="rwkv")(*args)
    return (res[0], res[1]) if layer == 0 else (res[0], vf)


def _np_consts():
    i = np.arange(CHUNK)
    tri = (i[None, :] <= i[:, None]).astype(np.float32)
    t = np.arange(LANES)
    same = (t[:, None] // CHUNK) == (t[None, :] // CHUNK)
    strict = (same & (t[None, :] < t[:, None])).astype(np.float32)
    incl = (same & (t[None, :] <= t[:, None])).astype(np.float32)
    bd = same.astype(np.float32)
    lvl = np.stack([(((t[:, None] >> l) == (t[None, :] >> l))
                     & ((t[:, None] >> (l - 1)) != (t[None, :] >> (l - 1)))).astype(np.float32)
                    for l in range(1, 7)])
    ex = np.zeros((LANES, GROUP_W), np.float32)
    for h in range(SSD_HEADS):
        ex[h, h * SSD_HEADDIM:(h + 1) * SSD_HEADDIM] = 1.0
    return dict(tri=tri, triu=tri.T.copy(), strict=strict, incl=incl, bd=bd, lvl=lvl, ex=ex)


def _pad_cols(w, width):
    return jnp.pad(w, ((0, 0), (0, width - w.shape[1])))


def _pad_vec(v, width):
    v = v.reshape(1, -1)
    return jnp.pad(v, ((0, 0), (0, width - v.shape[1])))


def _layer(x2, v_first, layer, p, B, S, consts, g_final):
    w_in = _bf(p['w_in'])
    sw = 3 * GROUP_W + RWKV_W_RANK + RWKV_A_RANK + (RWKV_V_RANK if layer > 0 else 0)
    widths = [256, 256, 512, 16, 512, sw, 512, SSD_XBC, SSD_HEADS, 512, 1024, 512, 4, 4, 512, 512]
    offs = np.concatenate([[0], np.cumsum(widths)]).tolist()
    names = ['gla_q', 'gla_k', 'gla_v', 'gla_gk', 'gla_z', 'rwkv_shift', 'rwkv_z', 'ssd_xbc', 'ssd_dt',
             'ssd_z', 'mlstm_qk', 'mlstm_v', 'mlstm_i', 'mlstm_f', 'mlstm_o', 'mlstm_z']
    col = {n: w_in[:, offs[i]:offs[i + 1]] for i, n in enumerate(names)}
    g = p['norm_g'].reshape(1, D_MODEL)
    tri, triu = consts['tri'], consts['triu']
    rows_of = lambda t: t.reshape(t.shape[0], B, S // CHUNK, CHUNK).transpose(1, 2, 0, 3)

    w_gla = jnp.concatenate([col['gla_q'], col['gla_k'], col['gla_v'], col['gla_z'],
                             _pad_cols(col['gla_gk'], LANES)], axis=1)
    slab = _in_proj(x2, g, w_gla).reshape(B, S, GLA_SLAB)
    w2p = _bf(jnp.pad(p['gla_gk_w2'], ((0, LANES - GLA_RANK), (0, 0))))
    y_gla = _gla(slab, w2p, p['gla_gk_b'].reshape(1, -1), p['gla_norm_g'].reshape(1, -1), _bf(tri))

    sh = col['rwkv_shift']
    mu = p['rwkv_mu']
    if layer == 0:
        w_rwkv = jnp.concatenate([sh, col['rwkv_z']], axis=1)
        mu_p = mu.reshape(1, -1)
    else:
        w_rwkv = jnp.concatenate([_pad_cols(sh, 1792), col['rwkv_z']], axis=1)
        mu_p = _pad_vec(mu, 1792)
    slab = _in_proj(x2, g, w_rwkv).reshape(B, S, _rwkv_slab_width(layer))
    w2a2 = jnp.zeros((LANES, 2 * GROUP_W), F32)
    w2a2 = w2a2.at[0:RWKV_W_RANK, 0:GROUP_W].set(p['rwkv_w2'])
    w2a2 = w2a2.at[RWKV_W_RANK:, GROUP_W:].set(p['rwkv_a2'])
    vec = lambda t: t.reshape(1, GROUP_W)
    v0 = v2 = None
    if layer > 0:
        v0 = vec(p['rwkv_v0'])
        v2 = _bf(jnp.pad(p['rwkv_v2'], ((0, LANES - RWKV_V_RANK), (0, 0))))
    y_rwkv, v_first = _rwkv(layer, slab, v_first, mu_p, vec(p['rwkv_w0']), vec(p['rwkv_a0']), _bf(w2a2),
                            v0, v2, vec(p['rwkv_k_k']), vec(p['rwkv_k_a']), vec(p['rwkv_r_k']),
                            vec(p['rwkv_ln_g']), vec(p['rwkv_ln_b']), _bf(tri), _bf(consts['bd']),
                            consts['strict'], consts['incl'], consts['lvl'])

    w_ssd = jnp.concatenate([col['ssd_xbc'], col['ssd_z'], _pad_cols(col['ssd_dt'], LANES)], axis=1)
    slab, rows = _in_proj(x2, g, w_ssd, col['ssd_dt'].T)
    a_neg = -jnp.exp(p['ssd_a_log'])
    y_ssd = _ssd(slab.reshape(B, S, SSD_SLAB), rows_of(rows), p['ssd_conv_w'], p['ssd_conv_b'].reshape(1, -1),
                 _pad_vec(p['ssd_dt_bias'], LANES), _pad_vec(a_neg, LANES),
                 p['ssd_dt_bias'].reshape(-1, 1), a_neg.reshape(-1, 1),
                 jnp.repeat(p['ssd_d'], SSD_HEADDIM).reshape(1, -1), p['ssd_norm_g'].reshape(1, -1),
                 _bf(tri), _bf(triu), _bf(consts['ex']))

    w_ml = jnp.concatenate([col['mlstm_qk'], col['mlstm_v'], col['mlstm_o'], col['mlstm_z'],
                            _pad_cols(col['mlstm_i'], LANES), _pad_cols(col['mlstm_f'], LANES)], axis=1)
    wt_ml = jnp.concatenate([col['mlstm_i'], col['mlstm_f']], axis=1).T
    slab, rows = _in_proj(x2, g, w_ml, wt_ml)
    gb_col = jnp.concatenate([p['mlstm_ig_b'], p['mlstm_fg_b']]).reshape(-1, 1)
    y_ml = _mlstm(slab.reshape(B, S, MLSTM_SLAB), rows_of(rows), p['mlstm_conv_w'],
                  p['mlstm_conv_b'].reshape(1, -1), _pad_vec(p['mlstm_ig_b'], LANES),
                  _pad_vec(p['mlstm_fg_b'], LANES), gb_col, p['mlstm_norm_g'].reshape(1, -1),
                  _bf(tri), _bf(triu))

    T = B * S
    ys = [y.reshape(T, GROUP_W) for y in (y_gla, y_rwkv, y_ssd, y_ml)]
    return _out_proj(x2, ys, _bf(p['w_out']), g_final), v_first


_PARAM_NAMES_0 = ['norm_g', 'w_in', 'w_out', 'gla_gk_w2', 'gla_gk_b', 'gla_norm_g', 'rwkv_mu', 'rwkv_w0',
                  'rwkv_w2', 'rwkv_a0', 'rwkv_a2', 'rwkv_k_k', 'rwkv_k_a', 'rwkv_r_k', 'rwkv_ln_g',
                  'rwkv_ln_b', 'ssd_conv_w', 'ssd_conv_b', 'ssd_dt_bias', 'ssd_a_log', 'ssd_d',
                  'ssd_norm_g', 'mlstm_conv_w', 'mlstm_conv_b', 'mlstm_ig_b', 'mlstm_fg_b', 'mlstm_norm_g']
_PARAM_NAMES_1 = (_PARAM_NAMES_0[:11] + ['rwkv_v0', 'rwkv_v2'] + _PARAM_NAMES_0[11:])


def kernel(x,
           norm_g_0, w_in_0, w_out_0, gla_gk_w2_0, gla_gk_b_0, gla_norm_g_0,
           rwkv_mu_0, rwkv_w0_0, rwkv_w2_0, rwkv_a0_0, rwkv_a2_0,
           rwkv_k_k_0, rwkv_k_a_0, rwkv_r_k_0, rwkv_ln_g_0, rwkv_ln_b_0,
           ssd_conv_w_0, ssd_conv_b_0, ssd_dt_bias_0, ssd_a_log_0, ssd_d_0, ssd_norm_g_0,
           mlstm_conv_w_0, mlstm_conv_b_0, mlstm_ig_b_0, mlstm_fg_b_0, mlstm_norm_g_0,
           norm_g_1, w_in_1, w_out_1, gla_gk_w2_1, gla_gk_b_1, gla_norm_g_1,
           rwkv_mu_1, rwkv_w0_1, rwkv_w2_1, rwkv_a0_1, rwkv_a2_1, rwkv_v0_1, rwkv_v2_1,
           rwkv_k_k_1, rwkv_k_a_1, rwkv_r_k_1, rwkv_ln_g_1, rwkv_ln_b_1,
           ssd_conv_w_1, ssd_conv_b_1, ssd_dt_bias_1, ssd_a_log_1, ssd_d_1, ssd_norm_g_1,
           mlstm_conv_w_1, mlstm_conv_b_1, mlstm_ig_b_1, mlstm_fg_b_1, mlstm_norm_g_1,
           final_norm_g):
    params = (norm_g_0, w_in_0, w_out_0, gla_gk_w2_0, gla_gk_b_0, gla_norm_g_0,
              rwkv_mu_0, rwkv_w0_0, rwkv_w2_0, rwkv_a0_0, rwkv_a2_0,
              rwkv_k_k_0, rwkv_k_a_0, rwkv_r_k_0, rwkv_ln_g_0, rwkv_ln_b_0,
              ssd_conv_w_0, ssd_conv_b_0, ssd_dt_bias_0, ssd_a_log_0, ssd_d_0, ssd_norm_g_0,
              mlstm_conv_w_0, mlstm_conv_b_0, mlstm_ig_b_0, mlstm_fg_b_0, mlstm_norm_g_0,
              norm_g_1, w_in_1, w_out_1, gla_gk_w2_1, gla_gk_b_1, gla_norm_g_1,
              rwkv_mu_1, rwkv_w0_1, rwkv_w2_1, rwkv_a0_1, rwkv_a2_1, rwkv_v0_1, rwkv_v2_1,
              rwkv_k_k_1, rwkv_k_a_1, rwkv_r_k_1, rwkv_ln_g_1, rwkv_ln_b_1,
              ssd_conv_w_1, ssd_conv_b_1, ssd_dt_bias_1, ssd_a_log_1, ssd_d_1, ssd_norm_g_1,
              mlstm_conv_w_1, mlstm_conv_b_1, mlstm_ig_b_1, mlstm_fg_b_1, mlstm_norm_g_1,
              final_norm_g)
    n0 = len(_PARAM_NAMES_0)
    n1 = len(_PARAM_NAMES_1)
    p0 = dict(zip(_PARAM_NAMES_0, params[:n0]))
    p1 = dict(zip(_PARAM_NAMES_1, params[n0:n0 + n1]))
    final_norm_g = params[n0 + n1]
    B, S, _ = x.shape
    consts = {k: jnp.asarray(v) for k, v in _np_consts().items()}
    x2 = x.reshape(B * S, D_MODEL)
    x2, v_first = _layer(x2, None, 0, p0, B, S, consts, None)
    x2, _ = _layer(x2, v_first, 1, p1, B, S, consts, final_norm_g.reshape(1, D_MODEL))
    return x2.reshape(B, S, D_MODEL)
```

```python
import functools

import numpy as np
import jax
import jax.numpy as jnp
from jax import lax
from jax.experimental import pallas as pl
from jax.experimental.pallas import tpu as pltpu

F32 = jnp.float32
BF16 = jnp.bfloat16

D_MODEL = 2048
CHUNK = 64
GROUP_W = 512
NORM_EPS = 1e-6
LANES = 128
CARRY_ROWS = 8

GLA_HEADS, GLA_DK, GLA_DV, GLA_RANK = 4, 64, 128, 16
GLA_GATE_NORMALIZER = 16.0
RWKV_HEAD, RWKV_W_RANK, RWKV_A_RANK, RWKV_V_RANK = 64, 64, 64, 32
RWKV_LN_EPS = 64e-5
RWKV_DECAY_OFFSET = 0.5
SSD_HEADS, SSD_HEADDIM, SSD_STATE, SSD_CONV = 8, 64, 128, 4
SSD_XBC = 1024
MLSTM_HEADS, MLSTM_HEAD, MLSTM_CONV = 4, 128, 4

ROW_TILE_IN = 256
ROW_TILE_OUT = 512
VMEM_LIMIT_PROJ = 48 * 2**20


def _bf(x):
    return x.astype(BF16)


def _mm(a, b):
    return jnp.dot(_bf(a), _bf(b), preferred_element_type=F32)


def _mm_nt(a, b):
    return lax.dot_general(_bf(a), _bf(b), (((1,), (1,)), ((), ())), preferred_element_type=F32)


def _mm_tn(a, b):
    return lax.dot_general(_bf(a), _bf(b), (((0,), (0,)), ((), ())), preferred_element_type=F32)


def _split3(x):
    hi = _bf(x)
    r1 = x - hi.astype(F32)
    mid = _bf(r1)
    lo = _bf(r1 - mid.astype(F32))
    return hi, mid, lo


def _mm_sel_rhs(x, sel):
    hi, mid, lo = _split3(x)
    d = lambda a: jnp.dot(a, sel, preferred_element_type=F32)
    return d(hi) + d(mid) + d(lo)


def _mm_sel_lhs(sel, x):
    hi, mid, lo = _split3(x)
    d = lambda a: jnp.dot(sel, a, preferred_element_type=F32)
    return d(hi) + d(mid) + d(lo)


def _rowsum(x):
    ones = jnp.ones((x.shape[-1], LANES), BF16)
    hi = _bf(x)
    lo = _bf(x - hi.astype(F32))
    return (jnp.dot(hi, ones, preferred_element_type=F32)
            + jnp.dot(lo, ones, preferred_element_type=F32))


def _sigmoid(x):
    return 1.0 / (1.0 + jnp.exp(-x))


def _silu(x):
    return x * _sigmoid(x)


def _softplus(x):
    return jnp.maximum(x, 0.0) + jnp.log1p(jnp.exp(-jnp.abs(x)))


def _log_sigmoid(x):
    return -_softplus(-x)


def _lane_half_masks():
    lane = lax.broadcasted_iota(jnp.int32, (1, LANES), 1)
    lo = (lane < LANES // 2).astype(F32)
    return lo, 1.0 - lo


def _causal(n):
    r = lax.broadcasted_iota(jnp.int32, (n, n), 0)
    c = lax.broadcasted_iota(jnp.int32, (n, n), 1)
    return c <= r


def _shifted_rows(buf_ref, cur, offsets):
    buf_ref[CARRY_ROWS:CARRY_ROWS + CHUNK, :] = cur
    outs = [buf_ref[CARRY_ROWS - off:CARRY_ROWS - off + CHUNK, :] for off in offsets]
    tail = buf_ref[CHUNK:CHUNK + CARRY_ROWS, :]
    buf_ref[0:CARRY_ROWS, :] = tail
    return outs


def _in_proj_kernel(*refs, has_t):
    if has_t:
        x_ref, g_ref, w_ref, wt_ref, o_ref, ot_ref = refs
    else:
        x_ref, g_ref, w_ref, o_ref = refs
    x = x_ref[...]
    h = x * lax.rsqrt(jnp.mean(x * x, axis=-1, keepdims=True) + NORM_EPS) * g_ref[...]
    hb = _bf(h)
    o_ref[...] = lax.dot_general(hb, w_ref[...], (((1,), (1,)), ((), ())), preferred_element_type=F32)
    if has_t:
        ot_ref[...] = lax.dot_general(wt_ref[...], hb, (((1,), (1,)), ((), ())),
                                      preferred_element_type=F32)


def _in_proj(x2, g, w, wt=None):
    T = x2.shape[0]
    n = w.shape[0]
    tm = ROW_TILE_IN
    in_specs = [pl.BlockSpec((tm, D_MODEL), lambda i: (i, 0)),
                pl.BlockSpec((1, D_MODEL), lambda i: (0, 0)),
                pl.BlockSpec((n, D_MODEL), lambda i: (0, 0))]
    out_shape = [jax.ShapeDtypeStruct((T, n), F32)]
    out_specs = [pl.BlockSpec((tm, n), lambda i: (i, 0))]
    args = [x2, g, w]
    if wt is not None:
        r = wt.shape[0]
        in_specs.append(pl.BlockSpec((r, D_MODEL), lambda i: (0, 0)))
        out_shape.append(jax.ShapeDtypeStruct((r, T), F32))
        out_specs.append(pl.BlockSpec((r, tm), lambda i: (0, i)))
        args.append(wt)
    res = pl.pallas_call(
        functools.partial(_in_proj_kernel, has_t=wt is not None),
        grid=(T // tm,), in_specs=in_specs, out_specs=out_specs, out_shape=out_shape,
        compiler_params=pltpu.CompilerParams(dimension_semantics=("arbitrary",),
                                             vmem_limit_bytes=VMEM_LIMIT_PROJ),
        name="in_proj")(*args)
    return res if wt is not None else res[0]


def _out_proj_kernel(*refs, final):
    if final:
        x_ref, y0, y1, y2, y3, w_ref, g_ref, o_ref = refs
    else:
        x_ref, y0, y1, y2, y3, w_ref, o_ref = refs
    acc = x_ref[...]
    for gi, y in enumerate((y0, y1, y2, y3)):
        acc = acc + jnp.dot(_bf(y[...]), w_ref[gi * GROUP_W:(gi + 1) * GROUP_W, :],
                            preferred_element_type=F32)
    if final:
        acc = acc * lax.rsqrt(jnp.mean(acc * acc, axis=-1, keepdims=True) + NORM_EPS) * g_ref[...]
    o_ref[...] = acc


def _out_proj(x2, ys, w, g_final=None):
    T = x2.shape[0]
    tm = ROW_TILE_OUT
    final = g_final is not None
    in_specs = [pl.BlockSpec((tm, D_MODEL), lambda i: (i, 0))]
    in_specs += [pl.BlockSpec((tm, GROUP_W), lambda i: (i, 0)) for _ in range(4)]
    in_specs += [pl.BlockSpec((D_MODEL, D_MODEL), lambda i: (0, 0))]
    args = [x2, *ys, w]
    if final:
        in_specs.append(pl.BlockSpec((1, D_MODEL), lambda i: (0, 0)))
        args.append(g_final)
    return pl.pallas_call(
        functools.partial(_out_proj_kernel, final=final),
        grid=(T // tm,), in_specs=in_specs,
        out_specs=pl.BlockSpec((tm, D_MODEL), lambda i: (i, 0)),
        out_shape=jax.ShapeDtypeStruct((T, D_MODEL), F32),
        compiler_params=pltpu.CompilerParams(dimension_semantics=("arbitrary",),
                                             vmem_limit_bytes=VMEM_LIMIT_PROJ),
        name="out_proj")(*args)


GLA_SLAB = 1664


BATCH_BLOCK = 4


def _lockstep(gens):
    gens = list(gens)
    while gens:
        alive = []
        for g in gens:
            try:
                next(g)
                alive.append(g)
            except StopIteration:
                pass
        gens = alive


def _gla_kernel(slab_ref, w2_ref, gkb_ref, ng_ref, tri_ref, o_ref, st_ref, *, nb):
    @pl.when(pl.program_id(1) == 0)
    def _():
        st_ref[...] = jnp.zeros_like(st_ref)

    causal = _causal(CHUNK)
    masks = _lane_half_masks()

    def head(b, h, qg, kg, kd, dec):
        p, j = divmod(h, 2)
        ls = slice(p * LANES, (p + 1) * LANES)
        qm = qg[:, ls] * masks[j]
        att = jnp.where(causal, _mm_nt(qm, kg[:, ls]), 0.0)
        yield
        v_h = slab_ref[b, :, 512 + h * GLA_DV:512 + (h + 1) * GLA_DV]
        st = st_ref[b, h]
        o = _mm(att, v_h) + _mm_nt(qm, st)
        st_ref[b, h] = st * dec[:, ls] + _mm_tn(v_h, kd[:, ls] * masks[j])
        yield
        ms = jnp.mean(o * o, axis=-1, keepdims=True)
        yield
        o = o * lax.rsqrt(ms + NORM_EPS)
        o = o * ng_ref[:, h * GLA_DV:(h + 1) * GLA_DV]
        z_h = slab_ref[b, :, 1024 + h * GLA_DV:1024 + (h + 1) * GLA_DV]
        o_ref[b, :, h * GLA_DV:(h + 1) * GLA_DV] = o * _silu(z_h)

    gens = []
    for b in range(nb):
        q = slab_ref[b, :, 0:256] * (GLA_DK ** -0.5)
        k = slab_ref[b, :, 256:512]
        gk = _mm(slab_ref[b, :, 1536:1664], w2_ref[...]) + gkb_ref[...]
        log_a = _log_sigmoid(gk) / GLA_GATE_NORMALIZER
        cum = _mm_sel_lhs(tri_ref[...], log_a)
        last = cum[CHUNK - 1:CHUNK, :]
        qg = q * jnp.exp(cum)
        kg = k * jnp.exp(-cum)
        kd = k * jnp.exp(last - cum)
        dec = jnp.exp(last)
        gens += [head(b, h, qg, kg, kd, dec) for h in range(GLA_HEADS)]
    _lockstep(gens)


def _gla(slab, w2p, gkb, ng, tri):
    B, S, _ = slab.shape
    nb = BATCH_BLOCK
    const = lambda shape: pl.BlockSpec(shape, lambda b, c: (0,) * len(shape))
    return pl.pallas_call(
        functools.partial(_gla_kernel, nb=nb), grid=(B // nb, S // CHUNK),
        in_specs=[pl.BlockSpec((nb, CHUNK, GLA_SLAB), lambda b, c: (b, c, 0)),
                  const((LANES, 256)), const((1, 256)), const((1, GROUP_W)), const((CHUNK, CHUNK))],
        out_specs=pl.BlockSpec((nb, CHUNK, GROUP_W), lambda b, c: (b, c, 0)),
        out_shape=jax.ShapeDtypeStruct((B, S, GROUP_W), F32),
        scratch_shapes=[pltpu.VMEM((nb, GLA_HEADS, GLA_DV, LANES), F32)],
        compiler_params=pltpu.CompilerParams(dimension_semantics=("arbitrary", "arbitrary")),
        name="gla")(slab, w2p, gkb, ng, tri)


SSD_SLAB = 1664


def _ssd_kernel(slab_ref, rows_ref, cw_ref, cb_ref, dtb_ref, a_ref, dtb_col_ref, a_col_ref,
                dskip_ref, ng_ref, tri_ref, triu_ref, ex_ref, o_ref, buf_ref, st_ref, *, nb):
    @pl.when(pl.program_id(1) == 0)
    def _():
        st_ref[...] = jnp.zeros_like(st_ref)
        for b in range(nb):
            buf_ref[b, 0:CARRY_ROWS, :] = jnp.zeros((CARRY_ROWS, SSD_XBC), F32)

    causal = _causal(CHUNK)
    masks = _lane_half_masks()

    def pair(b, p, xbc, cum_col, cum_row, xdt, xw, ecum, dec):
        g = p // 2
        ls = slice(p * LANES, (p + 1) * LANES)
        bm = xbc[:, 512 + g * SSD_STATE:512 + (g + 1) * SSD_STATE]
        cm = xbc[:, 768 + g * SSD_STATE:768 + (g + 1) * SSD_STATE]
        cbm = _mm_nt(cm, bm)
        st = st_ref[b, p]
        y = _mm(cm, st) * ecum[:, ls]
        st_ref[b, p] = st * dec[:, ls] + _mm_tn(bm, xw[:, ls])
        yield
        for j in range(2):
            h = 2 * p + j
            seg = cum_col[:, h:h + 1] - cum_row[h:h + 1, :]
            lmat = jnp.exp(jnp.where(causal, seg, -jnp.inf))
            y = y + _mm(cbm * lmat, xdt[:, ls] * masks[j])
        yield
        y = y + dskip_ref[:, ls] * xbc[:, ls]
        o_ref[b, :, ls] = y * _silu(slab_ref[b, :, 1024 + p * LANES:1024 + (p + 1) * LANES])

    gens = []
    for b in range(nb):
        taps = _shifted_rows(buf_ref.at[b], slab_ref[b, :, 0:SSD_XBC], (3, 2, 1, 0))
        xbc = cb_ref[...]
        for j in range(SSD_CONV):
            xbc = xbc + taps[j] * cw_ref[j:j + 1, :]
        xbc = _silu(xbc)
        dt_col = _softplus(slab_ref[b, :, 1536:1664] + dtb_ref[...])
        cum_col = _mm_sel_lhs(tri_ref[...], dt_col * a_ref[...])
        dt_row = _softplus(rows_ref[b, 0] + dtb_col_ref[...])
        cum_row = _mm_sel_rhs(dt_row * a_col_ref[...], triu_ref[...])
        dt_b = _mm_sel_rhs(dt_col, ex_ref[...])
        cum_b = _mm_sel_rhs(cum_col, ex_ref[...])
        last_b = cum_b[CHUNK - 1:CHUNK, :]
        xdt = xbc[:, 0:512] * dt_b
        xw = xdt * jnp.exp(last_b - cum_b)
        ecum = jnp.exp(cum_b)
        dec = jnp.exp(last_b)
        gens += [pair(b, p, xbc, cum_col, cum_row, xdt, xw, ecum, dec) for p in range(SSD_HEADS // 2)]
    _lockstep(gens)
    for b in range(nb):
        y = o_ref[b]
        inv = lax.rsqrt(_rowsum(y * y) * (1.0 / GROUP_W) + NORM_EPS)
        for p in range(GROUP_W // LANES):
            ls = slice(p * LANES, (p + 1) * LANES)
            o_ref[b, :, ls] = y[:, ls] * inv * ng_ref[:, ls]


def _ssd(slab, rows, cw, cb, dtb, a, dtb_col, a_col, dskip, ng, tri, triu, ex):
    B, S, _ = slab.shape
    nb = BATCH_BLOCK
    const = lambda shape: pl.BlockSpec(shape, lambda b, c: (0,) * len(shape))
    return pl.pallas_call(
        functools.partial(_ssd_kernel, nb=nb), grid=(B // nb, S // CHUNK),
        in_specs=[pl.BlockSpec((nb, CHUNK, SSD_SLAB), lambda b, c: (b, c, 0)),
                  pl.BlockSpec((nb, 1, SSD_HEADS, CHUNK), lambda b, c: (b, c, 0, 0)),
                  const((SSD_CONV, SSD_XBC)), const((1, SSD_XBC)), const((1, LANES)), const((1, LANES)),
                  const((SSD_HEADS, 1)), const((SSD_HEADS, 1)), const((1, GROUP_W)), const((1, GROUP_W)),
                  const((CHUNK, CHUNK)), const((CHUNK, CHUNK)), const((LANES, GROUP_W))],
        out_specs=pl.BlockSpec((nb, CHUNK, GROUP_W), lambda b, c: (b, c, 0)),
        out_shape=jax.ShapeDtypeStruct((B, S, GROUP_W), F32),
        scratch_shapes=[pltpu.VMEM((nb, CARRY_ROWS + CHUNK, SSD_XBC), F32),
                        pltpu.VMEM((nb, SSD_HEADS // 2, SSD_STATE, LANES), F32)],
        compiler_params=pltpu.CompilerParams(dimension_semantics=("arbitrary", "arbitrary")),
        name="ssd")(slab, rows, cw, cb, dtb, a, dtb_col, a_col, dskip, ng, tri, triu, ex)


MLSTM_SLAB = 2816


def _mlstm_kernel(slab_ref, rows_ref, cw_ref, cb_ref, igb_ref, fgb_ref, gb_col_ref, ng_ref,
                  tri_ref, triu_ref, o_ref, buf_ref, c_ref, nm_ref, *, nb):
    @pl.when(pl.program_id(1) == 0)
    def _():
        c_ref[...] = jnp.zeros_like(c_ref)
        nm_ref[...] = jnp.zeros_like(nm_ref)
        for b in range(nb):
            buf_ref[b, 0:CARRY_ROWS, :] = jnp.zeros((CARRY_ROWS, 2 * GROUP_W), F32)

    causal = _causal(CHUNK)

    def head(b, h, qk, logi_col, cum_col, logi_row, cum_row):
        ls = slice(h * MLSTM_HEAD, (h + 1) * MLSTM_HEAD)
        q = qk[:, ls]
        k = qk[:, GROUP_W + h * MLSTM_HEAD:GROUP_W + (h + 1) * MLSTM_HEAD] * (MLSTM_HEAD ** -0.5)
        v = slab_ref[b, :, 1024 + h * MLSTM_HEAD:1024 + (h + 1) * MLSTM_HEAD]
        ci = cum_col[:, h:h + 1]
        li = logi_col[:, h:h + 1]
        cr = cum_row[h:h + 1, :]
        lir = logi_row[h:h + 1, :]
        last = cr[:, CHUNK - 1:CHUNK]
        c_prev = c_ref[b, h]
        n_prev = nm_ref[b, h, 0:1, :]
        m_prev = nm_ref[b, h, 1:2, 0:1]

        g = last - ci + li
        g_max = jnp.max(g, axis=0, keepdims=True)
        log_d = jnp.where(causal, ci - cr + lir, -jnp.inf)
        row_max = jnp.max(log_d, axis=-1, keepdims=True)
        qk_h = _mm_nt(q, k)
        qc = _mm(q, c_prev)
        qn = _rowsum(q * n_prev)
        yield
        kw = k * jnp.exp(g - g_max)
        c_loc = _mm_tn(kw, v)
        n_loc = jnp.sum(kw, axis=0, keepdims=True)
        m_new = jnp.maximum(last + m_prev, g_max)
        a_old = jnp.exp(last + m_prev - m_new)
        a_new = jnp.exp(g_max - m_new)
        c_ref[b, h] = a_old * c_prev + a_new * c_loc
        nm_ref[b, h, 0:1, :] = a_old * n_prev + a_new * n_loc
        nm_ref[b, h, 1:2, :] = jnp.broadcast_to(m_new, (1, MLSTM_HEAD))
        m_inter = ci + m_prev
        m_l = jnp.maximum(m_inter, row_max)
        wqk = qk_h * jnp.exp(log_d - m_l)
        w_inter = jnp.exp(m_inter - m_l)
        num = _mm(wqk, v) + w_inter * qc
        den = _rowsum(wqk) + w_inter * qn
        yield
        den = jnp.maximum(jnp.abs(den), jnp.exp(-m_l))
        hh = num / den * _sigmoid(slab_ref[b, :, 1536 + h * MLSTM_HEAD:1536 + (h + 1) * MLSTM_HEAD])
        mu = _rowsum(hh) * (1.0 / MLSTM_HEAD)
        yield
        yc = hh - mu
        var = _rowsum(yc * yc) * (1.0 / MLSTM_HEAD)
        yield
        hh = yc * lax.rsqrt(var + NORM_EPS) * ng_ref[:, ls]
        o_ref[b, :, ls] = hh * _silu(slab_ref[b, :, 2048 + h * MLSTM_HEAD:2048 + (h + 1) * MLSTM_HEAD])

    gens = []
    for b in range(nb):
        taps = _shifted_rows(buf_ref.at[b], slab_ref[b, :, 0:2 * GROUP_W], (3, 2, 1, 0))
        qk = cb_ref[...]
        for j in range(MLSTM_CONV):
            qk = qk + taps[j] * cw_ref[j:j + 1, :]
        qk = _silu(qk)
        logi_col = slab_ref[b, :, 2560:2688] + igb_ref[...]
        logf_col = _log_sigmoid(slab_ref[b, :, 2688:2816] + fgb_ref[...])
        cum_col = _mm_sel_lhs(tri_ref[...], logf_col)
        pre_row = rows_ref[b, 0] + gb_col_ref[...]
        logi_row = pre_row[0:MLSTM_HEADS, :]
        logf_row = _log_sigmoid(pre_row[MLSTM_HEADS:2 * MLSTM_HEADS, :])
        cum_row = _mm_sel_rhs(logf_row, triu_ref[...])
        gens += [head(b, h, qk, logi_col, cum_col, logi_row, cum_row) for h in range(MLSTM_HEADS)]
    _lockstep(gens)


def _mlstm(slab, rows, cw, cb, igb, fgb, gb_col, ng, tri, triu):
    B, S, _ = slab.shape
    nb = BATCH_BLOCK
    const = lambda shape: pl.BlockSpec(shape, lambda b, c: (0,) * len(shape))
    return pl.pallas_call(
        functools.partial(_mlstm_kernel, nb=nb), grid=(B // nb, S // CHUNK),
        in_specs=[pl.BlockSpec((nb, CHUNK, MLSTM_SLAB), lambda b, c: (b, c, 0)),
                  pl.BlockSpec((nb, 1, 2 * MLSTM_HEADS, CHUNK), lambda b, c: (b, c, 0, 0)),
                  const((MLSTM_CONV, 2 * GROUP_W)), const((1, 2 * GROUP_W)), const((1, LANES)),
                  const((1, LANES)), const((2 * MLSTM_HEADS, 1)), const((1, GROUP_W)),
                  const((CHUNK, CHUNK)), const((CHUNK, CHUNK))],
        out_specs=pl.BlockSpec((nb, CHUNK, GROUP_W), lambda b, c: (b, c, 0)),
        out_shape=jax.ShapeDtypeStruct((B, S, GROUP_W), F32),
        scratch_shapes=[pltpu.VMEM((nb, CARRY_ROWS + CHUNK, 2 * GROUP_W), F32),
                        pltpu.VMEM((nb, MLSTM_HEADS, MLSTM_HEAD, MLSTM_HEAD), F32),
                        pltpu.VMEM((nb, MLSTM_HEADS, CARRY_ROWS, MLSTM_HEAD), F32)],
        compiler_params=pltpu.CompilerParams(dimension_semantics=("arbitrary", "arbitrary")),
        name="mlstm")(slab, rows, cw, cb, igb, fgb, gb_col, ng, tri, triu)


def _rwkv_slab_width(layer):
    return 2176 if layer == 0 else 2304


def _rwkv_shift_cols(layer):
    return 1664 if layer == 0 else 1792


def _rwkv_kernel(*refs, layer, nb):
    if layer == 0:
        (slab_ref, mu_ref, w0_ref, a0_ref, w2a2_ref, kk_ref, ka_ref, rk_ref, lng_ref, lnb_ref,
         tri_ref, bd_ref, strict_ref, incl_ref, lvl_ref, o_ref, vf_out_ref, buf_ref, st_ref) = refs
    else:
        (slab_ref, vf_ref, mu_ref, w0_ref, a0_ref, w2a2_ref, v0_ref, v2_ref, kk_ref, ka_ref, rk_ref,
         lng_ref, lnb_ref, tri_ref, bd_ref, strict_ref, incl_ref, lvl_ref, o_ref, buf_ref, st_ref) = refs
    ws = _rwkv_shift_cols(layer)
    zoff = _rwkv_slab_width(layer) - GROUP_W
    npair = GROUP_W // LANES

    @pl.when(pl.program_id(1) == 0)
    def _():
        st_ref[...] = jnp.zeros_like(st_ref)
        for b in range(nb):
            buf_ref[b, 0:CARRY_ROWS, :] = jnp.zeros((CARRY_ROWS, ws), F32)

    lane = lax.broadcasted_iota(jnp.int32, (1, LANES), 1)
    masks = _lane_half_masks()
    bd = bd_ref[...]
    strict = strict_ref[...] > 0.5
    incl = incl_ref[...] > 0.5
    rows2 = lambda t: jnp.concatenate([t * masks[0], t * masks[1]], axis=0)
    eye = (lax.broadcasted_iota(jnp.int32, (LANES, LANES), 0)
           == lax.broadcasted_iota(jnp.int32, (LANES, LANES), 1)).astype(F32)

    inst = []
    for b in range(nb):
        f = slab_ref[b, :, 0:ws]
        (prev,) = _shifted_rows(buf_ref.at[b], f, (1,))
        f = f + mu_ref[...] * (prev - f)
        r = f[:, 0:512]
        k = f[:, 512:1024]
        v = f[:, 1024:1536]
        lora = f[:, 1536:1664]
        lora = jnp.where(lane < RWKV_W_RANK, jnp.tanh(lora), lora)
        wa = _mm(lora, w2a2_ref[...])
        w_log = -_softplus(-(w0_ref[...] + wa[:, 0:512])) - RWKV_DECAY_OFFSET
        lw = -jnp.exp(w_log)
        a = _sigmoid(a0_ref[...] + wa[:, 512:1024])
        if layer == 0:
            vf_out_ref[b] = v
        else:
            mix = _sigmoid(v0_ref[...] + _mm(f[:, 1664:1792], v2_ref[...]))
            v = v + (vf_ref[b] - v) * mix
        kk = k * kk_ref[...]
        k = k * (1.0 + (a - 1.0) * ka_ref[...])
        cum = _mm_sel_lhs(tri_ref[...], lw)
        last = cum[CHUNK - 1:CHUNK, :]
        e_pos = jnp.exp(cum)
        e_neg = jnp.exp(-cum)
        e_end = jnp.exp(last - cum)
        e_prev = jnp.exp(cum - lw)
        gam = jnp.exp(last)
        for p in range(npair):
            ls = slice(p * LANES, (p + 1) * LANES)
            kk_p = kk[:, ls]
            ss = _mm_sel_rhs(kk_p * kk_p, bd)
            kk_p = kk_p / jnp.maximum(jnp.sqrt(ss), 1e-12)
            k_p, r_p, v_p = k[:, ls], r[:, ls], v[:, ls]
            b_p = kk_p * a[:, ls]
            inst.append(dict(
                b=b, p=p, ls=ls, r=r_p, k=k_p, v=v_p, gam=gam[:, ls],
                la=rows2(-kk_p * e_prev[:, ls]), lr=rows2(r_p * e_pos[:, ls]),
                rb=rows2(b_p * e_neg[:, ls]), rk=rows2(k_p * e_neg[:, ls]),
                bh=rows2(b_p * e_end[:, ls]), kh=rows2(k_p * e_end[:, ls]), vs=rows2(v_p)))

    for d in inst:
        aa = _mm_nt(jnp.concatenate([d['la'], d['lr']], axis=0),
                    jnp.concatenate([d['rb'], d['rk']], axis=0))
        d['nab'] = jnp.where(strict, aa[0:LANES, 0:LANES], 0.0)
        d['aak'] = jnp.where(strict, aa[0:LANES, LANES:2 * LANES], 0.0)
        d['arb'] = jnp.where(incl, aa[LANES:2 * LANES, 0:LANES], 0.0)
        d['ark'] = jnp.where(incl, aa[LANES:2 * LANES, LANES:2 * LANES], 0.0)
        d['t'] = eye + d['nab'] * lvl_ref[0]
        d['x'] = _mm(d['aak'], d['vs'])
    for lv in range(1, 6):
        for d in inst:
            d['nt'] = _mm(d['nab'] * lvl_ref[lv], d['t'])
        for d in inst:
            d['t'] = d['t'] + _mm(d['t'], d['nt'])
    for d in inst:
        d['wu'] = _mm(d['t'], jnp.concatenate([d['la'], d['x']], axis=1))
    for d in inst:
        qy = _mm(d['arb'], d['wu'])
        d['qt'] = d['lr'] + qy[:, 0:LANES]
        d['y0'] = _mm(d['ark'], d['vs']) + qy[:, LANES:2 * LANES]
    for d in inst:
        st = st_ref[d['b'], d['p']]
        uy = _mm_nt(jnp.concatenate([d['wu'][:, 0:LANES], d['qt']], axis=0), st)
        ust = uy[0:LANES] + d['wu'][:, LANES:2 * LANES]
        d['yst'] = uy[LANES:2 * LANES] + d['y0']
        st_ref[d['b'], d['p']] = st * d['gam'] + _mm_tn(jnp.concatenate([ust, d['vs']], axis=0),
                                                        jnp.concatenate([d['bh'], d['kh']], axis=0))
    inv = 1.0 / RWKV_HEAD
    for d in inst:
        ls = d['ls']
        y = d['yst'][0:CHUNK] + d['yst'][CHUNK:2 * CHUNK]
        mu_y = _mm_sel_rhs(y, bd) * inv
        yc = y - mu_y
        var = _mm_sel_rhs(yc * yc, bd) * inv
        yn = yc * lax.rsqrt(var + RWKV_LN_EPS) * lng_ref[:, ls] + lnb_ref[:, ls]
        bonus = _mm_sel_rhs(d['r'] * d['k'] * rk_ref[:, ls], bd) * d['v']
        z_p = slab_ref[d['b'], :, zoff + d['p'] * LANES:zoff + (d['p'] + 1) * LANES]
        o_ref[d['b'], :, ls] = (yn + bonus) * _silu(z_p)


def _rwkv(layer, slab, vf, mu, w0, a0, w2a2, v0, v2, kkw, ka, rk, lng, lnb, tri, bd, strict, incl, lvl):
    B, S, W = slab.shape
    nb = BATCH_BLOCK
    ws = _rwkv_shift_cols(layer)
    const = lambda shape: pl.BlockSpec(shape, lambda b, c: (0,) * len(shape))
    tok = lambda w: pl.BlockSpec((nb, CHUNK, w), lambda b, c: (b, c, 0))
    vecw = const((1, GROUP_W))
    in_specs = [tok(W)]
    args = [slab]
    if layer > 0:
        in_specs.append(tok(GROUP_W))
        args.append(vf)
    in_specs += [const((1, ws)), vecw, vecw, const((LANES, 2 * GROUP_W))]
    args += [mu, w0, a0, w2a2]
    if layer > 0:
        in_specs += [vecw, const((LANES, GROUP_W))]
        args += [v0, v2]
    in_specs += [vecw, vecw, vecw, vecw, vecw, const((CHUNK, CHUNK)), const((LANES, LANES)),
                 const((LANES, LANES)), const((LANES, LANES)), const((6, LANES, LANES))]
    args += [kkw, ka, rk, lng, lnb, tri, bd, strict, incl, lvl]
    out_shape = [jax.ShapeDtypeStruct((B, S, GROUP_W), F32)]
    out_specs = [tok(GROUP_W)]
    if layer == 0:
        out_shape.append(jax.ShapeDtypeStruct((B, S, GROUP_W), F32))
        out_specs.append(tok(GROUP_W))
    res = pl.pallas_call(
        functools.partial(_rwkv_kernel, layer=layer, nb=nb), grid=(B // nb, S // CHUNK),
        in_specs=in_specs, out_specs=out_specs, out_shape=out_shape,
        scratch_shapes=[pltpu.VMEM((nb, CARRY_ROWS + CHUNK, ws), F32),
                        pltpu.VMEM((nb, GROUP_W // LANES, LANES, LANES), F32)],
        compiler_params=pltpu.CompilerParams(dimension_semantics=("arbitrary", "arbitrary"),
                                             vmem_limit_bytes=VMEM_LIMIT_PROJ),
        name="rwkv")(*args)
    return (res[0], res[1]) if layer == 0 else (res[0], vf)


def _np_consts():
    i = np.arange(CHUNK)
    tri = (i[None, :] <= i[:, None]).astype(np.float32)
    t = np.arange(LANES)
    same = (t[:, None] // CHUNK) == (t[None, :] // CHUNK)
    strict = (same & (t[None, :] < t[:, None])).astype(np.float32)
    incl = (same & (t[None, :] <= t[:, None])).astype(np.float32)
    bd = same.astype(np.float32)
    lvl = np.stack([(((t[:, None] >> l) == (t[None, :] >> l))
                     & ((t[:, None] >> (l - 1)) != (t[None, :] >> (l - 1)))).astype(np.float32)
                    for l in range(1, 7)])
    ex = np.zeros((LANES, GROUP_W), np.float32)
    for h in range(SSD_HEADS):
        ex[h, h * SSD_HEADDIM:(h + 1) * SSD_HEADDIM] = 1.0
    return dict(tri=tri, triu=tri.T.copy(), strict=strict, incl=incl, bd=bd, lvl=lvl, ex=ex)


def _pad_rows(w, height):
    return jnp.pad(w, ((0, height - w.shape[0]), (0, 0)))


def _pad_vec(v, width):
    v = v.reshape(1, -1)
    return jnp.pad(v, ((0, 0), (0, width - v.shape[1])))


def _layer(x2, v_first, layer, p, B, S, consts, g_final):
    w_in_t = _bf(p['w_in'].T)
    sw = 3 * GROUP_W + RWKV_W_RANK + RWKV_A_RANK + (RWKV_V_RANK if layer > 0 else 0)
    widths = [256, 256, 512, 16, 512, sw, 512, SSD_XBC, SSD_HEADS, 512, 1024, 512, 4, 4, 512, 512]
    offs = np.concatenate([[0], np.cumsum(widths)]).tolist()
    names = ['gla_q', 'gla_k', 'gla_v', 'gla_gk', 'gla_z', 'rwkv_shift', 'rwkv_z', 'ssd_xbc', 'ssd_dt',
             'ssd_z', 'mlstm_qk', 'mlstm_v', 'mlstm_i', 'mlstm_f', 'mlstm_o', 'mlstm_z']
    col = {n: w_in_t[offs[i]:offs[i + 1], :] for i, n in enumerate(names)}
    g = p['norm_g'].reshape(1, D_MODEL)
    tri, triu = consts['tri'], consts['triu']
    rows_of = lambda t: t.reshape(t.shape[0], B, S // CHUNK, CHUNK).transpose(1, 2, 0, 3)

    w_gla = jnp.concatenate([col['gla_q'], col['gla_k'], col['gla_v'], col['gla_z'],
                             _pad_rows(col['gla_gk'], LANES)], axis=0)
    slab = _in_proj(x2, g, w_gla).reshape(B, S, GLA_SLAB)
    w2p = _bf(jnp.pad(p['gla_gk_w2'], ((0, LANES - GLA_RANK), (0, 0))))
    y_gla = _gla(slab, w2p, p['gla_gk_b'].reshape(1, -1), p['gla_norm_g'].reshape(1, -1), _bf(tri))

    sh = col['rwkv_shift']
    mu = p['rwkv_mu']
    if layer == 0:
        w_rwkv = jnp.concatenate([sh, col['rwkv_z']], axis=0)
        mu_p = mu.reshape(1, -1)
    else:
        w_rwkv = jnp.concatenate([_pad_rows(sh, 1792), col['rwkv_z']], axis=0)
        mu_p = _pad_vec(mu, 1792)
    slab = _in_proj(x2, g, w_rwkv).reshape(B, S, _rwkv_slab_width(layer))
    w2a2 = jnp.zeros((LANES, 2 * GROUP_W), F32)
    w2a2 = w2a2.at[0:RWKV_W_RANK, 0:GROUP_W].set(p['rwkv_w2'])
    w2a2 = w2a2.at[RWKV_W_RANK:, GROUP_W:].set(p['rwkv_a2'])
    vec = lambda t: t.reshape(1, GROUP_W)
    v0 = v2 = None
    if layer > 0:
        v0 = vec(p['rwkv_v0'])
        v2 = _bf(jnp.pad(p['rwkv_v2'], ((0, LANES - RWKV_V_RANK), (0, 0))))
    y_rwkv, v_first = _rwkv(layer, slab, v_first, mu_p, vec(p['rwkv_w0']), vec(p['rwkv_a0']), _bf(w2a2),
                            v0, v2, vec(p['rwkv_k_k']), vec(p['rwkv_k_a']), vec(p['rwkv_r_k']),
                            vec(p['rwkv_ln_g']), vec(p['rwkv_ln_b']), _bf(tri), _bf(consts['bd']),
                            consts['strict'], consts['incl'], consts['lvl'])

    w_ssd = jnp.concatenate([col['ssd_xbc'], col['ssd_z'], _pad_rows(col['ssd_dt'], LANES)], axis=0)
    slab, rows = _in_proj(x2, g, w_ssd, col['ssd_dt'])
    a_neg = -jnp.exp(p['ssd_a_log'])
    y_ssd = _ssd(slab.reshape(B, S, SSD_SLAB), rows_of(rows), p['ssd_conv_w'], p['ssd_conv_b'].reshape(1, -1),
                 _pad_vec(p['ssd_dt_bias'], LANES), _pad_vec(a_neg, LANES),
                 p['ssd_dt_bias'].reshape(-1, 1), a_neg.reshape(-1, 1),
                 jnp.repeat(p['ssd_d'], SSD_HEADDIM).reshape(1, -1), p['ssd_norm_g'].reshape(1, -1),
                 _bf(tri), _bf(triu), _bf(consts['ex']))

    w_ml = jnp.concatenate([col['mlstm_qk'], col['mlstm_v'], col['mlstm_o'], col['mlstm_z'],
                            _pad_rows(col['mlstm_i'], LANES), _pad_rows(col['mlstm_f'], LANES)], axis=0)
    wt_ml = jnp.concatenate([col['mlstm_i'], col['mlstm_f']], axis=0)
    slab, rows = _in_proj(x2, g, w_ml, wt_ml)
    gb_col = jnp.concatenate([p['mlstm_ig_b'], p['mlstm_fg_b']]).reshape(-1, 1)
    y_ml = _mlstm(slab.reshape(B, S, MLSTM_SLAB), rows_of(rows), p['mlstm_conv_w'],
                  p['mlstm_conv_b'].reshape(1, -1), _pad_vec(p['mlstm_ig_b'], LANES),
                  _pad_vec(p['mlstm_fg_b'], LANES), gb_col, p['mlstm_norm_g'].reshape(1, -1),
                  _bf(tri), _bf(triu))

    T = B * S
    ys = [y.reshape(T, GROUP_W) for y in (y_gla, y_rwkv, y_ssd, y_ml)]
    return _out_proj(x2, ys, _bf(p['w_out']), g_final), v_first


_PARAM_NAMES_0 = ['norm_g', 'w_in', 'w_out', 'gla_gk_w2', 'gla_gk_b', 'gla_norm_g', 'rwkv_mu', 'rwkv_w0',
                  'rwkv_w2', 'rwkv_a0', 'rwkv_a2', 'rwkv_k_k', 'rwkv_k_a', 'rwkv_r_k', 'rwkv_ln_g',
                  'rwkv_ln_b', 'ssd_conv_w', 'ssd_conv_b', 'ssd_dt_bias', 'ssd_a_log', 'ssd_d',
                  'ssd_norm_g', 'mlstm_conv_w', 'mlstm_conv_b', 'mlstm_ig_b', 'mlstm_fg_b', 'mlstm_norm_g']
_PARAM_NAMES_1 = (_PARAM_NAMES_0[:11] + ['rwkv_v0', 'rwkv_v2'] + _PARAM_NAMES_0[11:])


def kernel(x,
           norm_g_0, w_in_0, w_out_0, gla_gk_w2_0, gla_gk_b_0, gla_norm_g_0,
           rwkv_mu_0, rwkv_w0_0, rwkv_w2_0, rwkv_a0_0, rwkv_a2_0,
           rwkv_k_k_0, rwkv_k_a_0, rwkv_r_k_0, rwkv_ln_g_0, rwkv_ln_b_0,
           ssd_conv_w_0, ssd_conv_b_0, ssd_dt_bias_0, ssd_a_log_0, ssd_d_0, ssd_norm_g_0,
           mlstm_conv_w_0, mlstm_conv_b_0, mlstm_ig_b_0, mlstm_fg_b_0, mlstm_norm_g_0,
           norm_g_1, w_in_1, w_out_1, gla_gk_w2_1, gla_gk_b_1, gla_norm_g_1,
           rwkv_mu_1, rwkv_w0_1, rwkv_w2_1, rwkv_a0_1, rwkv_a2_1, rwkv_v0_1, rwkv_v2_1,
           rwkv_k_k_1, rwkv_k_a_1, rwkv_r_k_1, rwkv_ln_g_1, rwkv_ln_b_1,
           ssd_conv_w_1, ssd_conv_b_1, ssd_dt_bias_1, ssd_a_log_1, ssd_d_1, ssd_norm_g_1,
           mlstm_conv_w_1, mlstm_conv_b_1, mlstm_ig_b_1, mlstm_fg_b_1, mlstm_norm_g_1,
           final_norm_g):
    params = (norm_g_0, w_in_0, w_out_0, gla_gk_w2_0, gla_gk_b_0, gla_norm_g_0,
              rwkv_mu_0, rwkv_w0_0, rwkv_w2_0, rwkv_a0_0, rwkv_a2_0,
              rwkv_k_k_0, rwkv_k_a_0, rwkv_r_k_0, rwkv_ln_g_0, rwkv_ln_b_0,
              ssd_conv_w_0, ssd_conv_b_0, ssd_dt_bias_0, ssd_a_log_0, ssd_d_0, ssd_norm_g_0,
              mlstm_conv_w_0, mlstm_conv_b_0, mlstm_ig_b_0, mlstm_fg_b_0, mlstm_norm_g_0,
              norm_g_1, w_in_1, w_out_1, gla_gk_w2_1, gla_gk_b_1, gla_norm_g_1,
              rwkv_mu_1, rwkv_w0_1, rwkv_w2_1, rwkv_a0_1, rwkv_a2_1, rwkv_v0_1, rwkv_v2_1,
              rwkv_k_k_1, rwkv_k_a_1, rwkv_r_k_1, rwkv_ln_g_1, rwkv_ln_b_1,
              ssd_conv_w_1, ssd_conv_b_1, ssd_dt_bias_1, ssd_a_log_1, ssd_d_1, ssd_norm_g_1,
              mlstm_conv_w_1, mlstm_conv_b_1, mlstm_ig_b_1, mlstm_fg_b_1, mlstm_norm_g_1,
              final_norm_g)
    n0 = len(_PARAM_NAMES_0)
    n1 = len(_PARAM_NAMES_1)
    p0 = dict(zip(_PARAM_NAMES_0, params[:n0]))
    p1 = dict(zip(_PARAM_NAMES_1, params[n0:n0 + n1]))
    final_norm_g = params[n0 + n1]
    B, S, _ = x.shape
    consts = {k: jnp.asarray(v) for k, v in _np_consts().items()}
    x2 = x.reshape(B * S, D_MODEL)
    x2, v_first = _layer(x2, None, 0, p0, B, S, consts, None)
    x2, _ = _layer(x2, v_first, 1, p1, B, S, consts, final_norm_g.reshape(1, D_MODEL))
    return x2.reshape(B, S, D_MODEL)
```

```python
import functools

import numpy as np
import jax
import jax.numpy as jnp
from jax import lax
from jax.experimental import pallas as pl
from jax.experimental.pallas import tpu as pltpu

F32 = jnp.float32
BF16 = jnp.bfloat16

D_MODEL = 2048
CHUNK = 64
GROUP_W = 512
NORM_EPS = 1e-6
LANES = 128
CARRY_ROWS = 8

GLA_HEADS, GLA_DK, GLA_DV, GLA_RANK = 4, 64, 128, 16
GLA_GATE_NORMALIZER = 16.0
RWKV_HEAD, RWKV_W_RANK, RWKV_A_RANK, RWKV_V_RANK = 64, 64, 64, 32
RWKV_LN_EPS = 64e-5
RWKV_DECAY_OFFSET = 0.5
SSD_HEADS, SSD_HEADDIM, SSD_STATE, SSD_CONV = 8, 64, 128, 4
SSD_XBC = 1024
MLSTM_HEADS, MLSTM_HEAD, MLSTM_CONV = 4, 128, 4

ROW_TILE_IN = 512
SMALL_GK, SMALL_DT, SMALL_I, SMALL_F = 0, 16, 24, 28
ROWS_DT, ROWS_IF, ROWS_TOTAL = 0, 8, 16
ROW_TILE_OUT = 512
VMEM_LIMIT_PROJ = 48 * 2**20


def _bf(x):
    return x.astype(BF16)


def _mm(a, b):
    return jnp.dot(_bf(a), _bf(b), preferred_element_type=F32)


def _mm_nt(a, b):
    return lax.dot_general(_bf(a), _bf(b), (((1,), (1,)), ((), ())), preferred_element_type=F32)


def _mm_tn(a, b):
    return lax.dot_general(_bf(a), _bf(b), (((0,), (0,)), ((), ())), preferred_element_type=F32)


def _split3(x):
    hi = _bf(x)
    r1 = x - hi.astype(F32)
    mid = _bf(r1)
    lo = _bf(r1 - mid.astype(F32))
    return hi, mid, lo


def _mm_sel_rhs(x, sel):
    hi, mid, lo = _split3(x)
    d = lambda a: jnp.dot(a, sel, preferred_element_type=F32)
    return d(hi) + d(mid) + d(lo)


def _mm_sel_lhs(sel, x):
    hi, mid, lo = _split3(x)
    d = lambda a: jnp.dot(sel, a, preferred_element_type=F32)
    return d(hi) + d(mid) + d(lo)


def _rowsum(x):
    ones = jnp.ones((x.shape[-1], LANES), BF16)
    hi = _bf(x)
    lo = _bf(x - hi.astype(F32))
    return (jnp.dot(hi, ones, preferred_element_type=F32)
            + jnp.dot(lo, ones, preferred_element_type=F32))


def _sigmoid(x):
    return 1.0 / (1.0 + jnp.exp(-x))


def _silu(x):
    return x * _sigmoid(x)


def _softplus(x):
    return jnp.maximum(x, 0.0) + jnp.log1p(jnp.exp(-jnp.abs(x)))


def _log_sigmoid(x):
    return -_softplus(-x)


def _lane_half_masks():
    lane = lax.broadcasted_iota(jnp.int32, (1, LANES), 1)
    lo = (lane < LANES // 2).astype(F32)
    return lo, 1.0 - lo


def _causal(n):
    r = lax.broadcasted_iota(jnp.int32, (n, n), 0)
    c = lax.broadcasted_iota(jnp.int32, (n, n), 1)
    return c <= r


def _shifted_rows(buf_ref, cur, offsets):
    buf_ref[CARRY_ROWS:CARRY_ROWS + CHUNK, :] = cur
    outs = [buf_ref[CARRY_ROWS - off:CARRY_ROWS - off + CHUNK, :] for off in offsets]
    tail = buf_ref[CHUNK:CHUNK + CARRY_ROWS, :]
    buf_ref[0:CARRY_ROWS, :] = tail
    return outs


def _in_proj_kernel(*refs, pieces, zero_lanes, has_small):
    x_ref, g_ref = refs[0], refs[1]
    w_refs = refs[2:2 + len(pieces)]
    rest = refs[2 + len(pieces):]
    if has_small:
        ws_ref, wr_ref, o_ref, os_ref, or_ref = rest
    else:
        (o_ref,) = rest
    nt = lambda a, b: lax.dot_general(a, b, (((1,), (1,)), ((), ())), preferred_element_type=F32)
    tm = x_ref.shape[0]
    for lo, hi in zero_lanes:
        o_ref[:, lo:hi] = jnp.zeros((tm, hi - lo), F32)
    for half in range(2):
        rs = slice(half * (tm // 2), (half + 1) * (tm // 2))
        x = x_ref[rs, :]
        h = x * lax.rsqrt(jnp.mean(x * x, axis=-1, keepdims=True) + NORM_EPS) * g_ref[...]
        hb = _bf(h)
        for w_ref, (_, n_rows, lane_off) in zip(w_refs, pieces):
            o_ref[rs, lane_off:lane_off + n_rows] = nt(hb, w_ref[...])
        if has_small:
            os_ref[rs, :] = nt(hb, ws_ref[...])
            or_ref[:, rs] = nt(wr_ref[...], hb)


def _in_proj(x2, g, w_t, pieces, width, zero_lanes=(), w_small=None, w_rows=None):
    T = x2.shape[0]
    tm = ROW_TILE_IN
    const2 = lambda shape: pl.BlockSpec(shape, lambda i: (0, 0))
    in_specs = [pl.BlockSpec((tm, D_MODEL), lambda i: (i, 0)), const2((1, D_MODEL))]
    args = [x2, g]
    for off, n_rows, _ in pieces:
        in_specs.append(pl.BlockSpec((pl.Element(n_rows), pl.Element(D_MODEL)), lambda i, off=off: (off, 0)))
        args.append(w_t)
    out_shape = [jax.ShapeDtypeStruct((T, width), F32)]
    out_specs = [pl.BlockSpec((tm, width), lambda i: (i, 0))]
    if w_small is not None:
        r = w_rows.shape[0]
        in_specs += [const2((LANES, D_MODEL)), const2((r, D_MODEL))]
        args += [w_small, w_rows]
        out_shape += [jax.ShapeDtypeStruct((T, LANES), F32), jax.ShapeDtypeStruct((r, T), F32)]
        out_specs += [pl.BlockSpec((tm, LANES), lambda i: (i, 0)), pl.BlockSpec((r, tm), lambda i: (0, i))]
    res = pl.pallas_call(
        functools.partial(_in_proj_kernel, pieces=tuple(pieces), zero_lanes=tuple(zero_lanes),
                          has_small=w_small is not None),
        grid=(T // tm,), in_specs=in_specs, out_specs=out_specs, out_shape=out_shape,
        compiler_params=pltpu.CompilerParams(dimension_semantics=("arbitrary",),
                                             vmem_limit_bytes=VMEM_LIMIT_PROJ),
        name="in_proj")(*args)
    return res if w_small is not None else res[0]


def _out_proj_kernel(*refs, final):
    if final:
        x_ref, y0, y1, y2, y3, w_ref, g_ref, o_ref = refs
    else:
        x_ref, y0, y1, y2, y3, w_ref, o_ref = refs
    acc = x_ref[...]
    for gi, y in enumerate((y0, y1, y2, y3)):
        acc = acc + jnp.dot(_bf(y[...]), w_ref[gi * GROUP_W:(gi + 1) * GROUP_W, :],
                            preferred_element_type=F32)
    if final:
        acc = acc * lax.rsqrt(jnp.mean(acc * acc, axis=-1, keepdims=True) + NORM_EPS) * g_ref[...]
    o_ref[...] = acc


def _out_proj(x2, ys, w, g_final=None):
    T = x2.shape[0]
    tm = ROW_TILE_OUT
    final = g_final is not None
    in_specs = [pl.BlockSpec((tm, D_MODEL), lambda i: (i, 0))]
    in_specs += [pl.BlockSpec((tm, GROUP_W), lambda i: (i, 0)) for _ in range(4)]
    in_specs += [pl.BlockSpec((D_MODEL, D_MODEL), lambda i: (0, 0))]
    args = [x2, *ys, w]
    if final:
        in_specs.append(pl.BlockSpec((1, D_MODEL), lambda i: (0, 0)))
        args.append(g_final)
    return pl.pallas_call(
        functools.partial(_out_proj_kernel, final=final),
        grid=(T // tm,), in_specs=in_specs,
        out_specs=pl.BlockSpec((tm, D_MODEL), lambda i: (i, 0)),
        out_shape=jax.ShapeDtypeStruct((T, D_MODEL), F32),
        compiler_params=pltpu.CompilerParams(dimension_semantics=("arbitrary",),
                                             vmem_limit_bytes=VMEM_LIMIT_PROJ),
        name="out_proj")(*args)


GLA_SLAB = 1536


BATCH_BLOCK = 4


def _lockstep(gens):
    gens = list(gens)
    while gens:
        alive = []
        for g in gens:
            try:
                next(g)
                alive.append(g)
            except StopIteration:
                pass
        gens = alive


def _gla_kernel(slab_ref, small_ref, w2_ref, gkb_ref, ng_ref, tri_ref, o_ref, st_ref, *, nb):
    @pl.when(pl.program_id(1) == 0)
    def _():
        st_ref[...] = jnp.zeros_like(st_ref)

    causal = _causal(CHUNK)
    masks = _lane_half_masks()

    def head(b, h, qg, kg, kd, dec):
        p, j = divmod(h, 2)
        ls = slice(p * LANES, (p + 1) * LANES)
        qm = qg[:, ls] * masks[j]
        att = jnp.where(causal, _mm_nt(qm, kg[:, ls]), 0.0)
        yield
        v_h = slab_ref[b, :, 512 + h * GLA_DV:512 + (h + 1) * GLA_DV]
        st = st_ref[b, h]
        o = _mm(att, v_h) + _mm_nt(qm, st)
        st_ref[b, h] = st * dec[:, ls] + _mm_tn(v_h, kd[:, ls] * masks[j])
        yield
        ms = jnp.mean(o * o, axis=-1, keepdims=True)
        yield
        o = o * lax.rsqrt(ms + NORM_EPS)
        o = o * ng_ref[:, h * GLA_DV:(h + 1) * GLA_DV]
        z_h = slab_ref[b, :, 1024 + h * GLA_DV:1024 + (h + 1) * GLA_DV]
        o_ref[b, :, h * GLA_DV:(h + 1) * GLA_DV] = o * _silu(z_h)

    gens = []
    for b in range(nb):
        q = slab_ref[b, :, 0:256] * (GLA_DK ** -0.5)
        k = slab_ref[b, :, 256:512]
        gk = _mm(small_ref[b], w2_ref[...]) + gkb_ref[...]
        log_a = _log_sigmoid(gk) / GLA_GATE_NORMALIZER
        cum = _mm_sel_lhs(tri_ref[...], log_a)
        last = cum[CHUNK - 1:CHUNK, :]
        qg = q * jnp.exp(cum)
        kg = k * jnp.exp(-cum)
        kd = k * jnp.exp(last - cum)
        dec = jnp.exp(last)
        gens += [head(b, h, qg, kg, kd, dec) for h in range(GLA_HEADS)]
    _lockstep(gens)


def _gla(slab, small, w2p, gkb, ng, tri):
    B, S, _ = slab.shape
    nb = BATCH_BLOCK
    const = lambda shape: pl.BlockSpec(shape, lambda b, c: (0,) * len(shape))
    return pl.pallas_call(
        functools.partial(_gla_kernel, nb=nb), grid=(B // nb, S // CHUNK),
        in_specs=[pl.BlockSpec((nb, CHUNK, GLA_SLAB), lambda b, c: (b, c, 0)),
                  pl.BlockSpec((nb, CHUNK, LANES), lambda b, c: (b, c, 0)),
                  const((LANES, 256)), const((1, 256)), const((1, GROUP_W)), const((CHUNK, CHUNK))],
        out_specs=pl.BlockSpec((nb, CHUNK, GROUP_W), lambda b, c: (b, c, 0)),
        out_shape=jax.ShapeDtypeStruct((B, S, GROUP_W), F32),
        scratch_shapes=[pltpu.VMEM((nb, GLA_HEADS, GLA_DV, LANES), F32)],
        compiler_params=pltpu.CompilerParams(dimension_semantics=("arbitrary", "arbitrary")),
        name="gla")(slab, small, w2p, gkb, ng, tri)


SSD_SLAB = 1536


def _ssd_kernel(slab_ref, small_ref, rows_ref, cw_ref, cb_ref, dtb_ref, a_ref, dtb_col_ref, a_col_ref,
                dskip_ref, ng_ref, tri_ref, triu_ref, ex_ref, o_ref, buf_ref, st_ref, *, nb):
    @pl.when(pl.program_id(1) == 0)
    def _():
        st_ref[...] = jnp.zeros_like(st_ref)
        for b in range(nb):
            buf_ref[b, 0:CARRY_ROWS, :] = jnp.zeros((CARRY_ROWS, SSD_XBC), F32)

    causal = _causal(CHUNK)
    masks = _lane_half_masks()

    def pair(b, p, xbc, cum_col, cum_row, xdt, xw, ecum, dec):
        g = p // 2
        ls = slice(p * LANES, (p + 1) * LANES)
        bm = xbc[:, 512 + g * SSD_STATE:512 + (g + 1) * SSD_STATE]
        cm = xbc[:, 768 + g * SSD_STATE:768 + (g + 1) * SSD_STATE]
        cbm = _mm_nt(cm, bm)
        st = st_ref[b, p]
        y = _mm(cm, st) * ecum[:, ls]
        st_ref[b, p] = st * dec[:, ls] + _mm_tn(bm, xw[:, ls])
        yield
        for j in range(2):
            h = 2 * p + j
            seg = cum_col[:, SMALL_DT + h:SMALL_DT + h + 1] - cum_row[h:h + 1, :]
            lmat = jnp.exp(jnp.where(causal, seg, -jnp.inf))
            y = y + _mm(cbm * lmat, xdt[:, ls] * masks[j])
        yield
        y = y + dskip_ref[:, ls] * xbc[:, ls]
        o_ref[b, :, ls] = y * _silu(slab_ref[b, :, 1024 + p * LANES:1024 + (p + 1) * LANES])

    gens = []
    for b in range(nb):
        taps = _shifted_rows(buf_ref.at[b], slab_ref[b, :, 0:SSD_XBC], (3, 2, 1, 0))
        xbc = cb_ref[...]
        for j in range(SSD_CONV):
            xbc = xbc + taps[j] * cw_ref[j:j + 1, :]
        xbc = _silu(xbc)
        dt_col = _softplus(small_ref[b] + dtb_ref[...])
        cum_col = _mm_sel_lhs(tri_ref[...], dt_col * a_ref[...])
        dt_row = _softplus(rows_ref[b, 0, ROWS_DT:ROWS_DT + SSD_HEADS, :] + dtb_col_ref[...])
        cum_row = _mm_sel_rhs(dt_row * a_col_ref[...], triu_ref[...])
        dt_b = _mm_sel_rhs(dt_col, ex_ref[...])
        cum_b = _mm_sel_rhs(cum_col, ex_ref[...])
        last_b = cum_b[CHUNK - 1:CHUNK, :]
        xdt = xbc[:, 0:512] * dt_b
        xw = xdt * jnp.exp(last_b - cum_b)
        ecum = jnp.exp(cum_b)
        dec = jnp.exp(last_b)
        gens += [pair(b, p, xbc, cum_col, cum_row, xdt, xw, ecum, dec) for p in range(SSD_HEADS // 2)]
    _lockstep(gens)
    for b in range(nb):
        y = o_ref[b]
        inv = lax.rsqrt(_rowsum(y * y) * (1.0 / GROUP_W) + NORM_EPS)
        for p in range(GROUP_W // LANES):
            ls = slice(p * LANES, (p + 1) * LANES)
            o_ref[b, :, ls] = y[:, ls] * inv * ng_ref[:, ls]


def _ssd(slab, small, rows, cw, cb, dtb, a, dtb_col, a_col, dskip, ng, tri, triu, ex):
    B, S, _ = slab.shape
    nb = BATCH_BLOCK
    const = lambda shape: pl.BlockSpec(shape, lambda b, c: (0,) * len(shape))
    return pl.pallas_call(
        functools.partial(_ssd_kernel, nb=nb), grid=(B // nb, S // CHUNK),
        in_specs=[pl.BlockSpec((nb, CHUNK, SSD_SLAB), lambda b, c: (b, c, 0)),
                  pl.BlockSpec((nb, CHUNK, LANES), lambda b, c: (b, c, 0)),
                  pl.BlockSpec((nb, 1, ROWS_TOTAL, CHUNK), lambda b, c: (b, c, 0, 0)),
                  const((SSD_CONV, SSD_XBC)), const((1, SSD_XBC)), const((1, LANES)), const((1, LANES)),
                  const((SSD_HEADS, 1)), const((SSD_HEADS, 1)), const((1, GROUP_W)), const((1, GROUP_W)),
                  const((CHUNK, CHUNK)), const((CHUNK, CHUNK)), const((LANES, GROUP_W))],
        out_specs=pl.BlockSpec((nb, CHUNK, GROUP_W), lambda b, c: (b, c, 0)),
        out_shape=jax.ShapeDtypeStruct((B, S, GROUP_W), F32),
        scratch_shapes=[pltpu.VMEM((nb, CARRY_ROWS + CHUNK, SSD_XBC), F32),
                        pltpu.VMEM((nb, SSD_HEADS // 2, SSD_STATE, LANES), F32)],
        compiler_params=pltpu.CompilerParams(dimension_semantics=("arbitrary", "arbitrary")),
        name="ssd")(slab, small, rows, cw, cb, dtb, a, dtb_col, a_col, dskip, ng, tri, triu, ex)


MLSTM_SLAB = 2560


def _mlstm_kernel(slab_ref, small_ref, rows_ref, cw_ref, cb_ref, igb_ref, fgb_ref, gb_col_ref, ng_ref,
                  tri_ref, triu_ref, o_ref, buf_ref, c_ref, nm_ref, *, nb):
    @pl.when(pl.program_id(1) == 0)
    def _():
        c_ref[...] = jnp.zeros_like(c_ref)
        nm_ref[...] = jnp.zeros_like(nm_ref)
        for b in range(nb):
            buf_ref[b, 0:CARRY_ROWS, :] = jnp.zeros((CARRY_ROWS, 2 * GROUP_W), F32)

    causal = _causal(CHUNK)

    def head(b, h, qk, logi_col, cum_col, logi_row, cum_row):
        ls = slice(h * MLSTM_HEAD, (h + 1) * MLSTM_HEAD)
        q = qk[:, ls]
        k = qk[:, GROUP_W + h * MLSTM_HEAD:GROUP_W + (h + 1) * MLSTM_HEAD] * (MLSTM_HEAD ** -0.5)
        v = slab_ref[b, :, 1024 + h * MLSTM_HEAD:1024 + (h + 1) * MLSTM_HEAD]
        ci = cum_col[:, SMALL_F + h:SMALL_F + h + 1]
        li = logi_col[:, SMALL_I + h:SMALL_I + h + 1]
        cr = cum_row[h:h + 1, :]
        lir = logi_row[h:h + 1, :]
        last = cr[:, CHUNK - 1:CHUNK]
        c_prev = c_ref[b, h]
        n_prev = nm_ref[b, h, 0:1, :]
        m_prev = nm_ref[b, h, 1:2, 0:1]

        g = last - ci + li
        g_max = jnp.max(g, axis=0, keepdims=True)
        log_d = jnp.where(causal, ci - cr + lir, -jnp.inf)
        row_max = jnp.max(log_d, axis=-1, keepdims=True)
        qk_h = _mm_nt(q, k)
        qc = _mm(q, c_prev)
        qn = _rowsum(q * n_prev)
        yield
        kw = k * jnp.exp(g - g_max)
        c_loc = _mm_tn(kw, v)
        n_loc = jnp.sum(kw, axis=0, keepdims=True)
        m_new = jnp.maximum(last + m_prev, g_max)
        a_old = jnp.exp(last + m_prev - m_new)
        a_new = jnp.exp(g_max - m_new)
        c_ref[b, h] = a_old * c_prev + a_new * c_loc
        nm_ref[b, h, 0:1, :] = a_old * n_prev + a_new * n_loc
        nm_ref[b, h, 1:2, :] = jnp.broadcast_to(m_new, (1, MLSTM_HEAD))
        m_inter = ci + m_prev
        m_l = jnp.maximum(m_inter, row_max)
        wqk = qk_h * jnp.exp(log_d - m_l)
        w_inter = jnp.exp(m_inter - m_l)
        num = _mm(wqk, v) + w_inter * qc
        den = _rowsum(wqk) + w_inter * qn
        yield
        den = jnp.maximum(jnp.abs(den), jnp.exp(-m_l))
        hh = num / den * _sigmoid(slab_ref[b, :, 1536 + h * MLSTM_HEAD:1536 + (h + 1) * MLSTM_HEAD])
        mu = _rowsum(hh) * (1.0 / MLSTM_HEAD)
        yield
        yc = hh - mu
        var = _rowsum(yc * yc) * (1.0 / MLSTM_HEAD)
        yield
        hh = yc * lax.rsqrt(var + NORM_EPS) * ng_ref[:, ls]
        o_ref[b, :, ls] = hh * _silu(slab_ref[b, :, 2048 + h * MLSTM_HEAD:2048 + (h + 1) * MLSTM_HEAD])

    gens = []
    for b in range(nb):
        taps = _shifted_rows(buf_ref.at[b], slab_ref[b, :, 0:2 * GROUP_W], (3, 2, 1, 0))
        qk = cb_ref[...]
        for j in range(MLSTM_CONV):
            qk = qk + taps[j] * cw_ref[j:j + 1, :]
        qk = _silu(qk)
        logi_col = small_ref[b] + igb_ref[...]
        logf_col = _log_sigmoid(small_ref[b] + fgb_ref[...])
        cum_col = _mm_sel_lhs(tri_ref[...], logf_col)
        pre_row = rows_ref[b, 0, ROWS_IF:ROWS_IF + 2 * MLSTM_HEADS, :] + gb_col_ref[...]
        logi_row = pre_row[0:MLSTM_HEADS, :]
        logf_row = _log_sigmoid(pre_row[MLSTM_HEADS:2 * MLSTM_HEADS, :])
        cum_row = _mm_sel_rhs(logf_row, triu_ref[...])
        gens += [head(b, h, qk, logi_col, cum_col, logi_row, cum_row) for h in range(MLSTM_HEADS)]
    _lockstep(gens)


def _mlstm(slab, small, rows, cw, cb, igb, fgb, gb_col, ng, tri, triu):
    B, S, _ = slab.shape
    nb = BATCH_BLOCK
    const = lambda shape: pl.BlockSpec(shape, lambda b, c: (0,) * len(shape))
    return pl.pallas_call(
        functools.partial(_mlstm_kernel, nb=nb), grid=(B // nb, S // CHUNK),
        in_specs=[pl.BlockSpec((nb, CHUNK, MLSTM_SLAB), lambda b, c: (b, c, 0)),
                  pl.BlockSpec((nb, CHUNK, LANES), lambda b, c: (b, c, 0)),
                  pl.BlockSpec((nb, 1, ROWS_TOTAL, CHUNK), lambda b, c: (b, c, 0, 0)),
                  const((MLSTM_CONV, 2 * GROUP_W)), const((1, 2 * GROUP_W)), const((1, LANES)),
                  const((1, LANES)), const((2 * MLSTM_HEADS, 1)), const((1, GROUP_W)),
                  const((CHUNK, CHUNK)), const((CHUNK, CHUNK))],
        out_specs=pl.BlockSpec((nb, CHUNK, GROUP_W), lambda b, c: (b, c, 0)),
        out_shape=jax.ShapeDtypeStruct((B, S, GROUP_W), F32),
        scratch_shapes=[pltpu.VMEM((nb, CARRY_ROWS + CHUNK, 2 * GROUP_W), F32),
                        pltpu.VMEM((nb, MLSTM_HEADS, MLSTM_HEAD, MLSTM_HEAD), F32),
                        pltpu.VMEM((nb, MLSTM_HEADS, CARRY_ROWS, MLSTM_HEAD), F32)],
        compiler_params=pltpu.CompilerParams(dimension_semantics=("arbitrary", "arbitrary")),
        name="mlstm")(slab, small, rows, cw, cb, igb, fgb, gb_col, ng, tri, triu)


def _rwkv_slab_width(layer):
    return 2176 if layer == 0 else 2304


def _rwkv_shift_cols(layer):
    return 1664 if layer == 0 else 1792


def _rwkv_kernel(*refs, layer, nb):
    if layer == 0:
        (slab_ref, mu_ref, w0_ref, a0_ref, w2a2_ref, kk_ref, ka_ref, rk_ref, lng_ref, lnb_ref,
         tri_ref, bd_ref, strict_ref, incl_ref, lvl_ref, o_ref, vf_out_ref, buf_ref, st_ref) = refs
    else:
        (slab_ref, vf_ref, mu_ref, w0_ref, a0_ref, w2a2_ref, v0_ref, v2_ref, kk_ref, ka_ref, rk_ref,
         lng_ref, lnb_ref, tri_ref, bd_ref, strict_ref, incl_ref, lvl_ref, o_ref, buf_ref, st_ref) = refs
    ws = _rwkv_shift_cols(layer)
    zoff = _rwkv_slab_width(layer) - GROUP_W
    npair = GROUP_W // LANES

    @pl.when(pl.program_id(1) == 0)
    def _():
        st_ref[...] = jnp.zeros_like(st_ref)
        for b in range(nb):
            buf_ref[b, 0:CARRY_ROWS, :] = jnp.zeros((CARRY_ROWS, ws), F32)

    lane = lax.broadcasted_iota(jnp.int32, (1, LANES), 1)
    masks = _lane_half_masks()
    bd = bd_ref[...]
    strict = strict_ref[...] > 0.5
    incl = incl_ref[...] > 0.5
    rows2 = lambda t: jnp.concatenate([t * masks[0], t * masks[1]], axis=0)
    eye = (lax.broadcasted_iota(jnp.int32, (LANES, LANES), 0)
           == lax.broadcasted_iota(jnp.int32, (LANES, LANES), 1)).astype(F32)

    inst = []
    for b in range(nb):
        f = slab_ref[b, :, 0:ws]
        (prev,) = _shifted_rows(buf_ref.at[b], f, (1,))
        f = f + mu_ref[...] * (prev - f)
        r = f[:, 0:512]
        k = f[:, 512:1024]
        v = f[:, 1024:1536]
        lora = f[:, 1536:1664]
        lora = jnp.where(lane < RWKV_W_RANK, jnp.tanh(lora), lora)
        wa = _mm(lora, w2a2_ref[...])
        w_log = -_softplus(-(w0_ref[...] + wa[:, 0:512])) - RWKV_DECAY_OFFSET
        lw = -jnp.exp(w_log)
        a = _sigmoid(a0_ref[...] + wa[:, 512:1024])
        if layer == 0:
            vf_out_ref[b] = v
        else:
            mix = _sigmoid(v0_ref[...] + _mm(f[:, 1664:1792], v2_ref[...]))
            v = v + (vf_ref[b] - v) * mix
        kk = k * kk_ref[...]
        k = k * (1.0 + (a - 1.0) * ka_ref[...])
        cum = _mm_sel_lhs(tri_ref[...], lw)
        last = cum[CHUNK - 1:CHUNK, :]
        e_pos = jnp.exp(cum)
        e_neg = jnp.exp(-cum)
        e_end = jnp.exp(last - cum)
        e_prev = jnp.exp(cum - lw)
        gam = jnp.exp(last)
        for p in range(npair):
            ls = slice(p * LANES, (p + 1) * LANES)
            kk_p = kk[:, ls]
            ss = _mm_sel_rhs(kk_p * kk_p, bd)
            kk_p = kk_p / jnp.maximum(jnp.sqrt(ss), 1e-12)
            k_p, r_p, v_p = k[:, ls], r[:, ls], v[:, ls]
            b_p = kk_p * a[:, ls]
            inst.append(dict(
                b=b, p=p, ls=ls, r=r_p, k=k_p, v=v_p, gam=gam[:, ls],
                la=rows2(-kk_p * e_prev[:, ls]), lr=rows2(r_p * e_pos[:, ls]),
                rb=rows2(b_p * e_neg[:, ls]), rk=rows2(k_p * e_neg[:, ls]),
                bh=rows2(b_p * e_end[:, ls]), kh=rows2(k_p * e_end[:, ls]), vs=rows2(v_p)))

    for d in inst:
        aa = _mm_nt(jnp.concatenate([d['la'], d['lr']], axis=0),
                    jnp.concatenate([d['rb'], d['rk']], axis=0))
        d['nab'] = jnp.where(strict, aa[0:LANES, 0:LANES], 0.0)
        d['aak'] = jnp.where(strict, aa[0:LANES, LANES:2 * LANES], 0.0)
        d['arb'] = jnp.where(incl, aa[LANES:2 * LANES, 0:LANES], 0.0)
        d['ark'] = jnp.where(incl, aa[LANES:2 * LANES, LANES:2 * LANES], 0.0)
        d['t'] = eye + d['nab'] * lvl_ref[0]
        d['x'] = _mm(d['aak'], d['vs'])
    for lv in range(1, 6):
        for d in inst:
            d['nt'] = _mm(d['nab'] * lvl_ref[lv], d['t'])
        for d in inst:
            d['t'] = d['t'] + _mm(d['t'], d['nt'])
    for d in inst:
        d['wu'] = _mm(d['t'], jnp.concatenate([d['la'], d['x']], axis=1))
    for d in inst:
        qy = _mm(d['arb'], d['wu'])
        d['qt'] = d['lr'] + qy[:, 0:LANES]
        d['y0'] = _mm(d['ark'], d['vs']) + qy[:, LANES:2 * LANES]
    for d in inst:
        st = st_ref[d['b'], d['p']]
        uy = _mm_nt(jnp.concatenate([d['wu'][:, 0:LANES], d['qt']], axis=0), st)
        ust = uy[0:LANES] + d['wu'][:, LANES:2 * LANES]
        d['yst'] = uy[LANES:2 * LANES] + d['y0']
        st_ref[d['b'], d['p']] = st * d['gam'] + _mm_tn(jnp.concatenate([ust, d['vs']], axis=0),
                                                        jnp.concatenate([d['bh'], d['kh']], axis=0))
    inv = 1.0 / RWKV_HEAD
    for d in inst:
        ls = d['ls']
        y = d['yst'][0:CHUNK] + d['yst'][CHUNK:2 * CHUNK]
        mu_y = _mm_sel_rhs(y, bd) * inv
        yc = y - mu_y
        var = _mm_sel_rhs(yc * yc, bd) * inv
        yn = yc * lax.rsqrt(var + RWKV_LN_EPS) * lng_ref[:, ls] + lnb_ref[:, ls]
        bonus = _mm_sel_rhs(d['r'] * d['k'] * rk_ref[:, ls], bd) * d['v']
        z_p = slab_ref[d['b'], :, zoff + d['p'] * LANES:zoff + (d['p'] + 1) * LANES]
        o_ref[d['b'], :, ls] = (yn + bonus) * _silu(z_p)


def _rwkv(layer, slab, vf, mu, w0, a0, w2a2, v0, v2, kkw, ka, rk, lng, lnb, tri, bd, strict, incl, lvl):
    B, S, W = slab.shape
    nb = BATCH_BLOCK
    ws = _rwkv_shift_cols(layer)
    const = lambda shape: pl.BlockSpec(shape, lambda b, c: (0,) * len(shape))
    tok = lambda w: pl.BlockSpec((nb, CHUNK, w), lambda b, c: (b, c, 0))
    vecw = const((1, GROUP_W))
    in_specs = [tok(W)]
    args = [slab]
    if layer > 0:
        in_specs.append(tok(GROUP_W))
        args.append(vf)
    in_specs += [const((1, ws)), vecw, vecw, const((LANES, 2 * GROUP_W))]
    args += [mu, w0, a0, w2a2]
    if layer > 0:
        in_specs += [vecw, const((LANES, GROUP_W))]
        args += [v0, v2]
    in_specs += [vecw, vecw, vecw, vecw, vecw, const((CHUNK, CHUNK)), const((LANES, LANES)),
                 const((LANES, LANES)), const((LANES, LANES)), const((6, LANES, LANES))]
    args += [kkw, ka, rk, lng, lnb, tri, bd, strict, incl, lvl]
    out_shape = [jax.ShapeDtypeStruct((B, S, GROUP_W), F32)]
    out_specs = [tok(GROUP_W)]
    if layer == 0:
        out_shape.append(jax.ShapeDtypeStruct((B, S, GROUP_W), F32))
        out_specs.append(tok(GROUP_W))
    res = pl.pallas_call(
        functools.partial(_rwkv_kernel, layer=layer, nb=nb), grid=(B // nb, S // CHUNK),
        in_specs=in_specs, out_specs=out_specs, out_shape=out_shape,
        scratch_shapes=[pltpu.VMEM((nb, CARRY_ROWS + CHUNK, ws), F32),
                        pltpu.VMEM((nb, GROUP_W // LANES, LANES, LANES), F32)],
        compiler_params=pltpu.CompilerParams(dimension_semantics=("arbitrary", "arbitrary"),
                                             vmem_limit_bytes=VMEM_LIMIT_PROJ),
        name="rwkv")(*args)
    return (res[0], res[1]) if layer == 0 else (res[0], vf)


def _np_consts():
    i = np.arange(CHUNK)
    tri = (i[None, :] <= i[:, None]).astype(np.float32)
    t = np.arange(LANES)
    same = (t[:, None] // CHUNK) == (t[None, :] // CHUNK)
    strict = (same & (t[None, :] < t[:, None])).astype(np.float32)
    incl = (same & (t[None, :] <= t[:, None])).astype(np.float32)
    bd = same.astype(np.float32)
    lvl = np.stack([(((t[:, None] >> l) == (t[None, :] >> l))
                     & ((t[:, None] >> (l - 1)) != (t[None, :] >> (l - 1)))).astype(np.float32)
                    for l in range(1, 7)])
    ex = np.zeros((LANES, GROUP_W), np.float32)
    for h in range(SSD_HEADS):
        ex[SMALL_DT + h, h * SSD_HEADDIM:(h + 1) * SSD_HEADDIM] = 1.0
    return dict(tri=tri, triu=tri.T.copy(), strict=strict, incl=incl, bd=bd, lvl=lvl, ex=ex)


def _pad_rows(w, height):
    return jnp.pad(w, ((0, height - w.shape[0]), (0, 0)))


def _pad_vec(v, width, offset=0):
    v = v.reshape(1, -1)
    return jnp.pad(v, ((0, 0), (offset, width - offset - v.shape[1])))


BF16_ROW_TILE = 16


def _layer(x2, v_first, layer, p, B, S, consts, g_final):
    sw = 3 * GROUP_W + RWKV_W_RANK + RWKV_A_RANK + (RWKV_V_RANK if layer > 0 else 0)
    names = ['gla_q', 'gla_k', 'gla_v', 'gla_gk', 'gla_z', 'rwkv_shift', 'rwkv_z', 'ssd_xbc', 'ssd_dt',
             'pad0', 'ssd_z', 'mlstm_qk', 'mlstm_v', 'mlstm_i', 'mlstm_f', 'pad1', 'mlstm_o', 'mlstm_z']
    widths = [256, 256, 512, 16, 512, sw, 512, SSD_XBC, SSD_HEADS, 8, 512, 1024, 512, 4, 4, 8, 512, 512]
    offs = np.concatenate([[0], np.cumsum(widths)]).tolist()
    off = dict(zip(names, offs[:-1]))
    assert all(off[n] % BF16_ROW_TILE == 0 for n in names if n not in ('mlstm_f', 'pad0', 'pad1'))
    w_in = p['w_in']
    zpad = jnp.zeros((D_MODEL, 8), w_in.dtype)
    cut0, cut1 = off['pad0'], off['pad1'] - 8
    w_t = _bf(jnp.concatenate([w_in[:, :cut0], zpad, w_in[:, cut0:cut1], zpad, w_in[:, cut1:]], axis=1).T)
    rows = lambda name, n: w_t[off[name]:off[name] + n, :]
    g = p['norm_g'].reshape(1, D_MODEL)
    tri, triu = consts['tri'], consts['triu']
    rows_of = lambda t: t.reshape(t.shape[0], B, S // CHUNK, CHUNK).transpose(1, 2, 0, 3)

    w_small = _pad_rows(jnp.concatenate([rows('gla_gk', GLA_RANK), rows('ssd_dt', SSD_HEADS),
                                         rows('mlstm_i', 2 * MLSTM_HEADS)], axis=0), LANES)
    w_rows = jnp.concatenate([rows('ssd_dt', SSD_HEADS), rows('mlstm_i', 2 * MLSTM_HEADS)], axis=0)
    mu = p['rwkv_mu']
    if layer == 0:
        pieces, zero_lanes = [(off['rwkv_shift'], sw + GROUP_W, 0)], ()
        mu_p = mu.reshape(1, -1)
    else:
        pieces = [(off['rwkv_shift'], sw, 0), (off['rwkv_z'], GROUP_W, 1792)]
        zero_lanes = ((1664, 1792),)
        mu_p = _pad_vec(mu, 1792)
    slab, small, gate_rows = _in_proj(x2, g, w_t, pieces, _rwkv_slab_width(layer), zero_lanes,
                                      w_small, w_rows)
    slab = slab.reshape(B, S, _rwkv_slab_width(layer))
    small = small.reshape(B, S, LANES)
    gate_rows = rows_of(gate_rows)
    w2a2 = jnp.zeros((LANES, 2 * GROUP_W), F32)
    w2a2 = w2a2.at[0:RWKV_W_RANK, 0:GROUP_W].set(p['rwkv_w2'])
    w2a2 = w2a2.at[RWKV_W_RANK:, GROUP_W:].set(p['rwkv_a2'])
    vec = lambda t: t.reshape(1, GROUP_W)
    v0 = v2 = None
    if layer > 0:
        v0 = vec(p['rwkv_v0'])
        v2 = _bf(jnp.pad(p['rwkv_v2'], ((0, LANES - RWKV_V_RANK), (0, 0))))
    y_rwkv, v_first = _rwkv(layer, slab, v_first, mu_p, vec(p['rwkv_w0']), vec(p['rwkv_a0']), _bf(w2a2),
                            v0, v2, vec(p['rwkv_k_k']), vec(p['rwkv_k_a']), vec(p['rwkv_r_k']),
                            vec(p['rwkv_ln_g']), vec(p['rwkv_ln_b']), _bf(tri), _bf(consts['bd']),
                            consts['strict'], consts['incl'], consts['lvl'])

    slab = _in_proj(x2, g, w_t, [(off['gla_q'], 1024, 0), (off['gla_z'], GROUP_W, 1024)], GLA_SLAB)
    w2p = _bf(jnp.pad(p['gla_gk_w2'], ((SMALL_GK, LANES - SMALL_GK - GLA_RANK), (0, 0))))
    y_gla = _gla(slab.reshape(B, S, GLA_SLAB), small, w2p, p['gla_gk_b'].reshape(1, -1),
                 p['gla_norm_g'].reshape(1, -1), _bf(tri))

    slab = _in_proj(x2, g, w_t, [(off['ssd_xbc'], SSD_XBC, 0), (off['ssd_z'], GROUP_W, SSD_XBC)], SSD_SLAB)
    a_neg = -jnp.exp(p['ssd_a_log'])
    y_ssd = _ssd(slab.reshape(B, S, SSD_SLAB), small, gate_rows, p['ssd_conv_w'],
                 p['ssd_conv_b'].reshape(1, -1),
                 _pad_vec(p['ssd_dt_bias'], LANES, SMALL_DT), _pad_vec(a_neg, LANES, SMALL_DT),
                 p['ssd_dt_bias'].reshape(-1, 1), a_neg.reshape(-1, 1),
                 jnp.repeat(p['ssd_d'], SSD_HEADDIM).reshape(1, -1), p['ssd_norm_g'].reshape(1, -1),
                 _bf(tri), _bf(triu), _bf(consts['ex']))

    slab = _in_proj(x2, g, w_t, [(off['mlstm_qk'], 3 * GROUP_W, 0), (off['mlstm_o'], 2 * GROUP_W, 3 * GROUP_W)],
                    MLSTM_SLAB)
    gb_col = jnp.concatenate([p['mlstm_ig_b'], p['mlstm_fg_b']]).reshape(-1, 1)
    y_ml = _mlstm(slab.reshape(B, S, MLSTM_SLAB), small, gate_rows, p['mlstm_conv_w'],
                  p['mlstm_conv_b'].reshape(1, -1), _pad_vec(p['mlstm_ig_b'], LANES, SMALL_I),
                  _pad_vec(p['mlstm_fg_b'], LANES, SMALL_F), gb_col, p['mlstm_norm_g'].reshape(1, -1),
                  _bf(tri), _bf(triu))

    T = B * S
    ys = [y.reshape(T, GROUP_W) for y in (y_gla, y_rwkv, y_ssd, y_ml)]
    return _out_proj(x2, ys, _bf(p['w_out']), g_final), v_first


_PARAM_NAMES_0 = ['norm_g', 'w_in', 'w_out', 'gla_gk_w2', 'gla_gk_b', 'gla_norm_g', 'rwkv_mu', 'rwkv_w0',
                  'rwkv_w2', 'rwkv_a0', 'rwkv_a2', 'rwkv_k_k', 'rwkv_k_a', 'rwkv_r_k', 'rwkv_ln_g',
                  'rwkv_ln_b', 'ssd_conv_w', 'ssd_conv_b', 'ssd_dt_bias', 'ssd_a_log', 'ssd_d',
                  'ssd_norm_g', 'mlstm_conv_w', 'mlstm_conv_b', 'mlstm_ig_b', 'mlstm_fg_b', 'mlstm_norm_g']
_PARAM_NAMES_1 = (_PARAM_NAMES_0[:11] + ['rwkv_v0', 'rwkv_v2'] + _PARAM_NAMES_0[11:])


def kernel(x,
           norm_g_0, w_in_0, w_out_0, gla_gk_w2_0, gla_gk_b_0, gla_norm_g_0,
           rwkv_mu_0, rwkv_w0_0, rwkv_w2_0, rwkv_a0_0, rwkv_a2_0,
           rwkv_k_k_0, rwkv_k_a_0, rwkv_r_k_0, rwkv_ln_g_0, rwkv_ln_b_0,
           ssd_conv_w_0, ssd_conv_b_0, ssd_dt_bias_0, ssd_a_log_0, ssd_d_0, ssd_norm_g_0,
           mlstm_conv_w_0, mlstm_conv_b_0, mlstm_ig_b_0, mlstm_fg_b_0, mlstm_norm_g_0,
           norm_g_1, w_in_1, w_out_1, gla_gk_w2_1, gla_gk_b_1, gla_norm_g_1,
           rwkv_mu_1, rwkv_w0_1, rwkv_w2_1, rwkv_a0_1, rwkv_a2_1, rwkv_v0_1, rwkv_v2_1,
           rwkv_k_k_1, rwkv_k_a_1, rwkv_r_k_1, rwkv_ln_g_1, rwkv_ln_b_1,
           ssd_conv_w_1, ssd_conv_b_1, ssd_dt_bias_1, ssd_a_log_1, ssd_d_1, ssd_norm_g_1,
           mlstm_conv_w_1, mlstm_conv_b_1, mlstm_ig_b_1, mlstm_fg_b_1, mlstm_norm_g_1,
           final_norm_g):
    params = (norm_g_0, w_in_0, w_out_0, gla_gk_w2_0, gla_gk_b_0, gla_norm_g_0,
              rwkv_mu_0, rwkv_w0_0, rwkv_w2_0, rwkv_a0_0, rwkv_a2_0,
              rwkv_k_k_0, rwkv_k_a_0, rwkv_r_k_0, rwkv_ln_g_0, rwkv_ln_b_0,
              ssd_conv_w_0, ssd_conv_b_0, ssd_dt_bias_0, ssd_a_log_0, ssd_d_0, ssd_norm_g_0,
              mlstm_conv_w_0, mlstm_conv_b_0, mlstm_ig_b_0, mlstm_fg_b_0, mlstm_norm_g_0,
              norm_g_1, w_in_1, w_out_1, gla_gk_w2_1, gla_gk_b_1, gla_norm_g_1,
              rwkv_mu_1, rwkv_w0_1, rwkv_w2_1, rwkv_a0_1, rwkv_a2_1, rwkv_v0_1, rwkv_v2_1,
              rwkv_k_k_1, rwkv_k_a_1, rwkv_r_k_1, rwkv_ln_g_1, rwkv_ln_b_1,
              ssd_conv_w_1, ssd_conv_b_1, ssd_dt_bias_1, ssd_a_log_1, ssd_d_1, ssd_norm_g_1,
              mlstm_conv_w_1, mlstm_conv_b_1, mlstm_ig_b_1, mlstm_fg_b_1, mlstm_norm_g_1,
              final_norm_g)
    n0 = len(_PARAM_NAMES_0)
    n1 = len(_PARAM_NAMES_1)
    p0 = dict(zip(_PARAM_NAMES_0, params[:n0]))
    p1 = dict(zip(_PARAM_NAMES_1, params[n0:n0 + n1]))
    final_norm_g = params[n0 + n1]
    B, S, _ = x.shape
    consts = {k: jnp.asarray(v) for k, v in _np_consts().items()}
    x2 = x.reshape(B * S, D_MODEL)
    x2, v_first = _layer(x2, None, 0, p0, B, S, consts, None)
    x2, _ = _layer(x2, v_first, 1, p1, B, S, consts, final_norm_g.reshape(1, D_MODEL))
    return x2.reshape(B, S, D_MODEL)
```

```python
import functools

import numpy as np
import jax
import jax.numpy as jnp
from jax import lax
from jax.experimental import pallas as pl
from jax.experimental.pallas import tpu as pltpu

F32 = jnp.float32
BF16 = jnp.bfloat16

D_MODEL = 2048
CHUNK = 64
GROUP_W = 512
NORM_EPS = 1e-6
LANES = 128
CARRY_ROWS = 8

GLA_HEADS, GLA_DK, GLA_DV, GLA_RANK = 4, 64, 128, 16
GLA_GATE_NORMALIZER = 16.0
RWKV_HEAD, RWKV_W_RANK, RWKV_A_RANK, RWKV_V_RANK = 64, 64, 64, 32
RWKV_LN_EPS = 64e-5
RWKV_DECAY_OFFSET = 0.5
SSD_HEADS, SSD_HEADDIM, SSD_STATE, SSD_CONV = 8, 64, 128, 4
SSD_XBC = 1024
MLSTM_HEADS, MLSTM_HEAD, MLSTM_CONV = 4, 128, 4

ROW_TILE_IN = 512
SMALL_GK, SMALL_DT, SMALL_I, SMALL_F = 0, 16, 24, 28
ROWS_DT, ROWS_IF, ROWS_TOTAL = 0, 8, 16
ROW_TILE_OUT = 512
VMEM_LIMIT_PROJ = 48 * 2**20


def _bf(x):
    return x.astype(BF16)


def _mm(a, b):
    return jnp.dot(_bf(a), _bf(b), preferred_element_type=F32)


def _mm_nt(a, b):
    return lax.dot_general(_bf(a), _bf(b), (((1,), (1,)), ((), ())), preferred_element_type=F32)


def _mm_tn(a, b):
    return lax.dot_general(_bf(a), _bf(b), (((0,), (0,)), ((), ())), preferred_element_type=F32)


def _split3(x):
    hi = _bf(x)
    r1 = x - hi.astype(F32)
    mid = _bf(r1)
    lo = _bf(r1 - mid.astype(F32))
    return hi, mid, lo


def _mm_sel_rhs(x, sel):
    hi, mid, lo = _split3(x)
    d = lambda a: jnp.dot(a, sel, preferred_element_type=F32)
    return d(hi) + d(mid) + d(lo)


def _mm_sel2(x, sel):
    hi = _bf(x)
    lo = _bf(x - hi.astype(F32))
    d = lambda a: jnp.dot(a, sel, preferred_element_type=F32)
    return d(hi) + d(lo)


def _mm_sel_lhs(sel, x):
    hi, mid, lo = _split3(x)
    d = lambda a: jnp.dot(sel, a, preferred_element_type=F32)
    return d(hi) + d(mid) + d(lo)


def _rowsum(x):
    ones = jnp.ones((x.shape[-1], LANES), BF16)
    hi = _bf(x)
    lo = _bf(x - hi.astype(F32))
    return (jnp.dot(hi, ones, preferred_element_type=F32)
            + jnp.dot(lo, ones, preferred_element_type=F32))


def _sigmoid(x):
    return 1.0 / (1.0 + jnp.exp(-x))


def _silu(x):
    return x * _sigmoid(x)


def _softplus(x):
    return jnp.maximum(x, 0.0) + jnp.log1p(jnp.exp(-jnp.abs(x)))


def _log_sigmoid(x):
    return -_softplus(-x)


def _lane_half_masks():
    lane = lax.broadcasted_iota(jnp.int32, (1, LANES), 1)
    lo = (lane < LANES // 2).astype(F32)
    return lo, 1.0 - lo


def _causal(n):
    r = lax.broadcasted_iota(jnp.int32, (n, n), 0)
    c = lax.broadcasted_iota(jnp.int32, (n, n), 1)
    return c <= r


def _shifted_rows(buf_ref, cur, offsets):
    buf_ref[CARRY_ROWS:CARRY_ROWS + CHUNK, :] = cur
    outs = [buf_ref[CARRY_ROWS - off:CARRY_ROWS - off + CHUNK, :] for off in offsets]
    tail = buf_ref[CHUNK:CHUNK + CARRY_ROWS, :]
    buf_ref[0:CARRY_ROWS, :] = tail
    return outs


def _in_proj_kernel(*refs, pieces, zero_lanes, has_small):
    x_ref, g_ref = refs[0], refs[1]
    w_refs = refs[2:2 + len(pieces)]
    rest = refs[2 + len(pieces):]
    if has_small:
        ws_ref, wr_ref, o_ref, os_ref, or_ref = rest
    else:
        (o_ref,) = rest
    nt = lambda a, b: lax.dot_general(a, b, (((1,), (1,)), ((), ())), preferred_element_type=F32)
    tm = x_ref.shape[0]
    for lo, hi in zero_lanes:
        o_ref[:, lo:hi] = jnp.zeros((tm, hi - lo), F32)
    for half in range(2):
        rs = slice(half * (tm // 2), (half + 1) * (tm // 2))
        x = x_ref[rs, :]
        h = x * lax.rsqrt(jnp.mean(x * x, axis=-1, keepdims=True) + NORM_EPS) * g_ref[...]
        hb = _bf(h)
        for w_ref, (_, n_rows, lane_off) in zip(w_refs, pieces):
            o_ref[rs, lane_off:lane_off + n_rows] = nt(hb, w_ref[...])
        if has_small:
            os_ref[rs, :] = nt(hb, ws_ref[...])
            or_ref[:, rs] = nt(wr_ref[...], hb)


def _in_proj(x2, g, w_t, pieces, width, zero_lanes=(), w_small=None, w_rows=None):
    T = x2.shape[0]
    tm = ROW_TILE_IN
    const2 = lambda shape: pl.BlockSpec(shape, lambda i: (0, 0))
    in_specs = [pl.BlockSpec((tm, D_MODEL), lambda i: (i, 0)), const2((1, D_MODEL))]
    args = [x2, g]
    for off, n_rows, _ in pieces:
        in_specs.append(pl.BlockSpec((pl.Element(n_rows), pl.Element(D_MODEL)), lambda i, off=off: (off, 0)))
        args.append(w_t)
    out_shape = [jax.ShapeDtypeStruct((T, width), F32)]
    out_specs = [pl.BlockSpec((tm, width), lambda i: (i, 0))]
    if w_small is not None:
        r = w_rows.shape[0]
        in_specs += [const2((LANES, D_MODEL)), const2((r, D_MODEL))]
        args += [w_small, w_rows]
        out_shape += [jax.ShapeDtypeStruct((T, LANES), F32), jax.ShapeDtypeStruct((r, T), F32)]
        out_specs += [pl.BlockSpec((tm, LANES), lambda i: (i, 0)), pl.BlockSpec((r, tm), lambda i: (0, i))]
    res = pl.pallas_call(
        functools.partial(_in_proj_kernel, pieces=tuple(pieces), zero_lanes=tuple(zero_lanes),
                          has_small=w_small is not None),
        grid=(T // tm,), in_specs=in_specs, out_specs=out_specs, out_shape=out_shape,
        compiler_params=pltpu.CompilerParams(dimension_semantics=("arbitrary",),
                                             vmem_limit_bytes=VMEM_LIMIT_PROJ),
        name="in_proj")(*args)
    return res if w_small is not None else res[0]


def _out_proj_kernel(*refs, final):
    if final:
        x_ref, y0, y1, y2, y3, w_ref, g_ref, o_ref = refs
    else:
        x_ref, y0, y1, y2, y3, w_ref, o_ref = refs
    acc = x_ref[...]
    for gi, y in enumerate((y0, y1, y2, y3)):
        acc = acc + jnp.dot(_bf(y[...]), w_ref[gi * GROUP_W:(gi + 1) * GROUP_W, :],
                            preferred_element_type=F32)
    if final:
        acc = acc * lax.rsqrt(jnp.mean(acc * acc, axis=-1, keepdims=True) + NORM_EPS) * g_ref[...]
    o_ref[...] = acc


def _out_proj(x2, ys, w, g_final=None):
    T = x2.shape[0]
    tm = ROW_TILE_OUT
    final = g_final is not None
    in_specs = [pl.BlockSpec((tm, D_MODEL), lambda i: (i, 0))]
    in_specs += [pl.BlockSpec((tm, GROUP_W), lambda i: (i, 0)) for _ in range(4)]
    in_specs += [pl.BlockSpec((D_MODEL, D_MODEL), lambda i: (0, 0))]
    args = [x2, *ys, w]
    if final:
        in_specs.append(pl.BlockSpec((1, D_MODEL), lambda i: (0, 0)))
        args.append(g_final)
    return pl.pallas_call(
        functools.partial(_out_proj_kernel, final=final),
        grid=(T // tm,), in_specs=in_specs,
        out_specs=pl.BlockSpec((tm, D_MODEL), lambda i: (i, 0)),
        out_shape=jax.ShapeDtypeStruct((T, D_MODEL), F32),
        compiler_params=pltpu.CompilerParams(dimension_semantics=("arbitrary",),
                                             vmem_limit_bytes=VMEM_LIMIT_PROJ),
        name="out_proj")(*args)


GLA_SLAB = 1536


BATCH_BLOCK = 4


def _lockstep(gens):
    gens = list(gens)
    while gens:
        alive = []
        for g in gens:
            try:
                next(g)
                alive.append(g)
            except StopIteration:
                pass
        gens = alive


def _gla_kernel(slab_ref, small_ref, w2_ref, gkb_ref, ng_ref, tri_ref, o_ref, st_ref, *, nb):
    @pl.when(pl.program_id(1) == 0)
    def _():
        st_ref[...] = jnp.zeros_like(st_ref)

    causal = _causal(CHUNK)
    masks = _lane_half_masks()

    def head(b, h, qg, kg, kd, dec):
        p, j = divmod(h, 2)
        ls = slice(p * LANES, (p + 1) * LANES)
        qm = qg[:, ls] * masks[j]
        att = jnp.where(causal, _mm_nt(qm, kg[:, ls]), 0.0)
        yield
        v_h = slab_ref[b, :, 512 + h * GLA_DV:512 + (h + 1) * GLA_DV]
        st = st_ref[b, h]
        o = _mm(att, v_h) + _mm_nt(qm, st)
        st_ref[b, h] = st * dec[:, ls] + _mm_tn(v_h, kd[:, ls] * masks[j])
        yield
        ms = jnp.mean(o * o, axis=-1, keepdims=True)
        yield
        o = o * lax.rsqrt(ms + NORM_EPS)
        o = o * ng_ref[:, h * GLA_DV:(h + 1) * GLA_DV]
        z_h = slab_ref[b, :, 1024 + h * GLA_DV:1024 + (h + 1) * GLA_DV]
        o_ref[b, :, h * GLA_DV:(h + 1) * GLA_DV] = o * _silu(z_h)

    gens = []
    for b in range(nb):
        q = slab_ref[b, :, 0:256] * (GLA_DK ** -0.5)
        k = slab_ref[b, :, 256:512]
        gk = _mm(small_ref[b], w2_ref[...]) + gkb_ref[...]
        log_a = _log_sigmoid(gk) / GLA_GATE_NORMALIZER
        cum = _mm_sel_lhs(tri_ref[...], log_a)
        last = cum[CHUNK - 1:CHUNK, :]
        qg = q * jnp.exp(cum)
        kg = k * jnp.exp(-cum)
        kd = k * jnp.exp(last - cum)
        dec = jnp.exp(last)
        gens += [head(b, h, qg, kg, kd, dec) for h in range(GLA_HEADS)]
    _lockstep(gens)


def _gla(slab, small, w2p, gkb, ng, tri):
    B, S, _ = slab.shape
    nb = BATCH_BLOCK
    const = lambda shape: pl.BlockSpec(shape, lambda b, c: (0,) * len(shape))
    return pl.pallas_call(
        functools.partial(_gla_kernel, nb=nb), grid=(B // nb, S // CHUNK),
        in_specs=[pl.BlockSpec((nb, CHUNK, GLA_SLAB), lambda b, c: (b, c, 0)),
                  pl.BlockSpec((nb, CHUNK, LANES), lambda b, c: (b, c, 0)),
                  const((LANES, 256)), const((1, 256)), const((1, GROUP_W)), const((CHUNK, CHUNK))],
        out_specs=pl.BlockSpec((nb, CHUNK, GROUP_W), lambda b, c: (b, c, 0)),
        out_shape=jax.ShapeDtypeStruct((B, S, GROUP_W), F32),
        scratch_shapes=[pltpu.VMEM((nb, GLA_HEADS, GLA_DV, LANES), F32)],
        compiler_params=pltpu.CompilerParams(dimension_semantics=("arbitrary", "arbitrary")),
        name="gla")(slab, small, w2p, gkb, ng, tri)


SSD_SLAB = 1536


def _ssd_kernel(slab_ref, small_ref, rows_ref, cw_ref, cb_ref, dtb_ref, a_ref, dtb_col_ref, a_col_ref,
                dskip_ref, ng_ref, tri_ref, triu_ref, ex_ref, o_ref, buf_ref, st_ref, *, nb):
    @pl.when(pl.program_id(1) == 0)
    def _():
        st_ref[...] = jnp.zeros_like(st_ref)
        for b in range(nb):
            buf_ref[b, 0:CARRY_ROWS, :] = jnp.zeros((CARRY_ROWS, SSD_XBC), F32)

    causal = _causal(CHUNK)
    masks = _lane_half_masks()

    def pair(b, p, xbc, cum_col, cum_row, xdt, xw, ecum, dec):
        g = p // 2
        ls = slice(p * LANES, (p + 1) * LANES)
        bm = xbc[:, 512 + g * SSD_STATE:512 + (g + 1) * SSD_STATE]
        cm = xbc[:, 768 + g * SSD_STATE:768 + (g + 1) * SSD_STATE]
        cbm = _mm_nt(cm, bm)
        st = st_ref[b, p]
        y = _mm(cm, st) * ecum[:, ls]
        st_ref[b, p] = st * dec[:, ls] + _mm_tn(bm, xw[:, ls])
        yield
        for j in range(2):
            h = 2 * p + j
            seg = cum_col[:, SMALL_DT + h:SMALL_DT + h + 1] - cum_row[h:h + 1, :]
            lmat = jnp.exp(jnp.where(causal, seg, -jnp.inf))
            y = y + _mm(cbm * lmat, xdt[:, ls] * masks[j])
        yield
        y = y + dskip_ref[:, ls] * xbc[:, ls]
        o_ref[b, :, ls] = y * _silu(slab_ref[b, :, 1024 + p * LANES:1024 + (p + 1) * LANES])

    gens = []
    for b in range(nb):
        taps = _shifted_rows(buf_ref.at[b], slab_ref[b, :, 0:SSD_XBC], (3, 2, 1, 0))
        xbc = cb_ref[...]
        for j in range(SSD_CONV):
            xbc = xbc + taps[j] * cw_ref[j:j + 1, :]
        xbc = _silu(xbc)
        dt_col = _softplus(small_ref[b] + dtb_ref[...])
        cum_col = _mm_sel_lhs(tri_ref[...], dt_col * a_ref[...])
        dt_row = _softplus(rows_ref[b, 0, ROWS_DT:ROWS_DT + SSD_HEADS, :] + dtb_col_ref[...])
        cum_row = _mm_sel_rhs(dt_row * a_col_ref[...], triu_ref[...])
        dt_b = _mm_sel_rhs(dt_col, ex_ref[...])
        cum_b = _mm_sel_rhs(cum_col, ex_ref[...])
        last_b = cum_b[CHUNK - 1:CHUNK, :]
        xdt = xbc[:, 0:512] * dt_b
        xw = xdt * jnp.exp(last_b - cum_b)
        ecum = jnp.exp(cum_b)
        dec = jnp.exp(last_b)
        gens += [pair(b, p, xbc, cum_col, cum_row, xdt, xw, ecum, dec) for p in range(SSD_HEADS // 2)]
    _lockstep(gens)
    for b in range(nb):
        y = o_ref[b]
        inv = lax.rsqrt(_rowsum(y * y) * (1.0 / GROUP_W) + NORM_EPS)
        for p in range(GROUP_W // LANES):
            ls = slice(p * LANES, (p + 1) * LANES)
            o_ref[b, :, ls] = y[:, ls] * inv * ng_ref[:, ls]


def _ssd(slab, small, rows, cw, cb, dtb, a, dtb_col, a_col, dskip, ng, tri, triu, ex):
    B, S, _ = slab.shape
    nb = BATCH_BLOCK
    const = lambda shape: pl.BlockSpec(shape, lambda b, c: (0,) * len(shape))
    return pl.pallas_call(
        functools.partial(_ssd_kernel, nb=nb), grid=(B // nb, S // CHUNK),
        in_specs=[pl.BlockSpec((nb, CHUNK, SSD_SLAB), lambda b, c: (b, c, 0)),
                  pl.BlockSpec((nb, CHUNK, LANES), lambda b, c: (b, c, 0)),
                  pl.BlockSpec((nb, 1, ROWS_TOTAL, CHUNK), lambda b, c: (b, c, 0, 0)),
                  const((SSD_CONV, SSD_XBC)), const((1, SSD_XBC)), const((1, LANES)), const((1, LANES)),
                  const((SSD_HEADS, 1)), const((SSD_HEADS, 1)), const((1, GROUP_W)), const((1, GROUP_W)),
                  const((CHUNK, CHUNK)), const((CHUNK, CHUNK)), const((LANES, GROUP_W))],
        out_specs=pl.BlockSpec((nb, CHUNK, GROUP_W), lambda b, c: (b, c, 0)),
        out_shape=jax.ShapeDtypeStruct((B, S, GROUP_W), F32),
        scratch_shapes=[pltpu.VMEM((nb, CARRY_ROWS + CHUNK, SSD_XBC), F32),
                        pltpu.VMEM((nb, SSD_HEADS // 2, SSD_STATE, LANES), F32)],
        compiler_params=pltpu.CompilerParams(dimension_semantics=("arbitrary", "arbitrary")),
        name="ssd")(slab, small, rows, cw, cb, dtb, a, dtb_col, a_col, dskip, ng, tri, triu, ex)


MLSTM_SLAB = 2560


def _mlstm_kernel(slab_ref, small_ref, rows_ref, cw_ref, cb_ref, igb_ref, fgb_ref, gb_col_ref, ng_ref,
                  tri_ref, triu_ref, o_ref, buf_ref, c_ref, nm_ref, *, nb):
    @pl.when(pl.program_id(1) == 0)
    def _():
        c_ref[...] = jnp.zeros_like(c_ref)
        nm_ref[...] = jnp.zeros_like(nm_ref)
        for b in range(nb):
            buf_ref[b, 0:CARRY_ROWS, :] = jnp.zeros((CARRY_ROWS, 2 * GROUP_W), F32)

    causal = _causal(CHUNK)

    def head(b, h, qk, logi_col, cum_col, logi_row, cum_row):
        ls = slice(h * MLSTM_HEAD, (h + 1) * MLSTM_HEAD)
        q = qk[:, ls]
        k = qk[:, GROUP_W + h * MLSTM_HEAD:GROUP_W + (h + 1) * MLSTM_HEAD] * (MLSTM_HEAD ** -0.5)
        v = slab_ref[b, :, 1024 + h * MLSTM_HEAD:1024 + (h + 1) * MLSTM_HEAD]
        ci = cum_col[:, SMALL_F + h:SMALL_F + h + 1]
        li = logi_col[:, SMALL_I + h:SMALL_I + h + 1]
        cr = cum_row[h:h + 1, :]
        lir = logi_row[h:h + 1, :]
        last = cr[:, CHUNK - 1:CHUNK]
        c_prev = c_ref[b, h]
        n_prev = nm_ref[b, h, 0:1, :]
        m_prev = nm_ref[b, h, 1:2, 0:1]

        g = last - ci + li
        g_max = jnp.max(g, axis=0, keepdims=True)
        log_d = jnp.where(causal, ci - cr + lir, -jnp.inf)
        row_max = jnp.max(log_d, axis=-1, keepdims=True)
        qk_h = _mm_nt(q, k)
        qc = _mm(q, c_prev)
        qn = _rowsum(q * n_prev)
        yield
        kw = k * jnp.exp(g - g_max)
        c_loc = _mm_tn(kw, v)
        n_loc = jnp.sum(kw, axis=0, keepdims=True)
        m_new = jnp.maximum(last + m_prev, g_max)
        a_old = jnp.exp(last + m_prev - m_new)
        a_new = jnp.exp(g_max - m_new)
        c_ref[b, h] = a_old * c_prev + a_new * c_loc
        nm_ref[b, h, 0:1, :] = a_old * n_prev + a_new * n_loc
        nm_ref[b, h, 1:2, :] = jnp.broadcast_to(m_new, (1, MLSTM_HEAD))
        m_inter = ci + m_prev
        m_l = jnp.maximum(m_inter, row_max)
        wqk = qk_h * jnp.exp(log_d - m_l)
        w_inter = jnp.exp(m_inter - m_l)
        num = _mm(wqk, v) + w_inter * qc
        den = _rowsum(wqk) + w_inter * qn
        yield
        den = jnp.maximum(jnp.abs(den), jnp.exp(-m_l))
        hh = num / den * _sigmoid(slab_ref[b, :, 1536 + h * MLSTM_HEAD:1536 + (h + 1) * MLSTM_HEAD])
        mu = _rowsum(hh) * (1.0 / MLSTM_HEAD)
        yield
        yc = hh - mu
        var = _rowsum(yc * yc) * (1.0 / MLSTM_HEAD)
        yield
        hh = yc * lax.rsqrt(var + NORM_EPS) * ng_ref[:, ls]
        o_ref[b, :, ls] = hh * _silu(slab_ref[b, :, 2048 + h * MLSTM_HEAD:2048 + (h + 1) * MLSTM_HEAD])

    gens = []
    for b in range(nb):
        taps = _shifted_rows(buf_ref.at[b], slab_ref[b, :, 0:2 * GROUP_W], (3, 2, 1, 0))
        qk = cb_ref[...]
        for j in range(MLSTM_CONV):
            qk = qk + taps[j] * cw_ref[j:j + 1, :]
        qk = _silu(qk)
        logi_col = small_ref[b] + igb_ref[...]
        logf_col = _log_sigmoid(small_ref[b] + fgb_ref[...])
        cum_col = _mm_sel_lhs(tri_ref[...], logf_col)
        pre_row = rows_ref[b, 0, ROWS_IF:ROWS_IF + 2 * MLSTM_HEADS, :] + gb_col_ref[...]
        logi_row = pre_row[0:MLSTM_HEADS, :]
        logf_row = _log_sigmoid(pre_row[MLSTM_HEADS:2 * MLSTM_HEADS, :])
        cum_row = _mm_sel_rhs(logf_row, triu_ref[...])
        gens += [head(b, h, qk, logi_col, cum_col, logi_row, cum_row) for h in range(MLSTM_HEADS)]
    _lockstep(gens)


def _mlstm(slab, small, rows, cw, cb, igb, fgb, gb_col, ng, tri, triu):
    B, S, _ = slab.shape
    nb = BATCH_BLOCK
    const = lambda shape: pl.BlockSpec(shape, lambda b, c: (0,) * len(shape))
    return pl.pallas_call(
        functools.partial(_mlstm_kernel, nb=nb), grid=(B // nb, S // CHUNK),
        in_specs=[pl.BlockSpec((nb, CHUNK, MLSTM_SLAB), lambda b, c: (b, c, 0)),
                  pl.BlockSpec((nb, CHUNK, LANES), lambda b, c: (b, c, 0)),
                  pl.BlockSpec((nb, 1, ROWS_TOTAL, CHUNK), lambda b, c: (b, c, 0, 0)),
                  const((MLSTM_CONV, 2 * GROUP_W)), const((1, 2 * GROUP_W)), const((1, LANES)),
                  const((1, LANES)), const((2 * MLSTM_HEADS, 1)), const((1, GROUP_W)),
                  const((CHUNK, CHUNK)), const((CHUNK, CHUNK))],
        out_specs=pl.BlockSpec((nb, CHUNK, GROUP_W), lambda b, c: (b, c, 0)),
        out_shape=jax.ShapeDtypeStruct((B, S, GROUP_W), F32),
        scratch_shapes=[pltpu.VMEM((nb, CARRY_ROWS + CHUNK, 2 * GROUP_W), F32),
                        pltpu.VMEM((nb, MLSTM_HEADS, MLSTM_HEAD, MLSTM_HEAD), F32),
                        pltpu.VMEM((nb, MLSTM_HEADS, CARRY_ROWS, MLSTM_HEAD), F32)],
        compiler_params=pltpu.CompilerParams(dimension_semantics=("arbitrary", "arbitrary")),
        name="mlstm")(slab, small, rows, cw, cb, igb, fgb, gb_col, ng, tri, triu)


def _rwkv_slab_width(layer):
    return 2176 if layer == 0 else 2304


def _rwkv_shift_cols(layer):
    return 1664 if layer == 0 else 1792


def _rwkv_kernel(*refs, layer, nb):
    if layer == 0:
        (slab_ref, mu_ref, w0_ref, a0_ref, w2a2_ref, kk_ref, ka_ref, rk_ref, lng_ref, lnb_ref,
         tri_ref, bd_ref, strict_ref, incl_ref, lvl_ref, o_ref, vf_out_ref, buf_ref, st_ref) = refs
    else:
        (slab_ref, vf_ref, mu_ref, w0_ref, a0_ref, w2a2_ref, v0_ref, v2_ref, kk_ref, ka_ref, rk_ref,
         lng_ref, lnb_ref, tri_ref, bd_ref, strict_ref, incl_ref, lvl_ref, o_ref, buf_ref, st_ref) = refs
    ws = _rwkv_shift_cols(layer)
    zoff = _rwkv_slab_width(layer) - GROUP_W
    npair = GROUP_W // LANES

    @pl.when(pl.program_id(1) == 0)
    def _():
        st_ref[...] = jnp.zeros_like(st_ref)
        for b in range(nb):
            buf_ref[b, 0:CARRY_ROWS, :] = jnp.zeros((CARRY_ROWS, ws), F32)

    lane = lax.broadcasted_iota(jnp.int32, (1, LANES), 1)
    masks = _lane_half_masks()
    bd = bd_ref[...]
    strict = strict_ref[...] > 0.5
    incl = incl_ref[...] > 0.5
    masks_b = (_bf(masks[0]), _bf(masks[1]))

    def rows2(t):
        tb = _bf(t)
        return jnp.concatenate([tb * masks_b[0], tb * masks_b[1]], axis=0)

    eye = (lax.broadcasted_iota(jnp.int32, (LANES, LANES), 0)
           == lax.broadcasted_iota(jnp.int32, (LANES, LANES), 1)).astype(F32)

    inv = 1.0 / RWKV_HEAD

    def sequence(b):
        inst = []
        f = slab_ref[b, :, 0:ws]
        (prev,) = _shifted_rows(buf_ref.at[b], f, (1,))
        f = f + mu_ref[...] * (prev - f)
        r = f[:, 0:512]
        k = f[:, 512:1024]
        v = f[:, 1024:1536]
        lora = f[:, 1536:1664]
        lora = jnp.where(lane < RWKV_W_RANK, jnp.tanh(lora), lora)
        wa = _mm(lora, w2a2_ref[...])
        if layer == 0:
            vf_out_ref[b] = v
        else:
            mix = _sigmoid(v0_ref[...] + _mm(f[:, 1664:1792], v2_ref[...]))
            v = v + (vf_ref[b] - v) * mix
        yield
        w_log = -_softplus(-(w0_ref[...] + wa[:, 0:512])) - RWKV_DECAY_OFFSET
        lw = -jnp.exp(w_log)
        a = _sigmoid(a0_ref[...] + wa[:, 512:1024])
        kk = k * kk_ref[...]
        k = k * (1.0 + (a - 1.0) * ka_ref[...])
        cum = _mm_sel_lhs(tri_ref[...], lw)
        ss = [_mm_sel2(kk[:, p * LANES:(p + 1) * LANES] ** 2, bd) for p in range(npair)]
        yield
        last = cum[CHUNK - 1:CHUNK, :]
        e_pos = jnp.exp(cum)
        e_neg = jnp.exp(-cum)
        e_end = jnp.exp(last - cum)
        e_prev = jnp.exp(cum - lw)
        gam = jnp.exp(last)
        for p in range(npair):
            ls = slice(p * LANES, (p + 1) * LANES)
            kk_p = kk[:, ls] / jnp.maximum(jnp.sqrt(ss[p]), 1e-12)
            k_p, r_p, v_p = k[:, ls], r[:, ls], v[:, ls]
            b_p = kk_p * a[:, ls]
            inst.append(dict(
                p=p, ls=ls, r=r_p, k=k_p, v=v_p, gam=gam[:, ls],
                la=rows2(-kk_p * e_prev[:, ls]), lr=rows2(r_p * e_pos[:, ls]),
                rb=rows2(b_p * e_neg[:, ls]), rk=rows2(k_p * e_neg[:, ls]),
                bh=rows2(b_p * e_end[:, ls]), kh=rows2(k_p * e_end[:, ls]), vs=rows2(v_p)))
        yield
        for d in inst:
            aa = _mm_nt(jnp.concatenate([d['la'], d['lr']], axis=0),
                        jnp.concatenate([d['rb'], d['rk']], axis=0))
            d['nab'] = _bf(jnp.where(strict, aa[0:LANES, 0:LANES], 0.0))
            d['aak'] = _bf(jnp.where(strict, aa[0:LANES, LANES:2 * LANES], 0.0))
            d['arb'] = _bf(jnp.where(incl, aa[LANES:2 * LANES, 0:LANES], 0.0))
            d['ark'] = _bf(jnp.where(incl, aa[LANES:2 * LANES, LANES:2 * LANES], 0.0))
            d['t'] = eye + (d['nab'] * lvl_ref[0]).astype(F32)
        yield
        for d in inst:
            d['x'] = _mm(d['aak'], d['vs'])
        for lv in range(1, 6):
            for d in inst:
                d['tb'] = _bf(d['t'])
                d['nt'] = _mm(d['nab'] * lvl_ref[lv], d['tb'])
            yield
            for d in inst:
                d['t'] = d['t'] + _mm(d['tb'], d['nt'])
            yield
        for d in inst:
            d['wu'] = _mm(d['t'], jnp.concatenate([d['la'], _bf(d['x'])], axis=1))
        yield
        for d in inst:
            d['wub'] = _bf(d['wu'])
            qy = _mm(d['arb'], d['wub'])
            d['qt'] = d['lr'].astype(F32) + qy[:, 0:LANES]
            d['y0'] = _mm(d['ark'], d['vs']) + qy[:, LANES:2 * LANES]
        yield
        for d in inst:
            st = st_ref[b, d['p']]
            uy = _mm_nt(jnp.concatenate([d['wub'][:, 0:LANES], _bf(d['qt'])], axis=0), st)
            d['ust'] = uy[0:LANES] + d['wu'][:, LANES:2 * LANES]
            d['st'] = st
            yst = uy[LANES:2 * LANES] + d['y0']
            d['y'] = yst[0:CHUNK] + yst[CHUNK:2 * CHUNK]
        yield
        for d in inst:
            st_ref[b, d['p']] = d['st'] * d['gam'] + _mm_tn(jnp.concatenate([_bf(d['ust']), d['vs']], axis=0),
                                                            jnp.concatenate([d['bh'], d['kh']], axis=0))
            d['mu'] = _mm_sel2(d['y'], bd) * inv
            d['bonus'] = _mm_sel2(d['r'] * d['k'] * rk_ref[:, d['ls']], bd) * d['v']
        yield
        for d in inst:
            d['yc'] = d['y'] - d['mu']
            d['var'] = _mm_sel2(d['yc'] * d['yc'], bd) * inv
        yield
        for d in inst:
            ls = d['ls']
            yn = d['yc'] * lax.rsqrt(d['var'] + RWKV_LN_EPS) * lng_ref[:, ls] + lnb_ref[:, ls]
            z_p = slab_ref[b, :, zoff + d['p'] * LANES:zoff + (d['p'] + 1) * LANES]
            o_ref[b, :, ls] = (yn + d['bonus']) * _silu(z_p)

    _lockstep([sequence(b) for b in range(nb)])


def _rwkv(layer, slab, vf, mu, w0, a0, w2a2, v0, v2, kkw, ka, rk, lng, lnb, tri, bd, strict, incl, lvl):
    B, S, W = slab.shape
    nb = BATCH_BLOCK
    ws = _rwkv_shift_cols(layer)
    const = lambda shape: pl.BlockSpec(shape, lambda b, c: (0,) * len(shape))
    tok = lambda w: pl.BlockSpec((nb, CHUNK, w), lambda b, c: (b, c, 0))
    vecw = const((1, GROUP_W))
    in_specs = [tok(W)]
    args = [slab]
    if layer > 0:
        in_specs.append(tok(GROUP_W))
        args.append(vf)
    in_specs += [const((1, ws)), vecw, vecw, const((LANES, 2 * GROUP_W))]
    args += [mu, w0, a0, w2a2]
    if layer > 0:
        in_specs += [vecw, const((LANES, GROUP_W))]
        args += [v0, v2]
    in_specs += [vecw, vecw, vecw, vecw, vecw, const((CHUNK, CHUNK)), const((LANES, LANES)),
                 const((LANES, LANES)), const((LANES, LANES)), const((6, LANES, LANES))]
    args += [kkw, ka, rk, lng, lnb, tri, bd, strict, incl, lvl]
    out_shape = [jax.ShapeDtypeStruct((B, S, GROUP_W), F32)]
    out_specs = [tok(GROUP_W)]
    if layer == 0:
        out_shape.append(jax.ShapeDtypeStruct((B, S, GROUP_W), F32))
        out_specs.append(tok(GROUP_W))
    res = pl.pallas_call(
        functools.partial(_rwkv_kernel, layer=layer, nb=nb), grid=(B // nb, S // CHUNK),
        in_specs=in_specs, out_specs=out_specs, out_shape=out_shape,
        scratch_shapes=[pltpu.VMEM((nb, CARRY_ROWS + CHUNK, ws), F32),
                        pltpu.VMEM((nb, GROUP_W // LANES, LANES, LANES), F32)],
        compiler_params=pltpu.CompilerParams(dimension_semantics=("arbitrary", "arbitrary"),
                                             vmem_limit_bytes=VMEM_LIMIT_PROJ),
        name="rwkv")(*args)
    return (res[0], res[1]) if layer == 0 else (res[0], vf)


def _np_consts():
    i = np.arange(CHUNK)
    tri = (i[None, :] <= i[:, None]).astype(np.float32)
    t = np.arange(LANES)
    same = (t[:, None] // CHUNK) == (t[None, :] // CHUNK)
    strict = (same & (t[None, :] < t[:, None])).astype(np.float32)
    incl = (same & (t[None, :] <= t[:, None])).astype(np.float32)
    bd = same.astype(np.float32)
    lvl = np.stack([(((t[:, None] >> l) == (t[None, :] >> l))
                     & ((t[:, None] >> (l - 1)) != (t[None, :] >> (l - 1)))).astype(np.float32)
                    for l in range(1, 7)])
    ex = np.zeros((LANES, GROUP_W), np.float32)
    for h in range(SSD_HEADS):
        ex[SMALL_DT + h, h * SSD_HEADDIM:(h + 1) * SSD_HEADDIM] = 1.0
    return dict(tri=tri, triu=tri.T.copy(), strict=strict, incl=incl, bd=bd, lvl=lvl, ex=ex)


def _pad_rows(w, height):
    return jnp.pad(w, ((0, height - w.shape[0]), (0, 0)))


def _pad_vec(v, width, offset=0):
    v = v.reshape(1, -1)
    return jnp.pad(v, ((0, 0), (offset, width - offset - v.shape[1])))


BF16_ROW_TILE = 16


def _layer(x2, v_first, layer, p, B, S, consts, g_final):
    sw = 3 * GROUP_W + RWKV_W_RANK + RWKV_A_RANK + (RWKV_V_RANK if layer > 0 else 0)
    names = ['gla_q', 'gla_k', 'gla_v', 'gla_gk', 'gla_z', 'rwkv_shift', 'rwkv_z', 'ssd_xbc', 'ssd_dt',
             'pad0', 'ssd_z', 'mlstm_qk', 'mlstm_v', 'mlstm_i', 'mlstm_f', 'pad1', 'mlstm_o', 'mlstm_z']
    widths = [256, 256, 512, 16, 512, sw, 512, SSD_XBC, SSD_HEADS, 8, 512, 1024, 512, 4, 4, 8, 512, 512]
    offs = np.concatenate([[0], np.cumsum(widths)]).tolist()
    off = dict(zip(names, offs[:-1]))
    assert all(off[n] % BF16_ROW_TILE == 0 for n in names if n not in ('mlstm_f', 'pad0', 'pad1'))
    w_in = p['w_in']
    zpad = jnp.zeros((D_MODEL, 8), w_in.dtype)
    cut0, cut1 = off['pad0'], off['pad1'] - 8
    w_t = _bf(jnp.concatenate([w_in[:, :cut0], zpad, w_in[:, cut0:cut1], zpad, w_in[:, cut1:]], axis=1).T)
    rows = lambda name, n: w_t[off[name]:off[name] + n, :]
    g = p['norm_g'].reshape(1, D_MODEL)
    tri, triu = consts['tri'], consts['triu']
    rows_of = lambda t: t.reshape(t.shape[0], B, S // CHUNK, CHUNK).transpose(1, 2, 0, 3)

    w_small = _pad_rows(jnp.concatenate([rows('gla_gk', GLA_RANK), rows('ssd_dt', SSD_HEADS),
                                         rows('mlstm_i', 2 * MLSTM_HEADS)], axis=0), LANES)
    w_rows = jnp.concatenate([rows('ssd_dt', SSD_HEADS), rows('mlstm_i', 2 * MLSTM_HEADS)], axis=0)
    mu = p['rwkv_mu']
    if layer == 0:
        pieces, zero_lanes = [(off['rwkv_shift'], sw + GROUP_W, 0)], ()
        mu_p = mu.reshape(1, -1)
    else:
        pieces = [(off['rwkv_shift'], sw, 0), (off['rwkv_z'], GROUP_W, 1792)]
        zero_lanes = ((1664, 1792),)
        mu_p = _pad_vec(mu, 1792)
    slab, small, gate_rows = _in_proj(x2, g, w_t, pieces, _rwkv_slab_width(layer), zero_lanes,
                                      w_small, w_rows)
    slab = slab.reshape(B, S, _rwkv_slab_width(layer))
    small = small.reshape(B, S, LANES)
    gate_rows = rows_of(gate_rows)
    w2a2 = jnp.zeros((LANES, 2 * GROUP_W), F32)
    w2a2 = w2a2.at[0:RWKV_W_RANK, 0:GROUP_W].set(p['rwkv_w2'])
    w2a2 = w2a2.at[RWKV_W_RANK:, GROUP_W:].set(p['rwkv_a2'])
    vec = lambda t: t.reshape(1, GROUP_W)
    v0 = v2 = None
    if layer > 0:
        v0 = vec(p['rwkv_v0'])
        v2 = _bf(jnp.pad(p['rwkv_v2'], ((0, LANES - RWKV_V_RANK), (0, 0))))
    y_rwkv, v_first = _rwkv(layer, slab, v_first, mu_p, vec(p['rwkv_w0']), vec(p['rwkv_a0']), _bf(w2a2),
                            v0, v2, vec(p['rwkv_k_k']), vec(p['rwkv_k_a']), vec(p['rwkv_r_k']),
                            vec(p['rwkv_ln_g']), vec(p['rwkv_ln_b']), _bf(tri), _bf(consts['bd']),
                            consts['strict'], consts['incl'], _bf(consts['lvl']))

    slab = _in_proj(x2, g, w_t, [(off['gla_q'], 1024, 0), (off['gla_z'], GROUP_W, 1024)], GLA_SLAB)
    w2p = _bf(jnp.pad(p['gla_gk_w2'], ((SMALL_GK, LANES - SMALL_GK - GLA_RANK), (0, 0))))
    y_gla = _gla(slab.reshape(B, S, GLA_SLAB), small, w2p, p['gla_gk_b'].reshape(1, -1),
                 p['gla_norm_g'].reshape(1, -1), _bf(tri))

    slab = _in_proj(x2, g, w_t, [(off['ssd_xbc'], SSD_XBC, 0), (off['ssd_z'], GROUP_W, SSD_XBC)], SSD_SLAB)
    a_neg = -jnp.exp(p['ssd_a_log'])
    y_ssd = _ssd(slab.reshape(B, S, SSD_SLAB), small, gate_rows, p['ssd_conv_w'],
                 p['ssd_conv_b'].reshape(1, -1),
                 _pad_vec(p['ssd_dt_bias'], LANES, SMALL_DT), _pad_vec(a_neg, LANES, SMALL_DT),
                 p['ssd_dt_bias'].reshape(-1, 1), a_neg.reshape(-1, 1),
                 jnp.repeat(p['ssd_d'], SSD_HEADDIM).reshape(1, -1), p['ssd_norm_g'].reshape(1, -1),
                 _bf(tri), _bf(triu), _bf(consts['ex']))

    slab = _in_proj(x2, g, w_t, [(off['mlstm_qk'], 3 * GROUP_W, 0), (off['mlstm_o'], 2 * GROUP_W, 3 * GROUP_W)],
                    MLSTM_SLAB)
    gb_col = jnp.concatenate([p['mlstm_ig_b'], p['mlstm_fg_b']]).reshape(-1, 1)
    y_ml = _mlstm(slab.reshape(B, S, MLSTM_SLAB), small, gate_rows, p['mlstm_conv_w'],
                  p['mlstm_conv_b'].reshape(1, -1), _pad_vec(p['mlstm_ig_b'], LANES, SMALL_I),
                  _pad_vec(p['mlstm_fg_b'], LANES, SMALL_F), gb_col, p['mlstm_norm_g'].reshape(1, -1),
                  _bf(tri), _bf(triu))

    T = B * S
    ys = [y.reshape(T, GROUP_W) for y in (y_gla, y_rwkv, y_ssd, y_ml)]
    return _out_proj(x2, ys, _bf(p['w_out']), g_final), v_first


_PARAM_NAMES_0 = ['norm_g', 'w_in', 'w_out', 'gla_gk_w2', 'gla_gk_b', 'gla_norm_g', 'rwkv_mu', 'rwkv_w0',
                  'rwkv_w2', 'rwkv_a0', 'rwkv_a2', 'rwkv_k_k', 'rwkv_k_a', 'rwkv_r_k', 'rwkv_ln_g',
                  'rwkv_ln_b', 'ssd_conv_w', 'ssd_conv_b', 'ssd_dt_bias', 'ssd_a_log', 'ssd_d',
                  'ssd_norm_g', 'mlstm_conv_w', 'mlstm_conv_b', 'mlstm_ig_b', 'mlstm_fg_b', 'mlstm_norm_g']
_PARAM_NAMES_1 = (_PARAM_NAMES_0[:11] + ['rwkv_v0', 'rwkv_v2'] + _PARAM_NAMES_0[11:])


def kernel(x,
           norm_g_0, w_in_0, w_out_0, gla_gk_w2_0, gla_gk_b_0, gla_norm_g_0,
           rwkv_mu_0, rwkv_w0_0, rwkv_w2_0, rwkv_a0_0, rwkv_a2_0,
           rwkv_k_k_0, rwkv_k_a_0, rwkv_r_k_0, rwkv_ln_g_0, rwkv_ln_b_0,
           ssd_conv_w_0, ssd_conv_b_0, ssd_dt_bias_0, ssd_a_log_0, ssd_d_0, ssd_norm_g_0,
           mlstm_conv_w_0, mlstm_conv_b_0, mlstm_ig_b_0, mlstm_fg_b_0, mlstm_norm_g_0,
           norm_g_1, w_in_1, w_out_1, gla_gk_w2_1, gla_gk_b_1, gla_norm_g_1,
           rwkv_mu_1, rwkv_w0_1, rwkv_w2_1, rwkv_a0_1, rwkv_a2_1, rwkv_v0_1, rwkv_v2_1,
           rwkv_k_k_1, rwkv_k_a_1, rwkv_r_k_1, rwkv_ln_g_1, rwkv_ln_b_1,
           ssd_conv_w_1, ssd_conv_b_1, ssd_dt_bias_1, ssd_a_log_1, ssd_d_1, ssd_norm_g_1,
           mlstm_conv_w_1, mlstm_conv_b_1, mlstm_ig_b_1, mlstm_fg_b_1, mlstm_norm_g_1,
           final_norm_g):
    params = (norm_g_0, w_in_0, w_out_0, gla_gk_w2_0, gla_gk_b_0, gla_norm_g_0,
              rwkv_mu_0, rwkv_w0_0, rwkv_w2_0, rwkv_a0_0, rwkv_a2_0,
              rwkv_k_k_0, rwkv_k_a_0, rwkv_r_k_0, rwkv_ln_g_0, rwkv_ln_b_0,
              ssd_conv_w_0, ssd_conv_b_0, ssd_dt_bias_0, ssd_a_log_0, ssd_d_0, ssd_norm_g_0,
              mlstm_conv_w_0, mlstm_conv_b_0, mlstm_ig_b_0, mlstm_fg_b_0, mlstm_norm_g_0,
              norm_g_1, w_in_1, w_out_1, gla_gk_w2_1, gla_gk_b_1, gla_norm_g_1,
              rwkv_mu_1, rwkv_w0_1, rwkv_w2_1, rwkv_a0_1, rwkv_a2_1, rwkv_v0_1, rwkv_v2_1,
              rwkv_k_k_1, rwkv_k_a_1, rwkv_r_k_1, rwkv_ln_g_1, rwkv_ln_b_1,
              ssd_conv_w_1, ssd_conv_b_1, ssd_dt_bias_1, ssd_a_log_1, ssd_d_1, ssd_norm_g_1,
              mlstm_conv_w_1, mlstm_conv_b_1, mlstm_ig_b_1, mlstm_fg_b_1, mlstm_norm_g_1,
              final_norm_g)
    n0 = len(_PARAM_NAMES_0)
    n1 = len(_PARAM_NAMES_1)
    p0 = dict(zip(_PARAM_NAMES_0, params[:n0]))
    p1 = dict(zip(_PARAM_NAMES_1, params[n0:n0 + n1]))
    final_norm_g = params[n0 + n1]
    B, S, _ = x.shape
    consts = {k: jnp.asarray(v) for k, v in _np_consts().items()}
    x2 = x.reshape(B * S, D_MODEL)
    x2, v_first = _layer(x2, None, 0, p0, B, S, consts, None)
    x2, _ = _layer(x2, v_first, 1, p1, B, S, consts, final_norm_g.reshape(1, D_MODEL))
    return x2.reshape(B, S, D_MODEL)
```

```python
import functools

import numpy as np
import jax
import jax.numpy as jnp
from jax import lax
from jax.experimental import pallas as pl
from jax.experimental.pallas import tpu as pltpu

F32 = jnp.float32
BF16 = jnp.bfloat16

D_MODEL = 2048
CHUNK = 64
GROUP_W = 512
NORM_EPS = 1e-6
LANES = 128
CARRY_ROWS = 8

GLA_HEADS, GLA_DK, GLA_DV, GLA_RANK = 4, 64, 128, 16
GLA_GATE_NORMALIZER = 16.0
RWKV_HEAD, RWKV_W_RANK, RWKV_A_RANK, RWKV_V_RANK = 64, 64, 64, 32
RWKV_LN_EPS = 64e-5
RWKV_DECAY_OFFSET = 0.5
SSD_HEADS, SSD_HEADDIM, SSD_STATE, SSD_CONV = 8, 64, 128, 4
SSD_XBC = 1024
MLSTM_HEADS, MLSTM_HEAD, MLSTM_CONV = 4, 128, 4

ROW_TILE_IN = 512
SMALL_GK, SMALL_DT, SMALL_I, SMALL_F = 0, 16, 24, 28
ROWS_DT, ROWS_IF, ROWS_TOTAL = 0, 8, 16
ROW_TILE_OUT = 512
VMEM_LIMIT_PROJ = 48 * 2**20


def _bf(x):
    return x.astype(BF16)


def _mm(a, b):
    return jnp.dot(_bf(a), _bf(b), preferred_element_type=F32)


def _mm_nt(a, b):
    return lax.dot_general(_bf(a), _bf(b), (((1,), (1,)), ((), ())), preferred_element_type=F32)


def _mm_tn(a, b):
    return lax.dot_general(_bf(a), _bf(b), (((0,), (0,)), ((), ())), preferred_element_type=F32)


def _split3(x):
    hi = _bf(x)
    r1 = x - hi.astype(F32)
    mid = _bf(r1)
    lo = _bf(r1 - mid.astype(F32))
    return hi, mid, lo


def _mm_sel_rhs(x, sel):
    hi, mid, lo = _split3(x)
    d = lambda a: jnp.dot(a, sel, preferred_element_type=F32)
    return d(hi) + d(mid) + d(lo)


def _mm_sel2(x, sel):
    hi = _bf(x)
    lo = _bf(x - hi.astype(F32))
    d = lambda a: jnp.dot(a, sel, preferred_element_type=F32)
    return d(hi) + d(lo)


def _mm_sel_lhs(sel, x):
    hi, mid, lo = _split3(x)
    d = lambda a: jnp.dot(sel, a, preferred_element_type=F32)
    return d(hi) + d(mid) + d(lo)


def _rowsum(x):
    ones = jnp.ones((x.shape[-1], LANES), BF16)
    hi = _bf(x)
    lo = _bf(x - hi.astype(F32))
    return (jnp.dot(hi, ones, preferred_element_type=F32)
            + jnp.dot(lo, ones, preferred_element_type=F32))


def _sigmoid(x):
    return 1.0 / (1.0 + jnp.exp(-x))


def _silu(x):
    return x * _sigmoid(x)


def _softplus(x):
    return jnp.maximum(x, 0.0) + jnp.log1p(jnp.exp(-jnp.abs(x)))


def _log_sigmoid(x):
    return -_softplus(-x)


def _lane_half_masks():
    lane = lax.broadcasted_iota(jnp.int32, (1, LANES), 1)
    lo = (lane < LANES // 2).astype(F32)
    return lo, 1.0 - lo


def _causal(n):
    r = lax.broadcasted_iota(jnp.int32, (n, n), 0)
    c = lax.broadcasted_iota(jnp.int32, (n, n), 1)
    return c <= r


def _shifted_rows(buf_ref, cur, offsets):
    buf_ref[CARRY_ROWS:CARRY_ROWS + CHUNK, :] = cur
    outs = [buf_ref[CARRY_ROWS - off:CARRY_ROWS - off + CHUNK, :] for off in offsets]
    tail = buf_ref[CHUNK:CHUNK + CARRY_ROWS, :]
    buf_ref[0:CARRY_ROWS, :] = tail
    return outs


def _in_proj_kernel(*refs, pieces, zero_lanes, has_small):
    x_ref, g_ref = refs[0], refs[1]
    w_refs = refs[2:2 + len(pieces)]
    rest = refs[2 + len(pieces):]
    if has_small:
        ws_ref, wr_ref, o_ref, os_ref, or_ref = rest
    else:
        (o_ref,) = rest
    nt = lambda a, b: lax.dot_general(a, b, (((1,), (1,)), ((), ())), preferred_element_type=F32)
    tm = x_ref.shape[0]
    for lo, hi in zero_lanes:
        o_ref[:, lo:hi] = jnp.zeros((tm, hi - lo), F32)
    for half in range(2):
        rs = slice(half * (tm // 2), (half + 1) * (tm // 2))
        x = x_ref[rs, :]
        h = x * lax.rsqrt(jnp.mean(x * x, axis=-1, keepdims=True) + NORM_EPS) * g_ref[...]
        hb = _bf(h)
        for w_ref, (_, n_rows, lane_off) in zip(w_refs, pieces):
            o_ref[rs, lane_off:lane_off + n_rows] = nt(hb, w_ref[...])
        if has_small:
            os_ref[rs, :] = nt(hb, ws_ref[...])
            or_ref[:, rs] = nt(wr_ref[...], hb)


def _in_proj(x2, g, w_t, pieces, width, zero_lanes=(), w_small=None, w_rows=None):
    T = x2.shape[0]
    tm = ROW_TILE_IN
    const2 = lambda shape: pl.BlockSpec(shape, lambda i: (0, 0))
    in_specs = [pl.BlockSpec((tm, D_MODEL), lambda i: (i, 0)), const2((1, D_MODEL))]
    args = [x2, g]
    for off, n_rows, _ in pieces:
        in_specs.append(pl.BlockSpec((pl.Element(n_rows), pl.Element(D_MODEL)), lambda i, off=off: (off, 0)))
        args.append(w_t)
    out_shape = [jax.ShapeDtypeStruct((T, width), F32)]
    out_specs = [pl.BlockSpec((tm, width), lambda i: (i, 0))]
    if w_small is not None:
        r = w_rows.shape[0]
        in_specs += [const2((LANES, D_MODEL)), const2((r, D_MODEL))]
        args += [w_small, w_rows]
        out_shape += [jax.ShapeDtypeStruct((T, LANES), F32), jax.ShapeDtypeStruct((r, T), F32)]
        out_specs += [pl.BlockSpec((tm, LANES), lambda i: (i, 0)), pl.BlockSpec((r, tm), lambda i: (0, i))]
    res = pl.pallas_call(
        functools.partial(_in_proj_kernel, pieces=tuple(pieces), zero_lanes=tuple(zero_lanes),
                          has_small=w_small is not None),
        grid=(T // tm,), in_specs=in_specs, out_specs=out_specs, out_shape=out_shape,
        compiler_params=pltpu.CompilerParams(dimension_semantics=("arbitrary",),
                                             vmem_limit_bytes=VMEM_LIMIT_PROJ),
        name="in_proj")(*args)
    return res if w_small is not None else res[0]


def _out_proj_kernel(*refs, final):
    if final:
        x_ref, y0, y1, y2, y3, w_ref, g_ref, o_ref = refs
    else:
        x_ref, y0, y1, y2, y3, w_ref, o_ref = refs
    acc = x_ref[...]
    for gi, y in enumerate((y0, y1, y2, y3)):
        acc = acc + jnp.dot(_bf(y[...]), w_ref[gi * GROUP_W:(gi + 1) * GROUP_W, :],
                            preferred_element_type=F32)
    if final:
        acc = acc * lax.rsqrt(jnp.mean(acc * acc, axis=-1, keepdims=True) + NORM_EPS) * g_ref[...]
    o_ref[...] = acc


def _out_proj(x2, ys, w, g_final=None):
    T = x2.shape[0]
    tm = ROW_TILE_OUT
    final = g_final is not None
    in_specs = [pl.BlockSpec((tm, D_MODEL), lambda i: (i, 0))]
    in_specs += [pl.BlockSpec((tm, GROUP_W), lambda i: (i, 0)) for _ in range(4)]
    in_specs += [pl.BlockSpec((D_MODEL, D_MODEL), lambda i: (0, 0))]
    args = [x2, *ys, w]
    if final:
        in_specs.append(pl.BlockSpec((1, D_MODEL), lambda i: (0, 0)))
        args.append(g_final)
    return pl.pallas_call(
        functools.partial(_out_proj_kernel, final=final),
        grid=(T // tm,), in_specs=in_specs,
        out_specs=pl.BlockSpec((tm, D_MODEL), lambda i: (i, 0)),
        out_shape=jax.ShapeDtypeStruct((T, D_MODEL), F32),
        compiler_params=pltpu.CompilerParams(dimension_semantics=("arbitrary",),
                                             vmem_limit_bytes=VMEM_LIMIT_PROJ),
        name="out_proj")(*args)


GLA_SLAB = 1536


BATCH_BLOCK = 4


def _lockstep(gens):
    gens = list(gens)
    while gens:
        alive = []
        for g in gens:
            try:
                next(g)
                alive.append(g)
            except StopIteration:
                pass
        gens = alive


def _gla_kernel(slab_ref, small_ref, w2_ref, gkb_ref, ng_ref, tri_ref, o_ref, st_ref, *, nb):
    @pl.when(pl.program_id(1) == 0)
    def _():
        st_ref[...] = jnp.zeros_like(st_ref)

    causal = _causal(CHUNK)
    masks = _lane_half_masks()

    def head(b, h, qg, kg, kd, dec):
        p, j = divmod(h, 2)
        ls = slice(p * LANES, (p + 1) * LANES)
        qm = qg[:, ls] * masks[j]
        att = jnp.where(causal, _mm_nt(qm, kg[:, ls]), 0.0)
        yield
        v_h = slab_ref[b, :, 512 + h * GLA_DV:512 + (h + 1) * GLA_DV]
        st = st_ref[b, h]
        o = _mm(att, v_h) + _mm_nt(qm, st)
        st_ref[b, h] = st * dec[:, ls] + _mm_tn(v_h, kd[:, ls] * masks[j])
        yield
        ms = jnp.mean(o * o, axis=-1, keepdims=True)
        yield
        o = o * lax.rsqrt(ms + NORM_EPS)
        o = o * ng_ref[:, h * GLA_DV:(h + 1) * GLA_DV]
        z_h = slab_ref[b, :, 1024 + h * GLA_DV:1024 + (h + 1) * GLA_DV]
        o_ref[b, :, h * GLA_DV:(h + 1) * GLA_DV] = o * _silu(z_h)

    gens = []
    for b in range(nb):
        q = slab_ref[b, :, 0:256] * (GLA_DK ** -0.5)
        k = slab_ref[b, :, 256:512]
        gk = _mm(small_ref[b], w2_ref[...]) + gkb_ref[...]
        log_a = _log_sigmoid(gk) / GLA_GATE_NORMALIZER
        cum = _mm_sel_lhs(tri_ref[...], log_a)
        last = cum[CHUNK - 1:CHUNK, :]
        qg = q * jnp.exp(cum)
        kg = k * jnp.exp(-cum)
        kd = k * jnp.exp(last - cum)
        dec = jnp.exp(last)
        gens += [head(b, h, qg, kg, kd, dec) for h in range(GLA_HEADS)]
    _lockstep(gens)


def _gla(slab, small, w2p, gkb, ng, tri):
    B, S, _ = slab.shape
    nb = BATCH_BLOCK
    const = lambda shape: pl.BlockSpec(shape, lambda b, c: (0,) * len(shape))
    return pl.pallas_call(
        functools.partial(_gla_kernel, nb=nb), grid=(B // nb, S // CHUNK),
        in_specs=[pl.BlockSpec((nb, CHUNK, GLA_SLAB), lambda b, c: (b, c, 0)),
                  pl.BlockSpec((nb, CHUNK, LANES), lambda b, c: (b, c, 0)),
                  const((LANES, 256)), const((1, 256)), const((1, GROUP_W)), const((CHUNK, CHUNK))],
        out_specs=pl.BlockSpec((nb, CHUNK, GROUP_W), lambda b, c: (b, c, 0)),
        out_shape=jax.ShapeDtypeStruct((B, S, GROUP_W), F32),
        scratch_shapes=[pltpu.VMEM((nb, GLA_HEADS, GLA_DV, LANES), F32)],
        compiler_params=pltpu.CompilerParams(dimension_semantics=("arbitrary", "arbitrary")),
        name="gla")(slab, small, w2p, gkb, ng, tri)


SSD_SLAB = 1536


def _ssd_kernel(slab_ref, small_ref, rows_ref, cw_ref, cb_ref, dtb_ref, a_ref, dtb_col_ref, a_col_ref,
                dskip_ref, ng_ref, tri_ref, triu2_ref, ex_ref, o_ref, buf_ref, st_ref, *, nb):
    @pl.when(pl.program_id(1) == 0)
    def _():
        st_ref[...] = jnp.zeros_like(st_ref)
        for b in range(nb):
            buf_ref[b, 0:CARRY_ROWS, :] = jnp.zeros((CARRY_ROWS, SSD_XBC), F32)

    row_i = lax.broadcasted_iota(jnp.int32, (CHUNK, LANES), 0)
    col_i = lax.broadcasted_iota(jnp.int32, (CHUNK, LANES), 1)
    causal2 = jnp.bitwise_and(col_i, CHUNK - 1) <= row_i
    masks = _lane_half_masks()
    masks_b = (_bf(masks[0]), _bf(masks[1]))

    def pair(b, p, xbc, cum_b, cum_row2, xdt, xw, ecum, dec):
        g = p // 2
        ls = slice(p * LANES, (p + 1) * LANES)
        bm = xbc[:, 512 + g * SSD_STATE:512 + (g + 1) * SSD_STATE]
        cm = xbc[:, 768 + g * SSD_STATE:768 + (g + 1) * SSD_STATE]
        bmb = _bf(bm)
        cbm2 = _mm_nt(cm, jnp.concatenate([bmb, bmb], axis=0))
        st = st_ref[b, p]
        y = _mm(cm, st) * ecum[:, ls]
        st_ref[b, p] = st * dec[:, ls] + _mm_tn(bmb, xw[:, ls])
        lmat = jnp.exp(jnp.where(causal2, cum_b[:, ls] - cum_row2[p:p + 1, :], -jnp.inf))
        yield
        xb = _bf(xdt[:, ls])
        xs = jnp.concatenate([xb * masks_b[0], xb * masks_b[1]], axis=0)
        y = y + _mm(cbm2 * lmat, xs)
        yield
        y = y + dskip_ref[:, ls] * xbc[:, ls]
        o_ref[b, :, ls] = y * _silu(slab_ref[b, :, 1024 + p * LANES:1024 + (p + 1) * LANES])

    gens = []
    for b in range(nb):
        taps = _shifted_rows(buf_ref.at[b], slab_ref[b, :, 0:SSD_XBC], (3, 2, 1, 0))
        xbc = cb_ref[...]
        for j in range(SSD_CONV):
            xbc = xbc + taps[j] * cw_ref[j:j + 1, :]
        xbc = _silu(xbc)
        dt_col = _softplus(small_ref[b] + dtb_ref[...])
        cum_col = _mm_sel_lhs(tri_ref[...], dt_col * a_ref[...])
        dt_row = _softplus(rows_ref[b, 0, ROWS_DT:ROWS_DT + SSD_HEADS, :] + dtb_col_ref[...])
        da_row = dt_row * a_col_ref[...]
        cum_row2 = (_mm_sel_rhs(da_row[0:SSD_HEADS // 2], triu2_ref[0])
                    + _mm_sel_rhs(da_row[SSD_HEADS // 2:SSD_HEADS], triu2_ref[1]))
        dt_b = _mm_sel_rhs(dt_col, ex_ref[...])
        cum_b = _mm_sel_rhs(cum_col, ex_ref[...])
        last_b = cum_b[CHUNK - 1:CHUNK, :]
        xdt = xbc[:, 0:512] * dt_b
        xw = xdt * jnp.exp(last_b - cum_b)
        ecum = jnp.exp(cum_b)
        dec = jnp.exp(last_b)
        gens += [pair(b, p, xbc, cum_b, cum_row2, xdt, xw, ecum, dec) for p in range(SSD_HEADS // 2)]
    _lockstep(gens)
    for b in range(nb):
        y = o_ref[b]
        inv = lax.rsqrt(_rowsum(y * y) * (1.0 / GROUP_W) + NORM_EPS)
        for p in range(GROUP_W // LANES):
            ls = slice(p * LANES, (p + 1) * LANES)
            o_ref[b, :, ls] = y[:, ls] * inv * ng_ref[:, ls]


def _ssd(slab, small, rows, cw, cb, dtb, a, dtb_col, a_col, dskip, ng, tri, triu, ex):
    B, S, _ = slab.shape
    nb = BATCH_BLOCK
    const = lambda shape: pl.BlockSpec(shape, lambda b, c: (0,) * len(shape))
    return pl.pallas_call(
        functools.partial(_ssd_kernel, nb=nb), grid=(B // nb, S // CHUNK),
        in_specs=[pl.BlockSpec((nb, CHUNK, SSD_SLAB), lambda b, c: (b, c, 0)),
                  pl.BlockSpec((nb, CHUNK, LANES), lambda b, c: (b, c, 0)),
                  pl.BlockSpec((nb, 1, ROWS_TOTAL, CHUNK), lambda b, c: (b, c, 0, 0)),
                  const((SSD_CONV, SSD_XBC)), const((1, SSD_XBC)), const((1, LANES)), const((1, LANES)),
                  const((SSD_HEADS, 1)), const((SSD_HEADS, 1)), const((1, GROUP_W)), const((1, GROUP_W)),
                  const((CHUNK, CHUNK)), const((2, CHUNK, LANES)), const((LANES, GROUP_W))],
        out_specs=pl.BlockSpec((nb, CHUNK, GROUP_W), lambda b, c: (b, c, 0)),
        out_shape=jax.ShapeDtypeStruct((B, S, GROUP_W), F32),
        scratch_shapes=[pltpu.VMEM((nb, CARRY_ROWS + CHUNK, SSD_XBC), F32),
                        pltpu.VMEM((nb, SSD_HEADS // 2, SSD_STATE, LANES), F32)],
        compiler_params=pltpu.CompilerParams(dimension_semantics=("arbitrary", "arbitrary")),
        name="ssd")(slab, small, rows, cw, cb, dtb, a, dtb_col, a_col, dskip, ng, tri, triu, ex)


MLSTM_SLAB = 2560


def _mlstm_kernel(slab_ref, small_ref, rows_ref, cw_ref, cb_ref, igb_ref, fgb_ref, gb_col_ref, ng_ref,
                  tri_ref, triu_ref, exi_ref, exf_ref, o_ref, buf_ref, c_ref, nm_ref, *, nb):
    @pl.when(pl.program_id(1) == 0)
    def _():
        c_ref[...] = jnp.zeros_like(c_ref)
        nm_ref[...] = jnp.zeros_like(nm_ref)
        for b in range(nb):
            buf_ref[b, 0:CARRY_ROWS, :] = jnp.zeros((CARRY_ROWS, 2 * GROUP_W), F32)

    causal = _causal(CHUNK)

    def head(b, h, qk, li_b, ci_b, logi_row, cum_row):
        ls = slice(h * MLSTM_HEAD, (h + 1) * MLSTM_HEAD)
        q = qk[:, ls]
        k = qk[:, GROUP_W + h * MLSTM_HEAD:GROUP_W + (h + 1) * MLSTM_HEAD] * (MLSTM_HEAD ** -0.5)
        v = slab_ref[b, :, 1024 + h * MLSTM_HEAD:1024 + (h + 1) * MLSTM_HEAD]
        ci = ci_b[:, ls]
        li = li_b[:, ls]
        cr = cum_row[h:h + 1, :]
        lir = logi_row[h:h + 1, :]
        last = ci[CHUNK - 1:CHUNK, :]
        c_prev = c_ref[b, h]
        n_prev = nm_ref[b, h, 0:1, :]
        m_prev = nm_ref[b, h, 1:2, :]

        g = last - ci + li
        g_max = jnp.max(g, axis=0, keepdims=True)
        log_d = jnp.where(causal, ci[:, 0:CHUNK] - cr + lir, -jnp.inf)
        row_max = jnp.max(log_d, axis=-1, keepdims=True)
        qk_h = _mm_nt(q, k)
        qc = _mm(q, c_prev)
        qn = _rowsum(q * n_prev)
        yield
        kw = k * jnp.exp(g - g_max)
        c_loc = _mm_tn(kw, v)
        n_loc = jnp.sum(kw, axis=0, keepdims=True)
        m_new = jnp.maximum(last + m_prev, g_max)
        a_old = jnp.exp(last + m_prev - m_new)
        a_new = jnp.exp(g_max - m_new)
        c_ref[b, h] = a_old * c_prev + a_new * c_loc
        nm_ref[b, h, 0:1, :] = a_old * n_prev + a_new * n_loc
        nm_ref[b, h, 1:2, :] = m_new
        m_inter = ci + m_prev
        m_l = jnp.maximum(m_inter, row_max)
        wqk = qk_h * jnp.exp(log_d - m_l[:, 0:CHUNK])
        w_inter = jnp.exp(m_inter - m_l)
        num = _mm(wqk, v) + w_inter * qc
        den = _rowsum(wqk) + w_inter * qn
        yield
        den = jnp.maximum(jnp.abs(den), jnp.exp(-m_l))
        hh = num / den * _sigmoid(slab_ref[b, :, 1536 + h * MLSTM_HEAD:1536 + (h + 1) * MLSTM_HEAD])
        mu = _rowsum(hh) * (1.0 / MLSTM_HEAD)
        yield
        yc = hh - mu
        var = _rowsum(yc * yc) * (1.0 / MLSTM_HEAD)
        yield
        hh = yc * lax.rsqrt(var + NORM_EPS) * ng_ref[:, ls]
        o_ref[b, :, ls] = hh * _silu(slab_ref[b, :, 2048 + h * MLSTM_HEAD:2048 + (h + 1) * MLSTM_HEAD])

    gens = []
    for b in range(nb):
        taps = _shifted_rows(buf_ref.at[b], slab_ref[b, :, 0:2 * GROUP_W], (3, 2, 1, 0))
        qk = cb_ref[...]
        for j in range(MLSTM_CONV):
            qk = qk + taps[j] * cw_ref[j:j + 1, :]
        qk = _silu(qk)
        logi_col = small_ref[b] + igb_ref[...]
        logf_col = _log_sigmoid(small_ref[b] + fgb_ref[...])
        cum_col = _mm_sel_lhs(tri_ref[...], logf_col)
        li_b = _mm_sel_rhs(logi_col, exi_ref[...])
        ci_b = _mm_sel_rhs(cum_col, exf_ref[...])
        pre_row = rows_ref[b, 0, ROWS_IF:ROWS_IF + 2 * MLSTM_HEADS, :] + gb_col_ref[...]
        logi_row = pre_row[0:MLSTM_HEADS, :]
        logf_row = _log_sigmoid(pre_row[MLSTM_HEADS:2 * MLSTM_HEADS, :])
        cum_row = _mm_sel_rhs(logf_row, triu_ref[...])
        gens += [head(b, h, qk, li_b, ci_b, logi_row, cum_row) for h in range(MLSTM_HEADS)]
    _lockstep(gens)


def _mlstm(slab, small, rows, cw, cb, igb, fgb, gb_col, ng, tri, triu, exi, exf):
    B, S, _ = slab.shape
    nb = BATCH_BLOCK
    const = lambda shape: pl.BlockSpec(shape, lambda b, c: (0,) * len(shape))
    return pl.pallas_call(
        functools.partial(_mlstm_kernel, nb=nb), grid=(B // nb, S // CHUNK),
        in_specs=[pl.BlockSpec((nb, CHUNK, MLSTM_SLAB), lambda b, c: (b, c, 0)),
                  pl.BlockSpec((nb, CHUNK, LANES), lambda b, c: (b, c, 0)),
                  pl.BlockSpec((nb, 1, ROWS_TOTAL, CHUNK), lambda b, c: (b, c, 0, 0)),
                  const((MLSTM_CONV, 2 * GROUP_W)), const((1, 2 * GROUP_W)), const((1, LANES)),
                  const((1, LANES)), const((2 * MLSTM_HEADS, 1)), const((1, GROUP_W)),
                  const((CHUNK, CHUNK)), const((CHUNK, CHUNK)),
                  const((LANES, GROUP_W)), const((LANES, GROUP_W))],
        out_specs=pl.BlockSpec((nb, CHUNK, GROUP_W), lambda b, c: (b, c, 0)),
        out_shape=jax.ShapeDtypeStruct((B, S, GROUP_W), F32),
        scratch_shapes=[pltpu.VMEM((nb, CARRY_ROWS + CHUNK, 2 * GROUP_W), F32),
                        pltpu.VMEM((nb, MLSTM_HEADS, MLSTM_HEAD, MLSTM_HEAD), F32),
                        pltpu.VMEM((nb, MLSTM_HEADS, CARRY_ROWS, MLSTM_HEAD), F32)],
        compiler_params=pltpu.CompilerParams(dimension_semantics=("arbitrary", "arbitrary")),
        name="mlstm")(slab, small, rows, cw, cb, igb, fgb, gb_col, ng, tri, triu, exi, exf)


def _rwkv_slab_width(layer):
    return 2176 if layer == 0 else 2304


def _rwkv_shift_cols(layer):
    return 1664 if layer == 0 else 1792


def _rwkv_kernel(*refs, layer, nb):
    if layer == 0:
        (slab_ref, mu_ref, w0_ref, a0_ref, w2a2_ref, kk_ref, ka_ref, rk_ref, lng_ref, lnb_ref,
         tri_ref, bd_ref, strict_ref, incl_ref, lvl_ref, o_ref, vf_out_ref, buf_ref, st_ref) = refs
    else:
        (slab_ref, vf_ref, mu_ref, w0_ref, a0_ref, w2a2_ref, v0_ref, v2_ref, kk_ref, ka_ref, rk_ref,
         lng_ref, lnb_ref, tri_ref, bd_ref, strict_ref, incl_ref, lvl_ref, o_ref, buf_ref, st_ref) = refs
    ws = _rwkv_shift_cols(layer)
    zoff = _rwkv_slab_width(layer) - GROUP_W
    npair = GROUP_W // LANES

    @pl.when(pl.program_id(1) == 0)
    def _():
        st_ref[...] = jnp.zeros_like(st_ref)
        for b in range(nb):
            buf_ref[b, 0:CARRY_ROWS, :] = jnp.zeros((CARRY_ROWS, ws), F32)

    lane = lax.broadcasted_iota(jnp.int32, (1, LANES), 1)
    masks = _lane_half_masks()
    bd = bd_ref[...]
    strict = strict_ref[...] > 0.5
    incl = incl_ref[...] > 0.5
    masks_b = (_bf(masks[0]), _bf(masks[1]))

    def rows2(t):
        tb = _bf(t)
        return jnp.concatenate([tb * masks_b[0], tb * masks_b[1]], axis=0)

    eye = (lax.broadcasted_iota(jnp.int32, (LANES, LANES), 0)
           == lax.broadcasted_iota(jnp.int32, (LANES, LANES), 1)).astype(F32)

    inv = 1.0 / RWKV_HEAD

    def sequence(b):
        inst = []
        f = slab_ref[b, :, 0:ws]
        (prev,) = _shifted_rows(buf_ref.at[b], f, (1,))
        f = f + mu_ref[...] * (prev - f)
        r = f[:, 0:512]
        k = f[:, 512:1024]
        v = f[:, 1024:1536]
        lora = f[:, 1536:1664]
        lora = jnp.where(lane < RWKV_W_RANK, jnp.tanh(lora), lora)
        wa = _mm(lora, w2a2_ref[...])
        if layer == 0:
            vf_out_ref[b] = v
        else:
            mix = _sigmoid(v0_ref[...] + _mm(f[:, 1664:1792], v2_ref[...]))
            v = v + (vf_ref[b] - v) * mix
        yield
        w_log = -_softplus(-(w0_ref[...] + wa[:, 0:512])) - RWKV_DECAY_OFFSET
        lw = -jnp.exp(w_log)
        a = _sigmoid(a0_ref[...] + wa[:, 512:1024])
        kk = k * kk_ref[...]
        k = k * (1.0 + (a - 1.0) * ka_ref[...])
        cum = _mm_sel_lhs(tri_ref[...], lw)
        ss = [_mm_sel2(kk[:, p * LANES:(p + 1) * LANES] ** 2, bd) for p in range(npair)]
        yield
        last = cum[CHUNK - 1:CHUNK, :]
        e_pos = jnp.exp(cum)
        e_neg = jnp.exp(-cum)
        e_end = jnp.exp(last - cum)
        e_prev = jnp.exp(cum - lw)
        gam = jnp.exp(last)
        for p in range(npair):
            ls = slice(p * LANES, (p + 1) * LANES)
            kk_p = kk[:, ls] / jnp.maximum(jnp.sqrt(ss[p]), 1e-12)
            k_p, r_p, v_p = k[:, ls], r[:, ls], v[:, ls]
            b_p = kk_p * a[:, ls]
            inst.append(dict(
                p=p, ls=ls, r=r_p, k=k_p, v=v_p, gam=gam[:, ls],
                la=rows2(-kk_p * e_prev[:, ls]), lr=rows2(r_p * e_pos[:, ls]),
                rb=rows2(b_p * e_neg[:, ls]), rk=rows2(k_p * e_neg[:, ls]),
                bh=rows2(b_p * e_end[:, ls]), kh=rows2(k_p * e_end[:, ls]), vs=rows2(v_p)))
        yield
        for d in inst:
            aa = _mm_nt(jnp.concatenate([d['la'], d['lr']], axis=0),
                        jnp.concatenate([d['rb'], d['rk']], axis=0))
            d['nab'] = _bf(jnp.where(strict, aa[0:LANES, 0:LANES], 0.0))
            d['aak'] = _bf(jnp.where(strict, aa[0:LANES, LANES:2 * LANES], 0.0))
            d['arb'] = _bf(jnp.where(incl, aa[LANES:2 * LANES, 0:LANES], 0.0))
            d['ark'] = _bf(jnp.where(incl, aa[LANES:2 * LANES, LANES:2 * LANES], 0.0))
            d['t'] = eye + (d['nab'] * lvl_ref[0]).astype(F32)
        yield
        for d in inst:
            d['x'] = _mm(d['aak'], d['vs'])
        for lv in range(1, 6):
            for d in inst:
                d['tb'] = _bf(d['t'])
                d['nt'] = _mm(d['nab'] * lvl_ref[lv], d['tb'])
            yield
            for d in inst:
                d['t'] = d['t'] + _mm(d['tb'], d['nt'])
            yield
        for d in inst:
            d['wu'] = _mm(d['t'], jnp.concatenate([d['la'], _bf(d['x'])], axis=1))
        yield
        for d in inst:
            d['wub'] = _bf(d['wu'])
            qy = _mm(d['arb'], d['wub'])
            d['qt'] = d['lr'].astype(F32) + qy[:, 0:LANES]
            d['y0'] = _mm(d['ark'], d['vs']) + qy[:, LANES:2 * LANES]
        yield
        for d in inst:
            st = st_ref[b, d['p']]
            uy = _mm_nt(jnp.concatenate([d['wub'][:, 0:LANES], _bf(d['qt'])], axis=0), st)
            d['ust'] = uy[0:LANES] + d['wu'][:, LANES:2 * LANES]
            d['st'] = st
            yst = uy[LANES:2 * LANES] + d['y0']
            d['y'] = yst[0:CHUNK] + yst[CHUNK:2 * CHUNK]
        yield
        for d in inst:
            st_ref[b, d['p']] = d['st'] * d['gam'] + _mm_tn(jnp.concatenate([_bf(d['ust']), d['vs']], axis=0),
                                                            jnp.concatenate([d['bh'], d['kh']], axis=0))
            d['mu'] = _mm_sel2(d['y'], bd) * inv
            d['bonus'] = _mm_sel2(d['r'] * d['k'] * rk_ref[:, d['ls']], bd) * d['v']
        yield
        for d in inst:
            d['yc'] = d['y'] - d['mu']
            d['var'] = _mm_sel2(d['yc'] * d['yc'], bd) * inv
        yield
        for d in inst:
            ls = d['ls']
            yn = d['yc'] * lax.rsqrt(d['var'] + RWKV_LN_EPS) * lng_ref[:, ls] + lnb_ref[:, ls]
            z_p = slab_ref[b, :, zoff + d['p'] * LANES:zoff + (d['p'] + 1) * LANES]
            o_ref[b, :, ls] = (yn + d['bonus']) * _silu(z_p)

    _lockstep([sequence(b) for b in range(nb)])


def _rwkv(layer, slab, vf, mu, w0, a0, w2a2, v0, v2, kkw, ka, rk, lng, lnb, tri, bd, strict, incl, lvl):
    B, S, W = slab.shape
    nb = BATCH_BLOCK
    ws = _rwkv_shift_cols(layer)
    const = lambda shape: pl.BlockSpec(shape, lambda b, c: (0,) * len(shape))
    tok = lambda w: pl.BlockSpec((nb, CHUNK, w), lambda b, c: (b, c, 0))
    vecw = const((1, GROUP_W))
    in_specs = [tok(W)]
    args = [slab]
    if layer > 0:
        in_specs.append(tok(GROUP_W))
        args.append(vf)
    in_specs += [const((1, ws)), vecw, vecw, const((LANES, 2 * GROUP_W))]
    args += [mu, w0, a0, w2a2]
    if layer > 0:
        in_specs += [vecw, const((LANES, GROUP_W))]
        args += [v0, v2]
    in_specs += [vecw, vecw, vecw, vecw, vecw, const((CHUNK, CHUNK)), const((LANES, LANES)),
                 const((LANES, LANES)), const((LANES, LANES)), const((6, LANES, LANES))]
    args += [kkw, ka, rk, lng, lnb, tri, bd, strict, incl, lvl]
    out_shape = [jax.ShapeDtypeStruct((B, S, GROUP_W), F32)]
    out_specs = [tok(GROUP_W)]
    if layer == 0:
        out_shape.append(jax.ShapeDtypeStruct((B, S, GROUP_W), F32))
        out_specs.append(tok(GROUP_W))
    res = pl.pallas_call(
        functools.partial(_rwkv_kernel, layer=layer, nb=nb), grid=(B // nb, S // CHUNK),
        in_specs=in_specs, out_specs=out_specs, out_shape=out_shape,
        scratch_shapes=[pltpu.VMEM((nb, CARRY_ROWS + CHUNK, ws), F32),
                        pltpu.VMEM((nb, GROUP_W // LANES, LANES, LANES), F32)],
        compiler_params=pltpu.CompilerParams(dimension_semantics=("arbitrary", "arbitrary"),
                                             vmem_limit_bytes=VMEM_LIMIT_PROJ),
        name="rwkv")(*args)
    return (res[0], res[1]) if layer == 0 else (res[0], vf)


def _np_consts():
    i = np.arange(CHUNK)
    tri = (i[None, :] <= i[:, None]).astype(np.float32)
    t = np.arange(LANES)
    same = (t[:, None] // CHUNK) == (t[None, :] // CHUNK)
    strict = (same & (t[None, :] < t[:, None])).astype(np.float32)
    incl = (same & (t[None, :] <= t[:, None])).astype(np.float32)
    bd = same.astype(np.float32)
    lvl = np.stack([(((t[:, None] >> l) == (t[None, :] >> l))
                     & ((t[:, None] >> (l - 1)) != (t[None, :] >> (l - 1)))).astype(np.float32)
                    for l in range(1, 7)])
    ex = np.zeros((LANES, GROUP_W), np.float32)
    for h in range(SSD_HEADS):
        ex[SMALL_DT + h, h * SSD_HEADDIM:(h + 1) * SSD_HEADDIM] = 1.0
    exi = np.zeros((LANES, GROUP_W), np.float32)
    exf = np.zeros((LANES, GROUP_W), np.float32)
    for h in range(MLSTM_HEADS):
        exi[SMALL_I + h, h * MLSTM_HEAD:(h + 1) * MLSTM_HEAD] = 1.0
        exf[SMALL_F + h, h * MLSTM_HEAD:(h + 1) * MLSTM_HEAD] = 1.0
    triu = tri.T.copy()
    zero = np.zeros_like(triu)
    triu2 = np.stack([np.concatenate([triu, zero], axis=1), np.concatenate([zero, triu], axis=1)])
    return dict(tri=tri, triu=triu, triu2=triu2, strict=strict, incl=incl, bd=bd, lvl=lvl, ex=ex,
                exi=exi, exf=exf)


def _pad_rows(w, height):
    return jnp.pad(w, ((0, height - w.shape[0]), (0, 0)))


def _pad_vec(v, width, offset=0):
    v = v.reshape(1, -1)
    return jnp.pad(v, ((0, 0), (offset, width - offset - v.shape[1])))


BF16_ROW_TILE = 16


def _layer(x2, v_first, layer, p, B, S, consts, g_final):
    sw = 3 * GROUP_W + RWKV_W_RANK + RWKV_A_RANK + (RWKV_V_RANK if layer > 0 else 0)
    names = ['gla_q', 'gla_k', 'gla_v', 'gla_gk', 'gla_z', 'rwkv_shift', 'rwkv_z', 'ssd_xbc', 'ssd_dt',
             'pad0', 'ssd_z', 'mlstm_qk', 'mlstm_v', 'mlstm_i', 'mlstm_f', 'pad1', 'mlstm_o', 'mlstm_z']
    widths = [256, 256, 512, 16, 512, sw, 512, SSD_XBC, SSD_HEADS, 8, 512, 1024, 512, 4, 4, 8, 512, 512]
    offs = np.concatenate([[0], np.cumsum(widths)]).tolist()
    off = dict(zip(names, offs[:-1]))
    assert all(off[n] % BF16_ROW_TILE == 0 for n in names if n not in ('mlstm_f', 'pad0', 'pad1'))
    w_in = p['w_in']
    zpad = jnp.zeros((D_MODEL, 8), w_in.dtype)
    cut0, cut1 = off['pad0'], off['pad1'] - 8
    w_t = _bf(jnp.concatenate([w_in[:, :cut0], zpad, w_in[:, cut0:cut1], zpad, w_in[:, cut1:]], axis=1).T)
    rows = lambda name, n: w_t[off[name]:off[name] + n, :]
    g = p['norm_g'].reshape(1, D_MODEL)
    tri, triu = consts['tri'], consts['triu']
    rows_of = lambda t: t.reshape(t.shape[0], B, S // CHUNK, CHUNK).transpose(1, 2, 0, 3)

    w_small = _pad_rows(jnp.concatenate([rows('gla_gk', GLA_RANK), rows('ssd_dt', SSD_HEADS),
                                         rows('mlstm_i', 2 * MLSTM_HEADS)], axis=0), LANES)
    even_odd = np.concatenate([np.arange(0, SSD_HEADS, 2), np.arange(1, SSD_HEADS, 2)])
    w_rows = jnp.concatenate([rows('ssd_dt', SSD_HEADS)[even_odd], rows('mlstm_i', 2 * MLSTM_HEADS)], axis=0)
    mu = p['rwkv_mu']
    if layer == 0:
        pieces, zero_lanes = [(off['rwkv_shift'], sw + GROUP_W, 0)], ()
        mu_p = mu.reshape(1, -1)
    else:
        pieces = [(off['rwkv_shift'], sw, 0), (off['rwkv_z'], GROUP_W, 1792)]
        zero_lanes = ((1664, 1792),)
        mu_p = _pad_vec(mu, 1792)
    slab, small, gate_rows = _in_proj(x2, g, w_t, pieces, _rwkv_slab_width(layer), zero_lanes,
                                      w_small, w_rows)
    slab = slab.reshape(B, S, _rwkv_slab_width(layer))
    small = small.reshape(B, S, LANES)
    gate_rows = rows_of(gate_rows)
    w2a2 = jnp.zeros((LANES, 2 * GROUP_W), F32)
    w2a2 = w2a2.at[0:RWKV_W_RANK, 0:GROUP_W].set(p['rwkv_w2'])
    w2a2 = w2a2.at[RWKV_W_RANK:, GROUP_W:].set(p['rwkv_a2'])
    vec = lambda t: t.reshape(1, GROUP_W)
    v0 = v2 = None
    if layer > 0:
        v0 = vec(p['rwkv_v0'])
        v2 = _bf(jnp.pad(p['rwkv_v2'], ((0, LANES - RWKV_V_RANK), (0, 0))))
    y_rwkv, v_first = _rwkv(layer, slab, v_first, mu_p, vec(p['rwkv_w0']), vec(p['rwkv_a0']), _bf(w2a2),
                            v0, v2, vec(p['rwkv_k_k']), vec(p['rwkv_k_a']), vec(p['rwkv_r_k']),
                            vec(p['rwkv_ln_g']), vec(p['rwkv_ln_b']), _bf(tri), _bf(consts['bd']),
                            consts['strict'], consts['incl'], _bf(consts['lvl']))

    slab = _in_proj(x2, g, w_t, [(off['gla_q'], 1024, 0), (off['gla_z'], GROUP_W, 1024)], GLA_SLAB)
    w2p = _bf(jnp.pad(p['gla_gk_w2'], ((SMALL_GK, LANES - SMALL_GK - GLA_RANK), (0, 0))))
    y_gla = _gla(slab.reshape(B, S, GLA_SLAB), small, w2p, p['gla_gk_b'].reshape(1, -1),
                 p['gla_norm_g'].reshape(1, -1), _bf(tri))

    slab = _in_proj(x2, g, w_t, [(off['ssd_xbc'], SSD_XBC, 0), (off['ssd_z'], GROUP_W, SSD_XBC)], SSD_SLAB)
    a_neg = -jnp.exp(p['ssd_a_log'])
    y_ssd = _ssd(slab.reshape(B, S, SSD_SLAB), small, gate_rows, p['ssd_conv_w'],
                 p['ssd_conv_b'].reshape(1, -1),
                 _pad_vec(p['ssd_dt_bias'], LANES, SMALL_DT), _pad_vec(a_neg, LANES, SMALL_DT),
                 p['ssd_dt_bias'][even_odd].reshape(-1, 1), a_neg[even_odd].reshape(-1, 1),
                 jnp.repeat(p['ssd_d'], SSD_HEADDIM).reshape(1, -1), p['ssd_norm_g'].reshape(1, -1),
                 _bf(tri), _bf(consts['triu2']), _bf(consts['ex']))

    slab = _in_proj(x2, g, w_t, [(off['mlstm_qk'], 3 * GROUP_W, 0), (off['mlstm_o'], 2 * GROUP_W, 3 * GROUP_W)],
                    MLSTM_SLAB)
    gb_col = jnp.concatenate([p['mlstm_ig_b'], p['mlstm_fg_b']]).reshape(-1, 1)
    y_ml = _mlstm(slab.reshape(B, S, MLSTM_SLAB), small, gate_rows, p['mlstm_conv_w'],
                  p['mlstm_conv_b'].reshape(1, -1), _pad_vec(p['mlstm_ig_b'], LANES, SMALL_I),
                  _pad_vec(p['mlstm_fg_b'], LANES, SMALL_F), gb_col, p['mlstm_norm_g'].reshape(1, -1),
                  _bf(tri), _bf(triu), _bf(consts['exi']), _bf(consts['exf']))

    T = B * S
    ys = [y.reshape(T, GROUP_W) for y in (y_gla, y_rwkv, y_ssd, y_ml)]
    return _out_proj(x2, ys, _bf(p['w_out']), g_final), v_first


_PARAM_NAMES_0 = ['norm_g', 'w_in', 'w_out', 'gla_gk_w2', 'gla_gk_b', 'gla_norm_g', 'rwkv_mu', 'rwkv_w0',
                  'rwkv_w2', 'rwkv_a0', 'rwkv_a2', 'rwkv_k_k', 'rwkv_k_a', 'rwkv_r_k', 'rwkv_ln_g',
                  'rwkv_ln_b', 'ssd_conv_w', 'ssd_conv_b', 'ssd_dt_bias', 'ssd_a_log', 'ssd_d',
                  'ssd_norm_g', 'mlstm_conv_w', 'mlstm_conv_b', 'mlstm_ig_b', 'mlstm_fg_b', 'mlstm_norm_g']
_PARAM_NAMES_1 = (_PARAM_NAMES_0[:11] + ['rwkv_v0', 'rwkv_v2'] + _PARAM_NAMES_0[11:])


def kernel(x,
           norm_g_0, w_in_0, w_out_0, gla_gk_w2_0, gla_gk_b_0, gla_norm_g_0,
           rwkv_mu_0, rwkv_w0_0, rwkv_w2_0, rwkv_a0_0, rwkv_a2_0,
           rwkv_k_k_0, rwkv_k_a_0, rwkv_r_k_0, rwkv_ln_g_0, rwkv_ln_b_0,
           ssd_conv_w_0, ssd_conv_b_0, ssd_dt_bias_0, ssd_a_log_0, ssd_d_0, ssd_norm_g_0,
           mlstm_conv_w_0, mlstm_conv_b_0, mlstm_ig_b_0, mlstm_fg_b_0, mlstm_norm_g_0,
           norm_g_1, w_in_1, w_out_1, gla_gk_w2_1, gla_gk_b_1, gla_norm_g_1,
           rwkv_mu_1, rwkv_w0_1, rwkv_w2_1, rwkv_a0_1, rwkv_a2_1, rwkv_v0_1, rwkv_v2_1,
           rwkv_k_k_1, rwkv_k_a_1, rwkv_r_k_1, rwkv_ln_g_1, rwkv_ln_b_1,
           ssd_conv_w_1, ssd_conv_b_1, ssd_dt_bias_1, ssd_a_log_1, ssd_d_1, ssd_norm_g_1,
           mlstm_conv_w_1, mlstm_conv_b_1, mlstm_ig_b_1, mlstm_fg_b_1, mlstm_norm_g_1,
           final_norm_g):
    params = (norm_g_0, w_in_0, w_out_0, gla_gk_w2_0, gla_gk_b_0, gla_norm_g_0,
              rwkv_mu_0, rwkv_w0_0, rwkv_w2_0, rwkv_a0_0, rwkv_a2_0,
              rwkv_k_k_0, rwkv_k_a_0, rwkv_r_k_0, rwkv_ln_g_0, rwkv_ln_b_0,
              ssd_conv_w_0, ssd_conv_b_0, ssd_dt_bias_0, ssd_a_log_0, ssd_d_0, ssd_norm_g_0,
              mlstm_conv_w_0, mlstm_conv_b_0, mlstm_ig_b_0, mlstm_fg_b_0, mlstm_norm_g_0,
              norm_g_1, w_in_1, w_out_1, gla_gk_w2_1, gla_gk_b_1, gla_norm_g_1,
              rwkv_mu_1, rwkv_w0_1, rwkv_w2_1, rwkv_a0_1, rwkv_a2_1, rwkv_v0_1, rwkv_v2_1,
              rwkv_k_k_1, rwkv_k_a_1, rwkv_r_k_1, rwkv_ln_g_1, rwkv_ln_b_1,
              ssd_conv_w_1, ssd_conv_b_1, ssd_dt_bias_1, ssd_a_log_1, ssd_d_1, ssd_norm_g_1,
              mlstm_conv_w_1, mlstm_conv_b_1, mlstm_ig_b_1, mlstm_fg_b_1, mlstm_norm_g_1,
              final_norm_g)
    n0 = len(_PARAM_NAMES_0)
    n1 = len(_PARAM_NAMES_1)
    p0 = dict(zip(_PARAM_NAMES_0, params[:n0]))
    p1 = dict(zip(_PARAM_NAMES_1, params[n0:n0 + n1]))
    final_norm_g = params[n0 + n1]
    B, S, _ = x.shape
    consts = {k: jnp.asarray(v) for k, v in _np_consts().items()}
    x2 = x.reshape(B * S, D_MODEL)
    x2, v_first = _layer(x2, None, 0, p0, B, S, consts, None)
    x2, _ = _layer(x2, v_first, 1, p1, B, S, consts, final_norm_g.reshape(1, D_MODEL))
    return x2.reshape(B, S, D_MODEL)
```

```python
import functools

import numpy as np
import jax
import jax.numpy as jnp
from jax import lax
from jax.experimental import pallas as pl
from jax.experimental.pallas import tpu as pltpu

F32 = jnp.float32
BF16 = jnp.bfloat16

D_MODEL = 2048
CHUNK = 64
GROUP_W = 512
NORM_EPS = 1e-6
LANES = 128
CARRY_ROWS = 8

GLA_HEADS, GLA_DK, GLA_DV, GLA_RANK = 4, 64, 128, 16
GLA_GATE_NORMALIZER = 16.0
RWKV_HEAD, RWKV_W_RANK, RWKV_A_RANK, RWKV_V_RANK = 64, 64, 64, 32
RWKV_LN_EPS = 64e-5
RWKV_DECAY_OFFSET = 0.5
SSD_HEADS, SSD_HEADDIM, SSD_STATE, SSD_CONV = 8, 64, 128, 4
SSD_XBC = 1024
MLSTM_HEADS, MLSTM_HEAD, MLSTM_CONV = 4, 128, 4

ROW_TILE_IN = 512
SMALL_GK, SMALL_DT, SMALL_I, SMALL_F = 0, 16, 24, 28
ROWS_DT, ROWS_IF, ROWS_TOTAL = 0, 8, 16
IN_PROJ_SUBTILES = 2
IN_PROJ_N_CHUNK = 512
IN_PROJ_EPI_CHUNK = 256
ROW_TILE_OUT = 512
VMEM_LIMIT_PROJ = 48 * 2**20


def _bf(x):
    return x.astype(BF16)


def _mm(a, b):
    return jnp.dot(_bf(a), _bf(b), preferred_element_type=F32)


def _mm_nt(a, b):
    return lax.dot_general(_bf(a), _bf(b), (((1,), (1,)), ((), ())), preferred_element_type=F32)


def _mm_tn(a, b):
    return lax.dot_general(_bf(a), _bf(b), (((0,), (0,)), ((), ())), preferred_element_type=F32)


def _split3(x):
    hi = _bf(x)
    r1 = x - hi.astype(F32)
    mid = _bf(r1)
    lo = _bf(r1 - mid.astype(F32))
    return hi, mid, lo


def _mm_sel_rhs(x, sel):
    hi, mid, lo = _split3(x)
    d = lambda a: jnp.dot(a, sel, preferred_element_type=F32)
    return d(hi) + d(mid) + d(lo)


def _mm_sel2(x, sel):
    hi = _bf(x)
    lo = _bf(x - hi.astype(F32))
    d = lambda a: jnp.dot(a, sel, preferred_element_type=F32)
    return d(hi) + d(lo)


def _mm_sel_lhs(sel, x):
    hi, mid, lo = _split3(x)
    d = lambda a: jnp.dot(sel, a, preferred_element_type=F32)
    return d(hi) + d(mid) + d(lo)


def _rowsum(x):
    ones = jnp.ones((x.shape[-1], LANES), BF16)
    hi = _bf(x)
    lo = _bf(x - hi.astype(F32))
    return (jnp.dot(hi, ones, preferred_element_type=F32)
            + jnp.dot(lo, ones, preferred_element_type=F32))


def _sigmoid(x):
    return 1.0 / (1.0 + jnp.exp(-x))


def _silu(x):
    return x * _sigmoid(x)


def _softplus(x):
    return jnp.maximum(x, 0.0) + jnp.log1p(jnp.exp(-jnp.abs(x)))


def _log_sigmoid(x):
    return -_softplus(-x)


def _lane_half_masks():
    lane = lax.broadcasted_iota(jnp.int32, (1, LANES), 1)
    lo = (lane < LANES // 2).astype(F32)
    return lo, 1.0 - lo


def _causal(n):
    r = lax.broadcasted_iota(jnp.int32, (n, n), 0)
    c = lax.broadcasted_iota(jnp.int32, (n, n), 1)
    return c <= r


def _shifted_rows(buf_ref, cur, offsets):
    buf_ref[CARRY_ROWS:CARRY_ROWS + CHUNK, :] = cur
    outs = [buf_ref[CARRY_ROWS - off:CARRY_ROWS - off + CHUNK, :] for off in offsets]
    tail = buf_ref[CHUNK:CHUNK + CARRY_ROWS, :]
    buf_ref[0:CARRY_ROWS, :] = tail
    return outs


def _in_proj_kernel(*refs, pieces, zero_lanes, has_small, mix, acts, tiles_per_seq):
    x_ref = refs[0]
    w_refs = refs[1:1 + len(pieces)]
    rest = list(refs[1 + len(pieces):])
    if mix is not None:
        cw_ref, cb_ref = rest.pop(0), rest.pop(0)
        buf_ref = rest.pop()
    if has_small:
        ws_ref, wr_ref, o_ref, os_ref, or_ref = rest
    else:
        (o_ref,) = rest
    nt = lambda a, b: lax.dot_general(a, b, (((1,), (1,)), ((), ())), preferred_element_type=F32)
    tm = x_ref.shape[0]
    hm = tm // IN_PROJ_SUBTILES
    for lo, hi in zero_lanes:
        o_ref[:, lo:hi] = jnp.zeros((tm, hi - lo), F32)
    if mix is not None:
        mlo, mhi, kind = mix

        @pl.when(pl.program_id(0) % tiles_per_seq == 0)
        def _():
            buf_ref[0:CARRY_ROWS, :] = jnp.zeros((CARRY_ROWS, mhi - mlo), F32)

    def subtile(si):
        rs = slice(si * hm, (si + 1) * hm)
        hb = x_ref[rs, :]
        for w_ref, (_, n_rows, lane_off) in zip(w_refs, pieces):
            for c0 in range(0, n_rows, IN_PROJ_N_CHUNK):
                c1 = min(c0 + IN_PROJ_N_CHUNK, n_rows)
                o_ref[rs, lane_off + c0:lane_off + c1] = nt(hb, w_ref[c0:c1, :])
                yield
        if has_small:
            os_ref[rs, :] = nt(hb, ws_ref[...])
            or_ref[:, rs] = nt(wr_ref[...], hb)
            yield
        if mix is not None:
            r0 = CARRY_ROWS + si * hm
            for c0 in range(mlo, mhi, IN_PROJ_EPI_CHUNK):
                c1 = min(c0 + IN_PROJ_EPI_CHUNK, mhi)
                bs = slice(c0 - mlo, c1 - mlo)
                cur = o_ref[rs, c0:c1]
                buf_ref[r0:r0 + hm, bs] = cur
                if kind == 'lerp':
                    o_ref[rs, c0:c1] = cur + cw_ref[:, bs] * (buf_ref[r0 - 1:r0 - 1 + hm, bs] - cur)
                else:
                    ntap = cw_ref.shape[0]
                    acc = cb_ref[:, bs] + cur * cw_ref[ntap - 1:ntap, bs]
                    for j in range(ntap - 1):
                        back = ntap - 1 - j
                        acc = acc + buf_ref[r0 - back:r0 - back + hm, bs] * cw_ref[j:j + 1, bs]
                    o_ref[rs, c0:c1] = _silu(acc)
                yield
        for alo, ahi, fn in acts:
            for c0 in range(alo, ahi, IN_PROJ_EPI_CHUNK):
                c1 = min(c0 + IN_PROJ_EPI_CHUNK, ahi)
                v = o_ref[rs, c0:c1]
                o_ref[rs, c0:c1] = _silu(v) if fn == 'silu' else _sigmoid(v)
                yield

    n_mm = sum(-(-n_rows // IN_PROJ_N_CHUNK) for _, n_rows, _ in pieces) + (1 if has_small else 0)
    _lockstep([subtile(si) for si in range(IN_PROJ_SUBTILES)],
              starts=[si * n_mm for si in range(IN_PROJ_SUBTILES)])
    if mix is not None:
        buf_ref[0:CARRY_ROWS, :] = buf_ref[tm:tm + CARRY_ROWS, :]


def _in_proj(hb, w_t, pieces, width, seq_len, zero_lanes=(), w_small=None, w_rows=None,
             mix=None, mix_w=None, mix_b=None, acts=()):
    T = hb.shape[0]
    tm = ROW_TILE_IN
    const2 = lambda shape: pl.BlockSpec(shape, lambda i: (0, 0))
    in_specs = [pl.BlockSpec((tm, D_MODEL), lambda i: (i, 0))]
    args = [hb]
    for off, n_rows, _ in pieces:
        in_specs.append(pl.BlockSpec((pl.Element(n_rows), pl.Element(D_MODEL)), lambda i, off=off: (off, 0)))
        args.append(w_t)
    scratch = []
    if mix is not None:
        mw = mix[1] - mix[0]
        in_specs += [const2(mix_w.shape), const2((1, mw))]
        args += [mix_w, mix_b]
        scratch.append(pltpu.VMEM((CARRY_ROWS + tm, mw), F32))
    out_shape = [jax.ShapeDtypeStruct((T, width), F32)]
    out_specs = [pl.BlockSpec((tm, width), lambda i: (i, 0))]
    if w_small is not None:
        r = w_rows.shape[0]
        in_specs += [const2((LANES, D_MODEL)), const2((r, D_MODEL))]
        args += [w_small, w_rows]
        out_shape += [jax.ShapeDtypeStruct((T, LANES), F32), jax.ShapeDtypeStruct((r, T), F32)]
        out_specs += [pl.BlockSpec((tm, LANES), lambda i: (i, 0)), pl.BlockSpec((r, tm), lambda i: (0, i))]
    res = pl.pallas_call(
        functools.partial(_in_proj_kernel, pieces=tuple(pieces), zero_lanes=tuple(zero_lanes),
                          has_small=w_small is not None, mix=mix, acts=tuple(acts),
                          tiles_per_seq=seq_len // tm),
        grid=(T // tm,), in_specs=in_specs, out_specs=out_specs, out_shape=out_shape,
        scratch_shapes=scratch,
        compiler_params=pltpu.CompilerParams(dimension_semantics=("arbitrary",),
                                             vmem_limit_bytes=VMEM_LIMIT_PROJ),
        name="in_proj")(*args)
    return res if w_small is not None else res[0]


def _rms_norm_rows(x, g):
    return x * lax.rsqrt(jnp.mean(x * x, axis=-1, keepdims=True) + NORM_EPS) * g


def _norm_kernel(x_ref, g_ref, o_ref):
    o_ref[...] = _bf(_rms_norm_rows(x_ref[...], g_ref[...]))


def _norm_bf16(x2, g):
    T = x2.shape[0]
    tm = ROW_TILE_OUT
    return pl.pallas_call(
        _norm_kernel, grid=(T // tm,),
        in_specs=[pl.BlockSpec((tm, D_MODEL), lambda i: (i, 0)), pl.BlockSpec((1, D_MODEL), lambda i: (0, 0))],
        out_specs=pl.BlockSpec((tm, D_MODEL), lambda i: (i, 0)),
        out_shape=jax.ShapeDtypeStruct((T, D_MODEL), BF16),
        compiler_params=pltpu.CompilerParams(dimension_semantics=("arbitrary",)),
        name="norm")(x2, g)


def _out_proj_kernel(*refs, final):
    if final:
        x_ref, y0, y1, y2, y3, w_ref, g_ref, o_ref = refs
    else:
        x_ref, y0, y1, y2, y3, w_ref, g_ref, o_ref, hb_ref = refs
    acc = x_ref[...]
    for gi, y in enumerate((y0, y1, y2, y3)):
        acc = acc + jnp.dot(_bf(y[...]), w_ref[gi * GROUP_W:(gi + 1) * GROUP_W, :],
                            preferred_element_type=F32)
    normed = _rms_norm_rows(acc, g_ref[...])
    if final:
        o_ref[...] = normed
    else:
        o_ref[...] = acc
        hb_ref[...] = _bf(normed)


def _out_proj(x2, ys, w, g, final):
    T = x2.shape[0]
    tm = ROW_TILE_OUT
    row = lambda width: pl.BlockSpec((tm, width), lambda i: (i, 0))
    in_specs = [row(D_MODEL)] + [row(GROUP_W) for _ in range(4)]
    in_specs += [pl.BlockSpec((D_MODEL, D_MODEL), lambda i: (0, 0)), pl.BlockSpec((1, D_MODEL), lambda i: (0, 0))]
    out_shape = [jax.ShapeDtypeStruct((T, D_MODEL), F32)]
    out_specs = [row(D_MODEL)]
    if not final:
        out_shape.append(jax.ShapeDtypeStruct((T, D_MODEL), BF16))
        out_specs.append(row(D_MODEL))
    res = pl.pallas_call(
        functools.partial(_out_proj_kernel, final=final),
        grid=(T // tm,), in_specs=in_specs, out_specs=out_specs, out_shape=out_shape,
        compiler_params=pltpu.CompilerParams(dimension_semantics=("arbitrary",),
                                             vmem_limit_bytes=VMEM_LIMIT_PROJ),
        name="out_proj")(x2, *ys, w, g)
    return (res[0], None) if final else (res[0], res[1])


GLA_SLAB = 1536


BATCH_BLOCK = 4


def _lockstep(gens, starts=None):
    gens = list(gens)
    starts = [0] * len(gens) if starts is None else list(starts)
    pending = list(zip(starts, gens))
    tick = 0
    while pending:
        alive = []
        for start, g in pending:
            if start > tick:
                alive.append((start, g))
                continue
            try:
                next(g)
                alive.append((start, g))
            except StopIteration:
                pass
        pending = alive
        tick += 1


def _gla_kernel(slab_ref, small_ref, w2_ref, gkb_ref, ng_ref, tri_ref, o_ref, st_ref, *, nb):
    @pl.when(pl.program_id(1) == 0)
    def _():
        st_ref[...] = jnp.zeros_like(st_ref)

    causal = _causal(CHUNK)
    masks = _lane_half_masks()

    def head(b, h, qg, kg, kd, dec):
        p, j = divmod(h, 2)
        ls = slice(p * LANES, (p + 1) * LANES)
        qm = qg[:, ls] * masks[j]
        att = jnp.where(causal, _mm_nt(qm, kg[:, ls]), 0.0)
        yield
        v_h = slab_ref[b, :, 512 + h * GLA_DV:512 + (h + 1) * GLA_DV]
        st = st_ref[b, h]
        o = _mm(att, v_h) + _mm_nt(qm, st)
        st_ref[b, h] = st * dec[:, ls] + _mm_tn(v_h, kd[:, ls] * masks[j])
        yield
        ms = jnp.mean(o * o, axis=-1, keepdims=True)
        yield
        o = o * lax.rsqrt(ms + NORM_EPS)
        o = o * ng_ref[:, h * GLA_DV:(h + 1) * GLA_DV]
        z_h = slab_ref[b, :, 1024 + h * GLA_DV:1024 + (h + 1) * GLA_DV]
        o_ref[b, :, h * GLA_DV:(h + 1) * GLA_DV] = o * z_h

    gens = []
    for b in range(nb):
        q = slab_ref[b, :, 0:256] * (GLA_DK ** -0.5)
        k = slab_ref[b, :, 256:512]
        gk = _mm(small_ref[b], w2_ref[...]) + gkb_ref[...]
        log_a = _log_sigmoid(gk) / GLA_GATE_NORMALIZER
        cum = _mm_sel_lhs(tri_ref[...], log_a)
        last = cum[CHUNK - 1:CHUNK, :]
        qg = q * jnp.exp(cum)
        kg = k * jnp.exp(-cum)
        kd = k * jnp.exp(last - cum)
        dec = jnp.exp(last)
        gens += [head(b, h, qg, kg, kd, dec) for h in range(GLA_HEADS)]
    _lockstep(gens)


def _gla(slab, small, w2p, gkb, ng, tri):
    B, S, _ = slab.shape
    nb = BATCH_BLOCK
    const = lambda shape: pl.BlockSpec(shape, lambda b, c: (0,) * len(shape))
    return pl.pallas_call(
        functools.partial(_gla_kernel, nb=nb), grid=(B // nb, S // CHUNK),
        in_specs=[pl.BlockSpec((nb, CHUNK, GLA_SLAB), lambda b, c: (b, c, 0)),
                  pl.BlockSpec((nb, CHUNK, LANES), lambda b, c: (b, c, 0)),
                  const((LANES, 256)), const((1, 256)), const((1, GROUP_W)), const((CHUNK, CHUNK))],
        out_specs=pl.BlockSpec((nb, CHUNK, GROUP_W), lambda b, c: (b, c, 0)),
        out_shape=jax.ShapeDtypeStruct((B, S, GROUP_W), F32),
        scratch_shapes=[pltpu.VMEM((nb, GLA_HEADS, GLA_DV, LANES), F32)],
        compiler_params=pltpu.CompilerParams(dimension_semantics=("arbitrary", "arbitrary")),
        name="gla")(slab, small, w2p, gkb, ng, tri)


SSD_SLAB = 1536


def _ssd_kernel(slab_ref, small_ref, rows_ref, dtb_ref, a_ref, dtb_col_ref, a_col_ref,
                dskip_ref, ng_ref, tri_ref, triu2_ref, ex_ref, o_ref, st_ref, *, nb):
    @pl.when(pl.program_id(1) == 0)
    def _():
        st_ref[...] = jnp.zeros_like(st_ref)

    row_i = lax.broadcasted_iota(jnp.int32, (CHUNK, LANES), 0)
    col_i = lax.broadcasted_iota(jnp.int32, (CHUNK, LANES), 1)
    causal2 = jnp.bitwise_and(col_i, CHUNK - 1) <= row_i
    masks = _lane_half_masks()
    masks_b = (_bf(masks[0]), _bf(masks[1]))

    def pair(b, p, xbc, cum_b, cum_row2, xdt, xw, ecum, dec):
        g = p // 2
        ls = slice(p * LANES, (p + 1) * LANES)
        bm = xbc[:, 512 + g * SSD_STATE:512 + (g + 1) * SSD_STATE]
        cm = xbc[:, 768 + g * SSD_STATE:768 + (g + 1) * SSD_STATE]
        bmb = _bf(bm)
        cbm2 = _mm_nt(cm, jnp.concatenate([bmb, bmb], axis=0))
        st = st_ref[b, p]
        y = _mm(cm, st) * ecum[:, ls]
        st_ref[b, p] = st * dec[:, ls] + _mm_tn(bmb, xw[:, ls])
        lmat = jnp.exp(jnp.where(causal2, cum_b[:, ls] - cum_row2[p:p + 1, :], -jnp.inf))
        yield
        xb = _bf(xdt[:, ls])
        xs = jnp.concatenate([xb * masks_b[0], xb * masks_b[1]], axis=0)
        y = y + _mm(cbm2 * lmat, xs)
        yield
        y = y + dskip_ref[:, ls] * xbc[:, ls]
        o_ref[b, :, ls] = y * slab_ref[b, :, 1024 + p * LANES:1024 + (p + 1) * LANES]

    gens = []
    for b in range(nb):
        xbc = slab_ref[b, :, 0:SSD_XBC]
        dt_col = _softplus(small_ref[b] + dtb_ref[...])
        cum_col = _mm_sel_lhs(tri_ref[...], dt_col * a_ref[...])
        dt_row = _softplus(rows_ref[b, 0, ROWS_DT:ROWS_DT + SSD_HEADS, :] + dtb_col_ref[...])
        da_row = dt_row * a_col_ref[...]
        cum_row2 = (_mm_sel_rhs(da_row[0:SSD_HEADS // 2], triu2_ref[0])
                    + _mm_sel_rhs(da_row[SSD_HEADS // 2:SSD_HEADS], triu2_ref[1]))
        dt_b = _mm_sel_rhs(dt_col, ex_ref[...])
        cum_b = _mm_sel_rhs(cum_col, ex_ref[...])
        last_b = cum_b[CHUNK - 1:CHUNK, :]
        xdt = xbc[:, 0:512] * dt_b
        xw = xdt * jnp.exp(last_b - cum_b)
        ecum = jnp.exp(cum_b)
        dec = jnp.exp(last_b)
        gens += [pair(b, p, xbc, cum_b, cum_row2, xdt, xw, ecum, dec) for p in range(SSD_HEADS // 2)]
    _lockstep(gens)
    for b in range(nb):
        y = o_ref[b]
        inv = lax.rsqrt(_rowsum(y * y) * (1.0 / GROUP_W) + NORM_EPS)
        for p in range(GROUP_W // LANES):
            ls = slice(p * LANES, (p + 1) * LANES)
            o_ref[b, :, ls] = y[:, ls] * inv * ng_ref[:, ls]


def _ssd(slab, small, rows, dtb, a, dtb_col, a_col, dskip, ng, tri, triu, ex):
    B, S, _ = slab.shape
    nb = BATCH_BLOCK
    const = lambda shape: pl.BlockSpec(shape, lambda b, c: (0,) * len(shape))
    return pl.pallas_call(
        functools.partial(_ssd_kernel, nb=nb), grid=(B // nb, S // CHUNK),
        in_specs=[pl.BlockSpec((nb, CHUNK, SSD_SLAB), lambda b, c: (b, c, 0)),
                  pl.BlockSpec((nb, CHUNK, LANES), lambda b, c: (b, c, 0)),
                  pl.BlockSpec((nb, 1, ROWS_TOTAL, CHUNK), lambda b, c: (b, c, 0, 0)),
                  const((1, LANES)), const((1, LANES)),
                  const((SSD_HEADS, 1)), const((SSD_HEADS, 1)), const((1, GROUP_W)), const((1, GROUP_W)),
                  const((CHUNK, CHUNK)), const((2, CHUNK, LANES)), const((LANES, GROUP_W))],
        out_specs=pl.BlockSpec((nb, CHUNK, GROUP_W), lambda b, c: (b, c, 0)),
        out_shape=jax.ShapeDtypeStruct((B, S, GROUP_W), F32),
        scratch_shapes=[pltpu.VMEM((nb, SSD_HEADS // 2, SSD_STATE, LANES), F32)],
        compiler_params=pltpu.CompilerParams(dimension_semantics=("arbitrary", "arbitrary")),
        name="ssd")(slab, small, rows, dtb, a, dtb_col, a_col, dskip, ng, tri, triu, ex)


MLSTM_SLAB = 2560


def _mlstm_kernel(slab_ref, small_ref, rows_ref, igb_ref, fgb_ref, gb_col_ref, ng_ref,
                  tri_ref, triu_ref, exi_ref, exf_ref, o_ref, c_ref, nm_ref, *, nb):
    @pl.when(pl.program_id(1) == 0)
    def _():
        c_ref[...] = jnp.zeros_like(c_ref)
        nm_ref[...] = jnp.zeros_like(nm_ref)

    causal = _causal(CHUNK)

    def head(b, h, qk, li_b, ci_b, logi_row, cum_row):
        ls = slice(h * MLSTM_HEAD, (h + 1) * MLSTM_HEAD)
        q = qk[:, ls]
        k = qk[:, GROUP_W + h * MLSTM_HEAD:GROUP_W + (h + 1) * MLSTM_HEAD] * (MLSTM_HEAD ** -0.5)
        v = slab_ref[b, :, 1024 + h * MLSTM_HEAD:1024 + (h + 1) * MLSTM_HEAD]
        ci = ci_b[:, ls]
        li = li_b[:, ls]
        cr = cum_row[h:h + 1, :]
        lir = logi_row[h:h + 1, :]
        last = ci[CHUNK - 1:CHUNK, :]
        c_prev = c_ref[b, h]
        n_prev = nm_ref[b, h, 0:1, :]
        m_prev = nm_ref[b, h, 1:2, :]

        g = last - ci + li
        g_max = jnp.max(g, axis=0, keepdims=True)
        log_d = jnp.where(causal, ci[:, 0:CHUNK] - cr + lir, -jnp.inf)
        row_max = jnp.max(log_d, axis=-1, keepdims=True)
        yield
        qk_h = _mm_nt(q, k)
        qc = _mm(q, c_prev)
        yield
        qn = _rowsum(q * n_prev)
        kw = k * jnp.exp(g - g_max)
        yield
        c_loc = _mm_tn(kw, v)
        n_loc = jnp.sum(kw, axis=0, keepdims=True)
        m_new = jnp.maximum(last + m_prev, g_max)
        a_old = jnp.exp(last + m_prev - m_new)
        a_new = jnp.exp(g_max - m_new)
        m_inter = ci + m_prev
        m_l = jnp.maximum(m_inter, row_max)
        wqk = qk_h * jnp.exp(log_d - m_l[:, 0:CHUNK])
        w_inter = jnp.exp(m_inter - m_l)
        yield
        c_ref[b, h] = a_old * c_prev + a_new * c_loc
        nm_ref[b, h, 0:1, :] = a_old * n_prev + a_new * n_loc
        nm_ref[b, h, 1:2, :] = m_new
        num = _mm(wqk, v) + w_inter * qc
        yield
        den = _rowsum(wqk) + w_inter * qn
        yield
        den = jnp.maximum(jnp.abs(den), jnp.exp(-m_l))
        hh = num / den * slab_ref[b, :, 1536 + h * MLSTM_HEAD:1536 + (h + 1) * MLSTM_HEAD]
        mu = _rowsum(hh) * (1.0 / MLSTM_HEAD)
        yield
        yc = hh - mu
        var = _rowsum(yc * yc) * (1.0 / MLSTM_HEAD)
        yield
        hh = yc * lax.rsqrt(var + NORM_EPS) * ng_ref[:, ls]
        o_ref[b, :, ls] = hh * slab_ref[b, :, 2048 + h * MLSTM_HEAD:2048 + (h + 1) * MLSTM_HEAD]

    gens = []
    for b in range(nb):
        qk = slab_ref[b, :, 0:2 * GROUP_W]
        logi_col = small_ref[b] + igb_ref[...]
        logf_col = _log_sigmoid(small_ref[b] + fgb_ref[...])
        cum_col = _mm_sel_lhs(tri_ref[...], logf_col)
        li_b = _mm_sel_rhs(logi_col, exi_ref[...])
        ci_b = _mm_sel_rhs(cum_col, exf_ref[...])
        pre_row = rows_ref[b, 0, ROWS_IF:ROWS_IF + 2 * MLSTM_HEADS, :] + gb_col_ref[...]
        logi_row = pre_row[0:MLSTM_HEADS, :]
        logf_row = _log_sigmoid(pre_row[MLSTM_HEADS:2 * MLSTM_HEADS, :])
        cum_row = _mm_sel_rhs(logf_row, triu_ref[...])
        gens += [head(b, h, qk, li_b, ci_b, logi_row, cum_row) for h in range(MLSTM_HEADS)]
    _lockstep(gens)


def _mlstm(slab, small, rows, igb, fgb, gb_col, ng, tri, triu, exi, exf):
    B, S, _ = slab.shape
    nb = BATCH_BLOCK
    const = lambda shape: pl.BlockSpec(shape, lambda b, c: (0,) * len(shape))
    return pl.pallas_call(
        functools.partial(_mlstm_kernel, nb=nb), grid=(B // nb, S // CHUNK),
        in_specs=[pl.BlockSpec((nb, CHUNK, MLSTM_SLAB), lambda b, c: (b, c, 0)),
                  pl.BlockSpec((nb, CHUNK, LANES), lambda b, c: (b, c, 0)),
                  pl.BlockSpec((nb, 1, ROWS_TOTAL, CHUNK), lambda b, c: (b, c, 0, 0)),
                  const((1, LANES)),
                  const((1, LANES)), const((2 * MLSTM_HEADS, 1)), const((1, GROUP_W)),
                  const((CHUNK, CHUNK)), const((CHUNK, CHUNK)),
                  const((LANES, GROUP_W)), const((LANES, GROUP_W))],
        out_specs=pl.BlockSpec((nb, CHUNK, GROUP_W), lambda b, c: (b, c, 0)),
        out_shape=jax.ShapeDtypeStruct((B, S, GROUP_W), F32),
        scratch_shapes=[pltpu.VMEM((nb, MLSTM_HEADS, MLSTM_HEAD, MLSTM_HEAD), F32),
                        pltpu.VMEM((nb, MLSTM_HEADS, CARRY_ROWS, MLSTM_HEAD), F32)],
        compiler_params=pltpu.CompilerParams(dimension_semantics=("arbitrary", "arbitrary")),
        name="mlstm")(slab, small, rows, igb, fgb, gb_col, ng, tri, triu, exi, exf)


def _rwkv_slab_width(layer):
    return 2176 if layer == 0 else 2304


def _rwkv_shift_cols(layer):
    return 1664 if layer == 0 else 1792


def _rwkv_kernel(*refs, layer, nb):
    if layer == 0:
        (slab_ref, w0_ref, a0_ref, w2a2_ref, kk_ref, ka_ref, rk_ref, lng_ref, lnb_ref,
         tri_ref, bd_ref, strict_ref, incl_ref, lvl_ref, o_ref, vf_out_ref, st_ref) = refs
    else:
        (slab_ref, vf_ref, w0_ref, a0_ref, w2a2_ref, v0_ref, v2_ref, kk_ref, ka_ref, rk_ref,
         lng_ref, lnb_ref, tri_ref, bd_ref, strict_ref, incl_ref, lvl_ref, o_ref, st_ref) = refs
    ws = _rwkv_shift_cols(layer)
    zoff = _rwkv_slab_width(layer) - GROUP_W
    npair = GROUP_W // LANES

    @pl.when(pl.program_id(1) == 0)
    def _():
        st_ref[...] = jnp.zeros_like(st_ref)

    lane = lax.broadcasted_iota(jnp.int32, (1, LANES), 1)
    masks = _lane_half_masks()
    bd = bd_ref[...]
    strict = strict_ref[...] > 0.5
    incl = incl_ref[...] > 0.5
    masks_b = (_bf(masks[0]), _bf(masks[1]))

    def rows2(t):
        tb = _bf(t)
        return jnp.concatenate([tb * masks_b[0], tb * masks_b[1]], axis=0)

    eye = (lax.broadcasted_iota(jnp.int32, (LANES, LANES), 0)
           == lax.broadcasted_iota(jnp.int32, (LANES, LANES), 1)).astype(F32)

    inv = 1.0 / RWKV_HEAD

    def sequence(b):
        inst = []
        f = slab_ref[b, :, 0:ws]
        r = f[:, 0:512]
        k = f[:, 512:1024]
        v = f[:, 1024:1536]
        lora = f[:, 1536:1664]
        lora = jnp.where(lane < RWKV_W_RANK, jnp.tanh(lora), lora)
        wa = _mm(lora, w2a2_ref[...])
        if layer == 0:
            vf_out_ref[b] = v
        else:
            mix = _sigmoid(v0_ref[...] + _mm(f[:, 1664:1792], v2_ref[...]))
            v = v + (vf_ref[b] - v) * mix
        yield
        w_log = -_softplus(-(w0_ref[...] + wa[:, 0:512])) - RWKV_DECAY_OFFSET
        lw = -jnp.exp(w_log)
        a = _sigmoid(a0_ref[...] + wa[:, 512:1024])
        kk = k * kk_ref[...]
        k = k * (1.0 + (a - 1.0) * ka_ref[...])
        cum = _mm_sel_lhs(tri_ref[...], lw)
        ss = [_mm_sel2(kk[:, p * LANES:(p + 1) * LANES] ** 2, bd) for p in range(npair)]
        yield
        last = cum[CHUNK - 1:CHUNK, :]
        e_pos = jnp.exp(cum)
        e_neg = jnp.exp(-cum)
        e_end = jnp.exp(last - cum)
        e_prev = jnp.exp(cum - lw)
        gam = jnp.exp(last)
        for p in range(npair):
            ls = slice(p * LANES, (p + 1) * LANES)
            kk_p = kk[:, ls] / jnp.maximum(jnp.sqrt(ss[p]), 1e-12)
            k_p, r_p, v_p = k[:, ls], r[:, ls], v[:, ls]
            b_p = kk_p * a[:, ls]
            inst.append(dict(
                p=p, ls=ls, r=r_p, k=k_p, v=v_p, gam=gam[:, ls],
                la=rows2(-kk_p * e_prev[:, ls]), lr=rows2(r_p * e_pos[:, ls]),
                rb=rows2(b_p * e_neg[:, ls]), rk=rows2(k_p * e_neg[:, ls]),
                bh=rows2(b_p * e_end[:, ls]), kh=rows2(k_p * e_end[:, ls]), vs=rows2(v_p)))
        yield
        for d in inst:
            aa = _mm_nt(jnp.concatenate([d['la'], d['lr']], axis=0),
                        jnp.concatenate([d['rb'], d['rk']], axis=0))
            d['nab'] = _bf(jnp.where(strict, aa[0:LANES, 0:LANES], 0.0))
            d['aak'] = _bf(jnp.where(strict, aa[0:LANES, LANES:2 * LANES], 0.0))
            d['arb'] = _bf(jnp.where(incl, aa[LANES:2 * LANES, 0:LANES], 0.0))
            d['ark'] = _bf(jnp.where(incl, aa[LANES:2 * LANES, LANES:2 * LANES], 0.0))
            d['t'] = eye + (d['nab'] * lvl_ref[0]).astype(F32)
        yield
        for d in inst:
            d['x'] = _mm(d['aak'], d['vs'])
        for lv in range(1, 6):
            for d in inst:
                d['tb'] = _bf(d['t'])
                d['nt'] = _mm(d['nab'] * lvl_ref[lv], d['tb'])
            yield
            for d in inst:
                d['t'] = d['t'] + _mm(d['tb'], d['nt'])
            yield
        for d in inst:
            d['wu'] = _mm(d['t'], jnp.concatenate([d['la'], _bf(d['x'])], axis=1))
        yield
        for d in inst:
            d['wub'] = _bf(d['wu'])
            qy = _mm(d['arb'], d['wub'])
            d['qt'] = d['lr'].astype(F32) + qy[:, 0:LANES]
            d['y0'] = _mm(d['ark'], d['vs']) + qy[:, LANES:2 * LANES]
        yield
        for d in inst:
            st = st_ref[b, d['p']]
            uy = _mm_nt(jnp.concatenate([d['wub'][:, 0:LANES], _bf(d['qt'])], axis=0), st)
            d['ust'] = uy[0:LANES] + d['wu'][:, LANES:2 * LANES]
            d['st'] = st
            yst = uy[LANES:2 * LANES] + d['y0']
            d['y'] = yst[0:CHUNK] + yst[CHUNK:2 * CHUNK]
        yield
        for d in inst:
            st_ref[b, d['p']] = d['st'] * d['gam'] + _mm_tn(jnp.concatenate([_bf(d['ust']), d['vs']], axis=0),
                                                            jnp.concatenate([d['bh'], d['kh']], axis=0))
            d['mu'] = _mm_sel2(d['y'], bd) * inv
            d['bonus'] = _mm_sel2(d['r'] * d['k'] * rk_ref[:, d['ls']], bd) * d['v']
        yield
        for d in inst:
            d['yc'] = d['y'] - d['mu']
            d['var'] = _mm_sel2(d['yc'] * d['yc'], bd) * inv
        yield
        for d in inst:
            ls = d['ls']
            yn = d['yc'] * lax.rsqrt(d['var'] + RWKV_LN_EPS) * lng_ref[:, ls] + lnb_ref[:, ls]
            z_p = slab_ref[b, :, zoff + d['p'] * LANES:zoff + (d['p'] + 1) * LANES]
            o_ref[b, :, ls] = (yn + d['bonus']) * z_p

    _lockstep([sequence(b) for b in range(nb)])


def _rwkv(layer, slab, vf, w0, a0, w2a2, v0, v2, kkw, ka, rk, lng, lnb, tri, bd, strict, incl, lvl):
    B, S, W = slab.shape
    nb = BATCH_BLOCK
    const = lambda shape: pl.BlockSpec(shape, lambda b, c: (0,) * len(shape))
    tok = lambda w: pl.BlockSpec((nb, CHUNK, w), lambda b, c: (b, c, 0))
    vecw = const((1, GROUP_W))
    in_specs = [tok(W)]
    args = [slab]
    if layer > 0:
        in_specs.append(tok(GROUP_W))
        args.append(vf)
    in_specs += [vecw, vecw, const((LANES, 2 * GROUP_W))]
    args += [w0, a0, w2a2]
    if layer > 0:
        in_specs += [vecw, const((LANES, GROUP_W))]
        args += [v0, v2]
    in_specs += [vecw, vecw, vecw, vecw, vecw, const((CHUNK, CHUNK)), const((LANES, LANES)),
                 const((LANES, LANES)), const((LANES, LANES)), const((6, LANES, LANES))]
    args += [kkw, ka, rk, lng, lnb, tri, bd, strict, incl, lvl]
    out_shape = [jax.ShapeDtypeStruct((B, S, GROUP_W), F32)]
    out_specs = [tok(GROUP_W)]
    if layer == 0:
        out_shape.append(jax.ShapeDtypeStruct((B, S, GROUP_W), F32))
        out_specs.append(tok(GROUP_W))
    res = pl.pallas_call(
        functools.partial(_rwkv_kernel, layer=layer, nb=nb), grid=(B // nb, S // CHUNK),
        in_specs=in_specs, out_specs=out_specs, out_shape=out_shape,
        scratch_shapes=[pltpu.VMEM((nb, GROUP_W // LANES, LANES, LANES), F32)],
        compiler_params=pltpu.CompilerParams(dimension_semantics=("arbitrary", "arbitrary"),
                                             vmem_limit_bytes=VMEM_LIMIT_PROJ),
        name="rwkv")(*args)
    return (res[0], res[1]) if layer == 0 else (res[0], vf)


def _np_consts():
    i = np.arange(CHUNK)
    tri = (i[None, :] <= i[:, None]).astype(np.float32)
    t = np.arange(LANES)
    same = (t[:, None] // CHUNK) == (t[None, :] // CHUNK)
    strict = (same & (t[None, :] < t[:, None])).astype(np.float32)
    incl = (same & (t[None, :] <= t[:, None])).astype(np.float32)
    bd = same.astype(np.float32)
    lvl = np.stack([(((t[:, None] >> l) == (t[None, :] >> l))
                     & ((t[:, None] >> (l - 1)) != (t[None, :] >> (l - 1)))).astype(np.float32)
                    for l in range(1, 7)])
    ex = np.zeros((LANES, GROUP_W), np.float32)
    for h in range(SSD_HEADS):
        ex[SMALL_DT + h, h * SSD_HEADDIM:(h + 1) * SSD_HEADDIM] = 1.0
    exi = np.zeros((LANES, GROUP_W), np.float32)
    exf = np.zeros((LANES, GROUP_W), np.float32)
    for h in range(MLSTM_HEADS):
        exi[SMALL_I + h, h * MLSTM_HEAD:(h + 1) * MLSTM_HEAD] = 1.0
        exf[SMALL_F + h, h * MLSTM_HEAD:(h + 1) * MLSTM_HEAD] = 1.0
    triu = tri.T.copy()
    zero = np.zeros_like(triu)
    triu2 = np.stack([np.concatenate([triu, zero], axis=1), np.concatenate([zero, triu], axis=1)])
    return dict(tri=tri, triu=triu, triu2=triu2, strict=strict, incl=incl, bd=bd, lvl=lvl, ex=ex,
                exi=exi, exf=exf)


def _pad_rows(w, height):
    return jnp.pad(w, ((0, height - w.shape[0]), (0, 0)))


def _pad_vec(v, width, offset=0):
    v = v.reshape(1, -1)
    return jnp.pad(v, ((0, 0), (offset, width - offset - v.shape[1])))


BF16_ROW_TILE = 16


def _layer(x2, hb, v_first, layer, p, B, S, consts, g_next, final):
    sw = 3 * GROUP_W + RWKV_W_RANK + RWKV_A_RANK + (RWKV_V_RANK if layer > 0 else 0)
    names = ['gla_q', 'gla_k', 'gla_v', 'gla_gk', 'gla_z', 'rwkv_shift', 'rwkv_z', 'ssd_xbc', 'ssd_dt',
             'pad0', 'ssd_z', 'mlstm_qk', 'mlstm_v', 'mlstm_i', 'mlstm_f', 'pad1', 'mlstm_o', 'mlstm_z']
    widths = [256, 256, 512, 16, 512, sw, 512, SSD_XBC, SSD_HEADS, 8, 512, 1024, 512, 4, 4, 8, 512, 512]
    offs = np.concatenate([[0], np.cumsum(widths)]).tolist()
    off = dict(zip(names, offs[:-1]))
    assert all(off[n] % BF16_ROW_TILE == 0 for n in names if n not in ('mlstm_f', 'pad0', 'pad1'))
    w_in = p['w_in']
    zpad = jnp.zeros((D_MODEL, 8), w_in.dtype)
    cut0, cut1 = off['pad0'], off['pad1'] - 8
    w_t = _bf(jnp.concatenate([w_in[:, :cut0], zpad, w_in[:, cut0:cut1], zpad, w_in[:, cut1:]], axis=1).T)
    rows = lambda name, n: w_t[off[name]:off[name] + n, :]
    tri, triu = consts['tri'], consts['triu']
    rows_of = lambda t: t.reshape(t.shape[0], B, S // CHUNK, CHUNK).transpose(1, 2, 0, 3)

    w_small = _pad_rows(jnp.concatenate([rows('gla_gk', GLA_RANK), rows('ssd_dt', SSD_HEADS),
                                         rows('mlstm_i', 2 * MLSTM_HEADS)], axis=0), LANES)
    even_odd = np.concatenate([np.arange(0, SSD_HEADS, 2), np.arange(1, SSD_HEADS, 2)])
    w_rows = jnp.concatenate([rows('ssd_dt', SSD_HEADS)[even_odd], rows('mlstm_i', 2 * MLSTM_HEADS)], axis=0)
    mu = p['rwkv_mu']
    if layer == 0:
        pieces, zero_lanes = [(off['rwkv_shift'], sw + GROUP_W, 0)], ()
        mu_p = mu.reshape(1, -1)
    else:
        pieces = [(off['rwkv_shift'], sw, 0), (off['rwkv_z'], GROUP_W, 1792)]
        zero_lanes = ((1664, 1792),)
        mu_p = _pad_vec(mu, 1792)
    ws = _rwkv_shift_cols(layer)
    wslab = _rwkv_slab_width(layer)
    slab, small, gate_rows = _in_proj(hb, w_t, pieces, wslab, S, zero_lanes, w_small, w_rows,
                                      mix=(0, ws, 'lerp'), mix_w=mu_p, mix_b=jnp.zeros((1, ws), F32),
                                      acts=((wslab - GROUP_W, wslab, 'silu'),))
    slab = slab.reshape(B, S, wslab)
    small = small.reshape(B, S, LANES)
    gate_rows = rows_of(gate_rows)
    w2a2 = jnp.zeros((LANES, 2 * GROUP_W), F32)
    w2a2 = w2a2.at[0:RWKV_W_RANK, 0:GROUP_W].set(p['rwkv_w2'])
    w2a2 = w2a2.at[RWKV_W_RANK:, GROUP_W:].set(p['rwkv_a2'])
    vec = lambda t: t.reshape(1, GROUP_W)
    v0 = v2 = None
    if layer > 0:
        v0 = vec(p['rwkv_v0'])
        v2 = _bf(jnp.pad(p['rwkv_v2'], ((0, LANES - RWKV_V_RANK), (0, 0))))
    y_rwkv, v_first = _rwkv(layer, slab, v_first, vec(p['rwkv_w0']), vec(p['rwkv_a0']), _bf(w2a2),
                            v0, v2, vec(p['rwkv_k_k']), vec(p['rwkv_k_a']), vec(p['rwkv_r_k']),
                            vec(p['rwkv_ln_g']), vec(p['rwkv_ln_b']), _bf(tri), _bf(consts['bd']),
                            consts['strict'], consts['incl'], _bf(consts['lvl']))

    slab = _in_proj(hb, w_t, [(off['gla_q'], 1024, 0), (off['gla_z'], GROUP_W, 1024)], GLA_SLAB, S,
                    acts=((1024, 1536, 'silu'),))
    w2p = _bf(jnp.pad(p['gla_gk_w2'], ((SMALL_GK, LANES - SMALL_GK - GLA_RANK), (0, 0))))
    y_gla = _gla(slab.reshape(B, S, GLA_SLAB), small, w2p, p['gla_gk_b'].reshape(1, -1),
                 p['gla_norm_g'].reshape(1, -1), _bf(tri))

    slab = _in_proj(hb, w_t, [(off['ssd_xbc'], SSD_XBC, 0), (off['ssd_z'], GROUP_W, SSD_XBC)], SSD_SLAB, S,
                    mix=(0, SSD_XBC, 'conv_silu'), mix_w=p['ssd_conv_w'], mix_b=p['ssd_conv_b'].reshape(1, -1),
                    acts=((SSD_XBC, SSD_SLAB, 'silu'),))
    a_neg = -jnp.exp(p['ssd_a_log'])
    y_ssd = _ssd(slab.reshape(B, S, SSD_SLAB), small, gate_rows,
                 _pad_vec(p['ssd_dt_bias'], LANES, SMALL_DT), _pad_vec(a_neg, LANES, SMALL_DT),
                 p['ssd_dt_bias'][even_odd].reshape(-1, 1), a_neg[even_odd].reshape(-1, 1),
                 jnp.repeat(p['ssd_d'], SSD_HEADDIM).reshape(1, -1), p['ssd_norm_g'].reshape(1, -1),
                 _bf(tri), _bf(consts['triu2']), _bf(consts['ex']))

    slab = _in_proj(hb, w_t, [(off['mlstm_qk'], 3 * GROUP_W, 0), (off['mlstm_o'], 2 * GROUP_W, 3 * GROUP_W)],
                    MLSTM_SLAB, S, mix=(0, 2 * GROUP_W, 'conv_silu'), mix_w=p['mlstm_conv_w'],
                    mix_b=p['mlstm_conv_b'].reshape(1, -1),
                    acts=((3 * GROUP_W, 4 * GROUP_W, 'sigmoid'), (4 * GROUP_W, 5 * GROUP_W, 'silu')))
    gb_col = jnp.concatenate([p['mlstm_ig_b'], p['mlstm_fg_b']]).reshape(-1, 1)
    y_ml = _mlstm(slab.reshape(B, S, MLSTM_SLAB), small, gate_rows,
                  _pad_vec(p['mlstm_ig_b'], LANES, SMALL_I),
                  _pad_vec(p['mlstm_fg_b'], LANES, SMALL_F), gb_col, p['mlstm_norm_g'].reshape(1, -1),
                  _bf(tri), _bf(triu), _bf(consts['exi']), _bf(consts['exf']))

    T = B * S
    ys = [y.reshape(T, GROUP_W) for y in (y_gla, y_rwkv, y_ssd, y_ml)]
    x_next, hb_next = _out_proj(x2, ys, _bf(p['w_out']), g_next, final)
    return x_next, hb_next, v_first


_PARAM_NAMES_0 = ['norm_g', 'w_in', 'w_out', 'gla_gk_w2', 'gla_gk_b', 'gla_norm_g', 'rwkv_mu', 'rwkv_w0',
                  'rwkv_w2', 'rwkv_a0', 'rwkv_a2', 'rwkv_k_k', 'rwkv_k_a', 'rwkv_r_k', 'rwkv_ln_g',
                  'rwkv_ln_b', 'ssd_conv_w', 'ssd_conv_b', 'ssd_dt_bias', 'ssd_a_log', 'ssd_d',
                  'ssd_norm_g', 'mlstm_conv_w', 'mlstm_conv_b', 'mlstm_ig_b', 'mlstm_fg_b', 'mlstm_norm_g']
_PARAM_NAMES_1 = (_PARAM_NAMES_0[:11] + ['rwkv_v0', 'rwkv_v2'] + _PARAM_NAMES_0[11:])


def kernel(x,
           norm_g_0, w_in_0, w_out_0, gla_gk_w2_0, gla_gk_b_0, gla_norm_g_0,
           rwkv_mu_0, rwkv_w0_0, rwkv_w2_0, rwkv_a0_0, rwkv_a2_0,
           rwkv_k_k_0, rwkv_k_a_0, rwkv_r_k_0, rwkv_ln_g_0, rwkv_ln_b_0,
           ssd_conv_w_0, ssd_conv_b_0, ssd_dt_bias_0, ssd_a_log_0, ssd_d_0, ssd_norm_g_0,
           mlstm_conv_w_0, mlstm_conv_b_0, mlstm_ig_b_0, mlstm_fg_b_0, mlstm_norm_g_0,
           norm_g_1, w_in_1, w_out_1, gla_gk_w2_1, gla_gk_b_1, gla_norm_g_1,
           rwkv_mu_1, rwkv_w0_1, rwkv_w2_1, rwkv_a0_1, rwkv_a2_1, rwkv_v0_1, rwkv_v2_1,
           rwkv_k_k_1, rwkv_k_a_1, rwkv_r_k_1, rwkv_ln_g_1, rwkv_ln_b_1,
           ssd_conv_w_1, ssd_conv_b_1, ssd_dt_bias_1, ssd_a_log_1, ssd_d_1, ssd_norm_g_1,
           mlstm_conv_w_1, mlstm_conv_b_1, mlstm_ig_b_1, mlstm_fg_b_1, mlstm_norm_g_1,
           final_norm_g):
    params = (norm_g_0, w_in_0, w_out_0, gla_gk_w2_0, gla_gk_b_0, gla_norm_g_0,
              rwkv_mu_0, rwkv_w0_0, rwkv_w2_0, rwkv_a0_0, rwkv_a2_0,
              rwkv_k_k_0, rwkv_k_a_0, rwkv_r_k_0, rwkv_ln_g_0, rwkv_ln_b_0,
              ssd_conv_w_0, ssd_conv_b_0, ssd_dt_bias_0, ssd_a_log_0, ssd_d_0, ssd_norm_g_0,
              mlstm_conv_w_0, mlstm_conv_b_0, mlstm_ig_b_0, mlstm_fg_b_0, mlstm_norm_g_0,
              norm_g_1, w_in_1, w_out_1, gla_gk_w2_1, gla_gk_b_1, gla_norm_g_1,
              rwkv_mu_1, rwkv_w0_1, rwkv_w2_1, rwkv_a0_1, rwkv_a2_1, rwkv_v0_1, rwkv_v2_1,
              rwkv_k_k_1, rwkv_k_a_1, rwkv_r_k_1, rwkv_ln_g_1, rwkv_ln_b_1,
              ssd_conv_w_1, ssd_conv_b_1, ssd_dt_bias_1, ssd_a_log_1, ssd_d_1, ssd_norm_g_1,
              mlstm_conv_w_1, mlstm_conv_b_1, mlstm_ig_b_1, mlstm_fg_b_1, mlstm_norm_g_1,
              final_norm_g)
    n0 = len(_PARAM_NAMES_0)
    n1 = len(_PARAM_NAMES_1)
    p0 = dict(zip(_PARAM_NAMES_0, params[:n0]))
    p1 = dict(zip(_PARAM_NAMES_1, params[n0:n0 + n1]))
    final_norm_g = params[n0 + n1]
    B, S, _ = x.shape
    consts = {k: jnp.asarray(v) for k, v in _np_consts().items()}
    x2 = x.reshape(B * S, D_MODEL)
    hb = _norm_bf16(x2, p0['norm_g'].reshape(1, D_MODEL))
    x2, hb, v_first = _layer(x2, hb, None, 0, p0, B, S, consts, p1['norm_g'].reshape(1, D_MODEL), False)
    x2, _, _ = _layer(x2, hb, v_first, 1, p1, B, S, consts, final_norm_g.reshape(1, D_MODEL), True)
    return x2.reshape(B, S, D_MODEL)
```

```python
import functools

import numpy as np
import jax
import jax.numpy as jnp
from jax import lax
from jax.experimental import pallas as pl
from jax.experimental.pallas import tpu as pltpu

F32 = jnp.float32
BF16 = jnp.bfloat16

D_MODEL = 2048
CHUNK = 64
GROUP_W = 512
NORM_EPS = 1e-6
LANES = 128
CARRY_ROWS = 8

GLA_HEADS, GLA_DK, GLA_DV, GLA_RANK = 4, 64, 128, 16
GLA_GATE_NORMALIZER = 16.0
RWKV_HEAD, RWKV_W_RANK, RWKV_A_RANK, RWKV_V_RANK = 64, 64, 64, 32
RWKV_LN_EPS = 64e-5
RWKV_DECAY_OFFSET = 0.5
SSD_HEADS, SSD_HEADDIM, SSD_STATE, SSD_CONV = 8, 64, 128, 4
SSD_XBC = 1024
MLSTM_HEADS, MLSTM_HEAD, MLSTM_CONV = 4, 128, 4

ROW_TILE_IN = 512
SMALL_GK, SMALL_DT, SMALL_I, SMALL_F = 0, 16, 24, 28
ROWS_DT, ROWS_IF, ROWS_TOTAL = 0, 8, 16
ROW_TILE_OUT = 512
VMEM_LIMIT_PROJ = 48 * 2**20


def _bf(x):
    return x.astype(BF16)


def _mm(a, b):
    return jnp.dot(_bf(a), _bf(b), preferred_element_type=F32)


def _mm_nt(a, b):
    return lax.dot_general(_bf(a), _bf(b), (((1,), (1,)), ((), ())), preferred_element_type=F32)


def _mm_tn(a, b):
    return lax.dot_general(_bf(a), _bf(b), (((0,), (0,)), ((), ())), preferred_element_type=F32)


def _split3(x):
    hi = _bf(x)
    r1 = x - hi.astype(F32)
    mid = _bf(r1)
    lo = _bf(r1 - mid.astype(F32))
    return hi, mid, lo


def _mm_sel_rhs(x, sel):
    hi, mid, lo = _split3(x)
    d = lambda a: jnp.dot(a, sel, preferred_element_type=F32)
    return d(hi) + d(mid) + d(lo)


def _mm_sel2(x, sel):
    hi = _bf(x)
    lo = _bf(x - hi.astype(F32))
    d = lambda a: jnp.dot(a, sel, preferred_element_type=F32)
    return d(hi) + d(lo)


def _mm_sel_lhs(sel, x):
    hi, mid, lo = _split3(x)
    d = lambda a: jnp.dot(sel, a, preferred_element_type=F32)
    return d(hi) + d(mid) + d(lo)


def _rowsum(x):
    ones = jnp.ones((x.shape[-1], LANES), BF16)
    hi = _bf(x)
    lo = _bf(x - hi.astype(F32))
    return (jnp.dot(hi, ones, preferred_element_type=F32)
            + jnp.dot(lo, ones, preferred_element_type=F32))


def _sigmoid(x):
    return 1.0 / (1.0 + jnp.exp(-x))


def _silu(x):
    return x * _sigmoid(x)


def _softplus(x):
    return jnp.maximum(x, 0.0) + jnp.log1p(jnp.exp(-jnp.abs(x)))


def _log_sigmoid(x):
    return -_softplus(-x)


def _lane_half_masks():
    lane = lax.broadcasted_iota(jnp.int32, (1, LANES), 1)
    lo = (lane < LANES // 2).astype(F32)
    return lo, 1.0 - lo


def _causal(n):
    r = lax.broadcasted_iota(jnp.int32, (n, n), 0)
    c = lax.broadcasted_iota(jnp.int32, (n, n), 1)
    return c <= r


def _shifted_rows(buf_ref, cur, offsets):
    buf_ref[CARRY_ROWS:CARRY_ROWS + CHUNK, :] = cur
    outs = [buf_ref[CARRY_ROWS - off:CARRY_ROWS - off + CHUNK, :] for off in offsets]
    tail = buf_ref[CHUNK:CHUNK + CARRY_ROWS, :]
    buf_ref[0:CARRY_ROWS, :] = tail
    return outs


def _in_proj_kernel(*refs, pieces, zero_lanes, has_small):
    x_ref, g_ref = refs[0], refs[1]
    w_refs = refs[2:2 + len(pieces)]
    rest = refs[2 + len(pieces):]
    if has_small:
        ws_ref, wr_ref, o_ref, os_ref, or_ref = rest
    else:
        (o_ref,) = rest
    nt = lambda a, b: lax.dot_general(a, b, (((1,), (1,)), ((), ())), preferred_element_type=F32)
    tm = x_ref.shape[0]
    for lo, hi in zero_lanes:
        o_ref[:, lo:hi] = jnp.zeros((tm, hi - lo), F32)
    for half in range(2):
        rs = slice(half * (tm // 2), (half + 1) * (tm // 2))
        x = x_ref[rs, :]
        h = x * lax.rsqrt(jnp.mean(x * x, axis=-1, keepdims=True) + NORM_EPS) * g_ref[...]
        hb = _bf(h)
        for w_ref, (_, n_rows, lane_off) in zip(w_refs, pieces):
            o_ref[rs, lane_off:lane_off + n_rows] = nt(hb, w_ref[...])
        if has_small:
            os_ref[rs, :] = nt(hb, ws_ref[...])
            or_ref[:, rs] = nt(wr_ref[...], hb)


def _in_proj(x2, g, w_t, pieces, width, zero_lanes=(), w_small=None, w_rows=None):
    T = x2.shape[0]
    tm = ROW_TILE_IN
    const2 = lambda shape: pl.BlockSpec(shape, lambda i: (0, 0))
    in_specs = [pl.BlockSpec((tm, D_MODEL), lambda i: (i, 0)), const2((1, D_MODEL))]
    args = [x2, g]
    for (part, off), n_rows, _ in pieces:
        in_specs.append(pl.BlockSpec((pl.Element(n_rows), pl.Element(D_MODEL)), lambda i, off=off: (off, 0)))
        args.append(w_t[part])
    out_shape = [jax.ShapeDtypeStruct((T, width), F32)]
    out_specs = [pl.BlockSpec((tm, width), lambda i: (i, 0))]
    if w_small is not None:
        r = w_rows.shape[0]
        in_specs += [const2((LANES, D_MODEL)), const2((r, D_MODEL))]
        args += [w_small, w_rows]
        out_shape += [jax.ShapeDtypeStruct((T, LANES), F32), jax.ShapeDtypeStruct((r, T), F32)]
        out_specs += [pl.BlockSpec((tm, LANES), lambda i: (i, 0)), pl.BlockSpec((r, tm), lambda i: (0, i))]
    res = pl.pallas_call(
        functools.partial(_in_proj_kernel, pieces=tuple(pieces), zero_lanes=tuple(zero_lanes),
                          has_small=w_small is not None),
        grid=(T // tm,), in_specs=in_specs, out_specs=out_specs, out_shape=out_shape,
        compiler_params=pltpu.CompilerParams(dimension_semantics=("arbitrary",),
                                             vmem_limit_bytes=VMEM_LIMIT_PROJ),
        name="in_proj")(*args)
    return res if w_small is not None else res[0]


def _out_proj_kernel(*refs, final):
    if final:
        x_ref, y0, y1, y2, y3, w_ref, g_ref, o_ref = refs
    else:
        x_ref, y0, y1, y2, y3, w_ref, o_ref = refs
    acc = x_ref[...]
    for gi, y in enumerate((y0, y1, y2, y3)):
        acc = acc + jnp.dot(_bf(y[...]), w_ref[gi * GROUP_W:(gi + 1) * GROUP_W, :],
                            preferred_element_type=F32)
    if final:
        acc = acc * lax.rsqrt(jnp.mean(acc * acc, axis=-1, keepdims=True) + NORM_EPS) * g_ref[...]
    o_ref[...] = acc


def _out_proj(x2, ys, w, g_final=None):
    T = x2.shape[0]
    tm = ROW_TILE_OUT
    final = g_final is not None
    in_specs = [pl.BlockSpec((tm, D_MODEL), lambda i: (i, 0))]
    in_specs += [pl.BlockSpec((tm, GROUP_W), lambda i: (i, 0)) for _ in range(4)]
    in_specs += [pl.BlockSpec((D_MODEL, D_MODEL), lambda i: (0, 0))]
    args = [x2, *ys, w]
    if final:
        in_specs.append(pl.BlockSpec((1, D_MODEL), lambda i: (0, 0)))
        args.append(g_final)
    return pl.pallas_call(
        functools.partial(_out_proj_kernel, final=final),
        grid=(T // tm,), in_specs=in_specs,
        out_specs=pl.BlockSpec((tm, D_MODEL), lambda i: (i, 0)),
        out_shape=jax.ShapeDtypeStruct((T, D_MODEL), F32),
        compiler_params=pltpu.CompilerParams(dimension_semantics=("arbitrary",),
                                             vmem_limit_bytes=VMEM_LIMIT_PROJ),
        name="out_proj")(*args)


GLA_SLAB = 1536


BATCH_BLOCK = 4
MIXER_STARTS = (5, 5, 12)


def _lockstep(gens, starts=None):
    gens = list(gens)
    starts = [0] * len(gens) if starts is None else list(starts)
    pending = list(zip(starts, gens))
    tick = 0
    while pending:
        alive = []
        for start, g in pending:
            if start > tick:
                alive.append((start, g))
                continue
            try:
                next(g)
                alive.append((start, g))
            except StopIteration:
                pass
        pending = alive
        tick += 1


def _stepper(gens):
    gens = list(gens)
    while gens:
        alive = []
        for g in gens:
            try:
                next(g)
                alive.append(g)
            except StopIteration:
                pass
        gens = alive
        yield


def _mixers_kernel(*refs, parts):
    n_in = sum(p[1] for p in parts)
    n_out = sum(p[2] for p in parts)
    ins, outs, scs = refs[:n_in], refs[n_in:n_in + n_out], refs[n_in + n_out:]
    gens, starts, posts = [], [], []
    i = o = s = 0
    for stages_fn, ni, no, ns, start in parts:
        g, post = stages_fn(*ins[i:i + ni], *outs[o:o + no], *scs[s:s + ns])
        i, o, s = i + ni, o + no, s + ns
        gens += g
        starts += [start] * len(g)
        posts.append(post)
    _lockstep(gens, starts)
    for post in posts:
        if post is not None:
            post()


def _run_mixers(descs, B, S, name):
    nb = BATCH_BLOCK
    parts = tuple((d['stages'], len(d['args']), len(d['out_shape']), len(d['scratch']), d.get('start', 0))
                  for d in descs)
    res = pl.pallas_call(
        functools.partial(_mixers_kernel, parts=parts), grid=(B // nb, S // CHUNK),
        in_specs=[sp for d in descs for sp in d['in_specs']],
        out_specs=[sp for d in descs for sp in d['out_specs']],
        out_shape=[sh for d in descs for sh in d['out_shape']],
        scratch_shapes=[sc for d in descs for sc in d['scratch']],
        compiler_params=pltpu.CompilerParams(dimension_semantics=("arbitrary", "arbitrary"),
                                             vmem_limit_bytes=VMEM_LIMIT_PROJ),
        name=name)(*[a for d in descs for a in d['args']])
    out, k = [], 0
    for d in descs:
        out.append(res[k:k + len(d['out_shape'])])
        k += len(d['out_shape'])
    return out


def _gla_stages(slab_ref, small_ref, w2_ref, gkb_ref, ng_ref, tri_ref, o_ref, st_ref, *, nb):
    @pl.when(pl.program_id(1) == 0)
    def _():
        st_ref[...] = jnp.zeros_like(st_ref)

    causal = _causal(CHUNK)
    masks = _lane_half_masks()

    def head(b, h, qg, kg, kd, dec):
        p, j = divmod(h, 2)
        ls = slice(p * LANES, (p + 1) * LANES)
        qm = qg[:, ls] * masks[j]
        att = jnp.where(causal, _mm_nt(qm, kg[:, ls]), 0.0)
        yield
        v_h = slab_ref[b, :, 512 + h * GLA_DV:512 + (h + 1) * GLA_DV]
        st = st_ref[b, h]
        o = _mm(att, v_h) + _mm_nt(qm, st)
        st_ref[b, h] = st * dec[:, ls] + _mm_tn(v_h, kd[:, ls] * masks[j])
        yield
        ms = jnp.mean(o * o, axis=-1, keepdims=True)
        yield
        o = o * lax.rsqrt(ms + NORM_EPS)
        o = o * ng_ref[:, h * GLA_DV:(h + 1) * GLA_DV]
        z_h = slab_ref[b, :, 1024 + h * GLA_DV:1024 + (h + 1) * GLA_DV]
        o_ref[b, :, h * GLA_DV:(h + 1) * GLA_DV] = o * _silu(z_h)

    def sequence(b):
        q = slab_ref[b, :, 0:256] * (GLA_DK ** -0.5)
        k = slab_ref[b, :, 256:512]
        gk = _mm(small_ref[b], w2_ref[...]) + gkb_ref[...]
        log_a = _log_sigmoid(gk) / GLA_GATE_NORMALIZER
        cum = _mm_sel_lhs(tri_ref[...], log_a)
        last = cum[CHUNK - 1:CHUNK, :]
        qg = q * jnp.exp(cum)
        kg = k * jnp.exp(-cum)
        kd = k * jnp.exp(last - cum)
        dec = jnp.exp(last)
        yield
        yield from _stepper([head(b, h, qg, kg, kd, dec) for h in range(GLA_HEADS)])

    return [sequence(b) for b in range(nb)], None


def _gla(slab, small, w2p, gkb, ng, tri):
    B, S, _ = slab.shape
    nb = BATCH_BLOCK
    const = lambda shape: pl.BlockSpec(shape, lambda b, c: (0,) * len(shape))
    return dict(
        stages=functools.partial(_gla_stages, nb=nb),
        in_specs=[pl.BlockSpec((nb, CHUNK, GLA_SLAB), lambda b, c: (b, c, 0)),
                  pl.BlockSpec((nb, CHUNK, LANES), lambda b, c: (b, c, 0)),
                  const((LANES, 256)), const((1, 256)), const((1, GROUP_W)), const((CHUNK, CHUNK))],
        args=[slab, small, w2p, gkb, ng, tri],
        out_specs=[pl.BlockSpec((nb, CHUNK, GROUP_W), lambda b, c: (b, c, 0))],
        out_shape=[jax.ShapeDtypeStruct((B, S, GROUP_W), F32)],
        scratch=[pltpu.VMEM((nb, GLA_HEADS, GLA_DV, LANES), F32)])


SSD_SLAB = 1536


def _ssd_stages(slab_ref, small_ref, rows_ref, cw_ref, cb_ref, dtb_ref, a_ref, dtb_col_ref, a_col_ref,
                dskip_ref, ng_ref, tri_ref, triu2_ref, ex_ref, o_ref, buf_ref, st_ref, *, nb):
    @pl.when(pl.program_id(1) == 0)
    def _():
        st_ref[...] = jnp.zeros_like(st_ref)
        for b in range(nb):
            buf_ref[b, 0:CARRY_ROWS, :] = jnp.zeros((CARRY_ROWS, SSD_XBC), F32)

    row_i = lax.broadcasted_iota(jnp.int32, (CHUNK, LANES), 0)
    col_i = lax.broadcasted_iota(jnp.int32, (CHUNK, LANES), 1)
    causal2 = jnp.bitwise_and(col_i, CHUNK - 1) <= row_i
    masks = _lane_half_masks()
    masks_b = (_bf(masks[0]), _bf(masks[1]))

    def pair(b, p, xbc, cum_b, cum_row2, xdt, xw, ecum, dec):
        g = p // 2
        ls = slice(p * LANES, (p + 1) * LANES)
        bm = xbc[:, 512 + g * SSD_STATE:512 + (g + 1) * SSD_STATE]
        cm = xbc[:, 768 + g * SSD_STATE:768 + (g + 1) * SSD_STATE]
        bmb = _bf(bm)
        cbm2 = _mm_nt(cm, jnp.concatenate([bmb, bmb], axis=0))
        st = st_ref[b, p]
        y = _mm(cm, st) * ecum[:, ls]
        st_ref[b, p] = st * dec[:, ls] + _mm_tn(bmb, xw[:, ls])
        lmat = jnp.exp(jnp.where(causal2, cum_b[:, ls] - cum_row2[p:p + 1, :], -jnp.inf))
        yield
        xb = _bf(xdt[:, ls])
        xs = jnp.concatenate([xb * masks_b[0], xb * masks_b[1]], axis=0)
        y = y + _mm(cbm2 * lmat, xs)
        yield
        y = y + dskip_ref[:, ls] * xbc[:, ls]
        o_ref[b, :, ls] = y * _silu(slab_ref[b, :, 1024 + p * LANES:1024 + (p + 1) * LANES])

    def sequence(b):
        taps = _shifted_rows(buf_ref.at[b], slab_ref[b, :, 0:SSD_XBC], (3, 2, 1, 0))
        xbc = cb_ref[...]
        for j in range(SSD_CONV):
            xbc = xbc + taps[j] * cw_ref[j:j + 1, :]
        xbc = _silu(xbc)
        yield
        dt_col = _softplus(small_ref[b] + dtb_ref[...])
        cum_col = _mm_sel_lhs(tri_ref[...], dt_col * a_ref[...])
        dt_row = _softplus(rows_ref[b, 0, ROWS_DT:ROWS_DT + SSD_HEADS, :] + dtb_col_ref[...])
        da_row = dt_row * a_col_ref[...]
        cum_row2 = (_mm_sel_rhs(da_row[0:SSD_HEADS // 2], triu2_ref[0])
                    + _mm_sel_rhs(da_row[SSD_HEADS // 2:SSD_HEADS], triu2_ref[1]))
        dt_b = _mm_sel_rhs(dt_col, ex_ref[...])
        cum_b = _mm_sel_rhs(cum_col, ex_ref[...])
        last_b = cum_b[CHUNK - 1:CHUNK, :]
        xdt = xbc[:, 0:512] * dt_b
        xw = xdt * jnp.exp(last_b - cum_b)
        ecum = jnp.exp(cum_b)
        dec = jnp.exp(last_b)
        yield
        yield from _stepper([pair(b, p, xbc, cum_b, cum_row2, xdt, xw, ecum, dec)
                             for p in range(SSD_HEADS // 2)])

    gens = [sequence(b) for b in range(nb)]

    def group_norm():
        for b in range(nb):
            y = o_ref[b]
            inv = lax.rsqrt(_rowsum(y * y) * (1.0 / GROUP_W) + NORM_EPS)
            for p in range(GROUP_W // LANES):
                ls = slice(p * LANES, (p + 1) * LANES)
                o_ref[b, :, ls] = y[:, ls] * inv * ng_ref[:, ls]

    return gens, group_norm


def _ssd(slab, small, rows, cw, cb, dtb, a, dtb_col, a_col, dskip, ng, tri, triu, ex):
    B, S, _ = slab.shape
    nb = BATCH_BLOCK
    const = lambda shape: pl.BlockSpec(shape, lambda b, c: (0,) * len(shape))
    return dict(
        stages=functools.partial(_ssd_stages, nb=nb),
        in_specs=[pl.BlockSpec((nb, CHUNK, SSD_SLAB), lambda b, c: (b, c, 0)),
                  pl.BlockSpec((nb, CHUNK, LANES), lambda b, c: (b, c, 0)),
                  pl.BlockSpec((nb, 1, ROWS_TOTAL, CHUNK), lambda b, c: (b, c, 0, 0)),
                  const((SSD_CONV, SSD_XBC)), const((1, SSD_XBC)), const((1, LANES)), const((1, LANES)),
                  const((SSD_HEADS, 1)), const((SSD_HEADS, 1)), const((1, GROUP_W)), const((1, GROUP_W)),
                  const((CHUNK, CHUNK)), const((2, CHUNK, LANES)), const((LANES, GROUP_W))],
        args=[slab, small, rows, cw, cb, dtb, a, dtb_col, a_col, dskip, ng, tri, triu, ex],
        out_specs=[pl.BlockSpec((nb, CHUNK, GROUP_W), lambda b, c: (b, c, 0))],
        out_shape=[jax.ShapeDtypeStruct((B, S, GROUP_W), F32)],
        scratch=[pltpu.VMEM((nb, CARRY_ROWS + CHUNK, SSD_XBC), F32),
                 pltpu.VMEM((nb, SSD_HEADS // 2, SSD_STATE, LANES), F32)])


MLSTM_SLAB = 2560


def _mlstm_stages(slab_ref, small_ref, rows_ref, cw_ref, cb_ref, igb_ref, fgb_ref, gb_col_ref, ng_ref,
                  tri_ref, triu_ref, exi_ref, exf_ref, o_ref, buf_ref, c_ref, nm_ref, *, nb):
    @pl.when(pl.program_id(1) == 0)
    def _():
        c_ref[...] = jnp.zeros_like(c_ref)
        nm_ref[...] = jnp.zeros_like(nm_ref)
        for b in range(nb):
            buf_ref[b, 0:CARRY_ROWS, :] = jnp.zeros((CARRY_ROWS, 2 * GROUP_W), F32)

    causal = _causal(CHUNK)

    def head(b, h, qk, li_b, ci_b, logi_row, cum_row):
        ls = slice(h * MLSTM_HEAD, (h + 1) * MLSTM_HEAD)
        q = qk[:, ls]
        k = qk[:, GROUP_W + h * MLSTM_HEAD:GROUP_W + (h + 1) * MLSTM_HEAD] * (MLSTM_HEAD ** -0.5)
        v = slab_ref[b, :, 1024 + h * MLSTM_HEAD:1024 + (h + 1) * MLSTM_HEAD]
        ci = ci_b[:, ls]
        li = li_b[:, ls]
        cr = cum_row[h:h + 1, :]
        lir = logi_row[h:h + 1, :]
        last = ci[CHUNK - 1:CHUNK, :]
        c_prev = c_ref[b, h]
        n_prev = nm_ref[b, h, 0:1, :]
        m_prev = nm_ref[b, h, 1:2, :]

        g = last - ci + li
        g_max = jnp.max(g, axis=0, keepdims=True)
        log_d = jnp.where(causal, ci[:, 0:CHUNK] - cr + lir, -jnp.inf)
        row_max = jnp.max(log_d, axis=-1, keepdims=True)
        qk_h = _mm_nt(q, k)
        qc = _mm(q, c_prev)
        qn = _rowsum(q * n_prev)
        yield
        kw = k * jnp.exp(g - g_max)
        c_loc = _mm_tn(kw, v)
        n_loc = jnp.sum(kw, axis=0, keepdims=True)
        m_new = jnp.maximum(last + m_prev, g_max)
        a_old = jnp.exp(last + m_prev - m_new)
        a_new = jnp.exp(g_max - m_new)
        c_ref[b, h] = a_old * c_prev + a_new * c_loc
        nm_ref[b, h, 0:1, :] = a_old * n_prev + a_new * n_loc
        nm_ref[b, h, 1:2, :] = m_new
        m_inter = ci + m_prev
        m_l = jnp.maximum(m_inter, row_max)
        wqk = qk_h * jnp.exp(log_d - m_l[:, 0:CHUNK])
        w_inter = jnp.exp(m_inter - m_l)
        num = _mm(wqk, v) + w_inter * qc
        den = _rowsum(wqk) + w_inter * qn
        yield
        den = jnp.maximum(jnp.abs(den), jnp.exp(-m_l))
        hh = num / den * _sigmoid(slab_ref[b, :, 1536 + h * MLSTM_HEAD:1536 + (h + 1) * MLSTM_HEAD])
        mu = _rowsum(hh) * (1.0 / MLSTM_HEAD)
        yield
        yc = hh - mu
        var = _rowsum(yc * yc) * (1.0 / MLSTM_HEAD)
        yield
        hh = yc * lax.rsqrt(var + NORM_EPS) * ng_ref[:, ls]
        o_ref[b, :, ls] = hh * _silu(slab_ref[b, :, 2048 + h * MLSTM_HEAD:2048 + (h + 1) * MLSTM_HEAD])

    def sequence(b):
        taps = _shifted_rows(buf_ref.at[b], slab_ref[b, :, 0:2 * GROUP_W], (3, 2, 1, 0))
        qk = cb_ref[...]
        for j in range(MLSTM_CONV):
            qk = qk + taps[j] * cw_ref[j:j + 1, :]
        qk = _silu(qk)
        yield
        logi_col = small_ref[b] + igb_ref[...]
        logf_col = _log_sigmoid(small_ref[b] + fgb_ref[...])
        cum_col = _mm_sel_lhs(tri_ref[...], logf_col)
        li_b = _mm_sel_rhs(logi_col, exi_ref[...])
        ci_b = _mm_sel_rhs(cum_col, exf_ref[...])
        pre_row = rows_ref[b, 0, ROWS_IF:ROWS_IF + 2 * MLSTM_HEADS, :] + gb_col_ref[...]
        logi_row = pre_row[0:MLSTM_HEADS, :]
        logf_row = _log_sigmoid(pre_row[MLSTM_HEADS:2 * MLSTM_HEADS, :])
        cum_row = _mm_sel_rhs(logf_row, triu_ref[...])
        yield
        yield from _stepper([head(b, h, qk, li_b, ci_b, logi_row, cum_row) for h in range(MLSTM_HEADS)])

    return [sequence(b) for b in range(nb)], None


def _mlstm(slab, small, rows, cw, cb, igb, fgb, gb_col, ng, tri, triu, exi, exf):
    B, S, _ = slab.shape
    nb = BATCH_BLOCK
    const = lambda shape: pl.BlockSpec(shape, lambda b, c: (0,) * len(shape))
    return dict(
        stages=functools.partial(_mlstm_stages, nb=nb),
        in_specs=[pl.BlockSpec((nb, CHUNK, MLSTM_SLAB), lambda b, c: (b, c, 0)),
                  pl.BlockSpec((nb, CHUNK, LANES), lambda b, c: (b, c, 0)),
                  pl.BlockSpec((nb, 1, ROWS_TOTAL, CHUNK), lambda b, c: (b, c, 0, 0)),
                  const((MLSTM_CONV, 2 * GROUP_W)), const((1, 2 * GROUP_W)), const((1, LANES)),
                  const((1, LANES)), const((2 * MLSTM_HEADS, 1)), const((1, GROUP_W)),
                  const((CHUNK, CHUNK)), const((CHUNK, CHUNK)),
                  const((LANES, GROUP_W)), const((LANES, GROUP_W))],
        args=[slab, small, rows, cw, cb, igb, fgb, gb_col, ng, tri, triu, exi, exf],
        out_specs=[pl.BlockSpec((nb, CHUNK, GROUP_W), lambda b, c: (b, c, 0))],
        out_shape=[jax.ShapeDtypeStruct((B, S, GROUP_W), F32)],
        scratch=[pltpu.VMEM((nb, CARRY_ROWS + CHUNK, 2 * GROUP_W), F32),
                 pltpu.VMEM((nb, MLSTM_HEADS, MLSTM_HEAD, MLSTM_HEAD), F32),
                 pltpu.VMEM((nb, MLSTM_HEADS, CARRY_ROWS, MLSTM_HEAD), F32)])


def _rwkv_slab_width(layer):
    return 2176 if layer == 0 else 2304


def _rwkv_shift_cols(layer):
    return 1664 if layer == 0 else 1792


def _rwkv_stages(*refs, layer, nb):
    if layer == 0:
        (slab_ref, mu_ref, w0_ref, a0_ref, w2a2_ref, kk_ref, ka_ref, rk_ref, lng_ref, lnb_ref,
         tri_ref, bd_ref, strict_ref, incl_ref, lvl_ref, o_ref, vf_out_ref, buf_ref, st_ref) = refs
    else:
        (slab_ref, vf_ref, mu_ref, w0_ref, a0_ref, w2a2_ref, v0_ref, v2_ref, kk_ref, ka_ref, rk_ref,
         lng_ref, lnb_ref, tri_ref, bd_ref, strict_ref, incl_ref, lvl_ref, o_ref, buf_ref, st_ref) = refs
    ws = _rwkv_shift_cols(layer)
    zoff = _rwkv_slab_width(layer) - GROUP_W
    npair = GROUP_W // LANES

    @pl.when(pl.program_id(1) == 0)
    def _():
        st_ref[...] = jnp.zeros_like(st_ref)
        for b in range(nb):
            buf_ref[b, 0:CARRY_ROWS, :] = jnp.zeros((CARRY_ROWS, ws), F32)

    lane = lax.broadcasted_iota(jnp.int32, (1, LANES), 1)
    masks = _lane_half_masks()
    bd = bd_ref[...]
    strict = strict_ref[...] > 0.5
    incl = incl_ref[...] > 0.5
    masks_b = (_bf(masks[0]), _bf(masks[1]))

    def rows2(t):
        tb = _bf(t)
        return jnp.concatenate([tb * masks_b[0], tb * masks_b[1]], axis=0)

    eye = (lax.broadcasted_iota(jnp.int32, (LANES, LANES), 0)
           == lax.broadcasted_iota(jnp.int32, (LANES, LANES), 1)).astype(F32)

    inv = 1.0 / RWKV_HEAD

    def sequence(b):
        inst = []
        f = slab_ref[b, :, 0:ws]
        (prev,) = _shifted_rows(buf_ref.at[b], f, (1,))
        f = f + mu_ref[...] * (prev - f)
        r = f[:, 0:512]
        k = f[:, 512:1024]
        v = f[:, 1024:1536]
        lora = f[:, 1536:1664]
        lora = jnp.where(lane < RWKV_W_RANK, jnp.tanh(lora), lora)
        wa = _mm(lora, w2a2_ref[...])
        if layer == 0:
            vf_out_ref[b] = v
        else:
            mix = _sigmoid(v0_ref[...] + _mm(f[:, 1664:1792], v2_ref[...]))
            v = v + (vf_ref[b] - v) * mix
        yield
        w_log = -_softplus(-(w0_ref[...] + wa[:, 0:512])) - RWKV_DECAY_OFFSET
        lw = -jnp.exp(w_log)
        a = _sigmoid(a0_ref[...] + wa[:, 512:1024])
        kk = k * kk_ref[...]
        k = k * (1.0 + (a - 1.0) * ka_ref[...])
        cum = _mm_sel_lhs(tri_ref[...], lw)
        ss = [_mm_sel2(kk[:, p * LANES:(p + 1) * LANES] ** 2, bd) for p in range(npair)]
        yield
        last = cum[CHUNK - 1:CHUNK, :]
        e_pos = jnp.exp(cum)
        e_neg = jnp.exp(-cum)
        e_end = jnp.exp(last - cum)
        e_prev = jnp.exp(cum - lw)
        gam = jnp.exp(last)
        for p in range(npair):
            ls = slice(p * LANES, (p + 1) * LANES)
            kk_p = kk[:, ls] / jnp.maximum(jnp.sqrt(ss[p]), 1e-12)
            k_p, r_p, v_p = k[:, ls], r[:, ls], v[:, ls]
            b_p = kk_p * a[:, ls]
            inst.append(dict(
                p=p, ls=ls, r=r_p, k=k_p, v=v_p, gam=gam[:, ls],
                la=rows2(-kk_p * e_prev[:, ls]), lr=rows2(r_p * e_pos[:, ls]),
                rb=rows2(b_p * e_neg[:, ls]), rk=rows2(k_p * e_neg[:, ls]),
                bh=rows2(b_p * e_end[:, ls]), kh=rows2(k_p * e_end[:, ls]), vs=rows2(v_p)))
        yield
        for d in inst:
            aa = _mm_nt(jnp.concatenate([d['la'], d['lr']], axis=0),
                        jnp.concatenate([d['rb'], d['rk']], axis=0))
            d['nab'] = _bf(jnp.where(strict, aa[0:LANES, 0:LANES], 0.0))
            d['aak'] = _bf(jnp.where(strict, aa[0:LANES, LANES:2 * LANES], 0.0))
            d['arb'] = _bf(jnp.where(incl, aa[LANES:2 * LANES, 0:LANES], 0.0))
            d['ark'] = _bf(jnp.where(incl, aa[LANES:2 * LANES, LANES:2 * LANES], 0.0))
            d['t'] = eye + (d['nab'] * lvl_ref[0]).astype(F32)
        yield
        for d in inst:
            d['x'] = _mm(d['aak'], d['vs'])
        for lv in range(1, 6):
            for d in inst:
                d['tb'] = _bf(d['t'])
                d['nt'] = _mm(d['nab'] * lvl_ref[lv], d['tb'])
            yield
            for d in inst:
                d['t'] = d['t'] + _mm(d['tb'], d['nt'])
            yield
        for d in inst:
            d['wu'] = _mm(d['t'], jnp.concatenate([d['la'], _bf(d['x'])], axis=1))
        yield
        for d in inst:
            d['wub'] = _bf(d['wu'])
            qy = _mm(d['arb'], d['wub'])
            d['qt'] = d['lr'].astype(F32) + qy[:, 0:LANES]
            d['y0'] = _mm(d['ark'], d['vs']) + qy[:, LANES:2 * LANES]
        yield
        for d in inst:
            st = st_ref[b, d['p']]
            uy = _mm_nt(jnp.concatenate([d['wub'][:, 0:LANES], _bf(d['qt'])], axis=0), st)
            d['ust'] = uy[0:LANES] + d['wu'][:, LANES:2 * LANES]
            d['st'] = st
            yst = uy[LANES:2 * LANES] + d['y0']
            d['y'] = yst[0:CHUNK] + yst[CHUNK:2 * CHUNK]
        yield
        for d in inst:
            st_ref[b, d['p']] = d['st'] * d['gam'] + _mm_tn(jnp.concatenate([_bf(d['ust']), d['vs']], axis=0),
                                                            jnp.concatenate([d['bh'], d['kh']], axis=0))
            d['mu'] = _mm_sel2(d['y'], bd) * inv
            d['bonus'] = _mm_sel2(d['r'] * d['k'] * rk_ref[:, d['ls']], bd) * d['v']
        yield
        for d in inst:
            d['yc'] = d['y'] - d['mu']
            d['var'] = _mm_sel2(d['yc'] * d['yc'], bd) * inv
        yield
        for d in inst:
            ls = d['ls']
            yn = d['yc'] * lax.rsqrt(d['var'] + RWKV_LN_EPS) * lng_ref[:, ls] + lnb_ref[:, ls]
            z_p = slab_ref[b, :, zoff + d['p'] * LANES:zoff + (d['p'] + 1) * LANES]
            o_ref[b, :, ls] = (yn + d['bonus']) * _silu(z_p)

    return [sequence(b) for b in range(nb)], None


def _rwkv(layer, slab, vf, mu, w0, a0, w2a2, v0, v2, kkw, ka, rk, lng, lnb, tri, bd, strict, incl, lvl):
    B, S, W = slab.shape
    nb = BATCH_BLOCK
    ws = _rwkv_shift_cols(layer)
    const = lambda shape: pl.BlockSpec(shape, lambda b, c: (0,) * len(shape))
    tok = lambda w: pl.BlockSpec((nb, CHUNK, w), lambda b, c: (b, c, 0))
    vecw = const((1, GROUP_W))
    in_specs = [tok(W)]
    args = [slab]
    if layer > 0:
        in_specs.append(tok(GROUP_W))
        args.append(vf)
    in_specs += [const((1, ws)), vecw, vecw, const((LANES, 2 * GROUP_W))]
    args += [mu, w0, a0, w2a2]
    if layer > 0:
        in_specs += [vecw, const((LANES, GROUP_W))]
        args += [v0, v2]
    in_specs += [vecw, vecw, vecw, vecw, vecw, const((CHUNK, CHUNK)), const((LANES, LANES)),
                 const((LANES, LANES)), const((LANES, LANES)), const((6, LANES, LANES))]
    args += [kkw, ka, rk, lng, lnb, tri, bd, strict, incl, lvl]
    out_shape = [jax.ShapeDtypeStruct((B, S, GROUP_W), F32)]
    out_specs = [tok(GROUP_W)]
    if layer == 0:
        out_shape.append(jax.ShapeDtypeStruct((B, S, GROUP_W), F32))
        out_specs.append(tok(GROUP_W))
    return dict(
        stages=functools.partial(_rwkv_stages, layer=layer, nb=nb),
        in_specs=in_specs, args=args, out_specs=out_specs, out_shape=out_shape,
        scratch=[pltpu.VMEM((nb, CARRY_ROWS + CHUNK, ws), F32),
                 pltpu.VMEM((nb, GROUP_W // LANES, LANES, LANES), F32)])


def _np_consts():
    i = np.arange(CHUNK)
    tri = (i[None, :] <= i[:, None]).astype(np.float32)
    t = np.arange(LANES)
    same = (t[:, None] // CHUNK) == (t[None, :] // CHUNK)
    strict = (same & (t[None, :] < t[:, None])).astype(np.float32)
    incl = (same & (t[None, :] <= t[:, None])).astype(np.float32)
    bd = same.astype(np.float32)
    lvl = np.stack([(((t[:, None] >> l) == (t[None, :] >> l))
                     & ((t[:, None] >> (l - 1)) != (t[None, :] >> (l - 1)))).astype(np.float32)
                    for l in range(1, 7)])
    ex = np.zeros((LANES, GROUP_W), np.float32)
    for h in range(SSD_HEADS):
        ex[SMALL_DT + h, h * SSD_HEADDIM:(h + 1) * SSD_HEADDIM] = 1.0
    exi = np.zeros((LANES, GROUP_W), np.float32)
    exf = np.zeros((LANES, GROUP_W), np.float32)
    for h in range(MLSTM_HEADS):
        exi[SMALL_I + h, h * MLSTM_HEAD:(h + 1) * MLSTM_HEAD] = 1.0
        exf[SMALL_F + h, h * MLSTM_HEAD:(h + 1) * MLSTM_HEAD] = 1.0
    triu = tri.T.copy()
    zero = np.zeros_like(triu)
    triu2 = np.stack([np.concatenate([triu, zero], axis=1), np.concatenate([zero, triu], axis=1)])
    return dict(tri=tri, triu=triu, triu2=triu2, strict=strict, incl=incl, bd=bd, lvl=lvl, ex=ex,
                exi=exi, exf=exf)


def _pad_rows(w, height):
    return jnp.pad(w, ((0, height - w.shape[0]), (0, 0)))


def _pad_vec(v, width, offset=0):
    v = v.reshape(1, -1)
    return jnp.pad(v, ((0, 0), (offset, width - offset - v.shape[1])))


BF16_ROW_TILE = 16


def _layer(x2, v_first, layer, p, B, S, consts, g_final):
    sw = 3 * GROUP_W + RWKV_W_RANK + RWKV_A_RANK + (RWKV_V_RANK if layer > 0 else 0)
    names = ['gla_q', 'gla_k', 'gla_v', 'gla_gk', 'gla_z', 'rwkv_shift', 'rwkv_z', 'ssd_xbc', 'ssd_dt',
             'ssd_z', 'mlstm_qk', 'mlstm_v', 'mlstm_i', 'mlstm_f', 'mlstm_o', 'mlstm_z']
    widths = [256, 256, 512, 16, 512, sw, 512, SSD_XBC, SSD_HEADS, 512, 1024, 512, 4, 4, 512, 512]
    col0 = dict(zip(names, np.concatenate([[0], np.cumsum(widths)[:-1]]).tolist()))
    cuts = [0, col0['ssd_z'], col0['mlstm_o'], sum(widths)]
    w_in = p['w_in']
    w_full = _bf(w_in.T)
    w_t = (w_full, w_full[cuts[1]:cuts[2]], w_full[cuts[2]:cuts[3]])

    def off(name):
        part = max(i for i in range(3) if cuts[i] <= col0[name])
        return part, col0[name] - cuts[part]

    assert all(off(n)[1] % BF16_ROW_TILE == 0 for n in names if n not in ('ssd_dt', 'mlstm_i', 'mlstm_f'))

    def rows(name, n):
        part, r0 = off(name)
        return w_t[part][r0:r0 + n, :]

    g = p['norm_g'].reshape(1, D_MODEL)
    tri, triu = consts['tri'], consts['triu']
    rows_of = lambda t: t.reshape(t.shape[0], B, S // CHUNK, CHUNK).transpose(1, 2, 0, 3)

    w_small = _pad_rows(jnp.concatenate([rows('gla_gk', GLA_RANK), rows('ssd_dt', SSD_HEADS),
                                         rows('mlstm_i', 2 * MLSTM_HEADS)], axis=0), LANES)
    even_odd = np.concatenate([np.arange(0, SSD_HEADS, 2), np.arange(1, SSD_HEADS, 2)])
    w_rows = jnp.concatenate([rows('ssd_dt', SSD_HEADS)[even_odd], rows('mlstm_i', 2 * MLSTM_HEADS)], axis=0)
    mu = p['rwkv_mu']
    if layer == 0:
        pieces, zero_lanes = [(off('rwkv_shift'), sw + GROUP_W, 0)], ()
        mu_p = mu.reshape(1, -1)
    else:
        pieces = [(off('rwkv_shift'), sw, 0), (off('rwkv_z'), GROUP_W, 1792)]
        zero_lanes = ((1664, 1792),)
        mu_p = _pad_vec(mu, 1792)
    slab, small, gate_rows = _in_proj(x2, g, w_t, pieces, _rwkv_slab_width(layer), zero_lanes,
                                      w_small, w_rows)
    slab = slab.reshape(B, S, _rwkv_slab_width(layer))
    small = small.reshape(B, S, LANES)
    gate_rows = rows_of(gate_rows)
    w2a2 = jnp.zeros((LANES, 2 * GROUP_W), F32)
    w2a2 = w2a2.at[0:RWKV_W_RANK, 0:GROUP_W].set(p['rwkv_w2'])
    w2a2 = w2a2.at[RWKV_W_RANK:, GROUP_W:].set(p['rwkv_a2'])
    vec = lambda t: t.reshape(1, GROUP_W)
    v0 = v2 = None
    if layer > 0:
        v0 = vec(p['rwkv_v0'])
        v2 = _bf(jnp.pad(p['rwkv_v2'], ((0, LANES - RWKV_V_RANK), (0, 0))))
    d_rwkv = _rwkv(layer, slab, v_first, mu_p, vec(p['rwkv_w0']), vec(p['rwkv_a0']), _bf(w2a2),
                            v0, v2, vec(p['rwkv_k_k']), vec(p['rwkv_k_a']), vec(p['rwkv_r_k']),
                            vec(p['rwkv_ln_g']), vec(p['rwkv_ln_b']), _bf(tri), _bf(consts['bd']),
                            consts['strict'], consts['incl'], _bf(consts['lvl']))

    slab = _in_proj(x2, g, w_t, [(off('gla_q'), 1024, 0), (off('gla_z'), GROUP_W, 1024)], GLA_SLAB)
    w2p = _bf(jnp.pad(p['gla_gk_w2'], ((SMALL_GK, LANES - SMALL_GK - GLA_RANK), (0, 0))))
    d_gla = _gla(slab.reshape(B, S, GLA_SLAB), small, w2p, p['gla_gk_b'].reshape(1, -1),
                 p['gla_norm_g'].reshape(1, -1), _bf(tri))

    slab = _in_proj(x2, g, w_t, [(off('ssd_xbc'), SSD_XBC, 0), (off('ssd_z'), GROUP_W, SSD_XBC)], SSD_SLAB)
    a_neg = -jnp.exp(p['ssd_a_log'])
    d_ssd = _ssd(slab.reshape(B, S, SSD_SLAB), small, gate_rows, p['ssd_conv_w'],
                 p['ssd_conv_b'].reshape(1, -1),
                 _pad_vec(p['ssd_dt_bias'], LANES, SMALL_DT), _pad_vec(a_neg, LANES, SMALL_DT),
                 p['ssd_dt_bias'][even_odd].reshape(-1, 1), a_neg[even_odd].reshape(-1, 1),
                 jnp.repeat(p['ssd_d'], SSD_HEADDIM).reshape(1, -1), p['ssd_norm_g'].reshape(1, -1),
                 _bf(tri), _bf(consts['triu2']), _bf(consts['ex']))

    slab = _in_proj(x2, g, w_t, [(off('mlstm_qk'), 3 * GROUP_W, 0), (off('mlstm_o'), 2 * GROUP_W, 3 * GROUP_W)],
                    MLSTM_SLAB)
    gb_col = jnp.concatenate([p['mlstm_ig_b'], p['mlstm_fg_b']]).reshape(-1, 1)
    d_ml = _mlstm(slab.reshape(B, S, MLSTM_SLAB), small, gate_rows, p['mlstm_conv_w'],
                  p['mlstm_conv_b'].reshape(1, -1), _pad_vec(p['mlstm_ig_b'], LANES, SMALL_I),
                  _pad_vec(p['mlstm_fg_b'], LANES, SMALL_F), gb_col, p['mlstm_norm_g'].reshape(1, -1),
                  _bf(tri), _bf(triu), _bf(consts['exi']), _bf(consts['exf']))

    T = B * S
    d_ml['start'], d_ssd['start'], d_gla['start'] = MIXER_STARTS
    o_rwkv, o_ml, o_ssd, o_gla = _run_mixers([d_rwkv, d_ml, d_ssd, d_gla], B, S, "mixers")
    if layer == 0:
        v_first = o_rwkv[1]
    ys = [y.reshape(T, GROUP_W) for y in (o_gla[0], o_rwkv[0], o_ssd[0], o_ml[0])]
    return _out_proj(x2, ys, _bf(p['w_out']), g_final), v_first


_PARAM_NAMES_0 = ['norm_g', 'w_in', 'w_out', 'gla_gk_w2', 'gla_gk_b', 'gla_norm_g', 'rwkv_mu', 'rwkv_w0',
                  'rwkv_w2', 'rwkv_a0', 'rwkv_a2', 'rwkv_k_k', 'rwkv_k_a', 'rwkv_r_k', 'rwkv_ln_g',
                  'rwkv_ln_b', 'ssd_conv_w', 'ssd_conv_b', 'ssd_dt_bias', 'ssd_a_log', 'ssd_d',
                  'ssd_norm_g', 'mlstm_conv_w', 'mlstm_conv_b', 'mlstm_ig_b', 'mlstm_fg_b', 'mlstm_norm_g']
_PARAM_NAMES_1 = (_PARAM_NAMES_0[:11] + ['rwkv_v0', 'rwkv_v2'] + _PARAM_NAMES_0[11:])


def kernel(x,
           norm_g_0, w_in_0, w_out_0, gla_gk_w2_0, gla_gk_b_0, gla_norm_g_0,
           rwkv_mu_0, rwkv_w0_0, rwkv_w2_0, rwkv_a0_0, rwkv_a2_0,
           rwkv_k_k_0, rwkv_k_a_0, rwkv_r_k_0, rwkv_ln_g_0, rwkv_ln_b_0,
           ssd_conv_w_0, ssd_conv_b_0, ssd_dt_bias_0, ssd_a_log_0, ssd_d_0, ssd_norm_g_0,
           mlstm_conv_w_0, mlstm_conv_b_0, mlstm_ig_b_0, mlstm_fg_b_0, mlstm_norm_g_0,
           norm_g_1, w_in_1, w_out_1, gla_gk_w2_1, gla_gk_b_1, gla_norm_g_1,
           rwkv_mu_1, rwkv_w0_1, rwkv_w2_1, rwkv_a0_1, rwkv_a2_1, rwkv_v0_1, rwkv_v2_1,
           rwkv_k_k_1, rwkv_k_a_1, rwkv_r_k_1, rwkv_ln_g_1, rwkv_ln_b_1,
           ssd_conv_w_1, ssd_conv_b_1, ssd_dt_bias_1, ssd_a_log_1, ssd_d_1, ssd_norm_g_1,
           mlstm_conv_w_1, mlstm_conv_b_1, mlstm_ig_b_1, mlstm_fg_b_1, mlstm_norm_g_1,
           final_norm_g):
    params = (norm_g_0, w_in_0, w_out_0, gla_gk_w2_0, gla_gk_b_0, gla_norm_g_0,
              rwkv_mu_0, rwkv_w0_0, rwkv_w2_0, rwkv_a0_0, rwkv_a2_0,
              rwkv_k_k_0, rwkv_k_a_0, rwkv_r_k_0, rwkv_ln_g_0, rwkv_ln_b_0,
              ssd_conv_w_0, ssd_conv_b_0, ssd_dt_bias_0, ssd_a_log_0, ssd_d_0, ssd_norm_g_0,
              mlstm_conv_w_0, mlstm_conv_b_0, mlstm_ig_b_0, mlstm_fg_b_0, mlstm_norm_g_0,
              norm_g_1, w_in_1, w_out_1, gla_gk_w2_1, gla_gk_b_1, gla_norm_g_1,
              rwkv_mu_1, rwkv_w0_1, rwkv_w2_1, rwkv_a0_1, rwkv_a2_1, rwkv_v0_1, rwkv_v2_1,
              rwkv_k_k_1, rwkv_k_a_1, rwkv_r_k_1, rwkv_ln_g_1, rwkv_ln_b_1,
              ssd_conv_w_1, ssd_conv_b_1, ssd_dt_bias_1, ssd_a_log_1, ssd_d_1, ssd_norm_g_1,
              mlstm_conv_w_1, mlstm_conv_b_1, mlstm_ig_b_1, mlstm_fg_b_1, mlstm_norm_g_1,
              final_norm_g)
    n0 = len(_PARAM_NAMES_0)
    n1 = len(_PARAM_NAMES_1)
    p0 = dict(zip(_PARAM_NAMES_0, params[:n0]))
    p1 = dict(zip(_PARAM_NAMES_1, params[n0:n0 + n1]))
    final_norm_g = params[n0 + n1]
    B, S, _ = x.shape
    consts = {k: jnp.asarray(v) for k, v in _np_consts().items()}
    x2 = x.reshape(B * S, D_MODEL)
    x2, v_first = _layer(x2, None, 0, p0, B, S, consts, None)
    x2, _ = _layer(x2, v_first, 1, p1, B, S, consts, final_norm_g.reshape(1, D_MODEL))
    return x2.reshape(B, S, D_MODEL)
```

```python
import functools

import numpy as np
import jax
import jax.numpy as jnp
from jax import lax
from jax.experimental import pallas as pl
from jax.experimental.pallas import tpu as pltpu

F32 = jnp.float32
BF16 = jnp.bfloat16

D_MODEL = 2048
CHUNK = 64
GROUP_W = 512
NORM_EPS = 1e-6
LANES = 128
CARRY_ROWS = 8

GLA_HEADS, GLA_DK, GLA_DV, GLA_RANK = 4, 64, 128, 16
GLA_GATE_NORMALIZER = 16.0
RWKV_HEAD, RWKV_W_RANK, RWKV_A_RANK, RWKV_V_RANK = 64, 64, 64, 32
RWKV_LN_EPS = 64e-5
RWKV_DECAY_OFFSET = 0.5
SSD_HEADS, SSD_HEADDIM, SSD_STATE, SSD_CONV = 8, 64, 128, 4
SSD_XBC = 1024
MLSTM_HEADS, MLSTM_HEAD, MLSTM_CONV = 4, 128, 4

ROW_TILES_IN = (512, 1024)
SMALL_GK, SMALL_DT, SMALL_I, SMALL_F = 0, 16, 24, 28
ROWS_DT, ROWS_IF, ROWS_TOTAL = 0, 8, 16
ROW_TILE_OUT = 512
VMEM_LIMIT_PROJ = 48 * 2**20
VMEM_BUDGET_PROJ = 44 * 2**20


def _bf(x):
    return x.astype(BF16)


def _mm(a, b):
    return jnp.dot(_bf(a), _bf(b), preferred_element_type=F32)


def _mm_nt(a, b):
    return lax.dot_general(_bf(a), _bf(b), (((1,), (1,)), ((), ())), preferred_element_type=F32)


def _mm_tn(a, b):
    return lax.dot_general(_bf(a), _bf(b), (((0,), (0,)), ((), ())), preferred_element_type=F32)


def _split3(x):
    hi = _bf(x)
    r1 = x - hi.astype(F32)
    mid = _bf(r1)
    lo = _bf(r1 - mid.astype(F32))
    return hi, mid, lo


def _mm_sel_rhs(x, sel):
    hi, mid, lo = _split3(x)
    d = lambda a: jnp.dot(a, sel, preferred_element_type=F32)
    return d(hi) + d(mid) + d(lo)


def _mm_sel2(x, sel):
    hi = _bf(x)
    lo = _bf(x - hi.astype(F32))
    d = lambda a: jnp.dot(a, sel, preferred_element_type=F32)
    return d(hi) + d(lo)


def _mm_sel_lhs(sel, x):
    hi, mid, lo = _split3(x)
    d = lambda a: jnp.dot(sel, a, preferred_element_type=F32)
    return d(hi) + d(mid) + d(lo)


def _rowsum(x):
    ones = jnp.ones((x.shape[-1], LANES), BF16)
    hi = _bf(x)
    lo = _bf(x - hi.astype(F32))
    return (jnp.dot(hi, ones, preferred_element_type=F32)
            + jnp.dot(lo, ones, preferred_element_type=F32))


def _sigmoid(x):
    return 1.0 / (1.0 + jnp.exp(-x))


def _silu(x):
    return x * _sigmoid(x)


def _softplus(x):
    return jnp.maximum(x, 0.0) + jnp.log1p(jnp.exp(-jnp.abs(x)))


def _log_sigmoid(x):
    return -_softplus(-x)


def _lane_half_masks():
    lane = lax.broadcasted_iota(jnp.int32, (1, LANES), 1)
    lo = (lane < LANES // 2).astype(F32)
    return lo, 1.0 - lo


def _causal(n):
    r = lax.broadcasted_iota(jnp.int32, (n, n), 0)
    c = lax.broadcasted_iota(jnp.int32, (n, n), 1)
    return c <= r


def _shifted_rows(buf_ref, cur, offsets):
    buf_ref[CARRY_ROWS:CARRY_ROWS + CHUNK, :] = cur
    outs = [buf_ref[CARRY_ROWS - off:CARRY_ROWS - off + CHUNK, :] for off in offsets]
    tail = buf_ref[CHUNK:CHUNK + CARRY_ROWS, :]
    buf_ref[0:CARRY_ROWS, :] = tail
    return outs


def _in_proj_kernel(*refs, pieces, zero_lanes, has_small):
    x_ref, g_ref = refs[0], refs[1]
    w_refs = refs[2:2 + len(pieces)]
    rest = refs[2 + len(pieces):]
    if has_small:
        ws_ref, wr_ref, o_ref, os_ref, or_ref = rest
    else:
        (o_ref,) = rest
    nt = lambda a, b: lax.dot_general(a, b, (((1,), (1,)), ((), ())), preferred_element_type=F32)
    tm = x_ref.shape[0]
    for lo, hi in zero_lanes:
        o_ref[:, lo:hi] = jnp.zeros((tm, hi - lo), F32)
    for half in range(2):
        rs = slice(half * (tm // 2), (half + 1) * (tm // 2))
        x = x_ref[rs, :]
        h = x * lax.rsqrt(jnp.mean(x * x, axis=-1, keepdims=True) + NORM_EPS) * g_ref[...]
        hb = _bf(h)
        for w_ref, (_, n_rows, lane_off) in zip(w_refs, pieces):
            o_ref[rs, lane_off:lane_off + n_rows] = nt(hb, w_ref[...])
        if has_small:
            os_ref[rs, :] = nt(hb, ws_ref[...])
            or_ref[:, rs] = nt(wr_ref[...], hb)


def _in_proj(x2, g, w_t, pieces, width, zero_lanes=(), w_small=None, w_rows=None):
    T = x2.shape[0]
    tm = max(t for t in ROW_TILES_IN
             if 2 * (t * D_MODEL * 4 + t * (width + LANES) * 4 + (width + LANES) * D_MODEL * 2) <= VMEM_BUDGET_PROJ)
    const2 = lambda shape: pl.BlockSpec(shape, lambda i: (0, 0))
    in_specs = [pl.BlockSpec((tm, D_MODEL), lambda i: (i, 0)), const2((1, D_MODEL))]
    args = [x2, g]
    for (part, off), n_rows, _ in pieces:
        in_specs.append(pl.BlockSpec((pl.Element(n_rows), pl.Element(D_MODEL)), lambda i, off=off: (off, 0)))
        args.append(w_t[part])
    out_shape = [jax.ShapeDtypeStruct((T, width), F32)]
    out_specs = [pl.BlockSpec((tm, width), lambda i: (i, 0))]
    if w_small is not None:
        r = w_rows.shape[0]
        in_specs += [const2((LANES, D_MODEL)), const2((r, D_MODEL))]
        args += [w_small, w_rows]
        out_shape += [jax.ShapeDtypeStruct((T, LANES), F32), jax.ShapeDtypeStruct((r, T), F32)]
        out_specs += [pl.BlockSpec((tm, LANES), lambda i: (i, 0)), pl.BlockSpec((r, tm), lambda i: (0, i))]
    res = pl.pallas_call(
        functools.partial(_in_proj_kernel, pieces=tuple(pieces), zero_lanes=tuple(zero_lanes),
                          has_small=w_small is not None),
        grid=(T // tm,), in_specs=in_specs, out_specs=out_specs, out_shape=out_shape,
        compiler_params=pltpu.CompilerParams(dimension_semantics=("arbitrary",),
                                             vmem_limit_bytes=VMEM_LIMIT_PROJ),
        name="in_proj")(*args)
    return res if w_small is not None else res[0]


def _out_proj_kernel(*refs, final):
    if final:
        x_ref, y0, y1, y2, y3, w_ref, g_ref, o_ref = refs
    else:
        x_ref, y0, y1, y2, y3, w_ref, o_ref = refs
    acc = x_ref[...]
    for gi, y in enumerate((y0, y1, y2, y3)):
        acc = acc + jnp.dot(_bf(y[...]), w_ref[gi * GROUP_W:(gi + 1) * GROUP_W, :],
                            preferred_element_type=F32)
    if final:
        acc = acc * lax.rsqrt(jnp.mean(acc * acc, axis=-1, keepdims=True) + NORM_EPS) * g_ref[...]
    o_ref[...] = acc


def _out_proj(x2, ys, w, g_final=None):
    T = x2.shape[0]
    tm = ROW_TILE_OUT
    final = g_final is not None
    in_specs = [pl.BlockSpec((tm, D_MODEL), lambda i: (i, 0))]
    in_specs += [pl.BlockSpec((tm, GROUP_W), lambda i: (i, 0)) for _ in range(4)]
    in_specs += [pl.BlockSpec((D_MODEL, D_MODEL), lambda i: (0, 0))]
    args = [x2, *ys, w]
    if final:
        in_specs.append(pl.BlockSpec((1, D_MODEL), lambda i: (0, 0)))
        args.append(g_final)
    return pl.pallas_call(
        functools.partial(_out_proj_kernel, final=final),
        grid=(T // tm,), in_specs=in_specs,
        out_specs=pl.BlockSpec((tm, D_MODEL), lambda i: (i, 0)),
        out_shape=jax.ShapeDtypeStruct((T, D_MODEL), F32),
        compiler_params=pltpu.CompilerParams(dimension_semantics=("arbitrary",),
                                             vmem_limit_bytes=VMEM_LIMIT_PROJ),
        name="out_proj")(*args)


GLA_SLAB = 1536


BATCH_BLOCK = 4
MIXER_STARTS = (5, 5, 12)


def _lockstep(gens, starts=None):
    gens = list(gens)
    starts = [0] * len(gens) if starts is None else list(starts)
    pending = list(zip(starts, gens))
    tick = 0
    while pending:
        alive = []
        for start, g in pending:
            if start > tick:
                alive.append((start, g))
                continue
            try:
                next(g)
                alive.append((start, g))
            except StopIteration:
                pass
        pending = alive
        tick += 1


def _stepper(gens):
    gens = list(gens)
    while gens:
        alive = []
        for g in gens:
            try:
                next(g)
                alive.append(g)
            except StopIteration:
                pass
        gens = alive
        yield


def _mixers_kernel(*refs, parts):
    n_in = sum(p[1] for p in parts)
    n_out = sum(p[2] for p in parts)
    ins, outs, scs = refs[:n_in], refs[n_in:n_in + n_out], refs[n_in + n_out:]
    gens, starts, posts = [], [], []
    i = o = s = 0
    for stages_fn, ni, no, ns, start in parts:
        g, post = stages_fn(*ins[i:i + ni], *outs[o:o + no], *scs[s:s + ns])
        i, o, s = i + ni, o + no, s + ns
        gens += g
        starts += [start] * len(g)
        posts.append(post)
    _lockstep(gens, starts)
    for post in posts:
        if post is not None:
            post()


def _run_mixers(descs, B, S, name):
    nb = BATCH_BLOCK
    parts = tuple((d['stages'], len(d['args']), len(d['out_shape']), len(d['scratch']), d.get('start', 0))
                  for d in descs)
    res = pl.pallas_call(
        functools.partial(_mixers_kernel, parts=parts), grid=(B // nb, S // CHUNK),
        in_specs=[sp for d in descs for sp in d['in_specs']],
        out_specs=[sp for d in descs for sp in d['out_specs']],
        out_shape=[sh for d in descs for sh in d['out_shape']],
        scratch_shapes=[sc for d in descs for sc in d['scratch']],
        compiler_params=pltpu.CompilerParams(dimension_semantics=("arbitrary", "arbitrary"),
                                             vmem_limit_bytes=VMEM_LIMIT_PROJ),
        name=name)(*[a for d in descs for a in d['args']])
    out, k = [], 0
    for d in descs:
        out.append(res[k:k + len(d['out_shape'])])
        k += len(d['out_shape'])
    return out


def _gla_stages(slab_ref, small_ref, w2_ref, gkb_ref, ng_ref, tri_ref, o_ref, st_ref, *, nb):
    @pl.when(pl.program_id(1) == 0)
    def _():
        st_ref[...] = jnp.zeros_like(st_ref)

    causal = _causal(CHUNK)
    masks = _lane_half_masks()

    def head(b, h, qg, kg, kd, dec):
        p, j = divmod(h, 2)
        ls = slice(p * LANES, (p + 1) * LANES)
        qm = qg[:, ls] * masks[j]
        att = jnp.where(causal, _mm_nt(qm, kg[:, ls]), 0.0)
        yield
        v_h = slab_ref[b, :, 512 + h * GLA_DV:512 + (h + 1) * GLA_DV]
        st = st_ref[b, h]
        o = _mm(att, v_h) + _mm_nt(qm, st)
        st_ref[b, h] = st * dec[:, ls] + _mm_tn(v_h, kd[:, ls] * masks[j])
        yield
        ms = jnp.mean(o * o, axis=-1, keepdims=True)
        yield
        o = o * lax.rsqrt(ms + NORM_EPS)
        o = o * ng_ref[:, h * GLA_DV:(h + 1) * GLA_DV]
        z_h = slab_ref[b, :, 1024 + h * GLA_DV:1024 + (h + 1) * GLA_DV]
        o_ref[b, :, h * GLA_DV:(h + 1) * GLA_DV] = o * _silu(z_h)

    def sequence(b):
        q = slab_ref[b, :, 0:256] * (GLA_DK ** -0.5)
        k = slab_ref[b, :, 256:512]
        gk = _mm(small_ref[b], w2_ref[...]) + gkb_ref[...]
        log_a = _log_sigmoid(gk) / GLA_GATE_NORMALIZER
        cum = _mm_sel_lhs(tri_ref[...], log_a)
        last = cum[CHUNK - 1:CHUNK, :]
        qg = q * jnp.exp(cum)
        kg = k * jnp.exp(-cum)
        kd = k * jnp.exp(last - cum)
        dec = jnp.exp(last)
        yield
        yield from _stepper([head(b, h, qg, kg, kd, dec) for h in range(GLA_HEADS)])

    return [sequence(b) for b in range(nb)], None


def _gla(slab, small, w2p, gkb, ng, tri):
    B, S, _ = slab.shape
    nb = BATCH_BLOCK
    const = lambda shape: pl.BlockSpec(shape, lambda b, c: (0,) * len(shape))
    return dict(
        stages=functools.partial(_gla_stages, nb=nb),
        in_specs=[pl.BlockSpec((nb, CHUNK, GLA_SLAB), lambda b, c: (b, c, 0)),
                  pl.BlockSpec((nb, CHUNK, LANES), lambda b, c: (b, c, 0)),
                  const((LANES, 256)), const((1, 256)), const((1, GROUP_W)), const((CHUNK, CHUNK))],
        args=[slab, small, w2p, gkb, ng, tri],
        out_specs=[pl.BlockSpec((nb, CHUNK, GROUP_W), lambda b, c: (b, c, 0))],
        out_shape=[jax.ShapeDtypeStruct((B, S, GROUP_W), F32)],
        scratch=[pltpu.VMEM((nb, GLA_HEADS, GLA_DV, LANES), F32)])


SSD_SLAB = 1536


def _ssd_stages(slab_ref, small_ref, rows_ref, cw_ref, cb_ref, dtb_ref, a_ref, dtb_col_ref, a_col_ref,
                dskip_ref, ng_ref, tri_ref, triu2_ref, ex_ref, o_ref, buf_ref, st_ref, *, nb):
    @pl.when(pl.program_id(1) == 0)
    def _():
        st_ref[...] = jnp.zeros_like(st_ref)
        for b in range(nb):
            buf_ref[b, 0:CARRY_ROWS, :] = jnp.zeros((CARRY_ROWS, SSD_XBC), F32)

    row_i = lax.broadcasted_iota(jnp.int32, (CHUNK, LANES), 0)
    col_i = lax.broadcasted_iota(jnp.int32, (CHUNK, LANES), 1)
    causal2 = jnp.bitwise_and(col_i, CHUNK - 1) <= row_i
    masks = _lane_half_masks()
    masks_b = (_bf(masks[0]), _bf(masks[1]))

    def pair(b, p, xbc, cum_b, cum_row2, xdt, xw, ecum, dec):
        g = p // 2
        ls = slice(p * LANES, (p + 1) * LANES)
        bm = xbc[:, 512 + g * SSD_STATE:512 + (g + 1) * SSD_STATE]
        cm = xbc[:, 768 + g * SSD_STATE:768 + (g + 1) * SSD_STATE]
        bmb = _bf(bm)
        cbm2 = _mm_nt(cm, jnp.concatenate([bmb, bmb], axis=0))
        st = st_ref[b, p]
        y = _mm(cm, st) * ecum[:, ls]
        st_ref[b, p] = st * dec[:, ls] + _mm_tn(bmb, xw[:, ls])
        lmat = jnp.exp(jnp.where(causal2, cum_b[:, ls] - cum_row2[p:p + 1, :], -jnp.inf))
        yield
        xb = _bf(xdt[:, ls])
        xs = jnp.concatenate([xb * masks_b[0], xb * masks_b[1]], axis=0)
        y = y + _mm(cbm2 * lmat, xs)
        yield
        y = y + dskip_ref[:, ls] * xbc[:, ls]
        o_ref[b, :, ls] = y * _silu(slab_ref[b, :, 1024 + p * LANES:1024 + (p + 1) * LANES])

    def sequence(b):
        taps = _shifted_rows(buf_ref.at[b], slab_ref[b, :, 0:SSD_XBC], (3, 2, 1, 0))
        xbc = cb_ref[...]
        for j in range(SSD_CONV):
            xbc = xbc + taps[j] * cw_ref[j:j + 1, :]
        xbc = _silu(xbc)
        yield
        dt_col = _softplus(small_ref[b] + dtb_ref[...])
        cum_col = _mm_sel_lhs(tri_ref[...], dt_col * a_ref[...])
        dt_row = _softplus(rows_ref[b, 0, ROWS_DT:ROWS_DT + SSD_HEADS, :] + dtb_col_ref[...])
        da_row = dt_row * a_col_ref[...]
        cum_row2 = (_mm_sel_rhs(da_row[0:SSD_HEADS // 2], triu2_ref[0])
                    + _mm_sel_rhs(da_row[SSD_HEADS // 2:SSD_HEADS], triu2_ref[1]))
        dt_b = _mm_sel_rhs(dt_col, ex_ref[...])
        cum_b = _mm_sel_rhs(cum_col, ex_ref[...])
        last_b = cum_b[CHUNK - 1:CHUNK, :]
        xdt = xbc[:, 0:512] * dt_b
        xw = xdt * jnp.exp(last_b - cum_b)
        ecum = jnp.exp(cum_b)
        dec = jnp.exp(last_b)
        yield
        yield from _stepper([pair(b, p, xbc, cum_b, cum_row2, xdt, xw, ecum, dec)
                             for p in range(SSD_HEADS // 2)])

    gens = [sequence(b) for b in range(nb)]

    def group_norm():
        for b in range(nb):
            y = o_ref[b]
            inv = lax.rsqrt(_rowsum(y * y) * (1.0 / GROUP_W) + NORM_EPS)
            for p in range(GROUP_W // LANES):
                ls = slice(p * LANES, (p + 1) * LANES)
                o_ref[b, :, ls] = y[:, ls] * inv * ng_ref[:, ls]

    return gens, group_norm


def _ssd(slab, small, rows, cw, cb, dtb, a, dtb_col, a_col, dskip, ng, tri, triu, ex):
    B, S, _ = slab.shape
    nb = BATCH_BLOCK
    const = lambda shape: pl.BlockSpec(shape, lambda b, c: (0,) * len(shape))
    return dict(
        stages=functools.partial(_ssd_stages, nb=nb),
        in_specs=[pl.BlockSpec((nb, CHUNK, SSD_SLAB), lambda b, c: (b, c, 0)),
                  pl.BlockSpec((nb, CHUNK, LANES), lambda b, c: (b, c, 0)),
                  pl.BlockSpec((nb, 1, ROWS_TOTAL, CHUNK), lambda b, c: (b, c, 0, 0)),
                  const((SSD_CONV, SSD_XBC)), const((1, SSD_XBC)), const((1, LANES)), const((1, LANES)),
                  const((SSD_HEADS, 1)), const((SSD_HEADS, 1)), const((1, GROUP_W)), const((1, GROUP_W)),
                  const((CHUNK, CHUNK)), const((2, CHUNK, LANES)), const((LANES, GROUP_W))],
        args=[slab, small, rows, cw, cb, dtb, a, dtb_col, a_col, dskip, ng, tri, triu, ex],
        out_specs=[pl.BlockSpec((nb, CHUNK, GROUP_W), lambda b, c: (b, c, 0))],
        out_shape=[jax.ShapeDtypeStruct((B, S, GROUP_W), F32)],
        scratch=[pltpu.VMEM((nb, CARRY_ROWS + CHUNK, SSD_XBC), F32),
                 pltpu.VMEM((nb, SSD_HEADS // 2, SSD_STATE, LANES), F32)])


MLSTM_SLAB = 2560


def _mlstm_stages(slab_ref, small_ref, rows_ref, cw_ref, cb_ref, igb_ref, fgb_ref, gb_col_ref, ng_ref,
                  tri_ref, triu_ref, exi_ref, exf_ref, o_ref, buf_ref, c_ref, nm_ref, *, nb):
    @pl.when(pl.program_id(1) == 0)
    def _():
        c_ref[...] = jnp.zeros_like(c_ref)
        nm_ref[...] = jnp.zeros_like(nm_ref)
        for b in range(nb):
            buf_ref[b, 0:CARRY_ROWS, :] = jnp.zeros((CARRY_ROWS, 2 * GROUP_W), F32)

    causal = _causal(CHUNK)

    def head(b, h, qk, li_b, ci_b, logi_row, cum_row):
        ls = slice(h * MLSTM_HEAD, (h + 1) * MLSTM_HEAD)
        q = qk[:, ls]
        k = qk[:, GROUP_W + h * MLSTM_HEAD:GROUP_W + (h + 1) * MLSTM_HEAD] * (MLSTM_HEAD ** -0.5)
        v = slab_ref[b, :, 1024 + h * MLSTM_HEAD:1024 + (h + 1) * MLSTM_HEAD]
        ci = ci_b[:, ls]
        li = li_b[:, ls]
        cr = cum_row[h:h + 1, :]
        lir = logi_row[h:h + 1, :]
        last = ci[CHUNK - 1:CHUNK, :]
        c_prev = c_ref[b, h]
        n_prev = nm_ref[b, h, 0:1, :]
        m_prev = nm_ref[b, h, 1:2, :]

        g = last - ci + li
        g_max = jnp.max(g, axis=0, keepdims=True)
        log_d = jnp.where(causal, ci[:, 0:CHUNK] - cr + lir, -jnp.inf)
        row_max = jnp.max(log_d, axis=-1, keepdims=True)
        qk_h = _mm_nt(q, k)
        qc = _mm(q, c_prev)
        qn = _rowsum(q * n_prev)
        yield
        kw = k * jnp.exp(g - g_max)
        c_loc = _mm_tn(kw, v)
        n_loc = jnp.sum(kw, axis=0, keepdims=True)
        m_new = jnp.maximum(last + m_prev, g_max)
        a_old = jnp.exp(last + m_prev - m_new)
        a_new = jnp.exp(g_max - m_new)
        c_ref[b, h] = a_old * c_prev + a_new * c_loc
        nm_ref[b, h, 0:1, :] = a_old * n_prev + a_new * n_loc
        nm_ref[b, h, 1:2, :] = m_new
        m_inter = ci + m_prev
        m_l = jnp.maximum(m_inter, row_max)
        wqk = qk_h * jnp.exp(log_d - m_l[:, 0:CHUNK])
        w_inter = jnp.exp(m_inter - m_l)
        num = _mm(wqk, v) + w_inter * qc
        den = _rowsum(wqk) + w_inter * qn
        yield
        den = jnp.maximum(jnp.abs(den), jnp.exp(-m_l))
        hh = num / den * _sigmoid(slab_ref[b, :, 1536 + h * MLSTM_HEAD:1536 + (h + 1) * MLSTM_HEAD])
        mu = _rowsum(hh) * (1.0 / MLSTM_HEAD)
        yield
        yc = hh - mu
        var = _rowsum(yc * yc) * (1.0 / MLSTM_HEAD)
        yield
        hh = yc * lax.rsqrt(var + NORM_EPS) * ng_ref[:, ls]
        o_ref[b, :, ls] = hh * _silu(slab_ref[b, :, 2048 + h * MLSTM_HEAD:2048 + (h + 1) * MLSTM_HEAD])

    def sequence(b):
        taps = _shifted_rows(buf_ref.at[b], slab_ref[b, :, 0:2 * GROUP_W], (3, 2, 1, 0))
        qk = cb_ref[...]
        for j in range(MLSTM_CONV):
            qk = qk + taps[j] * cw_ref[j:j + 1, :]
        qk = _silu(qk)
        yield
        logi_col = small_ref[b] + igb_ref[...]
        logf_col = _log_sigmoid(small_ref[b] + fgb_ref[...])
        cum_col = _mm_sel_lhs(tri_ref[...], logf_col)
        li_b = _mm_sel_rhs(logi_col, exi_ref[...])
        ci_b = _mm_sel_rhs(cum_col, exf_ref[...])
        pre_row = rows_ref[b, 0, ROWS_IF:ROWS_IF + 2 * MLSTM_HEADS, :] + gb_col_ref[...]
        logi_row = pre_row[0:MLSTM_HEADS, :]
        logf_row = _log_sigmoid(pre_row[MLSTM_HEADS:2 * MLSTM_HEADS, :])
        cum_row = _mm_sel_rhs(logf_row, triu_ref[...])
        yield
        yield from _stepper([head(b, h, qk, li_b, ci_b, logi_row, cum_row) for h in range(MLSTM_HEADS)])

    return [sequence(b) for b in range(nb)], None


def _mlstm(slab, small, rows, cw, cb, igb, fgb, gb_col, ng, tri, triu, exi, exf):
    B, S, _ = slab.shape
    nb = BATCH_BLOCK
    const = lambda shape: pl.BlockSpec(shape, lambda b, c: (0,) * len(shape))
    return dict(
        stages=functools.partial(_mlstm_stages, nb=nb),
        in_specs=[pl.BlockSpec((nb, CHUNK, MLSTM_SLAB), lambda b, c: (b, c, 0)),
                  pl.BlockSpec((nb, CHUNK, LANES), lambda b, c: (b, c, 0)),
                  pl.BlockSpec((nb, 1, ROWS_TOTAL, CHUNK), lambda b, c: (b, c, 0, 0)),
                  const((MLSTM_CONV, 2 * GROUP_W)), const((1, 2 * GROUP_W)), const((1, LANES)),
                  const((1, LANES)), const((2 * MLSTM_HEADS, 1)), const((1, GROUP_W)),
                  const((CHUNK, CHUNK)), const((CHUNK, CHUNK)),
                  const((LANES, GROUP_W)), const((LANES, GROUP_W))],
        args=[slab, small, rows, cw, cb, igb, fgb, gb_col, ng, tri, triu, exi, exf],
        out_specs=[pl.BlockSpec((nb, CHUNK, GROUP_W), lambda b, c: (b, c, 0))],
        out_shape=[jax.ShapeDtypeStruct((B, S, GROUP_W), F32)],
        scratch=[pltpu.VMEM((nb, CARRY_ROWS + CHUNK, 2 * GROUP_W), F32),
                 pltpu.VMEM((nb, MLSTM_HEADS, MLSTM_HEAD, MLSTM_HEAD), F32),
                 pltpu.VMEM((nb, MLSTM_HEADS, CARRY_ROWS, MLSTM_HEAD), F32)])


def _rwkv_slab_width(layer):
    return 2176 if layer == 0 else 2304


def _rwkv_shift_cols(layer):
    return 1664 if layer == 0 else 1792


def _rwkv_stages(*refs, layer, nb):
    if layer == 0:
        (slab_ref, mu_ref, w0_ref, a0_ref, w2a2_ref, kk_ref, ka_ref, rk_ref, lng_ref, lnb_ref,
         tri_ref, bd_ref, strict_ref, incl_ref, lvl_ref, o_ref, vf_out_ref, buf_ref, st_ref) = refs
    else:
        (slab_ref, vf_ref, mu_ref, w0_ref, a0_ref, w2a2_ref, v0_ref, v2_ref, kk_ref, ka_ref, rk_ref,
         lng_ref, lnb_ref, tri_ref, bd_ref, strict_ref, incl_ref, lvl_ref, o_ref, buf_ref, st_ref) = refs
    ws = _rwkv_shift_cols(layer)
    zoff = _rwkv_slab_width(layer) - GROUP_W
    npair = GROUP_W // LANES

    @pl.when(pl.program_id(1) == 0)
    def _():
        st_ref[...] = jnp.zeros_like(st_ref)
        for b in range(nb):
            buf_ref[b, 0:CARRY_ROWS, :] = jnp.zeros((CARRY_ROWS, ws), F32)

    lane = lax.broadcasted_iota(jnp.int32, (1, LANES), 1)
    masks = _lane_half_masks()
    bd = bd_ref[...]
    strict = strict_ref[...] > 0.5
    incl = incl_ref[...] > 0.5
    masks_b = (_bf(masks[0]), _bf(masks[1]))

    def rows2(t):
        tb = _bf(t)
        return jnp.concatenate([tb * masks_b[0], tb * masks_b[1]], axis=0)

    eye = (lax.broadcasted_iota(jnp.int32, (LANES, LANES), 0)
           == lax.broadcasted_iota(jnp.int32, (LANES, LANES), 1)).astype(F32)

    inv = 1.0 / RWKV_HEAD

    def sequence(b):
        inst = []
        f = slab_ref[b, :, 0:ws]
        (prev,) = _shifted_rows(buf_ref.at[b], f, (1,))
        f = f + mu_ref[...] * (prev - f)
        r = f[:, 0:512]
        k = f[:, 512:1024]
        v = f[:, 1024:1536]
        lora = f[:, 1536:1664]
        lora = jnp.where(lane < RWKV_W_RANK, jnp.tanh(lora), lora)
        wa = _mm(lora, w2a2_ref[...])
        if layer == 0:
            vf_out_ref[b] = v
        else:
            mix = _sigmoid(v0_ref[...] + _mm(f[:, 1664:1792], v2_ref[...]))
            v = v + (vf_ref[b] - v) * mix
        yield
        w_log = -_softplus(-(w0_ref[...] + wa[:, 0:512])) - RWKV_DECAY_OFFSET
        lw = -jnp.exp(w_log)
        a = _sigmoid(a0_ref[...] + wa[:, 512:1024])
        kk = k * kk_ref[...]
        k = k * (1.0 + (a - 1.0) * ka_ref[...])
        cum = _mm_sel_lhs(tri_ref[...], lw)
        ss = [_mm_sel2(kk[:, p * LANES:(p + 1) * LANES] ** 2, bd) for p in range(npair)]
        yield
        last = cum[CHUNK - 1:CHUNK, :]
        e_pos = jnp.exp(cum)
        e_neg = jnp.exp(-cum)
        e_end = jnp.exp(last - cum)
        e_prev = jnp.exp(cum - lw)
        gam = jnp.exp(last)
        for p in range(npair):
            ls = slice(p * LANES, (p + 1) * LANES)
            kk_p = kk[:, ls] / jnp.maximum(jnp.sqrt(ss[p]), 1e-12)
            k_p, r_p, v_p = k[:, ls], r[:, ls], v[:, ls]
            b_p = kk_p * a[:, ls]
            inst.append(dict(
                p=p, ls=ls, r=r_p, k=k_p, v=v_p, gam=gam[:, ls],
                la=rows2(-kk_p * e_prev[:, ls]), lr=rows2(r_p * e_pos[:, ls]),
                rb=rows2(b_p * e_neg[:, ls]), rk=rows2(k_p * e_neg[:, ls]),
                bh=rows2(b_p * e_end[:, ls]), kh=rows2(k_p * e_end[:, ls]), vs=rows2(v_p)))
        yield
        for d in inst:
            aa = _mm_nt(jnp.concatenate([d['la'], d['lr']], axis=0),
                        jnp.concatenate([d['rb'], d['rk']], axis=0))
            d['nab'] = _bf(jnp.where(strict, aa[0:LANES, 0:LANES], 0.0))
            d['aak'] = _bf(jnp.where(strict, aa[0:LANES, LANES:2 * LANES], 0.0))
            d['arb'] = _bf(jnp.where(incl, aa[LANES:2 * LANES, 0:LANES], 0.0))
            d['ark'] = _bf(jnp.where(incl, aa[LANES:2 * LANES, LANES:2 * LANES], 0.0))
            d['t'] = eye + (d['nab'] * lvl_ref[0]).astype(F32)
        yield
        for d in inst:
            d['x'] = _mm(d['aak'], d['vs'])
        for lv in range(1, 6):
            for d in inst:
                d['tb'] = _bf(d['t'])
                d['nt'] = _mm(d['nab'] * lvl_ref[lv], d['tb'])
            yield
            for d in inst:
                d['t'] = d['t'] + _mm(d['tb'], d['nt'])
            yield
        for d in inst:
            d['wu'] = _mm(d['t'], jnp.concatenate([d['la'], _bf(d['x'])], axis=1))
        yield
        for d in inst:
            d['wub'] = _bf(d['wu'])
            qy = _mm(d['arb'], d['wub'])
            d['qt'] = d['lr'].astype(F32) + qy[:, 0:LANES]
            d['y0'] = _mm(d['ark'], d['vs']) + qy[:, LANES:2 * LANES]
        yield
        for d in inst:
            st = st_ref[b, d['p']]
            uy = _mm_nt(jnp.concatenate([d['wub'][:, 0:LANES], _bf(d['qt'])], axis=0), st)
            d['ust'] = uy[0:LANES] + d['wu'][:, LANES:2 * LANES]
            d['st'] = st
            yst = uy[LANES:2 * LANES] + d['y0']
            d['y'] = yst[0:CHUNK] + yst[CHUNK:2 * CHUNK]
        yield
        for d in inst:
            st_ref[b, d['p']] = d['st'] * d['gam'] + _mm_tn(jnp.concatenate([_bf(d['ust']), d['vs']], axis=0),
                                                            jnp.concatenate([d['bh'], d['kh']], axis=0))
            d['mu'] = _mm_sel2(d['y'], bd) * inv
            d['bonus'] = _mm_sel2(d['r'] * d['k'] * rk_ref[:, d['ls']], bd) * d['v']
        yield
        for d in inst:
            d['yc'] = d['y'] - d['mu']
            d['var'] = _mm_sel2(d['yc'] * d['yc'], bd) * inv
        yield
        for d in inst:
            ls = d['ls']
            yn = d['yc'] * lax.rsqrt(d['var'] + RWKV_LN_EPS) * lng_ref[:, ls] + lnb_ref[:, ls]
            z_p = slab_ref[b, :, zoff + d['p'] * LANES:zoff + (d['p'] + 1) * LANES]
            o_ref[b, :, ls] = (yn + d['bonus']) * _silu(z_p)

    return [sequence(b) for b in range(nb)], None


def _rwkv(layer, slab, vf, mu, w0, a0, w2a2, v0, v2, kkw, ka, rk, lng, lnb, tri, bd, strict, incl, lvl):
    B, S, W = slab.shape
    nb = BATCH_BLOCK
    ws = _rwkv_shift_cols(layer)
    const = lambda shape: pl.BlockSpec(shape, lambda b, c: (0,) * len(shape))
    tok = lambda w: pl.BlockSpec((nb, CHUNK, w), lambda b, c: (b, c, 0))
    vecw = const((1, GROUP_W))
    in_specs = [tok(W)]
    args = [slab]
    if layer > 0:
        in_specs.append(tok(GROUP_W))
        args.append(vf)
    in_specs += [const((1, ws)), vecw, vecw, const((LANES, 2 * GROUP_W))]
    args += [mu, w0, a0, w2a2]
    if layer > 0:
        in_specs += [vecw, const((LANES, GROUP_W))]
        args += [v0, v2]
    in_specs += [vecw, vecw, vecw, vecw, vecw, const((CHUNK, CHUNK)), const((LANES, LANES)),
                 const((LANES, LANES)), const((LANES, LANES)), const((6, LANES, LANES))]
    args += [kkw, ka, rk, lng, lnb, tri, bd, strict, incl, lvl]
    out_shape = [jax.ShapeDtypeStruct((B, S, GROUP_W), F32)]
    out_specs = [tok(GROUP_W)]
    if layer == 0:
        out_shape.append(jax.ShapeDtypeStruct((B, S, GROUP_W), F32))
        out_specs.append(tok(GROUP_W))
    return dict(
        stages=functools.partial(_rwkv_stages, layer=layer, nb=nb),
        in_specs=in_specs, args=args, out_specs=out_specs, out_shape=out_shape,
        scratch=[pltpu.VMEM((nb, CARRY_ROWS + CHUNK, ws), F32),
                 pltpu.VMEM((nb, GROUP_W // LANES, LANES, LANES), F32)])


def _np_consts():
    i = np.arange(CHUNK)
    tri = (i[None, :] <= i[:, None]).astype(np.float32)
    t = np.arange(LANES)
    same = (t[:, None] // CHUNK) == (t[None, :] // CHUNK)
    strict = (same & (t[None, :] < t[:, None])).astype(np.float32)
    incl = (same & (t[None, :] <= t[:, None])).astype(np.float32)
    bd = same.astype(np.float32)
    lvl = np.stack([(((t[:, None] >> l) == (t[None, :] >> l))
                     & ((t[:, None] >> (l - 1)) != (t[None, :] >> (l - 1)))).astype(np.float32)
                    for l in range(1, 7)])
    ex = np.zeros((LANES, GROUP_W), np.float32)
    for h in range(SSD_HEADS):
        ex[SMALL_DT + h, h * SSD_HEADDIM:(h + 1) * SSD_HEADDIM] = 1.0
    exi = np.zeros((LANES, GROUP_W), np.float32)
    exf = np.zeros((LANES, GROUP_W), np.float32)
    for h in range(MLSTM_HEADS):
        exi[SMALL_I + h, h * MLSTM_HEAD:(h + 1) * MLSTM_HEAD] = 1.0
        exf[SMALL_F + h, h * MLSTM_HEAD:(h + 1) * MLSTM_HEAD] = 1.0
    triu = tri.T.copy()
    zero = np.zeros_like(triu)
    triu2 = np.stack([np.concatenate([triu, zero], axis=1), np.concatenate([zero, triu], axis=1)])
    return dict(tri=tri, triu=triu, triu2=triu2, strict=strict, incl=incl, bd=bd, lvl=lvl, ex=ex,
                exi=exi, exf=exf)


def _pad_rows(w, height):
    return jnp.pad(w, ((0, height - w.shape[0]), (0, 0)))


def _pad_vec(v, width, offset=0):
    v = v.reshape(1, -1)
    return jnp.pad(v, ((0, 0), (offset, width - offset - v.shape[1])))


BF16_ROW_TILE = 16


def _layer(x2, v_first, layer, p, B, S, consts, g_final):
    sw = 3 * GROUP_W + RWKV_W_RANK + RWKV_A_RANK + (RWKV_V_RANK if layer > 0 else 0)
    names = ['gla_q', 'gla_k', 'gla_v', 'gla_gk', 'gla_z', 'rwkv_shift', 'rwkv_z', 'ssd_xbc', 'ssd_dt',
             'ssd_z', 'mlstm_qk', 'mlstm_v', 'mlstm_i', 'mlstm_f', 'mlstm_o', 'mlstm_z']
    widths = [256, 256, 512, 16, 512, sw, 512, SSD_XBC, SSD_HEADS, 512, 1024, 512, 4, 4, 512, 512]
    col0 = dict(zip(names, np.concatenate([[0], np.cumsum(widths)[:-1]]).tolist()))
    cuts = [0, col0['ssd_z'], col0['mlstm_o'], sum(widths)]
    w_in = p['w_in']
    w_full = _bf(w_in.T)
    w_t = (w_full, w_full[cuts[1]:cuts[2]], w_full[cuts[2]:cuts[3]])

    def off(name):
        part = max(i for i in range(3) if cuts[i] <= col0[name])
        return part, col0[name] - cuts[part]

    assert all(off(n)[1] % BF16_ROW_TILE == 0 for n in names if n not in ('ssd_dt', 'mlstm_i', 'mlstm_f'))

    def rows(name, n):
        part, r0 = off(name)
        return w_t[part][r0:r0 + n, :]

    g = p['norm_g'].reshape(1, D_MODEL)
    tri, triu = consts['tri'], consts['triu']
    rows_of = lambda t: t.reshape(t.shape[0], B, S // CHUNK, CHUNK).transpose(1, 2, 0, 3)

    w_small = _pad_rows(jnp.concatenate([rows('gla_gk', GLA_RANK), rows('ssd_dt', SSD_HEADS),
                                         rows('mlstm_i', 2 * MLSTM_HEADS)], axis=0), LANES)
    even_odd = np.concatenate([np.arange(0, SSD_HEADS, 2), np.arange(1, SSD_HEADS, 2)])
    w_rows = jnp.concatenate([rows('ssd_dt', SSD_HEADS)[even_odd], rows('mlstm_i', 2 * MLSTM_HEADS)], axis=0)
    mu = p['rwkv_mu']
    if layer == 0:
        pieces, zero_lanes = [(off('rwkv_shift'), sw + GROUP_W, 0)], ()
        mu_p = mu.reshape(1, -1)
    else:
        pieces = [(off('rwkv_shift'), sw, 0), (off('rwkv_z'), GROUP_W, 1792)]
        zero_lanes = ((1664, 1792),)
        mu_p = _pad_vec(mu, 1792)
    slab, small, gate_rows = _in_proj(x2, g, w_t, pieces, _rwkv_slab_width(layer), zero_lanes,
                                      w_small, w_rows)
    slab = slab.reshape(B, S, _rwkv_slab_width(layer))
    small = small.reshape(B, S, LANES)
    gate_rows = rows_of(gate_rows)
    w2a2 = jnp.zeros((LANES, 2 * GROUP_W), F32)
    w2a2 = w2a2.at[0:RWKV_W_RANK, 0:GROUP_W].set(p['rwkv_w2'])
    w2a2 = w2a2.at[RWKV_W_RANK:, GROUP_W:].set(p['rwkv_a2'])
    vec = lambda t: t.reshape(1, GROUP_W)
    v0 = v2 = None
    if layer > 0:
        v0 = vec(p['rwkv_v0'])
        v2 = _bf(jnp.pad(p['rwkv_v2'], ((0, LANES - RWKV_V_RANK), (0, 0))))
    d_rwkv = _rwkv(layer, slab, v_first, mu_p, vec(p['rwkv_w0']), vec(p['rwkv_a0']), _bf(w2a2),
                            v0, v2, vec(p['rwkv_k_k']), vec(p['rwkv_k_a']), vec(p['rwkv_r_k']),
                            vec(p['rwkv_ln_g']), vec(p['rwkv_ln_b']), _bf(tri), _bf(consts['bd']),
                            consts['strict'], consts['incl'], _bf(consts['lvl']))

    slab = _in_proj(x2, g, w_t, [(off('gla_q'), 1024, 0), (off('gla_z'), GROUP_W, 1024)], GLA_SLAB)
    w2p = _bf(jnp.pad(p['gla_gk_w2'], ((SMALL_GK, LANES - SMALL_GK - GLA_RANK), (0, 0))))
    d_gla = _gla(slab.reshape(B, S, GLA_SLAB), small, w2p, p['gla_gk_b'].reshape(1, -1),
                 p['gla_norm_g'].reshape(1, -1), _bf(tri))

    slab = _in_proj(x2, g, w_t, [(off('ssd_xbc'), SSD_XBC, 0), (off('ssd_z'), GROUP_W, SSD_XBC)], SSD_SLAB)
    a_neg = -jnp.exp(p['ssd_a_log'])
    d_ssd = _ssd(slab.reshape(B, S, SSD_SLAB), small, gate_rows, p['ssd_conv_w'],
                 p['ssd_conv_b'].reshape(1, -1),
                 _pad_vec(p['ssd_dt_bias'], LANES, SMALL_DT), _pad_vec(a_neg, LANES, SMALL_DT),
                 p['ssd_dt_bias'][even_odd].reshape(-1, 1), a_neg[even_odd].reshape(-1, 1),
                 jnp.repeat(p['ssd_d'], SSD_HEADDIM).reshape(1, -1), p['ssd_norm_g'].reshape(1, -1),
                 _bf(tri), _bf(consts['triu2']), _bf(consts['ex']))

    slab = _in_proj(x2, g, w_t, [(off('mlstm_qk'), 3 * GROUP_W, 0), (off('mlstm_o'), 2 * GROUP_W, 3 * GROUP_W)],
                    MLSTM_SLAB)
    gb_col = jnp.concatenate([p['mlstm_ig_b'], p['mlstm_fg_b']]).reshape(-1, 1)
    d_ml = _mlstm(slab.reshape(B, S, MLSTM_SLAB), small, gate_rows, p['mlstm_conv_w'],
                  p['mlstm_conv_b'].reshape(1, -1), _pad_vec(p['mlstm_ig_b'], LANES, SMALL_I),
                  _pad_vec(p['mlstm_fg_b'], LANES, SMALL_F), gb_col, p['mlstm_norm_g'].reshape(1, -1),
                  _bf(tri), _bf(triu), _bf(consts['exi']), _bf(consts['exf']))

    T = B * S
    d_ml['start'], d_ssd['start'], d_gla['start'] = MIXER_STARTS
    o_rwkv, o_ml, o_ssd, o_gla = _run_mixers([d_rwkv, d_ml, d_ssd, d_gla], B, S, "mixers")
    if layer == 0:
        v_first = o_rwkv[1]
    ys = [y.reshape(T, GROUP_W) for y in (o_gla[0], o_rwkv[0], o_ssd[0], o_ml[0])]
    return _out_proj(x2, ys, _bf(p['w_out']), g_final), v_first


_PARAM_NAMES_0 = ['norm_g', 'w_in', 'w_out', 'gla_gk_w2', 'gla_gk_b', 'gla_norm_g', 'rwkv_mu', 'rwkv_w0',
                  'rwkv_w2', 'rwkv_a0', 'rwkv_a2', 'rwkv_k_k', 'rwkv_k_a', 'rwkv_r_k', 'rwkv_ln_g',
                  'rwkv_ln_b', 'ssd_conv_w', 'ssd_conv_b', 'ssd_dt_bias', 'ssd_a_log', 'ssd_d',
                  'ssd_norm_g', 'mlstm_conv_w', 'mlstm_conv_b', 'mlstm_ig_b', 'mlstm_fg_b', 'mlstm_norm_g']
_PARAM_NAMES_1 = (_PARAM_NAMES_0[:11] + ['rwkv_v0', 'rwkv_v2'] + _PARAM_NAMES_0[11:])


def kernel(x,
           norm_g_0, w_in_0, w_out_0, gla_gk_w2_0, gla_gk_b_0, gla_norm_g_0,
           rwkv_mu_0, rwkv_w0_0, rwkv_w2_0, rwkv_a0_0, rwkv_a2_0,
           rwkv_k_k_0, rwkv_k_a_0, rwkv_r_k_0, rwkv_ln_g_0, rwkv_ln_b_0,
           ssd_conv_w_0, ssd_conv_b_0, ssd_dt_bias_0, ssd_a_log_0, ssd_d_0, ssd_norm_g_0,
           mlstm_conv_w_0, mlstm_conv_b_0, mlstm_ig_b_0, mlstm_fg_b_0, mlstm_norm_g_0,
           norm_g_1, w_in_1, w_out_1, gla_gk_w2_1, gla_gk_b_1, gla_norm_g_1,
           rwkv_mu_1, rwkv_w0_1, rwkv_w2_1, rwkv_a0_1, rwkv_a2_1, rwkv_v0_1, rwkv_v2_1,
           rwkv_k_k_1, rwkv_k_a_1, rwkv_r_k_1, rwkv_ln_g_1, rwkv_ln_b_1,
           ssd_conv_w_1, ssd_conv_b_1, ssd_dt_bias_1, ssd_a_log_1, ssd_d_1, ssd_norm_g_1,
           mlstm_conv_w_1, mlstm_conv_b_1, mlstm_ig_b_1, mlstm_fg_b_1, mlstm_norm_g_1,
           final_norm_g):
    params = (norm_g_0, w_in_0, w_out_0, gla_gk_w2_0, gla_gk_b_0, gla_norm_g_0,
              rwkv_mu_0, rwkv_w0_0, rwkv_w2_0, rwkv_a0_0, rwkv_a2_0,
              rwkv_k_k_0, rwkv_k_a_0, rwkv_r_k_0, rwkv_ln_g_0, rwkv_ln_b_0,
              ssd_conv_w_0, ssd_conv_b_0, ssd_dt_bias_0, ssd_a_log_0, ssd_d_0, ssd_norm_g_0,
              mlstm_conv_w_0, mlstm_conv_b_0, mlstm_ig_b_0, mlstm_fg_b_0, mlstm_norm_g_0,
              norm_g_1, w_in_1, w_out_1, gla_gk_w2_1, gla_gk_b_1, gla_norm_g_1,
              rwkv_mu_1, rwkv_w0_1, rwkv_w2_1, rwkv_a0_1, rwkv_a2_1, rwkv_v0_1, rwkv_v2_1,
              rwkv_k_k_1, rwkv_k_a_1, rwkv_r_k_1, rwkv_ln_g_1, rwkv_ln_b_1,
              ssd_conv_w_1, ssd_conv_b_1, ssd_dt_bias_1, ssd_a_log_1, ssd_d_1, ssd_norm_g_1,
              mlstm_conv_w_1, mlstm_conv_b_1, mlstm_ig_b_1, mlstm_fg_b_1, mlstm_norm_g_1,
              final_norm_g)
    n0 = len(_PARAM_NAMES_0)
    n1 = len(_PARAM_NAMES_1)
    p0 = dict(zip(_PARAM_NAMES_0, params[:n0]))
    p1 = dict(zip(_PARAM_NAMES_1, params[n0:n0 + n1]))
    final_norm_g = params[n0 + n1]
    B, S, _ = x.shape
    consts = {k: jnp.asarray(v) for k, v in _np_consts().items()}
    x2 = x.reshape(B * S, D_MODEL)
    x2, v_first = _layer(x2, None, 0, p0, B, S, consts, None)
    x2, _ = _layer(x2, v_first, 1, p1, B, S, consts, final_norm_g.reshape(1, D_MODEL))
    return x2.reshape(B, S, D_MODEL)
```

```python
import functools

import numpy as np
import jax
import jax.numpy as jnp
from jax import lax
from jax.experimental import pallas as pl
from jax.experimental.pallas import tpu as pltpu

F32 = jnp.float32
BF16 = jnp.bfloat16

D_MODEL = 2048
CHUNK = 64
GROUP_W = 512
NORM_EPS = 1e-6
LANES = 128
CARRY_ROWS = 8

GLA_HEADS, GLA_DK, GLA_DV, GLA_RANK = 4, 64, 128, 16
GLA_GATE_NORMALIZER = 16.0
RWKV_HEAD, RWKV_W_RANK, RWKV_A_RANK, RWKV_V_RANK = 64, 64, 64, 32
RWKV_LN_EPS = 64e-5
RWKV_DECAY_OFFSET = 0.5
SSD_HEADS, SSD_HEADDIM, SSD_STATE, SSD_CONV = 8, 64, 128, 4
SSD_XBC = 1024
MLSTM_HEADS, MLSTM_HEAD, MLSTM_CONV = 4, 128, 4

ROW_TILES_IN = (512, 1024)
SMALL_GK, SMALL_DT, SMALL_I, SMALL_F = 0, 16, 24, 28
ROWS_DT, ROWS_IF, ROWS_TOTAL = 0, 8, 16
ROW_TILE_OUT = 512
VMEM_LIMIT_PROJ = 48 * 2**20
VMEM_LIMIT_IN_PROJ = 56 * 2**20
VMEM_BUDGET_PROJ = 52 * 2**20
WEIGHT_CAST_ROWS = 256


def _bf(x):
    return x.astype(BF16)


def _mm(a, b):
    return jnp.dot(_bf(a), _bf(b), preferred_element_type=F32)


def _mm_nt(a, b):
    return lax.dot_general(_bf(a), _bf(b), (((1,), (1,)), ((), ())), preferred_element_type=F32)


def _mm_tn(a, b):
    return lax.dot_general(_bf(a), _bf(b), (((0,), (0,)), ((), ())), preferred_element_type=F32)


def _split3(x):
    hi = _bf(x)
    r1 = x - hi.astype(F32)
    mid = _bf(r1)
    lo = _bf(r1 - mid.astype(F32))
    return hi, mid, lo


def _mm_sel_rhs(x, sel):
    hi, mid, lo = _split3(x)
    d = lambda a: jnp.dot(a, sel, preferred_element_type=F32)
    return d(hi) + d(mid) + d(lo)


def _mm_sel2(x, sel):
    hi = _bf(x)
    lo = _bf(x - hi.astype(F32))
    d = lambda a: jnp.dot(a, sel, preferred_element_type=F32)
    return d(hi) + d(lo)


def _mm_sel_lhs(sel, x):
    hi, mid, lo = _split3(x)
    d = lambda a: jnp.dot(sel, a, preferred_element_type=F32)
    return d(hi) + d(mid) + d(lo)


def _rowsum(x):
    ones = jnp.ones((x.shape[-1], LANES), BF16)
    hi = _bf(x)
    lo = _bf(x - hi.astype(F32))
    return (jnp.dot(hi, ones, preferred_element_type=F32)
            + jnp.dot(lo, ones, preferred_element_type=F32))


def _sigmoid(x):
    return 1.0 / (1.0 + jnp.exp(-x))


def _silu(x):
    return x * _sigmoid(x)


def _softplus(x):
    return jnp.maximum(x, 0.0) + jnp.log1p(jnp.exp(-jnp.abs(x)))


def _log_sigmoid(x):
    return -_softplus(-x)


def _lane_half_masks():
    lane = lax.broadcasted_iota(jnp.int32, (1, LANES), 1)
    lo = (lane < LANES // 2).astype(F32)
    return lo, 1.0 - lo


def _causal(n):
    r = lax.broadcasted_iota(jnp.int32, (n, n), 0)
    c = lax.broadcasted_iota(jnp.int32, (n, n), 1)
    return c <= r


def _shifted_rows(buf_ref, cur, offsets):
    buf_ref[CARRY_ROWS:CARRY_ROWS + CHUNK, :] = cur
    outs = [buf_ref[CARRY_ROWS - off:CARRY_ROWS - off + CHUNK, :] for off in offsets]
    tail = buf_ref[CHUNK:CHUNK + CARRY_ROWS, :]
    buf_ref[0:CARRY_ROWS, :] = tail
    return outs


def _in_proj_kernel(*refs, pieces, zero_lanes, has_small):
    x_ref, g_ref = refs[0], refs[1]
    n = len(pieces)
    wf_refs = refs[2:2 + n]
    w_refs = refs[len(refs) - n:]
    rest = refs[2 + n:len(refs) - n]
    if has_small:
        ws_ref, wr_ref, o_ref, os_ref, or_ref = rest
    else:
        (o_ref,) = rest

    @pl.when(pl.program_id(0) == 0)
    def _():
        for wf_ref, w_ref in zip(wf_refs, w_refs):
            for r0 in range(0, w_ref.shape[0], WEIGHT_CAST_ROWS):
                r1 = min(r0 + WEIGHT_CAST_ROWS, w_ref.shape[0])
                w_ref[r0:r1, :] = _bf(wf_ref[r0:r1, :])

    nt = lambda a, b: lax.dot_general(a, b, (((1,), (1,)), ((), ())), preferred_element_type=F32)
    tm = x_ref.shape[0]
    for lo, hi in zero_lanes:
        o_ref[:, lo:hi] = jnp.zeros((tm, hi - lo), F32)
    for half in range(2):
        rs = slice(half * (tm // 2), (half + 1) * (tm // 2))
        x = x_ref[rs, :]
        h = x * lax.rsqrt(jnp.mean(x * x, axis=-1, keepdims=True) + NORM_EPS) * g_ref[...]
        hb = _bf(h)
        for w_ref, (_, n_rows, lane_off) in zip(w_refs, pieces):
            o_ref[rs, lane_off:lane_off + n_rows] = nt(hb, w_ref[...])
        if has_small:
            os_ref[rs, :] = nt(hb, ws_ref[...])
            or_ref[:, rs] = nt(wr_ref[...], hb)


def _in_proj(x2, g, w_t, pieces, width, zero_lanes=(), w_small=None, w_rows=None):
    T = x2.shape[0]
    w_rows_total = sum(n_rows for _, n_rows, _ in pieces)
    tm = max(t for t in ROW_TILES_IN
             if (w_rows_total * D_MODEL * (4 + 2)
                 + 2 * (t * D_MODEL * 4 + t * (width + LANES) * 4)) <= VMEM_BUDGET_PROJ)
    const2 = lambda shape: pl.BlockSpec(shape, lambda i: (0, 0))
    in_specs = [pl.BlockSpec((tm, D_MODEL), lambda i: (i, 0)), const2((1, D_MODEL))]
    args = [x2, g]
    for off, n_rows, _ in pieces:
        in_specs.append(pl.BlockSpec((pl.Element(n_rows), pl.Element(D_MODEL)), lambda i, off=off: (off, 0),
                                     pipeline_mode=pl.Buffered(1)))
        args.append(w_t)
    scratch = [pltpu.VMEM((n_rows, D_MODEL), BF16) for _, n_rows, _ in pieces]
    out_shape = [jax.ShapeDtypeStruct((T, width), F32)]
    out_specs = [pl.BlockSpec((tm, width), lambda i: (i, 0))]
    if w_small is not None:
        r = w_rows.shape[0]
        in_specs += [const2((LANES, D_MODEL)), const2((r, D_MODEL))]
        args += [w_small, w_rows]
        out_shape += [jax.ShapeDtypeStruct((T, LANES), F32), jax.ShapeDtypeStruct((r, T), F32)]
        out_specs += [pl.BlockSpec((tm, LANES), lambda i: (i, 0)), pl.BlockSpec((r, tm), lambda i: (0, i))]
    res = pl.pallas_call(
        functools.partial(_in_proj_kernel, pieces=tuple(pieces), zero_lanes=tuple(zero_lanes),
                          has_small=w_small is not None),
        grid=(T // tm,), in_specs=in_specs, out_specs=out_specs, out_shape=out_shape,
        scratch_shapes=scratch,
        compiler_params=pltpu.CompilerParams(dimension_semantics=("arbitrary",),
                                             vmem_limit_bytes=VMEM_LIMIT_IN_PROJ),
        name="in_proj")(*args)
    return res if w_small is not None else res[0]


def _out_proj_kernel(*refs, final):
    if final:
        x_ref, y0, y1, y2, y3, w_ref, g_ref, o_ref = refs
    else:
        x_ref, y0, y1, y2, y3, w_ref, o_ref = refs
    acc = x_ref[...]
    for gi, y in enumerate((y0, y1, y2, y3)):
        acc = acc + jnp.dot(_bf(y[...]), w_ref[gi * GROUP_W:(gi + 1) * GROUP_W, :],
                            preferred_element_type=F32)
    if final:
        acc = acc * lax.rsqrt(jnp.mean(acc * acc, axis=-1, keepdims=True) + NORM_EPS) * g_ref[...]
    o_ref[...] = acc


def _out_proj(x2, ys, w, g_final=None):
    T = x2.shape[0]
    tm = ROW_TILE_OUT
    final = g_final is not None
    in_specs = [pl.BlockSpec((tm, D_MODEL), lambda i: (i, 0))]
    in_specs += [pl.BlockSpec((tm, GROUP_W), lambda i: (i, 0)) for _ in range(4)]
    in_specs += [pl.BlockSpec((D_MODEL, D_MODEL), lambda i: (0, 0))]
    args = [x2, *ys, w]
    if final:
        in_specs.append(pl.BlockSpec((1, D_MODEL), lambda i: (0, 0)))
        args.append(g_final)
    return pl.pallas_call(
        functools.partial(_out_proj_kernel, final=final),
        grid=(T // tm,), in_specs=in_specs,
        out_specs=pl.BlockSpec((tm, D_MODEL), lambda i: (i, 0)),
        out_shape=jax.ShapeDtypeStruct((T, D_MODEL), F32),
        compiler_params=pltpu.CompilerParams(dimension_semantics=("arbitrary",),
                                             vmem_limit_bytes=VMEM_LIMIT_PROJ),
        name="out_proj")(*args)


GLA_SLAB = 1536


BATCH_BLOCK = 4
MIXER_STARTS = (5, 5, 12)


def _lockstep(gens, starts=None):
    gens = list(gens)
    starts = [0] * len(gens) if starts is None else list(starts)
    pending = list(zip(starts, gens))
    tick = 0
    while pending:
        alive = []
        for start, g in pending:
            if start > tick:
                alive.append((start, g))
                continue
            try:
                next(g)
                alive.append((start, g))
            except StopIteration:
                pass
        pending = alive
        tick += 1


def _stepper(gens):
    gens = list(gens)
    while gens:
        alive = []
        for g in gens:
            try:
                next(g)
                alive.append(g)
            except StopIteration:
                pass
        gens = alive
        yield


def _mixers_kernel(*refs, parts):
    n_in = sum(p[1] for p in parts)
    n_out = sum(p[2] for p in parts)
    ins, outs, scs = refs[:n_in], refs[n_in:n_in + n_out], refs[n_in + n_out:]
    gens, starts, posts = [], [], []
    i = o = s = 0
    for stages_fn, ni, no, ns, start in parts:
        g, post = stages_fn(*ins[i:i + ni], *outs[o:o + no], *scs[s:s + ns])
        i, o, s = i + ni, o + no, s + ns
        gens += g
        starts += [start] * len(g)
        posts.append(post)
    _lockstep(gens, starts)
    for post in posts:
        if post is not None:
            post()


def _run_mixers(descs, B, S, name):
    nb = BATCH_BLOCK
    parts = tuple((d['stages'], len(d['args']), len(d['out_shape']), len(d['scratch']), d.get('start', 0))
                  for d in descs)
    res = pl.pallas_call(
        functools.partial(_mixers_kernel, parts=parts), grid=(B // nb, S // CHUNK),
        in_specs=[sp for d in descs for sp in d['in_specs']],
        out_specs=[sp for d in descs for sp in d['out_specs']],
        out_shape=[sh for d in descs for sh in d['out_shape']],
        scratch_shapes=[sc for d in descs for sc in d['scratch']],
        compiler_params=pltpu.CompilerParams(dimension_semantics=("arbitrary", "arbitrary"),
                                             vmem_limit_bytes=VMEM_LIMIT_PROJ),
        name=name)(*[a for d in descs for a in d['args']])
    out, k = [], 0
    for d in descs:
        out.append(res[k:k + len(d['out_shape'])])
        k += len(d['out_shape'])
    return out


def _gla_stages(slab_ref, small_ref, w2_ref, gkb_ref, ng_ref, tri_ref, o_ref, st_ref, *, nb):
    @pl.when(pl.program_id(1) == 0)
    def _():
        st_ref[...] = jnp.zeros_like(st_ref)

    causal = _causal(CHUNK)
    masks = _lane_half_masks()

    def head(b, h, qg, kg, kd, dec):
        p, j = divmod(h, 2)
        ls = slice(p * LANES, (p + 1) * LANES)
        qm = qg[:, ls] * masks[j]
        att = jnp.where(causal, _mm_nt(qm, kg[:, ls]), 0.0)
        yield
        v_h = slab_ref[b, :, 512 + h * GLA_DV:512 + (h + 1) * GLA_DV]
        st = st_ref[b, h]
        o = _mm(att, v_h) + _mm_nt(qm, st)
        st_ref[b, h] = st * dec[:, ls] + _mm_tn(v_h, kd[:, ls] * masks[j])
        yield
        ms = jnp.mean(o * o, axis=-1, keepdims=True)
        yield
        o = o * lax.rsqrt(ms + NORM_EPS)
        o = o * ng_ref[:, h * GLA_DV:(h + 1) * GLA_DV]
        z_h = slab_ref[b, :, 1024 + h * GLA_DV:1024 + (h + 1) * GLA_DV]
        o_ref[b, :, h * GLA_DV:(h + 1) * GLA_DV] = o * _silu(z_h)

    def sequence(b):
        q = slab_ref[b, :, 0:256] * (GLA_DK ** -0.5)
        k = slab_ref[b, :, 256:512]
        gk = _mm(small_ref[b], w2_ref[...]) + gkb_ref[...]
        log_a = _log_sigmoid(gk) / GLA_GATE_NORMALIZER
        cum = _mm_sel_lhs(tri_ref[...], log_a)
        last = cum[CHUNK - 1:CHUNK, :]
        qg = q * jnp.exp(cum)
        kg = k * jnp.exp(-cum)
        kd = k * jnp.exp(last - cum)
        dec = jnp.exp(last)
        yield
        yield from _stepper([head(b, h, qg, kg, kd, dec) for h in range(GLA_HEADS)])

    return [sequence(b) for b in range(nb)], None


def _gla(slab, small, w2p, gkb, ng, tri):
    B, S, _ = slab.shape
    nb = BATCH_BLOCK
    const = lambda shape: pl.BlockSpec(shape, lambda b, c: (0,) * len(shape))
    return dict(
        stages=functools.partial(_gla_stages, nb=nb),
        in_specs=[pl.BlockSpec((nb, CHUNK, GLA_SLAB), lambda b, c: (b, c, 0)),
                  pl.BlockSpec((nb, CHUNK, LANES), lambda b, c: (b, c, 0)),
                  const((LANES, 256)), const((1, 256)), const((1, GROUP_W)), const((CHUNK, CHUNK))],
        args=[slab, small, w2p, gkb, ng, tri],
        out_specs=[pl.BlockSpec((nb, CHUNK, GROUP_W), lambda b, c: (b, c, 0))],
        out_shape=[jax.ShapeDtypeStruct((B, S, GROUP_W), F32)],
        scratch=[pltpu.VMEM((nb, GLA_HEADS, GLA_DV, LANES), F32)])


SSD_SLAB = 1536


def _ssd_stages(slab_ref, small_ref, rows_ref, cw_ref, cb_ref, dtb_ref, a_ref, dtb_col_ref, a_col_ref,
                dskip_ref, ng_ref, tri_ref, triu2_ref, ex_ref, o_ref, buf_ref, st_ref, *, nb):
    @pl.when(pl.program_id(1) == 0)
    def _():
        st_ref[...] = jnp.zeros_like(st_ref)
        for b in range(nb):
            buf_ref[b, 0:CARRY_ROWS, :] = jnp.zeros((CARRY_ROWS, SSD_XBC), F32)

    row_i = lax.broadcasted_iota(jnp.int32, (CHUNK, LANES), 0)
    col_i = lax.broadcasted_iota(jnp.int32, (CHUNK, LANES), 1)
    causal2 = jnp.bitwise_and(col_i, CHUNK - 1) <= row_i
    masks = _lane_half_masks()
    masks_b = (_bf(masks[0]), _bf(masks[1]))

    def pair(b, p, xbc, cum_b, cum_row2, xdt, xw, ecum, dec):
        g = p // 2
        ls = slice(p * LANES, (p + 1) * LANES)
        bm = xbc[:, 512 + g * SSD_STATE:512 + (g + 1) * SSD_STATE]
        cm = xbc[:, 768 + g * SSD_STATE:768 + (g + 1) * SSD_STATE]
        bmb = _bf(bm)
        cbm2 = _mm_nt(cm, jnp.concatenate([bmb, bmb], axis=0))
        st = st_ref[b, p]
        y = _mm(cm, st) * ecum[:, ls]
        st_ref[b, p] = st * dec[:, ls] + _mm_tn(bmb, xw[:, ls])
        lmat = jnp.exp(jnp.where(causal2, cum_b[:, ls] - cum_row2[p:p + 1, :], -jnp.inf))
        yield
        xb = _bf(xdt[:, ls])
        xs = jnp.concatenate([xb * masks_b[0], xb * masks_b[1]], axis=0)
        y = y + _mm(cbm2 * lmat, xs)
        yield
        y = y + dskip_ref[:, ls] * xbc[:, ls]
        o_ref[b, :, ls] = y * _silu(slab_ref[b, :, 1024 + p * LANES:1024 + (p + 1) * LANES])

    def sequence(b):
        taps = _shifted_rows(buf_ref.at[b], slab_ref[b, :, 0:SSD_XBC], (3, 2, 1, 0))
        xbc = cb_ref[...]
        for j in range(SSD_CONV):
            xbc = xbc + taps[j] * cw_ref[j:j + 1, :]
        xbc = _silu(xbc)
        yield
        dt_col = _softplus(small_ref[b] + dtb_ref[...])
        cum_col = _mm_sel_lhs(tri_ref[...], dt_col * a_ref[...])
        dt_row = _softplus(rows_ref[b, 0, ROWS_DT:ROWS_DT + SSD_HEADS, :] + dtb_col_ref[...])
        da_row = dt_row * a_col_ref[...]
        cum_row2 = (_mm_sel_rhs(da_row[0:SSD_HEADS // 2], triu2_ref[0])
                    + _mm_sel_rhs(da_row[SSD_HEADS // 2:SSD_HEADS], triu2_ref[1]))
        dt_b = _mm_sel_rhs(dt_col, ex_ref[...])
        cum_b = _mm_sel_rhs(cum_col, ex_ref[...])
        last_b = cum_b[CHUNK - 1:CHUNK, :]
        xdt = xbc[:, 0:512] * dt_b
        xw = xdt * jnp.exp(last_b - cum_b)
        ecum = jnp.exp(cum_b)
        dec = jnp.exp(last_b)
        yield
        yield from _stepper([pair(b, p, xbc, cum_b, cum_row2, xdt, xw, ecum, dec)
                             for p in range(SSD_HEADS // 2)])

    gens = [sequence(b) for b in range(nb)]

    def group_norm():
        for b in range(nb):
            y = o_ref[b]
            inv = lax.rsqrt(_rowsum(y * y) * (1.0 / GROUP_W) + NORM_EPS)
            for p in range(GROUP_W // LANES):
                ls = slice(p * LANES, (p + 1) * LANES)
                o_ref[b, :, ls] = y[:, ls] * inv * ng_ref[:, ls]

    return gens, group_norm


def _ssd(slab, small, rows, cw, cb, dtb, a, dtb_col, a_col, dskip, ng, tri, triu, ex):
    B, S, _ = slab.shape
    nb = BATCH_BLOCK
    const = lambda shape: pl.BlockSpec(shape, lambda b, c: (0,) * len(shape))
    return dict(
        stages=functools.partial(_ssd_stages, nb=nb),
        in_specs=[pl.BlockSpec((nb, CHUNK, SSD_SLAB), lambda b, c: (b, c, 0)),
                  pl.BlockSpec((nb, CHUNK, LANES), lambda b, c: (b, c, 0)),
                  pl.BlockSpec((nb, 1, ROWS_TOTAL, CHUNK), lambda b, c: (b, c, 0, 0)),
                  const((SSD_CONV, SSD_XBC)), const((1, SSD_XBC)), const((1, LANES)), const((1, LANES)),
                  const((SSD_HEADS, 1)), const((SSD_HEADS, 1)), const((1, GROUP_W)), const((1, GROUP_W)),
                  const((CHUNK, CHUNK)), const((2, CHUNK, LANES)), const((LANES, GROUP_W))],
        args=[slab, small, rows, cw, cb, dtb, a, dtb_col, a_col, dskip, ng, tri, triu, ex],
        out_specs=[pl.BlockSpec((nb, CHUNK, GROUP_W), lambda b, c: (b, c, 0))],
        out_shape=[jax.ShapeDtypeStruct((B, S, GROUP_W), F32)],
        scratch=[pltpu.VMEM((nb, CARRY_ROWS + CHUNK, SSD_XBC), F32),
                 pltpu.VMEM((nb, SSD_HEADS // 2, SSD_STATE, LANES), F32)])


MLSTM_SLAB = 2560


def _mlstm_stages(slab_ref, small_ref, rows_ref, cw_ref, cb_ref, igb_ref, fgb_ref, gb_col_ref, ng_ref,
                  tri_ref, triu_ref, exi_ref, exf_ref, o_ref, buf_ref, c_ref, nm_ref, *, nb):
    @pl.when(pl.program_id(1) == 0)
    def _():
        c_ref[...] = jnp.zeros_like(c_ref)
        nm_ref[...] = jnp.zeros_like(nm_ref)
        for b in range(nb):
            buf_ref[b, 0:CARRY_ROWS, :] = jnp.zeros((CARRY_ROWS, 2 * GROUP_W), F32)

    causal = _causal(CHUNK)

    def head(b, h, qk, li_b, ci_b, logi_row, cum_row):
        ls = slice(h * MLSTM_HEAD, (h + 1) * MLSTM_HEAD)
        q = qk[:, ls]
        k = qk[:, GROUP_W + h * MLSTM_HEAD:GROUP_W + (h + 1) * MLSTM_HEAD] * (MLSTM_HEAD ** -0.5)
        v = slab_ref[b, :, 1024 + h * MLSTM_HEAD:1024 + (h + 1) * MLSTM_HEAD]
        ci = ci_b[:, ls]
        li = li_b[:, ls]
        cr = cum_row[h:h + 1, :]
        lir = logi_row[h:h + 1, :]
        last = ci[CHUNK - 1:CHUNK, :]
        c_prev = c_ref[b, h]
        n_prev = nm_ref[b, h, 0:1, :]
        m_prev = nm_ref[b, h, 1:2, :]

        g = last - ci + li
        g_max = jnp.max(g, axis=0, keepdims=True)
        log_d = jnp.where(causal, ci[:, 0:CHUNK] - cr + lir, -jnp.inf)
        row_max = jnp.max(log_d, axis=-1, keepdims=True)
        qk_h = _mm_nt(q, k)
        qc = _mm(q, c_prev)
        qn = _rowsum(q * n_prev)
        yield
        kw = k * jnp.exp(g - g_max)
        c_loc = _mm_tn(kw, v)
        n_loc = jnp.sum(kw, axis=0, keepdims=True)
        m_new = jnp.maximum(last + m_prev, g_max)
        a_old = jnp.exp(last + m_prev - m_new)
        a_new = jnp.exp(g_max - m_new)
        c_ref[b, h] = a_old * c_prev + a_new * c_loc
        nm_ref[b, h, 0:1, :] = a_old * n_prev + a_new * n_loc
        nm_ref[b, h, 1:2, :] = m_new
        m_inter = ci + m_prev
        m_l = jnp.maximum(m_inter, row_max)
        wqk = qk_h * jnp.exp(log_d - m_l[:, 0:CHUNK])
        w_inter = jnp.exp(m_inter - m_l)
        num = _mm(wqk, v) + w_inter * qc
        den = _rowsum(wqk) + w_inter * qn
        yield
        den = jnp.maximum(jnp.abs(den), jnp.exp(-m_l))
        hh = num / den * _sigmoid(slab_ref[b, :, 1536 + h * MLSTM_HEAD:1536 + (h + 1) * MLSTM_HEAD])
        mu = _rowsum(hh) * (1.0 / MLSTM_HEAD)
        yield
        yc = hh - mu
        var = _rowsum(yc * yc) * (1.0 / MLSTM_HEAD)
        yield
        hh = yc * lax.rsqrt(var + NORM_EPS) * ng_ref[:, ls]
        o_ref[b, :, ls] = hh * _silu(slab_ref[b, :, 2048 + h * MLSTM_HEAD:2048 + (h + 1) * MLSTM_HEAD])

    def sequence(b):
        taps = _shifted_rows(buf_ref.at[b], slab_ref[b, :, 0:2 * GROUP_W], (3, 2, 1, 0))
        qk = cb_ref[...]
        for j in range(MLSTM_CONV):
            qk = qk + taps[j] * cw_ref[j:j + 1, :]
        qk = _silu(qk)
        yield
        logi_col = small_ref[b] + igb_ref[...]
        logf_col = _log_sigmoid(small_ref[b] + fgb_ref[...])
        cum_col = _mm_sel_lhs(tri_ref[...], logf_col)
        li_b = _mm_sel_rhs(logi_col, exi_ref[...])
        ci_b = _mm_sel_rhs(cum_col, exf_ref[...])
        pre_row = rows_ref[b, 0, ROWS_IF:ROWS_IF + 2 * MLSTM_HEADS, :] + gb_col_ref[...]
        logi_row = pre_row[0:MLSTM_HEADS, :]
        logf_row = _log_sigmoid(pre_row[MLSTM_HEADS:2 * MLSTM_HEADS, :])
        cum_row = _mm_sel_rhs(logf_row, triu_ref[...])
        yield
        yield from _stepper([head(b, h, qk, li_b, ci_b, logi_row, cum_row) for h in range(MLSTM_HEADS)])

    return [sequence(b) for b in range(nb)], None


def _mlstm(slab, small, rows, cw, cb, igb, fgb, gb_col, ng, tri, triu, exi, exf):
    B, S, _ = slab.shape
    nb = BATCH_BLOCK
    const = lambda shape: pl.BlockSpec(shape, lambda b, c: (0,) * len(shape))
    return dict(
        stages=functools.partial(_mlstm_stages, nb=nb),
        in_specs=[pl.BlockSpec((nb, CHUNK, MLSTM_SLAB), lambda b, c: (b, c, 0)),
                  pl.BlockSpec((nb, CHUNK, LANES), lambda b, c: (b, c, 0)),
                  pl.BlockSpec((nb, 1, ROWS_TOTAL, CHUNK), lambda b, c: (b, c, 0, 0)),
                  const((MLSTM_CONV, 2 * GROUP_W)), const((1, 2 * GROUP_W)), const((1, LANES)),
                  const((1, LANES)), const((2 * MLSTM_HEADS, 1)), const((1, GROUP_W)),
                  const((CHUNK, CHUNK)), const((CHUNK, CHUNK)),
                  const((LANES, GROUP_W)), const((LANES, GROUP_W))],
        args=[slab, small, rows, cw, cb, igb, fgb, gb_col, ng, tri, triu, exi, exf],
        out_specs=[pl.BlockSpec((nb, CHUNK, GROUP_W), lambda b, c: (b, c, 0))],
        out_shape=[jax.ShapeDtypeStruct((B, S, GROUP_W), F32)],
        scratch=[pltpu.VMEM((nb, CARRY_ROWS + CHUNK, 2 * GROUP_W), F32),
                 pltpu.VMEM((nb, MLSTM_HEADS, MLSTM_HEAD, MLSTM_HEAD), F32),
                 pltpu.VMEM((nb, MLSTM_HEADS, CARRY_ROWS, MLSTM_HEAD), F32)])


def _rwkv_slab_width(layer):
    return 2176 if layer == 0 else 2304


def _rwkv_shift_cols(layer):
    return 1664 if layer == 0 else 1792


def _rwkv_stages(*refs, layer, nb):
    if layer == 0:
        (slab_ref, mu_ref, w0_ref, a0_ref, w2a2_ref, kk_ref, ka_ref, rk_ref, lng_ref, lnb_ref,
         tri_ref, bd_ref, strict_ref, incl_ref, lvl_ref, o_ref, vf_out_ref, buf_ref, st_ref) = refs
    else:
        (slab_ref, vf_ref, mu_ref, w0_ref, a0_ref, w2a2_ref, v0_ref, v2_ref, kk_ref, ka_ref, rk_ref,
         lng_ref, lnb_ref, tri_ref, bd_ref, strict_ref, incl_ref, lvl_ref, o_ref, buf_ref, st_ref) = refs
    ws = _rwkv_shift_cols(layer)
    zoff = _rwkv_slab_width(layer) - GROUP_W
    npair = GROUP_W // LANES

    @pl.when(pl.program_id(1) == 0)
    def _():
        st_ref[...] = jnp.zeros_like(st_ref)
        for b in range(nb):
            buf_ref[b, 0:CARRY_ROWS, :] = jnp.zeros((CARRY_ROWS, ws), F32)

    lane = lax.broadcasted_iota(jnp.int32, (1, LANES), 1)
    masks = _lane_half_masks()
    bd = bd_ref[...]
    strict = strict_ref[...] > 0.5
    incl = incl_ref[...] > 0.5
    masks_b = (_bf(masks[0]), _bf(masks[1]))

    def rows2(t):
        tb = _bf(t)
        return jnp.concatenate([tb * masks_b[0], tb * masks_b[1]], axis=0)

    eye = (lax.broadcasted_iota(jnp.int32, (LANES, LANES), 0)
           == lax.broadcasted_iota(jnp.int32, (LANES, LANES), 1)).astype(F32)

    inv = 1.0 / RWKV_HEAD

    def sequence(b):
        inst = []
        f = slab_ref[b, :, 0:ws]
        (prev,) = _shifted_rows(buf_ref.at[b], f, (1,))
        f = f + mu_ref[...] * (prev - f)
        r = f[:, 0:512]
        k = f[:, 512:1024]
        v = f[:, 1024:1536]
        lora = f[:, 1536:1664]
        lora = jnp.where(lane < RWKV_W_RANK, jnp.tanh(lora), lora)
        wa = _mm(lora, w2a2_ref[...])
        if layer == 0:
            vf_out_ref[b] = v
        else:
            mix = _sigmoid(v0_ref[...] + _mm(f[:, 1664:1792], v2_ref[...]))
            v = v + (vf_ref[b] - v) * mix
        yield
        w_log = -_softplus(-(w0_ref[...] + wa[:, 0:512])) - RWKV_DECAY_OFFSET
        lw = -jnp.exp(w_log)
        a = _sigmoid(a0_ref[...] + wa[:, 512:1024])
        kk = k * kk_ref[...]
        k = k * (1.0 + (a - 1.0) * ka_ref[...])
        cum = _mm_sel_lhs(tri_ref[...], lw)
        ss = [_mm_sel2(kk[:, p * LANES:(p + 1) * LANES] ** 2, bd) for p in range(npair)]
        yield
        last = cum[CHUNK - 1:CHUNK, :]
        e_pos = jnp.exp(cum)
        e_neg = jnp.exp(-cum)
        e_end = jnp.exp(last - cum)
        e_prev = jnp.exp(cum - lw)
        gam = jnp.exp(last)
        for p in range(npair):
            ls = slice(p * LANES, (p + 1) * LANES)
            kk_p = kk[:, ls] / jnp.maximum(jnp.sqrt(ss[p]), 1e-12)
            k_p, r_p, v_p = k[:, ls], r[:, ls], v[:, ls]
            b_p = kk_p * a[:, ls]
            inst.append(dict(
                p=p, ls=ls, r=r_p, k=k_p, v=v_p, gam=gam[:, ls],
                la=rows2(-kk_p * e_prev[:, ls]), lr=rows2(r_p * e_pos[:, ls]),
                rb=rows2(b_p * e_neg[:, ls]), rk=rows2(k_p * e_neg[:, ls]),
                bh=rows2(b_p * e_end[:, ls]), kh=rows2(k_p * e_end[:, ls]), vs=rows2(v_p)))
        yield
        for d in inst:
            aa = _mm_nt(jnp.concatenate([d['la'], d['lr']], axis=0),
                        jnp.concatenate([d['rb'], d['rk']], axis=0))
            d['nab'] = _bf(jnp.where(strict, aa[0:LANES, 0:LANES], 0.0))
            d['aak'] = _bf(jnp.where(strict, aa[0:LANES, LANES:2 * LANES], 0.0))
            d['arb'] = _bf(jnp.where(incl, aa[LANES:2 * LANES, 0:LANES], 0.0))
            d['ark'] = _bf(jnp.where(incl, aa[LANES:2 * LANES, LANES:2 * LANES], 0.0))
            d['t'] = eye + (d['nab'] * lvl_ref[0]).astype(F32)
        yield
        for d in inst:
            d['x'] = _mm(d['aak'], d['vs'])
        for lv in range(1, 6):
            for d in inst:
                d['tb'] = _bf(d['t'])
                d['nt'] = _mm(d['nab'] * lvl_ref[lv], d['tb'])
            yield
            for d in inst:
                d['t'] = d['t'] + _mm(d['tb'], d['nt'])
            yield
        for d in inst:
            d['wu'] = _mm(d['t'], jnp.concatenate([d['la'], _bf(d['x'])], axis=1))
        yield
        for d in inst:
            d['wub'] = _bf(d['wu'])
            qy = _mm(d['arb'], d['wub'])
            d['qt'] = d['lr'].astype(F32) + qy[:, 0:LANES]
            d['y0'] = _mm(d['ark'], d['vs']) + qy[:, LANES:2 * LANES]
        yield
        for d in inst:
            st = st_ref[b, d['p']]
            uy = _mm_nt(jnp.concatenate([d['wub'][:, 0:LANES], _bf(d['qt'])], axis=0), st)
            d['ust'] = uy[0:LANES] + d['wu'][:, LANES:2 * LANES]
            d['st'] = st
            yst = uy[LANES:2 * LANES] + d['y0']
            d['y'] = yst[0:CHUNK] + yst[CHUNK:2 * CHUNK]
        yield
        for d in inst:
            st_ref[b, d['p']] = d['st'] * d['gam'] + _mm_tn(jnp.concatenate([_bf(d['ust']), d['vs']], axis=0),
                                                            jnp.concatenate([d['bh'], d['kh']], axis=0))
            d['mu'] = _mm_sel2(d['y'], bd) * inv
            d['bonus'] = _mm_sel2(d['r'] * d['k'] * rk_ref[:, d['ls']], bd) * d['v']
        yield
        for d in inst:
            d['yc'] = d['y'] - d['mu']
            d['var'] = _mm_sel2(d['yc'] * d['yc'], bd) * inv
        yield
        for d in inst:
            ls = d['ls']
            yn = d['yc'] * lax.rsqrt(d['var'] + RWKV_LN_EPS) * lng_ref[:, ls] + lnb_ref[:, ls]
            z_p = slab_ref[b, :, zoff + d['p'] * LANES:zoff + (d['p'] + 1) * LANES]
            o_ref[b, :, ls] = (yn + d['bonus']) * _silu(z_p)

    return [sequence(b) for b in range(nb)], None


def _rwkv(layer, slab, vf, mu, w0, a0, w2a2, v0, v2, kkw, ka, rk, lng, lnb, tri, bd, strict, incl, lvl):
    B, S, W = slab.shape
    nb = BATCH_BLOCK
    ws = _rwkv_shift_cols(layer)
    const = lambda shape: pl.BlockSpec(shape, lambda b, c: (0,) * len(shape))
    tok = lambda w: pl.BlockSpec((nb, CHUNK, w), lambda b, c: (b, c, 0))
    vecw = const((1, GROUP_W))
    in_specs = [tok(W)]
    args = [slab]
    if layer > 0:
        in_specs.append(tok(GROUP_W))
        args.append(vf)
    in_specs += [const((1, ws)), vecw, vecw, const((LANES, 2 * GROUP_W))]
    args += [mu, w0, a0, w2a2]
    if layer > 0:
        in_specs += [vecw, const((LANES, GROUP_W))]
        args += [v0, v2]
    in_specs += [vecw, vecw, vecw, vecw, vecw, const((CHUNK, CHUNK)), const((LANES, LANES)),
                 const((LANES, LANES)), const((LANES, LANES)), const((6, LANES, LANES))]
    args += [kkw, ka, rk, lng, lnb, tri, bd, strict, incl, lvl]
    out_shape = [jax.ShapeDtypeStruct((B, S, GROUP_W), F32)]
    out_specs = [tok(GROUP_W)]
    if layer == 0:
        out_shape.append(jax.ShapeDtypeStruct((B, S, GROUP_W), F32))
        out_specs.append(tok(GROUP_W))
    return dict(
        stages=functools.partial(_rwkv_stages, layer=layer, nb=nb),
        in_specs=in_specs, args=args, out_specs=out_specs, out_shape=out_shape,
        scratch=[pltpu.VMEM((nb, CARRY_ROWS + CHUNK, ws), F32),
                 pltpu.VMEM((nb, GROUP_W // LANES, LANES, LANES), F32)])


def _np_consts():
    i = np.arange(CHUNK)
    tri = (i[None, :] <= i[:, None]).astype(np.float32)
    t = np.arange(LANES)
    same = (t[:, None] // CHUNK) == (t[None, :] // CHUNK)
    strict = (same & (t[None, :] < t[:, None])).astype(np.float32)
    incl = (same & (t[None, :] <= t[:, None])).astype(np.float32)
    bd = same.astype(np.float32)
    lvl = np.stack([(((t[:, None] >> l) == (t[None, :] >> l))
                     & ((t[:, None] >> (l - 1)) != (t[None, :] >> (l - 1)))).astype(np.float32)
                    for l in range(1, 7)])
    ex = np.zeros((LANES, GROUP_W), np.float32)
    for h in range(SSD_HEADS):
        ex[SMALL_DT + h, h * SSD_HEADDIM:(h + 1) * SSD_HEADDIM] = 1.0
    exi = np.zeros((LANES, GROUP_W), np.float32)
    exf = np.zeros((LANES, GROUP_W), np.float32)
    for h in range(MLSTM_HEADS):
        exi[SMALL_I + h, h * MLSTM_HEAD:(h + 1) * MLSTM_HEAD] = 1.0
        exf[SMALL_F + h, h * MLSTM_HEAD:(h + 1) * MLSTM_HEAD] = 1.0
    triu = tri.T.copy()
    zero = np.zeros_like(triu)
    triu2 = np.stack([np.concatenate([triu, zero], axis=1), np.concatenate([zero, triu], axis=1)])
    return dict(tri=tri, triu=triu, triu2=triu2, strict=strict, incl=incl, bd=bd, lvl=lvl, ex=ex,
                exi=exi, exf=exf)


def _pad_rows(w, height):
    return jnp.pad(w, ((0, height - w.shape[0]), (0, 0)))


def _pad_vec(v, width, offset=0):
    v = v.reshape(1, -1)
    return jnp.pad(v, ((0, 0), (offset, width - offset - v.shape[1])))


F32_ROW_TILE = 8


def _layer(x2, v_first, layer, p, B, S, consts, g_final):
    sw = 3 * GROUP_W + RWKV_W_RANK + RWKV_A_RANK + (RWKV_V_RANK if layer > 0 else 0)
    names = ['gla_q', 'gla_k', 'gla_v', 'gla_gk', 'gla_z', 'rwkv_shift', 'rwkv_z', 'ssd_xbc', 'ssd_dt',
             'ssd_z', 'mlstm_qk', 'mlstm_v', 'mlstm_i', 'mlstm_f', 'mlstm_o', 'mlstm_z']
    widths = [256, 256, 512, 16, 512, sw, 512, SSD_XBC, SSD_HEADS, 512, 1024, 512, 4, 4, 512, 512]
    col0 = dict(zip(names, np.concatenate([[0], np.cumsum(widths)[:-1]]).tolist()))
    w_t = p['w_in'].T
    off = lambda name: col0[name]
    assert all(off(n) % F32_ROW_TILE == 0 for n in names if n != 'mlstm_f')
    rows = lambda name, n: _bf(w_t[off(name):off(name) + n, :])

    g = p['norm_g'].reshape(1, D_MODEL)
    tri, triu = consts['tri'], consts['triu']
    rows_of = lambda t: t.reshape(t.shape[0], B, S // CHUNK, CHUNK).transpose(1, 2, 0, 3)

    w_small = _pad_rows(jnp.concatenate([rows('gla_gk', GLA_RANK), rows('ssd_dt', SSD_HEADS),
                                         rows('mlstm_i', 2 * MLSTM_HEADS)], axis=0), LANES)
    even_odd = np.concatenate([np.arange(0, SSD_HEADS, 2), np.arange(1, SSD_HEADS, 2)])
    w_rows = jnp.concatenate([rows('ssd_dt', SSD_HEADS)[even_odd], rows('mlstm_i', 2 * MLSTM_HEADS)], axis=0)
    mu = p['rwkv_mu']
    if layer == 0:
        pieces, zero_lanes = [(off('rwkv_shift'), sw + GROUP_W, 0)], ()
        mu_p = mu.reshape(1, -1)
    else:
        pieces = [(off('rwkv_shift'), sw, 0), (off('rwkv_z'), GROUP_W, 1792)]
        zero_lanes = ((1664, 1792),)
        mu_p = _pad_vec(mu, 1792)
    slab, small, gate_rows = _in_proj(x2, g, w_t, pieces, _rwkv_slab_width(layer), zero_lanes,
                                      w_small, w_rows)
    slab = slab.reshape(B, S, _rwkv_slab_width(layer))
    small = small.reshape(B, S, LANES)
    gate_rows = rows_of(gate_rows)
    w2a2 = jnp.zeros((LANES, 2 * GROUP_W), F32)
    w2a2 = w2a2.at[0:RWKV_W_RANK, 0:GROUP_W].set(p['rwkv_w2'])
    w2a2 = w2a2.at[RWKV_W_RANK:, GROUP_W:].set(p['rwkv_a2'])
    vec = lambda t: t.reshape(1, GROUP_W)
    v0 = v2 = None
    if layer > 0:
        v0 = vec(p['rwkv_v0'])
        v2 = _bf(jnp.pad(p['rwkv_v2'], ((0, LANES - RWKV_V_RANK), (0, 0))))
    d_rwkv = _rwkv(layer, slab, v_first, mu_p, vec(p['rwkv_w0']), vec(p['rwkv_a0']), _bf(w2a2),
                            v0, v2, vec(p['rwkv_k_k']), vec(p['rwkv_k_a']), vec(p['rwkv_r_k']),
                            vec(p['rwkv_ln_g']), vec(p['rwkv_ln_b']), _bf(tri), _bf(consts['bd']),
                            consts['strict'], consts['incl'], _bf(consts['lvl']))

    slab = _in_proj(x2, g, w_t, [(off('gla_q'), 1024, 0), (off('gla_z'), GROUP_W, 1024)], GLA_SLAB)
    w2p = _bf(jnp.pad(p['gla_gk_w2'], ((SMALL_GK, LANES - SMALL_GK - GLA_RANK), (0, 0))))
    d_gla = _gla(slab.reshape(B, S, GLA_SLAB), small, w2p, p['gla_gk_b'].reshape(1, -1),
                 p['gla_norm_g'].reshape(1, -1), _bf(tri))

    slab = _in_proj(x2, g, w_t, [(off('ssd_xbc'), SSD_XBC, 0), (off('ssd_z'), GROUP_W, SSD_XBC)], SSD_SLAB)
    a_neg = -jnp.exp(p['ssd_a_log'])
    d_ssd = _ssd(slab.reshape(B, S, SSD_SLAB), small, gate_rows, p['ssd_conv_w'],
                 p['ssd_conv_b'].reshape(1, -1),
                 _pad_vec(p['ssd_dt_bias'], LANES, SMALL_DT), _pad_vec(a_neg, LANES, SMALL_DT),
                 p['ssd_dt_bias'][even_odd].reshape(-1, 1), a_neg[even_odd].reshape(-1, 1),
                 jnp.repeat(p['ssd_d'], SSD_HEADDIM).reshape(1, -1), p['ssd_norm_g'].reshape(1, -1),
                 _bf(tri), _bf(consts['triu2']), _bf(consts['ex']))

    slab = _in_proj(x2, g, w_t, [(off('mlstm_qk'), 3 * GROUP_W, 0), (off('mlstm_o'), 2 * GROUP_W, 3 * GROUP_W)],
                    MLSTM_SLAB)
    gb_col = jnp.concatenate([p['mlstm_ig_b'], p['mlstm_fg_b']]).reshape(-1, 1)
    d_ml = _mlstm(slab.reshape(B, S, MLSTM_SLAB), small, gate_rows, p['mlstm_conv_w'],
                  p['mlstm_conv_b'].reshape(1, -1), _pad_vec(p['mlstm_ig_b'], LANES, SMALL_I),
                  _pad_vec(p['mlstm_fg_b'], LANES, SMALL_F), gb_col, p['mlstm_norm_g'].reshape(1, -1),
                  _bf(tri), _bf(triu), _bf(consts['exi']), _bf(consts['exf']))

    T = B * S
    d_ml['start'], d_ssd['start'], d_gla['start'] = MIXER_STARTS
    o_rwkv, o_ml, o_ssd, o_gla = _run_mixers([d_rwkv, d_ml, d_ssd, d_gla], B, S, "mixers")
    if layer == 0:
        v_first = o_rwkv[1]
    ys = [y.reshape(T, GROUP_W) for y in (o_gla[0], o_rwkv[0], o_ssd[0], o_ml[0])]
    return _out_proj(x2, ys, _bf(p['w_out']), g_final), v_first


_PARAM_NAMES_0 = ['norm_g', 'w_in', 'w_out', 'gla_gk_w2', 'gla_gk_b', 'gla_norm_g', 'rwkv_mu', 'rwkv_w0',
                  'rwkv_w2', 'rwkv_a0', 'rwkv_a2', 'rwkv_k_k', 'rwkv_k_a', 'rwkv_r_k', 'rwkv_ln_g',
                  'rwkv_ln_b', 'ssd_conv_w', 'ssd_conv_b', 'ssd_dt_bias', 'ssd_a_log', 'ssd_d',
                  'ssd_norm_g', 'mlstm_conv_w', 'mlstm_conv_b', 'mlstm_ig_b', 'mlstm_fg_b', 'mlstm_norm_g']
_PARAM_NAMES_1 = (_PARAM_NAMES_0[:11] + ['rwkv_v0', 'rwkv_v2'] + _PARAM_NAMES_0[11:])


def kernel(x,
           norm_g_0, w_in_0, w_out_0, gla_gk_w2_0, gla_gk_b_0, gla_norm_g_0,
           rwkv_mu_0, rwkv_w0_0, rwkv_w2_0, rwkv_a0_0, rwkv_a2_0,
           rwkv_k_k_0, rwkv_k_a_0, rwkv_r_k_0, rwkv_ln_g_0, rwkv_ln_b_0,
           ssd_conv_w_0, ssd_conv_b_0, ssd_dt_bias_0, ssd_a_log_0, ssd_d_0, ssd_norm_g_0,
           mlstm_conv_w_0, mlstm_conv_b_0, mlstm_ig_b_0, mlstm_fg_b_0, mlstm_norm_g_0,
           norm_g_1, w_in_1, w_out_1, gla_gk_w2_1, gla_gk_b_1, gla_norm_g_1,
           rwkv_mu_1, rwkv_w0_1, rwkv_w2_1, rwkv_a0_1, rwkv_a2_1, rwkv_v0_1, rwkv_v2_1,
           rwkv_k_k_1, rwkv_k_a_1, rwkv_r_k_1, rwkv_ln_g_1, rwkv_ln_b_1,
           ssd_conv_w_1, ssd_conv_b_1, ssd_dt_bias_1, ssd_a_log_1, ssd_d_1, ssd_norm_g_1,
           mlstm_conv_w_1, mlstm_conv_b_1, mlstm_ig_b_1, mlstm_fg_b_1, mlstm_norm_g_1,
           final_norm_g):
    params = (norm_g_0, w_in_0, w_out_0, gla_gk_w2_0, gla_gk_b_0, gla_norm_g_0,
              rwkv_mu_0, rwkv_w0_0, rwkv_w2_0, rwkv_a0_0, rwkv_a2_0,
              rwkv_k_k_0, rwkv_k_a_0, rwkv_r_k_0, rwkv_ln_g_0, rwkv_ln_b_0,
              ssd_conv_w_0, ssd_conv_b_0, ssd_dt_bias_0, ssd_a_log_0, ssd_d_0, ssd_norm_g_0,
              mlstm_conv_w_0, mlstm_conv_b_0, mlstm_ig_b_0, mlstm_fg_b_0, mlstm_norm_g_0,
              norm_g_1, w_in_1, w_out_1, gla_gk_w2_1, gla_gk_b_1, gla_norm_g_1,
              rwkv_mu_1, rwkv_w0_1, rwkv_w2_1, rwkv_a0_1, rwkv_a2_1, rwkv_v0_1, rwkv_v2_1,
              rwkv_k_k_1, rwkv_k_a_1, rwkv_r_k_1, rwkv_ln_g_1, rwkv_ln_b_1,
              ssd_conv_w_1, ssd_conv_b_1, ssd_dt_bias_1, ssd_a_log_1, ssd_d_1, ssd_norm_g_1,
              mlstm_conv_w_1, mlstm_conv_b_1, mlstm_ig_b_1, mlstm_fg_b_1, mlstm_norm_g_1,
              final_norm_g)
    n0 = len(_PARAM_NAMES_0)
    n1 = len(_PARAM_NAMES_1)
    p0 = dict(zip(_PARAM_NAMES_0, params[:n0]))
    p1 = dict(zip(_PARAM_NAMES_1, params[n0:n0 + n1]))
    final_norm_g = params[n0 + n1]
    B, S, _ = x.shape
    consts = {k: jnp.asarray(v) for k, v in _np_consts().items()}
    x2 = x.reshape(B * S, D_MODEL)
    x2, v_first = _layer(x2, None, 0, p0, B, S, consts, None)
    x2, _ = _layer(x2, v_first, 1, p1, B, S, consts, final_norm_g.reshape(1, D_MODEL))
    return x2.reshape(B, S, D_MODEL)
```

```python
import functools

import numpy as np
import jax
import jax.numpy as jnp
from jax import lax
from jax.experimental import pallas as pl
from jax.experimental.pallas import tpu as pltpu

F32 = jnp.float32
BF16 = jnp.bfloat16

D_MODEL = 2048
CHUNK = 64
GROUP_W = 512
NORM_EPS = 1e-6
LANES = 128
CARRY_ROWS = 8

GLA_HEADS, GLA_DK, GLA_DV, GLA_RANK = 4, 64, 128, 16
GLA_GATE_NORMALIZER = 16.0
RWKV_HEAD, RWKV_W_RANK, RWKV_A_RANK, RWKV_V_RANK = 64, 64, 64, 32
RWKV_LN_EPS = 64e-5
RWKV_DECAY_OFFSET = 0.5
SSD_HEADS, SSD_HEADDIM, SSD_STATE, SSD_CONV = 8, 64, 128, 4
SSD_XBC = 1024
MLSTM_HEADS, MLSTM_HEAD, MLSTM_CONV = 4, 128, 4

ROW_TILES_IN = (512, 1024)
SMALL_GK, SMALL_DT, SMALL_I, SMALL_F = 0, 16, 24, 28
ROWS_DT, ROWS_IF, ROWS_TOTAL = 0, 8, 16
ROW_TILE_OUT = 512
VMEM_LIMIT_PROJ = 48 * 2**20
VMEM_LIMIT_IN_PROJ = 56 * 2**20
VMEM_BUDGET_PROJ = 52 * 2**20
WEIGHT_CAST_ROWS = 256


def _bf(x):
    return x.astype(BF16)


def _mm(a, b):
    return jnp.dot(_bf(a), _bf(b), preferred_element_type=F32)


def _mm_nt(a, b):
    return lax.dot_general(_bf(a), _bf(b), (((1,), (1,)), ((), ())), preferred_element_type=F32)


def _mm_tn(a, b):
    return lax.dot_general(_bf(a), _bf(b), (((0,), (0,)), ((), ())), preferred_element_type=F32)


def _split3(x):
    hi = _bf(x)
    r1 = x - hi.astype(F32)
    mid = _bf(r1)
    lo = _bf(r1 - mid.astype(F32))
    return hi, mid, lo


def _mm_sel_rhs(x, sel):
    hi, mid, lo = _split3(x)
    d = lambda a: jnp.dot(a, sel, preferred_element_type=F32)
    return d(hi) + d(mid) + d(lo)


def _mm_sel2(x, sel):
    hi = _bf(x)
    lo = _bf(x - hi.astype(F32))
    d = lambda a: jnp.dot(a, sel, preferred_element_type=F32)
    return d(hi) + d(lo)


def _mm_sel_lhs(sel, x):
    hi, mid, lo = _split3(x)
    d = lambda a: jnp.dot(sel, a, preferred_element_type=F32)
    return d(hi) + d(mid) + d(lo)


def _rowsum(x):
    ones = jnp.ones((x.shape[-1], LANES), BF16)
    hi = _bf(x)
    lo = _bf(x - hi.astype(F32))
    return (jnp.dot(hi, ones, preferred_element_type=F32)
            + jnp.dot(lo, ones, preferred_element_type=F32))


def _sigmoid(x):
    return 1.0 / (1.0 + jnp.exp(-x))


def _silu(x):
    return x * _sigmoid(x)


def _softplus(x):
    return jnp.maximum(x, 0.0) + jnp.log1p(jnp.exp(-jnp.abs(x)))


def _log_sigmoid(x):
    return -_softplus(-x)


def _lane_half_masks():
    lane = lax.broadcasted_iota(jnp.int32, (1, LANES), 1)
    lo = (lane < LANES // 2).astype(F32)
    return lo, 1.0 - lo


def _causal(n):
    r = lax.broadcasted_iota(jnp.int32, (n, n), 0)
    c = lax.broadcasted_iota(jnp.int32, (n, n), 1)
    return c <= r


def _shifted_rows(buf_ref, cur, offsets):
    buf_ref[CARRY_ROWS:CARRY_ROWS + CHUNK, :] = cur
    outs = [buf_ref[CARRY_ROWS - off:CARRY_ROWS - off + CHUNK, :] for off in offsets]
    tail = buf_ref[CHUNK:CHUNK + CARRY_ROWS, :]
    buf_ref[0:CARRY_ROWS, :] = tail
    return outs


def _in_proj_kernel(*refs, pieces, zero_lanes, has_small):
    x_ref, g_ref = refs[0], refs[1]
    n = len(pieces)
    wf_refs = refs[2:2 + n]
    w_refs = refs[len(refs) - n:]
    rest = refs[2 + n:len(refs) - n]
    if has_small:
        ws_ref, wr_ref, o_ref, os_ref, or_ref = rest
    else:
        (o_ref,) = rest

    @pl.when(pl.program_id(0) == 0)
    def _():
        for wf_ref, w_ref in zip(wf_refs, w_refs):
            for r0 in range(0, w_ref.shape[0], WEIGHT_CAST_ROWS):
                r1 = min(r0 + WEIGHT_CAST_ROWS, w_ref.shape[0])
                w_ref[r0:r1, :] = _bf(wf_ref[r0:r1, :])

    nt = lambda a, b: lax.dot_general(a, b, (((1,), (1,)), ((), ())), preferred_element_type=F32)
    tm = x_ref.shape[0]
    for lo, hi in zero_lanes:
        o_ref[:, lo:hi] = jnp.zeros((tm, hi - lo), F32)
    for half in range(2):
        rs = slice(half * (tm // 2), (half + 1) * (tm // 2))
        x = x_ref[rs, :]
        h = x * lax.rsqrt(jnp.mean(x * x, axis=-1, keepdims=True) + NORM_EPS) * g_ref[...]
        hb = _bf(h)
        for w_ref, (_, n_rows, lane_off) in zip(w_refs, pieces):
            o_ref[rs, lane_off:lane_off + n_rows] = nt(hb, w_ref[...])
        if has_small:
            os_ref[rs, :] = nt(hb, _bf(ws_ref[...]))
            or_ref[:, rs] = nt(_bf(wr_ref[...]), hb)


def _in_proj(x2, g, w_t, pieces, width, zero_lanes=(), w_small=None, w_rows=None):
    T = x2.shape[0]
    w_rows_total = sum(n_rows for _, n_rows, _ in pieces)
    tm = max(t for t in ROW_TILES_IN
             if (w_rows_total * D_MODEL * (4 + 2)
                 + 2 * (t * D_MODEL * 4 + t * (width + LANES) * 4)) <= VMEM_BUDGET_PROJ)
    const2 = lambda shape: pl.BlockSpec(shape, lambda i: (0, 0))
    in_specs = [pl.BlockSpec((tm, D_MODEL), lambda i: (i, 0)), const2((1, D_MODEL))]
    args = [x2, g]
    for off, n_rows, _ in pieces:
        in_specs.append(pl.BlockSpec((pl.Element(n_rows), pl.Element(D_MODEL)), lambda i, off=off: (off, 0),
                                     pipeline_mode=pl.Buffered(1)))
        args.append(w_t)
    scratch = [pltpu.VMEM((n_rows, D_MODEL), BF16) for _, n_rows, _ in pieces]
    out_shape = [jax.ShapeDtypeStruct((T, width), F32)]
    out_specs = [pl.BlockSpec((tm, width), lambda i: (i, 0))]
    if w_small is not None:
        r = w_rows.shape[0]
        in_specs += [const2((LANES, D_MODEL)), const2((r, D_MODEL))]
        args += [w_small, w_rows]
        out_shape += [jax.ShapeDtypeStruct((T, LANES), F32), jax.ShapeDtypeStruct((r, T), F32)]
        out_specs += [pl.BlockSpec((tm, LANES), lambda i: (i, 0)), pl.BlockSpec((r, tm), lambda i: (0, i))]
    res = pl.pallas_call(
        functools.partial(_in_proj_kernel, pieces=tuple(pieces), zero_lanes=tuple(zero_lanes),
                          has_small=w_small is not None),
        grid=(T // tm,), in_specs=in_specs, out_specs=out_specs, out_shape=out_shape,
        scratch_shapes=scratch,
        compiler_params=pltpu.CompilerParams(dimension_semantics=("arbitrary",),
                                             vmem_limit_bytes=VMEM_LIMIT_IN_PROJ),
        name="in_proj")(*args)
    return res if w_small is not None else res[0]


def _out_proj_kernel(*refs, final):
    if final:
        x_ref, y0, y1, y2, y3, wf_ref, g_ref, o_ref, w_ref = refs
    else:
        x_ref, y0, y1, y2, y3, wf_ref, o_ref, w_ref = refs

    @pl.when(pl.program_id(0) == 0)
    def _():
        for r0 in range(0, D_MODEL, WEIGHT_CAST_ROWS):
            w_ref[r0:r0 + WEIGHT_CAST_ROWS, :] = _bf(wf_ref[r0:r0 + WEIGHT_CAST_ROWS, :])

    acc = x_ref[...]
    for gi, y in enumerate((y0, y1, y2, y3)):
        acc = acc + jnp.dot(_bf(y[...]), w_ref[gi * GROUP_W:(gi + 1) * GROUP_W, :],
                            preferred_element_type=F32)
    if final:
        acc = acc * lax.rsqrt(jnp.mean(acc * acc, axis=-1, keepdims=True) + NORM_EPS) * g_ref[...]
    o_ref[...] = acc


def _out_proj(x2, ys, w, g_final=None):
    T = x2.shape[0]
    tm = ROW_TILE_OUT
    final = g_final is not None
    in_specs = [pl.BlockSpec((tm, D_MODEL), lambda i: (i, 0))]
    in_specs += [pl.BlockSpec((tm, GROUP_W), lambda i: (i, 0)) for _ in range(4)]
    in_specs += [pl.BlockSpec((D_MODEL, D_MODEL), lambda i: (0, 0), pipeline_mode=pl.Buffered(1))]
    args = [x2, *ys, w]
    if final:
        in_specs.append(pl.BlockSpec((1, D_MODEL), lambda i: (0, 0)))
        args.append(g_final)
    return pl.pallas_call(
        functools.partial(_out_proj_kernel, final=final),
        grid=(T // tm,), in_specs=in_specs,
        out_specs=pl.BlockSpec((tm, D_MODEL), lambda i: (i, 0)),
        out_shape=jax.ShapeDtypeStruct((T, D_MODEL), F32),
        scratch_shapes=[pltpu.VMEM((D_MODEL, D_MODEL), BF16)],
        compiler_params=pltpu.CompilerParams(dimension_semantics=("arbitrary",),
                                             vmem_limit_bytes=VMEM_LIMIT_IN_PROJ),
        name="out_proj")(*args)


GLA_SLAB = 1536


BATCH_BLOCK = 4
MIXER_STARTS = (5, 5, 12)


def _lockstep(gens, starts=None):
    gens = list(gens)
    starts = [0] * len(gens) if starts is None else list(starts)
    pending = list(zip(starts, gens))
    tick = 0
    while pending:
        alive = []
        for start, g in pending:
            if start > tick:
                alive.append((start, g))
                continue
            try:
                next(g)
                alive.append((start, g))
            except StopIteration:
                pass
        pending = alive
        tick += 1


def _stepper(gens):
    gens = list(gens)
    while gens:
        alive = []
        for g in gens:
            try:
                next(g)
                alive.append(g)
            except StopIteration:
                pass
        gens = alive
        yield


def _mixers_kernel(*refs, parts):
    n_in = sum(p[1] for p in parts)
    n_out = sum(p[2] for p in parts)
    ins, outs, scs = refs[:n_in], refs[n_in:n_in + n_out], refs[n_in + n_out:]
    gens, starts, posts = [], [], []
    i = o = s = 0
    for stages_fn, ni, no, ns, start in parts:
        g, post = stages_fn(*ins[i:i + ni], *outs[o:o + no], *scs[s:s + ns])
        i, o, s = i + ni, o + no, s + ns
        gens += g
        starts += [start] * len(g)
        posts.append(post)
    _lockstep(gens, starts)
    for post in posts:
        if post is not None:
            post()


def _run_mixers(descs, B, S, name):
    nb = BATCH_BLOCK
    parts = tuple((d['stages'], len(d['args']), len(d['out_shape']), len(d['scratch']), d.get('start', 0))
                  for d in descs)
    res = pl.pallas_call(
        functools.partial(_mixers_kernel, parts=parts), grid=(B // nb, S // CHUNK),
        in_specs=[sp for d in descs for sp in d['in_specs']],
        out_specs=[sp for d in descs for sp in d['out_specs']],
        out_shape=[sh for d in descs for sh in d['out_shape']],
        scratch_shapes=[sc for d in descs for sc in d['scratch']],
        compiler_params=pltpu.CompilerParams(dimension_semantics=("arbitrary", "arbitrary"),
                                             vmem_limit_bytes=VMEM_LIMIT_PROJ),
        name=name)(*[a for d in descs for a in d['args']])
    out, k = [], 0
    for d in descs:
        out.append(res[k:k + len(d['out_shape'])])
        k += len(d['out_shape'])
    return out


def _gla_stages(slab_ref, small_ref, w2_ref, gkb_ref, ng_ref, tri_ref, o_ref, st_ref, *, nb):
    @pl.when(pl.program_id(1) == 0)
    def _():
        st_ref[...] = jnp.zeros_like(st_ref)

    causal = _causal(CHUNK)
    masks = _lane_half_masks()

    def head(b, h, qg, kg, kd, dec):
        p, j = divmod(h, 2)
        ls = slice(p * LANES, (p + 1) * LANES)
        qm = qg[:, ls] * masks[j]
        att = jnp.where(causal, _mm_nt(qm, kg[:, ls]), 0.0)
        yield
        v_h = slab_ref[b, :, 512 + h * GLA_DV:512 + (h + 1) * GLA_DV]
        st = st_ref[b, h]
        o = _mm(att, v_h) + _mm_nt(qm, st)
        st_ref[b, h] = st * dec[:, ls] + _mm_tn(v_h, kd[:, ls] * masks[j])
        yield
        ms = jnp.mean(o * o, axis=-1, keepdims=True)
        yield
        o = o * lax.rsqrt(ms + NORM_EPS)
        o = o * ng_ref[:, h * GLA_DV:(h + 1) * GLA_DV]
        z_h = slab_ref[b, :, 1024 + h * GLA_DV:1024 + (h + 1) * GLA_DV]
        o_ref[b, :, h * GLA_DV:(h + 1) * GLA_DV] = o * _silu(z_h)

    def sequence(b):
        q = slab_ref[b, :, 0:256] * (GLA_DK ** -0.5)
        k = slab_ref[b, :, 256:512]
        gk = _mm(small_ref[b], w2_ref[...]) + gkb_ref[...]
        log_a = _log_sigmoid(gk) / GLA_GATE_NORMALIZER
        cum = _mm_sel_lhs(tri_ref[...], log_a)
        last = cum[CHUNK - 1:CHUNK, :]
        qg = q * jnp.exp(cum)
        kg = k * jnp.exp(-cum)
        kd = k * jnp.exp(last - cum)
        dec = jnp.exp(last)
        yield
        yield from _stepper([head(b, h, qg, kg, kd, dec) for h in range(GLA_HEADS)])

    return [sequence(b) for b in range(nb)], None


def _gla(slab, small, w2p, gkb, ng, tri):
    B, S, _ = slab.shape
    nb = BATCH_BLOCK
    const = lambda shape: pl.BlockSpec(shape, lambda b, c: (0,) * len(shape))
    return dict(
        stages=functools.partial(_gla_stages, nb=nb),
        in_specs=[pl.BlockSpec((nb, CHUNK, GLA_SLAB), lambda b, c: (b, c, 0)),
                  pl.BlockSpec((nb, CHUNK, LANES), lambda b, c: (b, c, 0)),
                  const((LANES, 256)), const((1, 256)), const((1, GROUP_W)), const((CHUNK, CHUNK))],
        args=[slab, small, w2p, gkb, ng, tri],
        out_specs=[pl.BlockSpec((nb, CHUNK, GROUP_W), lambda b, c: (b, c, 0))],
        out_shape=[jax.ShapeDtypeStruct((B, S, GROUP_W), F32)],
        scratch=[pltpu.VMEM((nb, GLA_HEADS, GLA_DV, LANES), F32)])


SSD_SLAB = 1536


def _ssd_stages(slab_ref, small_ref, rows_ref, cw_ref, cb_ref, dtb_ref, a_ref, dtb_col_ref, a_col_ref,
                dskip_ref, ng_ref, tri_ref, triu2_ref, ex_ref, o_ref, buf_ref, st_ref, *, nb):
    @pl.when(pl.program_id(1) == 0)
    def _():
        st_ref[...] = jnp.zeros_like(st_ref)
        for b in range(nb):
            buf_ref[b, 0:CARRY_ROWS, :] = jnp.zeros((CARRY_ROWS, SSD_XBC), F32)

    row_i = lax.broadcasted_iota(jnp.int32, (CHUNK, LANES), 0)
    col_i = lax.broadcasted_iota(jnp.int32, (CHUNK, LANES), 1)
    causal2 = jnp.bitwise_and(col_i, CHUNK - 1) <= row_i
    masks = _lane_half_masks()
    masks_b = (_bf(masks[0]), _bf(masks[1]))

    def pair(b, p, xbc, cum_b, cum_row2, xdt, xw, ecum, dec):
        g = p // 2
        ls = slice(p * LANES, (p + 1) * LANES)
        bm = xbc[:, 512 + g * SSD_STATE:512 + (g + 1) * SSD_STATE]
        cm = xbc[:, 768 + g * SSD_STATE:768 + (g + 1) * SSD_STATE]
        bmb = _bf(bm)
        cbm2 = _mm_nt(cm, jnp.concatenate([bmb, bmb], axis=0))
        st = st_ref[b, p]
        y = _mm(cm, st) * ecum[:, ls]
        st_ref[b, p] = st * dec[:, ls] + _mm_tn(bmb, xw[:, ls])
        lmat = jnp.exp(jnp.where(causal2, cum_b[:, ls] - cum_row2[p:p + 1, :], -jnp.inf))
        yield
        xb = _bf(xdt[:, ls])
        xs = jnp.concatenate([xb * masks_b[0], xb * masks_b[1]], axis=0)
        y = y + _mm(cbm2 * lmat, xs)
        yield
        y = y + dskip_ref[:, ls] * xbc[:, ls]
        o_ref[b, :, ls] = y * _silu(slab_ref[b, :, 1024 + p * LANES:1024 + (p + 1) * LANES])

    def sequence(b):
        taps = _shifted_rows(buf_ref.at[b], slab_ref[b, :, 0:SSD_XBC], (3, 2, 1, 0))
        xbc = cb_ref[...]
        for j in range(SSD_CONV):
            xbc = xbc + taps[j] * cw_ref[j:j + 1, :]
        xbc = _silu(xbc)
        yield
        dt_col = _softplus(small_ref[b] + dtb_ref[...])
        cum_col = _mm_sel_lhs(tri_ref[...], dt_col * a_ref[...])
        dt_row = _softplus(rows_ref[b, 0, ROWS_DT:ROWS_DT + SSD_HEADS, :] + dtb_col_ref[...])
        da_row = dt_row * a_col_ref[...]
        cum_row2 = (_mm_sel_rhs(da_row[0:SSD_HEADS // 2], triu2_ref[0])
                    + _mm_sel_rhs(da_row[SSD_HEADS // 2:SSD_HEADS], triu2_ref[1]))
        dt_b = _mm_sel_rhs(dt_col, ex_ref[...])
        cum_b = _mm_sel_rhs(cum_col, ex_ref[...])
        last_b = cum_b[CHUNK - 1:CHUNK, :]
        xdt = xbc[:, 0:512] * dt_b
        xw = xdt * jnp.exp(last_b - cum_b)
        ecum = jnp.exp(cum_b)
        dec = jnp.exp(last_b)
        yield
        yield from _stepper([pair(b, p, xbc, cum_b, cum_row2, xdt, xw, ecum, dec)
                             for p in range(SSD_HEADS // 2)])

    gens = [sequence(b) for b in range(nb)]

    def group_norm():
        for b in range(nb):
            y = o_ref[b]
            inv = lax.rsqrt(_rowsum(y * y) * (1.0 / GROUP_W) + NORM_EPS)
            for p in range(GROUP_W // LANES):
                ls = slice(p * LANES, (p + 1) * LANES)
                o_ref[b, :, ls] = y[:, ls] * inv * ng_ref[:, ls]

    return gens, group_norm


def _ssd(slab, small, rows, cw, cb, dtb, a, dtb_col, a_col, dskip, ng, tri, triu, ex):
    B, S, _ = slab.shape
    nb = BATCH_BLOCK
    const = lambda shape: pl.BlockSpec(shape, lambda b, c: (0,) * len(shape))
    return dict(
        stages=functools.partial(_ssd_stages, nb=nb),
        in_specs=[pl.BlockSpec((nb, CHUNK, SSD_SLAB), lambda b, c: (b, c, 0)),
                  pl.BlockSpec((nb, CHUNK, LANES), lambda b, c: (b, c, 0)),
                  pl.BlockSpec((nb, 1, ROWS_TOTAL, CHUNK), lambda b, c: (b, c, 0, 0)),
                  const((SSD_CONV, SSD_XBC)), const((1, SSD_XBC)), const((1, LANES)), const((1, LANES)),
                  const((SSD_HEADS, 1)), const((SSD_HEADS, 1)), const((1, GROUP_W)), const((1, GROUP_W)),
                  const((CHUNK, CHUNK)), const((2, CHUNK, LANES)), const((LANES, GROUP_W))],
        args=[slab, small, rows, cw, cb, dtb, a, dtb_col, a_col, dskip, ng, tri, triu, ex],
        out_specs=[pl.BlockSpec((nb, CHUNK, GROUP_W), lambda b, c: (b, c, 0))],
        out_shape=[jax.ShapeDtypeStruct((B, S, GROUP_W), F32)],
        scratch=[pltpu.VMEM((nb, CARRY_ROWS + CHUNK, SSD_XBC), F32),
                 pltpu.VMEM((nb, SSD_HEADS // 2, SSD_STATE, LANES), F32)])


MLSTM_SLAB = 2560


def _mlstm_stages(slab_ref, small_ref, rows_ref, cw_ref, cb_ref, igb_ref, fgb_ref, gb_col_ref, ng_ref,
                  tri_ref, triu_ref, exi_ref, exf_ref, o_ref, buf_ref, c_ref, nm_ref, *, nb):
    @pl.when(pl.program_id(1) == 0)
    def _():
        c_ref[...] = jnp.zeros_like(c_ref)
        nm_ref[...] = jnp.zeros_like(nm_ref)
        for b in range(nb):
            buf_ref[b, 0:CARRY_ROWS, :] = jnp.zeros((CARRY_ROWS, 2 * GROUP_W), F32)

    causal = _causal(CHUNK)

    def head(b, h, qk, li_b, ci_b, logi_row, cum_row):
        ls = slice(h * MLSTM_HEAD, (h + 1) * MLSTM_HEAD)
        q = qk[:, ls]
        k = qk[:, GROUP_W + h * MLSTM_HEAD:GROUP_W + (h + 1) * MLSTM_HEAD] * (MLSTM_HEAD ** -0.5)
        v = slab_ref[b, :, 1024 + h * MLSTM_HEAD:1024 + (h + 1) * MLSTM_HEAD]
        ci = ci_b[:, ls]
        li = li_b[:, ls]
        cr = cum_row[h:h + 1, :]
        lir = logi_row[h:h + 1, :]
        last = ci[CHUNK - 1:CHUNK, :]
        c_prev = c_ref[b, h]
        n_prev = nm_ref[b, h, 0:1, :]
        m_prev = nm_ref[b, h, 1:2, :]

        g = last - ci + li
        g_max = jnp.max(g, axis=0, keepdims=True)
        log_d = jnp.where(causal, ci[:, 0:CHUNK] - cr + lir, -jnp.inf)
        row_max = jnp.max(log_d, axis=-1, keepdims=True)
        qk_h = _mm_nt(q, k)
        qc = _mm(q, c_prev)
        qn = _rowsum(q * n_prev)
        yield
        kw = k * jnp.exp(g - g_max)
        c_loc = _mm_tn(kw, v)
        n_loc = jnp.sum(kw, axis=0, keepdims=True)
        m_new = jnp.maximum(last + m_prev, g_max)
        a_old = jnp.exp(last + m_prev - m_new)
        a_new = jnp.exp(g_max - m_new)
        c_ref[b, h] = a_old * c_prev + a_new * c_loc
        nm_ref[b, h, 0:1, :] = a_old * n_prev + a_new * n_loc
        nm_ref[b, h, 1:2, :] = m_new
        m_inter = ci + m_prev
        m_l = jnp.maximum(m_inter, row_max)
        wqk = qk_h * jnp.exp(log_d - m_l[:, 0:CHUNK])
        w_inter = jnp.exp(m_inter - m_l)
        num = _mm(wqk, v) + w_inter * qc
        den = _rowsum(wqk) + w_inter * qn
        yield
        den = jnp.maximum(jnp.abs(den), jnp.exp(-m_l))
        hh = num / den * _sigmoid(slab_ref[b, :, 1536 + h * MLSTM_HEAD:1536 + (h + 1) * MLSTM_HEAD])
        mu = _rowsum(hh) * (1.0 / MLSTM_HEAD)
        yield
        yc = hh - mu
        var = _rowsum(yc * yc) * (1.0 / MLSTM_HEAD)
        yield
        hh = yc * lax.rsqrt(var + NORM_EPS) * ng_ref[:, ls]
        o_ref[b, :, ls] = hh * _silu(slab_ref[b, :, 2048 + h * MLSTM_HEAD:2048 + (h + 1) * MLSTM_HEAD])

    def sequence(b):
        taps = _shifted_rows(buf_ref.at[b], slab_ref[b, :, 0:2 * GROUP_W], (3, 2, 1, 0))
        qk = cb_ref[...]
        for j in range(MLSTM_CONV):
            qk = qk + taps[j] * cw_ref[j:j + 1, :]
        qk = _silu(qk)
        yield
        logi_col = small_ref[b] + igb_ref[...]
        logf_col = _log_sigmoid(small_ref[b] + fgb_ref[...])
        cum_col = _mm_sel_lhs(tri_ref[...], logf_col)
        li_b = _mm_sel_rhs(logi_col, exi_ref[...])
        ci_b = _mm_sel_rhs(cum_col, exf_ref[...])
        pre_row = rows_ref[b, 0, ROWS_IF:ROWS_IF + 2 * MLSTM_HEADS, :] + gb_col_ref[...]
        logi_row = pre_row[0:MLSTM_HEADS, :]
        logf_row = _log_sigmoid(pre_row[MLSTM_HEADS:2 * MLSTM_HEADS, :])
        cum_row = _mm_sel_rhs(logf_row, triu_ref[...])
        yield
        yield from _stepper([head(b, h, qk, li_b, ci_b, logi_row, cum_row) for h in range(MLSTM_HEADS)])

    return [sequence(b) for b in range(nb)], None


def _mlstm(slab, small, rows, cw, cb, igb, fgb, gb_col, ng, tri, triu, exi, exf):
    B, S, _ = slab.shape
    nb = BATCH_BLOCK
    const = lambda shape: pl.BlockSpec(shape, lambda b, c: (0,) * len(shape))
    return dict(
        stages=functools.partial(_mlstm_stages, nb=nb),
        in_specs=[pl.BlockSpec((nb, CHUNK, MLSTM_SLAB), lambda b, c: (b, c, 0)),
                  pl.BlockSpec((nb, CHUNK, LANES), lambda b, c: (b, c, 0)),
                  pl.BlockSpec((nb, 1, ROWS_TOTAL, CHUNK), lambda b, c: (b, c, 0, 0)),
                  const((MLSTM_CONV, 2 * GROUP_W)), const((1, 2 * GROUP_W)), const((1, LANES)),
                  const((1, LANES)), const((2 * MLSTM_HEADS, 1)), const((1, GROUP_W)),
                  const((CHUNK, CHUNK)), const((CHUNK, CHUNK)),
                  const((LANES, GROUP_W)), const((LANES, GROUP_W))],
        args=[slab, small, rows, cw, cb, igb, fgb, gb_col, ng, tri, triu, exi, exf],
        out_specs=[pl.BlockSpec((nb, CHUNK, GROUP_W), lambda b, c: (b, c, 0))],
        out_shape=[jax.ShapeDtypeStruct((B, S, GROUP_W), F32)],
        scratch=[pltpu.VMEM((nb, CARRY_ROWS + CHUNK, 2 * GROUP_W), F32),
                 pltpu.VMEM((nb, MLSTM_HEADS, MLSTM_HEAD, MLSTM_HEAD), F32),
                 pltpu.VMEM((nb, MLSTM_HEADS, CARRY_ROWS, MLSTM_HEAD), F32)])


def _rwkv_slab_width(layer):
    return 2176 if layer == 0 else 2304


def _rwkv_shift_cols(layer):
    return 1664 if layer == 0 else 1792


def _rwkv_stages(*refs, layer, nb):
    if layer == 0:
        (slab_ref, mu_ref, w0_ref, a0_ref, w2a2_ref, kk_ref, ka_ref, rk_ref, lng_ref, lnb_ref,
         tri_ref, bd_ref, strict_ref, incl_ref, lvl_ref, o_ref, vf_out_ref, buf_ref, st_ref) = refs
    else:
        (slab_ref, vf_ref, mu_ref, w0_ref, a0_ref, w2a2_ref, v0_ref, v2_ref, kk_ref, ka_ref, rk_ref,
         lng_ref, lnb_ref, tri_ref, bd_ref, strict_ref, incl_ref, lvl_ref, o_ref, buf_ref, st_ref) = refs
    ws = _rwkv_shift_cols(layer)
    zoff = _rwkv_slab_width(layer) - GROUP_W
    npair = GROUP_W // LANES

    @pl.when(pl.program_id(1) == 0)
    def _():
        st_ref[...] = jnp.zeros_like(st_ref)
        for b in range(nb):
            buf_ref[b, 0:CARRY_ROWS, :] = jnp.zeros((CARRY_ROWS, ws), F32)

    lane = lax.broadcasted_iota(jnp.int32, (1, LANES), 1)
    masks = _lane_half_masks()
    bd = bd_ref[...]
    strict = strict_ref[...] > 0.5
    incl = incl_ref[...] > 0.5
    masks_b = (_bf(masks[0]), _bf(masks[1]))

    def rows2(t):
        tb = _bf(t)
        return jnp.concatenate([tb * masks_b[0], tb * masks_b[1]], axis=0)

    eye = (lax.broadcasted_iota(jnp.int32, (LANES, LANES), 0)
           == lax.broadcasted_iota(jnp.int32, (LANES, LANES), 1)).astype(F32)

    inv = 1.0 / RWKV_HEAD

    def sequence(b):
        inst = []
        f = slab_ref[b, :, 0:ws]
        (prev,) = _shifted_rows(buf_ref.at[b], f, (1,))
        f = f + mu_ref[...] * (prev - f)
        r = f[:, 0:512]
        k = f[:, 512:1024]
        v = f[:, 1024:1536]
        lora = f[:, 1536:1664]
        lora = jnp.where(lane < RWKV_W_RANK, jnp.tanh(lora), lora)
        wa = _mm(lora, w2a2_ref[...])
        if layer == 0:
            vf_out_ref[b] = v
        else:
            mix = _sigmoid(v0_ref[...] + _mm(f[:, 1664:1792], v2_ref[...]))
            v = v + (vf_ref[b] - v) * mix
        yield
        w_log = -_softplus(-(w0_ref[...] + wa[:, 0:512])) - RWKV_DECAY_OFFSET
        lw = -jnp.exp(w_log)
        a = _sigmoid(a0_ref[...] + wa[:, 512:1024])
        kk = k * kk_ref[...]
        k = k * (1.0 + (a - 1.0) * ka_ref[...])
        cum = _mm_sel_lhs(tri_ref[...], lw)
        ss = [_mm_sel2(kk[:, p * LANES:(p + 1) * LANES] ** 2, bd) for p in range(npair)]
        yield
        last = cum[CHUNK - 1:CHUNK, :]
        e_pos = jnp.exp(cum)
        e_neg = jnp.exp(-cum)
        e_end = jnp.exp(last - cum)
        e_prev = jnp.exp(cum - lw)
        gam = jnp.exp(last)
        for p in range(npair):
            ls = slice(p * LANES, (p + 1) * LANES)
            kk_p = kk[:, ls] / jnp.maximum(jnp.sqrt(ss[p]), 1e-12)
            k_p, r_p, v_p = k[:, ls], r[:, ls], v[:, ls]
            b_p = kk_p * a[:, ls]
            inst.append(dict(
                p=p, ls=ls, r=r_p, k=k_p, v=v_p, gam=gam[:, ls],
                la=rows2(-kk_p * e_prev[:, ls]), lr=rows2(r_p * e_pos[:, ls]),
                rb=rows2(b_p * e_neg[:, ls]), rk=rows2(k_p * e_neg[:, ls]),
                bh=rows2(b_p * e_end[:, ls]), kh=rows2(k_p * e_end[:, ls]), vs=rows2(v_p)))
        yield
        for d in inst:
            aa = _mm_nt(jnp.concatenate([d['la'], d['lr']], axis=0),
                        jnp.concatenate([d['rb'], d['rk']], axis=0))
            d['nab'] = _bf(jnp.where(strict, aa[0:LANES, 0:LANES], 0.0))
            d['aak'] = _bf(jnp.where(strict, aa[0:LANES, LANES:2 * LANES], 0.0))
            d['arb'] = _bf(jnp.where(incl, aa[LANES:2 * LANES, 0:LANES], 0.0))
            d['ark'] = _bf(jnp.where(incl, aa[LANES:2 * LANES, LANES:2 * LANES], 0.0))
            d['t'] = eye + (d['nab'] * lvl_ref[0]).astype(F32)
        yield
        for d in inst:
            d['x'] = _mm(d['aak'], d['vs'])
        for lv in range(1, 6):
            for d in inst:
                d['tb'] = _bf(d['t'])
                d['nt'] = _mm(d['nab'] * lvl_ref[lv], d['tb'])
            yield
            for d in inst:
                d['t'] = d['t'] + _mm(d['tb'], d['nt'])
            yield
        for d in inst:
            d['wu'] = _mm(d['t'], jnp.concatenate([d['la'], _bf(d['x'])], axis=1))
        yield
        for d in inst:
            d['wub'] = _bf(d['wu'])
            qy = _mm(d['arb'], d['wub'])
            d['qt'] = d['lr'].astype(F32) + qy[:, 0:LANES]
            d['y0'] = _mm(d['ark'], d['vs']) + qy[:, LANES:2 * LANES]
        yield
        for d in inst:
            st = st_ref[b, d['p']]
            uy = _mm_nt(jnp.concatenate([d['wub'][:, 0:LANES], _bf(d['qt'])], axis=0), st)
            d['ust'] = uy[0:LANES] + d['wu'][:, LANES:2 * LANES]
            d['st'] = st
            yst = uy[LANES:2 * LANES] + d['y0']
            d['y'] = yst[0:CHUNK] + yst[CHUNK:2 * CHUNK]
        yield
        for d in inst:
            st_ref[b, d['p']] = d['st'] * d['gam'] + _mm_tn(jnp.concatenate([_bf(d['ust']), d['vs']], axis=0),
                                                            jnp.concatenate([d['bh'], d['kh']], axis=0))
            d['mu'] = _mm_sel2(d['y'], bd) * inv
            d['bonus'] = _mm_sel2(d['r'] * d['k'] * rk_ref[:, d['ls']], bd) * d['v']
        yield
        for d in inst:
            d['yc'] = d['y'] - d['mu']
            d['var'] = _mm_sel2(d['yc'] * d['yc'], bd) * inv
        yield
        for d in inst:
            ls = d['ls']
            yn = d['yc'] * lax.rsqrt(d['var'] + RWKV_LN_EPS) * lng_ref[:, ls] + lnb_ref[:, ls]
            z_p = slab_ref[b, :, zoff + d['p'] * LANES:zoff + (d['p'] + 1) * LANES]
            o_ref[b, :, ls] = (yn + d['bonus']) * _silu(z_p)

    return [sequence(b) for b in range(nb)], None


def _rwkv(layer, slab, vf, mu, w0, a0, w2a2, v0, v2, kkw, ka, rk, lng, lnb, tri, bd, strict, incl, lvl):
    B, S, W = slab.shape
    nb = BATCH_BLOCK
    ws = _rwkv_shift_cols(layer)
    const = lambda shape: pl.BlockSpec(shape, lambda b, c: (0,) * len(shape))
    tok = lambda w: pl.BlockSpec((nb, CHUNK, w), lambda b, c: (b, c, 0))
    vecw = const((1, GROUP_W))
    in_specs = [tok(W)]
    args = [slab]
    if layer > 0:
        in_specs.append(tok(GROUP_W))
        args.append(vf)
    in_specs += [const((1, ws)), vecw, vecw, const((LANES, 2 * GROUP_W))]
    args += [mu, w0, a0, w2a2]
    if layer > 0:
        in_specs += [vecw, const((LANES, GROUP_W))]
        args += [v0, v2]
    in_specs += [vecw, vecw, vecw, vecw, vecw, const((CHUNK, CHUNK)), const((LANES, LANES)),
                 const((LANES, LANES)), const((LANES, LANES)), const((6, LANES, LANES))]
    args += [kkw, ka, rk, lng, lnb, tri, bd, strict, incl, lvl]
    out_shape = [jax.ShapeDtypeStruct((B, S, GROUP_W), F32)]
    out_specs = [tok(GROUP_W)]
    if layer == 0:
        out_shape.append(jax.ShapeDtypeStruct((B, S, GROUP_W), F32))
        out_specs.append(tok(GROUP_W))
    return dict(
        stages=functools.partial(_rwkv_stages, layer=layer, nb=nb),
        in_specs=in_specs, args=args, out_specs=out_specs, out_shape=out_shape,
        scratch=[pltpu.VMEM((nb, CARRY_ROWS + CHUNK, ws), F32),
                 pltpu.VMEM((nb, GROUP_W // LANES, LANES, LANES), F32)])


def _np_consts():
    i = np.arange(CHUNK)
    tri = (i[None, :] <= i[:, None]).astype(np.float32)
    t = np.arange(LANES)
    same = (t[:, None] // CHUNK) == (t[None, :] // CHUNK)
    strict = (same & (t[None, :] < t[:, None])).astype(np.float32)
    incl = (same & (t[None, :] <= t[:, None])).astype(np.float32)
    bd = same.astype(np.float32)
    lvl = np.stack([(((t[:, None] >> l) == (t[None, :] >> l))
                     & ((t[:, None] >> (l - 1)) != (t[None, :] >> (l - 1)))).astype(np.float32)
                    for l in range(1, 7)])
    ex = np.zeros((LANES, GROUP_W), np.float32)
    for h in range(SSD_HEADS):
        ex[SMALL_DT + h, h * SSD_HEADDIM:(h + 1) * SSD_HEADDIM] = 1.0
    exi = np.zeros((LANES, GROUP_W), np.float32)
    exf = np.zeros((LANES, GROUP_W), np.float32)
    for h in range(MLSTM_HEADS):
        exi[SMALL_I + h, h * MLSTM_HEAD:(h + 1) * MLSTM_HEAD] = 1.0
        exf[SMALL_F + h, h * MLSTM_HEAD:(h + 1) * MLSTM_HEAD] = 1.0
    triu = tri.T.copy()
    zero = np.zeros_like(triu)
    triu2 = np.stack([np.concatenate([triu, zero], axis=1), np.concatenate([zero, triu], axis=1)])
    return dict(tri=tri, triu=triu, triu2=triu2, strict=strict, incl=incl, bd=bd, lvl=lvl, ex=ex,
                exi=exi, exf=exf)


def _pad_rows(w, height):
    return jnp.pad(w, ((0, height - w.shape[0]), (0, 0)))


def _pad_vec(v, width, offset=0):
    v = v.reshape(1, -1)
    return jnp.pad(v, ((0, 0), (offset, width - offset - v.shape[1])))


F32_ROW_TILE = 8


def _layer(x2, v_first, layer, p, B, S, consts, g_final):
    sw = 3 * GROUP_W + RWKV_W_RANK + RWKV_A_RANK + (RWKV_V_RANK if layer > 0 else 0)
    names = ['gla_q', 'gla_k', 'gla_v', 'gla_gk', 'gla_z', 'rwkv_shift', 'rwkv_z', 'ssd_xbc', 'ssd_dt',
             'ssd_z', 'mlstm_qk', 'mlstm_v', 'mlstm_i', 'mlstm_f', 'mlstm_o', 'mlstm_z']
    widths = [256, 256, 512, 16, 512, sw, 512, SSD_XBC, SSD_HEADS, 512, 1024, 512, 4, 4, 512, 512]
    col0 = dict(zip(names, np.concatenate([[0], np.cumsum(widths)[:-1]]).tolist()))
    w_t = p['w_in'].T
    off = lambda name: col0[name]
    assert all(off(n) % F32_ROW_TILE == 0 for n in names if n != 'mlstm_f')
    rows = lambda name, n: w_t[off(name):off(name) + n, :]

    g = p['norm_g'].reshape(1, D_MODEL)
    tri, triu = consts['tri'], consts['triu']
    rows_of = lambda t: t.reshape(t.shape[0], B, S // CHUNK, CHUNK).transpose(1, 2, 0, 3)

    w_small = _pad_rows(jnp.concatenate([rows('gla_gk', GLA_RANK), rows('ssd_dt', SSD_HEADS),
                                         rows('mlstm_i', 2 * MLSTM_HEADS)], axis=0), LANES)
    even_odd = np.concatenate([np.arange(0, SSD_HEADS, 2), np.arange(1, SSD_HEADS, 2)])
    w_rows = jnp.concatenate([rows('ssd_dt', SSD_HEADS)[even_odd], rows('mlstm_i', 2 * MLSTM_HEADS)], axis=0)
    mu = p['rwkv_mu']
    if layer == 0:
        pieces, zero_lanes = [(off('rwkv_shift'), sw + GROUP_W, 0)], ()
        mu_p = mu.reshape(1, -1)
    else:
        pieces = [(off('rwkv_shift'), sw, 0), (off('rwkv_z'), GROUP_W, 1792)]
        zero_lanes = ((1664, 1792),)
        mu_p = _pad_vec(mu, 1792)
    slab, small, gate_rows = _in_proj(x2, g, w_t, pieces, _rwkv_slab_width(layer), zero_lanes,
                                      w_small, w_rows)
    slab = slab.reshape(B, S, _rwkv_slab_width(layer))
    small = small.reshape(B, S, LANES)
    gate_rows = rows_of(gate_rows)
    w2a2 = jnp.zeros((LANES, 2 * GROUP_W), F32)
    w2a2 = w2a2.at[0:RWKV_W_RANK, 0:GROUP_W].set(p['rwkv_w2'])
    w2a2 = w2a2.at[RWKV_W_RANK:, GROUP_W:].set(p['rwkv_a2'])
    vec = lambda t: t.reshape(1, GROUP_W)
    v0 = v2 = None
    if layer > 0:
        v0 = vec(p['rwkv_v0'])
        v2 = _bf(jnp.pad(p['rwkv_v2'], ((0, LANES - RWKV_V_RANK), (0, 0))))
    d_rwkv = _rwkv(layer, slab, v_first, mu_p, vec(p['rwkv_w0']), vec(p['rwkv_a0']), _bf(w2a2),
                            v0, v2, vec(p['rwkv_k_k']), vec(p['rwkv_k_a']), vec(p['rwkv_r_k']),
                            vec(p['rwkv_ln_g']), vec(p['rwkv_ln_b']), _bf(tri), _bf(consts['bd']),
                            consts['strict'], consts['incl'], _bf(consts['lvl']))

    slab = _in_proj(x2, g, w_t, [(off('gla_q'), 1024, 0), (off('gla_z'), GROUP_W, 1024)], GLA_SLAB)
    w2p = _bf(jnp.pad(p['gla_gk_w2'], ((SMALL_GK, LANES - SMALL_GK - GLA_RANK), (0, 0))))
    d_gla = _gla(slab.reshape(B, S, GLA_SLAB), small, w2p, p['gla_gk_b'].reshape(1, -1),
                 p['gla_norm_g'].reshape(1, -1), _bf(tri))

    slab = _in_proj(x2, g, w_t, [(off('ssd_xbc'), SSD_XBC, 0), (off('ssd_z'), GROUP_W, SSD_XBC)], SSD_SLAB)
    a_neg = -jnp.exp(p['ssd_a_log'])
    d_ssd = _ssd(slab.reshape(B, S, SSD_SLAB), small, gate_rows, p['ssd_conv_w'],
                 p['ssd_conv_b'].reshape(1, -1),
                 _pad_vec(p['ssd_dt_bias'], LANES, SMALL_DT), _pad_vec(a_neg, LANES, SMALL_DT),
                 p['ssd_dt_bias'][even_odd].reshape(-1, 1), a_neg[even_odd].reshape(-1, 1),
                 jnp.repeat(p['ssd_d'], SSD_HEADDIM).reshape(1, -1), p['ssd_norm_g'].reshape(1, -1),
                 _bf(tri), _bf(consts['triu2']), _bf(consts['ex']))

    slab = _in_proj(x2, g, w_t, [(off('mlstm_qk'), 3 * GROUP_W, 0), (off('mlstm_o'), 2 * GROUP_W, 3 * GROUP_W)],
                    MLSTM_SLAB)
    gb_col = jnp.concatenate([p['mlstm_ig_b'], p['mlstm_fg_b']]).reshape(-1, 1)
    d_ml = _mlstm(slab.reshape(B, S, MLSTM_SLAB), small, gate_rows, p['mlstm_conv_w'],
                  p['mlstm_conv_b'].reshape(1, -1), _pad_vec(p['mlstm_ig_b'], LANES, SMALL_I),
                  _pad_vec(p['mlstm_fg_b'], LANES, SMALL_F), gb_col, p['mlstm_norm_g'].reshape(1, -1),
                  _bf(tri), _bf(triu), _bf(consts['exi']), _bf(consts['exf']))

    T = B * S
    d_ml['start'], d_ssd['start'], d_gla['start'] = MIXER_STARTS
    o_rwkv, o_ml, o_ssd, o_gla = _run_mixers([d_rwkv, d_ml, d_ssd, d_gla], B, S, "mixers")
    if layer == 0:
        v_first = o_rwkv[1]
    ys = [y.reshape(T, GROUP_W) for y in (o_gla[0], o_rwkv[0], o_ssd[0], o_ml[0])]
    return _out_proj(x2, ys, p['w_out'], g_final), v_first


_PARAM_NAMES_0 = ['norm_g', 'w_in', 'w_out', 'gla_gk_w2', 'gla_gk_b', 'gla_norm_g', 'rwkv_mu', 'rwkv_w0',
                  'rwkv_w2', 'rwkv_a0', 'rwkv_a2', 'rwkv_k_k', 'rwkv_k_a', 'rwkv_r_k', 'rwkv_ln_g',
                  'rwkv_ln_b', 'ssd_conv_w', 'ssd_conv_b', 'ssd_dt_bias', 'ssd_a_log', 'ssd_d',
                  'ssd_norm_g', 'mlstm_conv_w', 'mlstm_conv_b', 'mlstm_ig_b', 'mlstm_fg_b', 'mlstm_norm_g']
_PARAM_NAMES_1 = (_PARAM_NAMES_0[:11] + ['rwkv_v0', 'rwkv_v2'] + _PARAM_NAMES_0[11:])


def kernel(x,
           norm_g_0, w_in_0, w_out_0, gla_gk_w2_0, gla_gk_b_0, gla_norm_g_0,
           rwkv_mu_0, rwkv_w0_0, rwkv_w2_0, rwkv_a0_0, rwkv_a2_0,
           rwkv_k_k_0, rwkv_k_a_0, rwkv_r_k_0, rwkv_ln_g_0, rwkv_ln_b_0,
           ssd_conv_w_0, ssd_conv_b_0, ssd_dt_bias_0, ssd_a_log_0, ssd_d_0, ssd_norm_g_0,
           mlstm_conv_w_0, mlstm_conv_b_0, mlstm_ig_b_0, mlstm_fg_b_0, mlstm_norm_g_0,
           norm_g_1, w_in_1, w_out_1, gla_gk_w2_1, gla_gk_b_1, gla_norm_g_1,
           rwkv_mu_1, rwkv_w0_1, rwkv_w2_1, rwkv_a0_1, rwkv_a2_1, rwkv_v0_1, rwkv_v2_1,
           rwkv_k_k_1, rwkv_k_a_1, rwkv_r_k_1, rwkv_ln_g_1, rwkv_ln_b_1,
           ssd_conv_w_1, ssd_conv_b_1, ssd_dt_bias_1, ssd_a_log_1, ssd_d_1, ssd_norm_g_1,
           mlstm_conv_w_1, mlstm_conv_b_1, mlstm_ig_b_1, mlstm_fg_b_1, mlstm_norm_g_1,
           final_norm_g):
    params = (norm_g_0, w_in_0, w_out_0, gla_gk_w2_0, gla_gk_b_0, gla_norm_g_0,
              rwkv_mu_0, rwkv_w0_0, rwkv_w2_0, rwkv_a0_0, rwkv_a2_0,
              rwkv_k_k_0, rwkv_k_a_0, rwkv_r_k_0, rwkv_ln_g_0, rwkv_ln_b_0,
              ssd_conv_w_0, ssd_conv_b_0, ssd_dt_bias_0, ssd_a_log_0, ssd_d_0, ssd_norm_g_0,
              mlstm_conv_w_0, mlstm_conv_b_0, mlstm_ig_b_0, mlstm_fg_b_0, mlstm_norm_g_0,
              norm_g_1, w_in_1, w_out_1, gla_gk_w2_1, gla_gk_b_1, gla_norm_g_1,
              rwkv_mu_1, rwkv_w0_1, rwkv_w2_1, rwkv_a0_1, rwkv_a2_1, rwkv_v0_1, rwkv_v2_1,
              rwkv_k_k_1, rwkv_k_a_1, rwkv_r_k_1, rwkv_ln_g_1, rwkv_ln_b_1,
              ssd_conv_w_1, ssd_conv_b_1, ssd_dt_bias_1, ssd_a_log_1, ssd_d_1, ssd_norm_g_1,
              mlstm_conv_w_1, mlstm_conv_b_1, mlstm_ig_b_1, mlstm_fg_b_1, mlstm_norm_g_1,
              final_norm_g)
    n0 = len(_PARAM_NAMES_0)
    n1 = len(_PARAM_NAMES_1)
    p0 = dict(zip(_PARAM_NAMES_0, params[:n0]))
    p1 = dict(zip(_PARAM_NAMES_1, params[n0:n0 + n1]))
    final_norm_g = params[n0 + n1]
    B, S, _ = x.shape
    consts = {k: jnp.asarray(v) for k, v in _np_consts().items()}
    x2 = x.reshape(B * S, D_MODEL)
    x2, v_first = _layer(x2, None, 0, p0, B, S, consts, None)
    x2, _ = _layer(x2, v_first, 1, p1, B, S, consts, final_norm_g.reshape(1, D_MODEL))
    return x2.reshape(B, S, D_MODEL)
```

```python
import functools

import numpy as np
import jax
import jax.numpy as jnp
from jax import lax
from jax.experimental import pallas as pl
from jax.experimental.pallas import tpu as pltpu

F32 = jnp.float32
BF16 = jnp.bfloat16

D_MODEL = 2048
CHUNK = 64
GROUP_W = 512
NORM_EPS = 1e-6
LANES = 128
CARRY_ROWS = 8

GLA_HEADS, GLA_DK, GLA_DV, GLA_RANK = 4, 64, 128, 16
GLA_GATE_NORMALIZER = 16.0
RWKV_HEAD, RWKV_W_RANK, RWKV_A_RANK, RWKV_V_RANK = 64, 64, 64, 32
RWKV_LN_EPS = 64e-5
RWKV_DECAY_OFFSET = 0.5
SSD_HEADS, SSD_HEADDIM, SSD_STATE, SSD_CONV = 8, 64, 128, 4
SSD_XBC = 1024
MLSTM_HEADS, MLSTM_HEAD, MLSTM_CONV = 4, 128, 4

ROW_TILES_IN = (512, 1024)
SMALL_GK, SMALL_DT, SMALL_I, SMALL_F = 0, 16, 24, 28
ROWS_DT, ROWS_IF, ROWS_TOTAL = 0, 8, 16
ROW_TILE_OUT = 512
VMEM_LIMIT_PROJ = 48 * 2**20
VMEM_LIMIT_IN_PROJ = 56 * 2**20
VMEM_BUDGET_PROJ = 52 * 2**20
WEIGHT_CAST_ROWS = 256


def _bf(x):
    return x.astype(BF16)


def _mm(a, b):
    return jnp.dot(_bf(a), _bf(b), preferred_element_type=F32)


def _mm_nt(a, b):
    return lax.dot_general(_bf(a), _bf(b), (((1,), (1,)), ((), ())), preferred_element_type=F32)


def _mm_tn(a, b):
    return lax.dot_general(_bf(a), _bf(b), (((0,), (0,)), ((), ())), preferred_element_type=F32)


def _split3(x):
    hi = _bf(x)
    r1 = x - hi.astype(F32)
    mid = _bf(r1)
    lo = _bf(r1 - mid.astype(F32))
    return hi, mid, lo


def _mm_sel_rhs(x, sel):
    hi, mid, lo = _split3(x)
    d = lambda a: jnp.dot(a, sel, preferred_element_type=F32)
    return d(hi) + d(mid) + d(lo)


def _mm_sel2(x, sel):
    hi = _bf(x)
    lo = _bf(x - hi.astype(F32))
    d = lambda a: jnp.dot(a, sel, preferred_element_type=F32)
    return d(hi) + d(lo)


def _mm_sel_lhs(sel, x):
    hi, mid, lo = _split3(x)
    d = lambda a: jnp.dot(sel, a, preferred_element_type=F32)
    return d(hi) + d(mid) + d(lo)


def _rowsum(x):
    ones = jnp.ones((x.shape[-1], LANES), BF16)
    hi = _bf(x)
    lo = _bf(x - hi.astype(F32))
    return (jnp.dot(hi, ones, preferred_element_type=F32)
            + jnp.dot(lo, ones, preferred_element_type=F32))


def _sigmoid(x):
    return 1.0 / (1.0 + jnp.exp(-x))


def _silu(x):
    return x * _sigmoid(x)


def _softplus(x):
    return jnp.maximum(x, 0.0) + jnp.log1p(jnp.exp(-jnp.abs(x)))


def _log_sigmoid(x):
    return -_softplus(-x)


def _lane_half_masks():
    lane = lax.broadcasted_iota(jnp.int32, (1, LANES), 1)
    lo = (lane < LANES // 2).astype(F32)
    return lo, 1.0 - lo


def _causal(n):
    r = lax.broadcasted_iota(jnp.int32, (n, n), 0)
    c = lax.broadcasted_iota(jnp.int32, (n, n), 1)
    return c <= r


def _shifted_rows(buf_ref, cur, offsets):
    buf_ref[CARRY_ROWS:CARRY_ROWS + CHUNK, :] = cur
    outs = [buf_ref[CARRY_ROWS - off:CARRY_ROWS - off + CHUNK, :] for off in offsets]
    tail = buf_ref[CHUNK:CHUNK + CARRY_ROWS, :]
    buf_ref[0:CARRY_ROWS, :] = tail
    return outs


def _in_proj_kernel(*refs, pieces, zero_lanes, has_small):
    x_ref, g_ref = refs[0], refs[1]
    n = len(pieces)
    wf_refs = refs[2:2 + n]
    w_refs = refs[len(refs) - n:]
    rest = refs[2 + n:len(refs) - n]
    if has_small:
        ws_ref, wr_ref, o_ref, os_ref, or_ref = rest
    else:
        (o_ref,) = rest

    @pl.when(pl.program_id(0) == 0)
    def _():
        for wf_ref, w_ref in zip(wf_refs, w_refs):
            for r0 in range(0, w_ref.shape[0], WEIGHT_CAST_ROWS):
                r1 = min(r0 + WEIGHT_CAST_ROWS, w_ref.shape[0])
                w_ref[r0:r1, :] = _bf(wf_ref[r0:r1, :])

    nt = lambda a, b: lax.dot_general(a, b, (((1,), (1,)), ((), ())), preferred_element_type=F32)
    tm = x_ref.shape[0]
    for lo, hi in zero_lanes:
        o_ref[:, lo:hi] = jnp.zeros((tm, hi - lo), F32)
    for half in range(2):
        rs = slice(half * (tm // 2), (half + 1) * (tm // 2))
        x = x_ref[rs, :]
        h = x * lax.rsqrt(jnp.mean(x * x, axis=-1, keepdims=True) + NORM_EPS) * g_ref[...]
        hb = _bf(h)
        for w_ref, (_, n_rows, lane_off) in zip(w_refs, pieces):
            o_ref[rs, lane_off:lane_off + n_rows] = nt(hb, w_ref[...])
        if has_small:
            os_ref[rs, :] = nt(hb, _bf(ws_ref[...]))
            or_ref[:, rs] = nt(_bf(wr_ref[...]), hb)


def _in_proj(x2, g, w_t, pieces, width, zero_lanes=(), w_small=None, w_rows=None):
    T = x2.shape[0]
    w_rows_total = sum(n_rows for _, n_rows, _ in pieces)
    tm = max(t for t in ROW_TILES_IN
             if (w_rows_total * D_MODEL * (4 + 2)
                 + 2 * (t * D_MODEL * 4 + t * (width + LANES) * 4)) <= VMEM_BUDGET_PROJ)
    const2 = lambda shape: pl.BlockSpec(shape, lambda i: (0, 0))
    in_specs = [pl.BlockSpec((tm, D_MODEL), lambda i: (i, 0)), const2((1, D_MODEL))]
    args = [x2, g]
    for off, n_rows, _ in pieces:
        in_specs.append(pl.BlockSpec((pl.Element(n_rows), pl.Element(D_MODEL)), lambda i, off=off: (off, 0),
                                     pipeline_mode=pl.Buffered(1)))
        args.append(w_t)
    scratch = [pltpu.VMEM((n_rows, D_MODEL), BF16) for _, n_rows, _ in pieces]
    out_shape = [jax.ShapeDtypeStruct((T, width), F32)]
    out_specs = [pl.BlockSpec((tm, width), lambda i: (i, 0))]
    if w_small is not None:
        r = w_rows.shape[0]
        in_specs += [const2((LANES, D_MODEL)), const2((r, D_MODEL))]
        args += [w_small, w_rows]
        out_shape += [jax.ShapeDtypeStruct((T, LANES), F32), jax.ShapeDtypeStruct((r, T), F32)]
        out_specs += [pl.BlockSpec((tm, LANES), lambda i: (i, 0)), pl.BlockSpec((r, tm), lambda i: (0, i))]
    res = pl.pallas_call(
        functools.partial(_in_proj_kernel, pieces=tuple(pieces), zero_lanes=tuple(zero_lanes),
                          has_small=w_small is not None),
        grid=(T // tm,), in_specs=in_specs, out_specs=out_specs, out_shape=out_shape,
        scratch_shapes=scratch,
        compiler_params=pltpu.CompilerParams(dimension_semantics=("arbitrary",),
                                             vmem_limit_bytes=VMEM_LIMIT_IN_PROJ),
        name="in_proj")(*args)
    return res if w_small is not None else res[0]


def _out_proj_kernel(*refs, final):
    if final:
        x_ref, y0, y1, y2, y3, wf_ref, g_ref, o_ref, w_ref = refs
    else:
        x_ref, y0, y1, y2, y3, wf_ref, o_ref, w_ref = refs

    @pl.when(pl.program_id(0) == 0)
    def _():
        for r0 in range(0, D_MODEL, WEIGHT_CAST_ROWS):
            w_ref[r0:r0 + WEIGHT_CAST_ROWS, :] = _bf(wf_ref[r0:r0 + WEIGHT_CAST_ROWS, :])

    acc = x_ref[...]
    for gi, y in enumerate((y0, y1, y2, y3)):
        acc = acc + jnp.dot(_bf(y[...]), w_ref[gi * GROUP_W:(gi + 1) * GROUP_W, :],
                            preferred_element_type=F32)
    if final:
        acc = acc * lax.rsqrt(jnp.mean(acc * acc, axis=-1, keepdims=True) + NORM_EPS) * g_ref[...]
    o_ref[...] = acc


def _out_proj(x2, ys, w, g_final=None):
    T = x2.shape[0]
    tm = ROW_TILE_OUT
    final = g_final is not None
    in_specs = [pl.BlockSpec((tm, D_MODEL), lambda i: (i, 0))]
    in_specs += [pl.BlockSpec((tm, GROUP_W), lambda i: (i, 0)) for _ in range(4)]
    in_specs += [pl.BlockSpec((D_MODEL, D_MODEL), lambda i: (0, 0), pipeline_mode=pl.Buffered(1))]
    args = [x2, *ys, w]
    if final:
        in_specs.append(pl.BlockSpec((1, D_MODEL), lambda i: (0, 0)))
        args.append(g_final)
    return pl.pallas_call(
        functools.partial(_out_proj_kernel, final=final),
        grid=(T // tm,), in_specs=in_specs,
        out_specs=pl.BlockSpec((tm, D_MODEL), lambda i: (i, 0)),
        out_shape=jax.ShapeDtypeStruct((T, D_MODEL), F32),
        scratch_shapes=[pltpu.VMEM((D_MODEL, D_MODEL), BF16)],
        compiler_params=pltpu.CompilerParams(dimension_semantics=("arbitrary",),
                                             vmem_limit_bytes=VMEM_LIMIT_IN_PROJ),
        name="out_proj")(*args)


GLA_SLAB = 1536


BATCH_BLOCK = 4
MIXER_STARTS = (5, 5, 12)


def _lockstep(gens, starts=None):
    gens = list(gens)
    starts = [0] * len(gens) if starts is None else list(starts)
    pending = list(zip(starts, gens))
    tick = 0
    while pending:
        alive = []
        for start, g in pending:
            if start > tick:
                alive.append((start, g))
                continue
            try:
                next(g)
                alive.append((start, g))
            except StopIteration:
                pass
        pending = alive
        tick += 1


def _stepper(gens):
    gens = list(gens)
    while gens:
        alive = []
        for g in gens:
            try:
                next(g)
                alive.append(g)
            except StopIteration:
                pass
        gens = alive
        yield


def _mixers_kernel(*refs, parts):
    n_in = sum(p[1] for p in parts)
    n_out = sum(p[2] for p in parts)
    ins, outs, scs = refs[:n_in], refs[n_in:n_in + n_out], refs[n_in + n_out:]
    gens, starts, posts = [], [], []
    i = o = s = 0
    for stages_fn, ni, no, ns, start in parts:
        g, post = stages_fn(*ins[i:i + ni], *outs[o:o + no], *scs[s:s + ns])
        i, o, s = i + ni, o + no, s + ns
        gens += g
        starts += [start] * len(g)
        posts.append(post)
    _lockstep(gens, starts)
    for post in posts:
        if post is not None:
            post()


def _run_mixers(descs, B, S, name):
    nb = BATCH_BLOCK
    parts = tuple((d['stages'], len(d['args']), len(d['out_shape']), len(d['scratch']), d.get('start', 0))
                  for d in descs)
    res = pl.pallas_call(
        functools.partial(_mixers_kernel, parts=parts), grid=(B // nb, S // CHUNK),
        in_specs=[sp for d in descs for sp in d['in_specs']],
        out_specs=[sp for d in descs for sp in d['out_specs']],
        out_shape=[sh for d in descs for sh in d['out_shape']],
        scratch_shapes=[sc for d in descs for sc in d['scratch']],
        compiler_params=pltpu.CompilerParams(dimension_semantics=("arbitrary", "arbitrary"),
                                             vmem_limit_bytes=VMEM_LIMIT_PROJ),
        name=name)(*[a for d in descs for a in d['args']])
    out, k = [], 0
    for d in descs:
        out.append(res[k:k + len(d['out_shape'])])
        k += len(d['out_shape'])
    return out


def _gla_stages(slab_ref, small_ref, w2_ref, gkb_ref, ng_ref, tri_ref, o_ref, st_ref, *, nb):
    @pl.when(pl.program_id(1) == 0)
    def _():
        st_ref[...] = jnp.zeros_like(st_ref)

    causal = _causal(CHUNK)
    masks = _lane_half_masks()

    def head(b, h, qg, kg, kd, dec):
        p, j = divmod(h, 2)
        ls = slice(p * LANES, (p + 1) * LANES)
        qm = qg[:, ls] * masks[j]
        att = jnp.where(causal, _mm_nt(qm, kg[:, ls]), 0.0)
        yield
        v_h = slab_ref[b, :, 512 + h * GLA_DV:512 + (h + 1) * GLA_DV]
        st = st_ref[b, h]
        o = _mm(att, v_h) + _mm_nt(qm, st)
        st_ref[b, h] = st * dec[:, ls] + _mm_tn(v_h, kd[:, ls] * masks[j])
        yield
        ms = jnp.mean(o * o, axis=-1, keepdims=True)
        yield
        o = o * lax.rsqrt(ms + NORM_EPS)
        o = o * ng_ref[:, h * GLA_DV:(h + 1) * GLA_DV]
        z_h = slab_ref[b, :, 1024 + h * GLA_DV:1024 + (h + 1) * GLA_DV]
        o_ref[b, :, h * GLA_DV:(h + 1) * GLA_DV] = o * _silu(z_h)

    def sequence(b):
        q = slab_ref[b, :, 0:256] * (GLA_DK ** -0.5)
        k = slab_ref[b, :, 256:512]
        gk = _mm(small_ref[b], w2_ref[...]) + gkb_ref[...]
        log_a = _log_sigmoid(gk) / GLA_GATE_NORMALIZER
        cum = _mm_sel_lhs(tri_ref[...], log_a)
        last = cum[CHUNK - 1:CHUNK, :]
        qg = q * jnp.exp(cum)
        kg = k * jnp.exp(-cum)
        kd = k * jnp.exp(last - cum)
        dec = jnp.exp(last)
        yield
        yield from _stepper([head(b, h, qg, kg, kd, dec) for h in range(GLA_HEADS)])

    return [sequence(b) for b in range(nb)], None


def _gla(slab, small, w2p, gkb, ng, tri):
    B, S, _ = slab.shape
    nb = BATCH_BLOCK
    const = lambda shape: pl.BlockSpec(shape, lambda b, c: (0,) * len(shape))
    return dict(
        stages=functools.partial(_gla_stages, nb=nb),
        in_specs=[pl.BlockSpec((nb, CHUNK, GLA_SLAB), lambda b, c: (b, c, 0)),
                  pl.BlockSpec((nb, CHUNK, LANES), lambda b, c: (b, c, 0)),
                  const((LANES, 256)), const((1, 256)), const((1, GROUP_W)), const((CHUNK, CHUNK))],
        args=[slab, small, w2p, gkb, ng, tri],
        out_specs=[pl.BlockSpec((nb, CHUNK, GROUP_W), lambda b, c: (b, c, 0))],
        out_shape=[jax.ShapeDtypeStruct((B, S, GROUP_W), F32)],
        scratch=[pltpu.VMEM((nb, GLA_HEADS, GLA_DV, LANES), F32)])


SSD_SLAB = 1536


def _ssd_stages(slab_ref, small_ref, rows_ref, cw_ref, cb_ref, dtb_ref, a_ref, dtb_col_ref, a_col_ref,
                dskip_ref, ng_ref, tri_ref, triu2_ref, ex_ref, o_ref, buf_ref, st_ref, *, nb):
    @pl.when(pl.program_id(1) == 0)
    def _():
        st_ref[...] = jnp.zeros_like(st_ref)
        for b in range(nb):
            buf_ref[b, 0:CARRY_ROWS, :] = jnp.zeros((CARRY_ROWS, SSD_XBC), F32)

    row_i = lax.broadcasted_iota(jnp.int32, (CHUNK, LANES), 0)
    col_i = lax.broadcasted_iota(jnp.int32, (CHUNK, LANES), 1)
    causal2 = jnp.bitwise_and(col_i, CHUNK - 1) <= row_i
    masks = _lane_half_masks()
    masks_b = (_bf(masks[0]), _bf(masks[1]))

    def pair(b, p, xbc, cum_b, cum_row2, xdt, xw, ecum, dec):
        g = p // 2
        ls = slice(p * LANES, (p + 1) * LANES)
        bm = xbc[:, 512 + g * SSD_STATE:512 + (g + 1) * SSD_STATE]
        cm = xbc[:, 768 + g * SSD_STATE:768 + (g + 1) * SSD_STATE]
        bmb = _bf(bm)
        cbm2 = _mm_nt(cm, jnp.concatenate([bmb, bmb], axis=0))
        st = st_ref[b, p]
        y = _mm(cm, st) * ecum[:, ls]
        st_ref[b, p] = st * dec[:, ls] + _mm_tn(bmb, xw[:, ls])
        lmat = jnp.exp(jnp.where(causal2, cum_b[:, ls] - cum_row2[p:p + 1, :], -jnp.inf))
        yield
        xb = _bf(xdt[:, ls])
        xs = jnp.concatenate([xb * masks_b[0], xb * masks_b[1]], axis=0)
        y = y + _mm(cbm2 * lmat, xs)
        yield
        y = y + dskip_ref[:, ls] * xbc[:, ls]
        o_ref[b, :, ls] = y * _silu(slab_ref[b, :, 1024 + p * LANES:1024 + (p + 1) * LANES])

    def sequence(b):
        taps = _shifted_rows(buf_ref.at[b], slab_ref[b, :, 0:SSD_XBC], (3, 2, 1, 0))
        xbc = cb_ref[...]
        for j in range(SSD_CONV):
            xbc = xbc + taps[j] * cw_ref[j:j + 1, :]
        xbc = _silu(xbc)
        yield
        dt_col = _softplus(small_ref[b] + dtb_ref[...])
        cum_col = _mm_sel_lhs(tri_ref[...], dt_col * a_ref[...])
        dt_row = _softplus(rows_ref[b, 0, ROWS_DT:ROWS_DT + SSD_HEADS, :] + dtb_col_ref[...])
        da_row = dt_row * a_col_ref[...]
        cum_row2 = (_mm_sel_rhs(da_row[0:SSD_HEADS // 2], triu2_ref[0])
                    + _mm_sel_rhs(da_row[SSD_HEADS // 2:SSD_HEADS], triu2_ref[1]))
        both_b = _mm_sel_rhs(jnp.concatenate([dt_col, cum_col], axis=0), ex_ref[...])
        dt_b, cum_b = both_b[0:CHUNK], both_b[CHUNK:2 * CHUNK]
        last_b = cum_b[CHUNK - 1:CHUNK, :]
        xdt = xbc[:, 0:512] * dt_b
        xw = xdt * jnp.exp(last_b - cum_b)
        ecum = jnp.exp(cum_b)
        dec = jnp.exp(last_b)
        yield
        yield from _stepper([pair(b, p, xbc, cum_b, cum_row2, xdt, xw, ecum, dec)
                             for p in range(SSD_HEADS // 2)])

    gens = [sequence(b) for b in range(nb)]

    def group_norm():
        ys = [o_ref[b] for b in range(nb)]
        ms = _rowsum(jnp.concatenate([y * y for y in ys], axis=0))
        for b, y in enumerate(ys):
            inv = lax.rsqrt(ms[b * CHUNK:(b + 1) * CHUNK] * (1.0 / GROUP_W) + NORM_EPS)
            for p in range(GROUP_W // LANES):
                ls = slice(p * LANES, (p + 1) * LANES)
                o_ref[b, :, ls] = y[:, ls] * inv * ng_ref[:, ls]

    return gens, group_norm


def _ssd(slab, small, rows, cw, cb, dtb, a, dtb_col, a_col, dskip, ng, tri, triu, ex):
    B, S, _ = slab.shape
    nb = BATCH_BLOCK
    const = lambda shape: pl.BlockSpec(shape, lambda b, c: (0,) * len(shape))
    return dict(
        stages=functools.partial(_ssd_stages, nb=nb),
        in_specs=[pl.BlockSpec((nb, CHUNK, SSD_SLAB), lambda b, c: (b, c, 0)),
                  pl.BlockSpec((nb, CHUNK, LANES), lambda b, c: (b, c, 0)),
                  pl.BlockSpec((nb, 1, ROWS_TOTAL, CHUNK), lambda b, c: (b, c, 0, 0)),
                  const((SSD_CONV, SSD_XBC)), const((1, SSD_XBC)), const((1, LANES)), const((1, LANES)),
                  const((SSD_HEADS, 1)), const((SSD_HEADS, 1)), const((1, GROUP_W)), const((1, GROUP_W)),
                  const((CHUNK, CHUNK)), const((2, CHUNK, LANES)), const((LANES, GROUP_W))],
        args=[slab, small, rows, cw, cb, dtb, a, dtb_col, a_col, dskip, ng, tri, triu, ex],
        out_specs=[pl.BlockSpec((nb, CHUNK, GROUP_W), lambda b, c: (b, c, 0))],
        out_shape=[jax.ShapeDtypeStruct((B, S, GROUP_W), F32)],
        scratch=[pltpu.VMEM((nb, CARRY_ROWS + CHUNK, SSD_XBC), F32),
                 pltpu.VMEM((nb, SSD_HEADS // 2, SSD_STATE, LANES), F32)])


MLSTM_SLAB = 2560


def _mlstm_stages(slab_ref, small_ref, rows_ref, cw_ref, cb_ref, igb_ref, fgb_ref, gb_col_ref, ng_ref,
                  tri_ref, triu_ref, exi_ref, exf_ref, o_ref, buf_ref, c_ref, nm_ref, *, nb):
    @pl.when(pl.program_id(1) == 0)
    def _():
        c_ref[...] = jnp.zeros_like(c_ref)
        nm_ref[...] = jnp.zeros_like(nm_ref)
        for b in range(nb):
            buf_ref[b, 0:CARRY_ROWS, :] = jnp.zeros((CARRY_ROWS, 2 * GROUP_W), F32)

    causal = _causal(CHUNK)

    def head(b, h, qk, li_b, ci_b, logi_row, cum_row):
        ls = slice(h * MLSTM_HEAD, (h + 1) * MLSTM_HEAD)
        q = qk[:, ls]
        k = qk[:, GROUP_W + h * MLSTM_HEAD:GROUP_W + (h + 1) * MLSTM_HEAD] * (MLSTM_HEAD ** -0.5)
        v = slab_ref[b, :, 1024 + h * MLSTM_HEAD:1024 + (h + 1) * MLSTM_HEAD]
        ci = ci_b[:, ls]
        li = li_b[:, ls]
        cr = cum_row[h:h + 1, :]
        lir = logi_row[h:h + 1, :]
        last = ci[CHUNK - 1:CHUNK, :]
        c_prev = c_ref[b, h]
        n_prev = nm_ref[b, h, 0:1, :]
        m_prev = nm_ref[b, h, 1:2, :]

        g = last - ci + li
        g_max = jnp.max(g, axis=0, keepdims=True)
        log_d = jnp.where(causal, ci[:, 0:CHUNK] - cr + lir, -jnp.inf)
        row_max = jnp.max(log_d, axis=-1, keepdims=True)
        qk_h = _mm_nt(q, k)
        qc = _mm(q, c_prev)
        qn = _rowsum(q * n_prev)
        yield
        kw = k * jnp.exp(g - g_max)
        c_loc = _mm_tn(kw, v)
        n_loc = jnp.sum(kw, axis=0, keepdims=True)
        m_new = jnp.maximum(last + m_prev, g_max)
        a_old = jnp.exp(last + m_prev - m_new)
        a_new = jnp.exp(g_max - m_new)
        c_ref[b, h] = a_old * c_prev + a_new * c_loc
        nm_ref[b, h, 0:1, :] = a_old * n_prev + a_new * n_loc
        nm_ref[b, h, 1:2, :] = m_new
        m_inter = ci + m_prev
        m_l = jnp.maximum(m_inter, row_max)
        wqk = qk_h * jnp.exp(log_d - m_l[:, 0:CHUNK])
        w_inter = jnp.exp(m_inter - m_l)
        num = _mm(wqk, v) + w_inter * qc
        den = _rowsum(wqk) + w_inter * qn
        yield
        den = jnp.maximum(jnp.abs(den), jnp.exp(-m_l))
        hh = num / den * _sigmoid(slab_ref[b, :, 1536 + h * MLSTM_HEAD:1536 + (h + 1) * MLSTM_HEAD])
        mu = _rowsum(hh) * (1.0 / MLSTM_HEAD)
        yield
        yc = hh - mu
        var = _rowsum(yc * yc) * (1.0 / MLSTM_HEAD)
        yield
        hh = yc * lax.rsqrt(var + NORM_EPS) * ng_ref[:, ls]
        o_ref[b, :, ls] = hh * _silu(slab_ref[b, :, 2048 + h * MLSTM_HEAD:2048 + (h + 1) * MLSTM_HEAD])

    def sequence(b):
        taps = _shifted_rows(buf_ref.at[b], slab_ref[b, :, 0:2 * GROUP_W], (3, 2, 1, 0))
        qk = cb_ref[...]
        for j in range(MLSTM_CONV):
            qk = qk + taps[j] * cw_ref[j:j + 1, :]
        qk = _silu(qk)
        yield
        logi_col = small_ref[b] + igb_ref[...]
        logf_col = _log_sigmoid(small_ref[b] + fgb_ref[...])
        cum_col = _mm_sel_lhs(tri_ref[...], logf_col)
        li_b = _mm_sel_rhs(logi_col, exi_ref[...])
        ci_b = _mm_sel_rhs(cum_col, exf_ref[...])
        pre_row = rows_ref[b, 0, ROWS_IF:ROWS_IF + 2 * MLSTM_HEADS, :] + gb_col_ref[...]
        logi_row = pre_row[0:MLSTM_HEADS, :]
        logf_row = _log_sigmoid(pre_row[MLSTM_HEADS:2 * MLSTM_HEADS, :])
        cum_row = _mm_sel_rhs(logf_row, triu_ref[...])
        yield
        yield from _stepper([head(b, h, qk, li_b, ci_b, logi_row, cum_row) for h in range(MLSTM_HEADS)])

    return [sequence(b) for b in range(nb)], None


def _mlstm(slab, small, rows, cw, cb, igb, fgb, gb_col, ng, tri, triu, exi, exf):
    B, S, _ = slab.shape
    nb = BATCH_BLOCK
    const = lambda shape: pl.BlockSpec(shape, lambda b, c: (0,) * len(shape))
    return dict(
        stages=functools.partial(_mlstm_stages, nb=nb),
        in_specs=[pl.BlockSpec((nb, CHUNK, MLSTM_SLAB), lambda b, c: (b, c, 0)),
                  pl.BlockSpec((nb, CHUNK, LANES), lambda b, c: (b, c, 0)),
                  pl.BlockSpec((nb, 1, ROWS_TOTAL, CHUNK), lambda b, c: (b, c, 0, 0)),
                  const((MLSTM_CONV, 2 * GROUP_W)), const((1, 2 * GROUP_W)), const((1, LANES)),
                  const((1, LANES)), const((2 * MLSTM_HEADS, 1)), const((1, GROUP_W)),
                  const((CHUNK, CHUNK)), const((CHUNK, CHUNK)),
                  const((LANES, GROUP_W)), const((LANES, GROUP_W))],
        args=[slab, small, rows, cw, cb, igb, fgb, gb_col, ng, tri, triu, exi, exf],
        out_specs=[pl.BlockSpec((nb, CHUNK, GROUP_W), lambda b, c: (b, c, 0))],
        out_shape=[jax.ShapeDtypeStruct((B, S, GROUP_W), F32)],
        scratch=[pltpu.VMEM((nb, CARRY_ROWS + CHUNK, 2 * GROUP_W), F32),
                 pltpu.VMEM((nb, MLSTM_HEADS, MLSTM_HEAD, MLSTM_HEAD), F32),
                 pltpu.VMEM((nb, MLSTM_HEADS, CARRY_ROWS, MLSTM_HEAD), F32)])


def _rwkv_slab_width(layer):
    return 2176 if layer == 0 else 2304


def _rwkv_shift_cols(layer):
    return 1664 if layer == 0 else 1792


def _rwkv_stages(*refs, layer, nb):
    if layer == 0:
        (slab_ref, mu_ref, w0_ref, a0_ref, w2a2_ref, kk_ref, ka_ref, rk_ref, lng_ref, lnb_ref,
         tri_ref, bd_ref, strict_ref, incl_ref, lvl_ref, o_ref, vf_out_ref, buf_ref, st_ref) = refs
    else:
        (slab_ref, vf_ref, mu_ref, w0_ref, a0_ref, w2a2_ref, v0_ref, v2_ref, kk_ref, ka_ref, rk_ref,
         lng_ref, lnb_ref, tri_ref, bd_ref, strict_ref, incl_ref, lvl_ref, o_ref, buf_ref, st_ref) = refs
    ws = _rwkv_shift_cols(layer)
    zoff = _rwkv_slab_width(layer) - GROUP_W
    npair = GROUP_W // LANES

    @pl.when(pl.program_id(1) == 0)
    def _():
        st_ref[...] = jnp.zeros_like(st_ref)
        for b in range(nb):
            buf_ref[b, 0:CARRY_ROWS, :] = jnp.zeros((CARRY_ROWS, ws), F32)

    lane = lax.broadcasted_iota(jnp.int32, (1, LANES), 1)
    masks = _lane_half_masks()
    bd = bd_ref[...]
    strict = strict_ref[...] > 0.5
    incl = incl_ref[...] > 0.5
    masks_b = (_bf(masks[0]), _bf(masks[1]))

    def rows2(t):
        tb = _bf(t)
        return jnp.concatenate([tb * masks_b[0], tb * masks_b[1]], axis=0)

    eye = (lax.broadcasted_iota(jnp.int32, (LANES, LANES), 0)
           == lax.broadcasted_iota(jnp.int32, (LANES, LANES), 1)).astype(F32)

    inv = 1.0 / RWKV_HEAD

    def sequence(b):
        inst = []
        f = slab_ref[b, :, 0:ws]
        (prev,) = _shifted_rows(buf_ref.at[b], f, (1,))
        f = f + mu_ref[...] * (prev - f)
        r = f[:, 0:512]
        k = f[:, 512:1024]
        v = f[:, 1024:1536]
        lora = f[:, 1536:1664]
        lora = jnp.where(lane < RWKV_W_RANK, jnp.tanh(lora), lora)
        wa = _mm(lora, w2a2_ref[...])
        if layer == 0:
            vf_out_ref[b] = v
        else:
            mix = _sigmoid(v0_ref[...] + _mm(f[:, 1664:1792], v2_ref[...]))
            v = v + (vf_ref[b] - v) * mix
        yield
        w_log = -_softplus(-(w0_ref[...] + wa[:, 0:512])) - RWKV_DECAY_OFFSET
        lw = -jnp.exp(w_log)
        a = _sigmoid(a0_ref[...] + wa[:, 512:1024])
        kk = k * kk_ref[...]
        k = k * (1.0 + (a - 1.0) * ka_ref[...])
        cum = _mm_sel_lhs(tri_ref[...], lw)
        ss_all = _mm_sel2(jnp.concatenate([kk[:, p * LANES:(p + 1) * LANES] ** 2 for p in range(npair)],
                                          axis=0), bd)
        ss = [ss_all[p * CHUNK:(p + 1) * CHUNK] for p in range(npair)]
        yield
        last = cum[CHUNK - 1:CHUNK, :]
        e_pos = jnp.exp(cum)
        e_neg = jnp.exp(-cum)
        e_end = jnp.exp(last - cum)
        e_prev = jnp.exp(cum - lw)
        gam = jnp.exp(last)
        for p in range(npair):
            ls = slice(p * LANES, (p + 1) * LANES)
            kk_p = kk[:, ls] / jnp.maximum(jnp.sqrt(ss[p]), 1e-12)
            k_p, r_p, v_p = k[:, ls], r[:, ls], v[:, ls]
            b_p = kk_p * a[:, ls]
            inst.append(dict(
                p=p, ls=ls, r=r_p, k=k_p, v=v_p, gam=gam[:, ls],
                la=rows2(-kk_p * e_prev[:, ls]), lr=rows2(r_p * e_pos[:, ls]),
                rb=rows2(b_p * e_neg[:, ls]), rk=rows2(k_p * e_neg[:, ls]),
                bh=rows2(b_p * e_end[:, ls]), kh=rows2(k_p * e_end[:, ls]), vs=rows2(v_p)))
        yield
        for d in inst:
            aa = _mm_nt(jnp.concatenate([d['la'], d['lr']], axis=0),
                        jnp.concatenate([d['rb'], d['rk']], axis=0))
            d['nab'] = _bf(jnp.where(strict, aa[0:LANES, 0:LANES], 0.0))
            d['aak'] = _bf(jnp.where(strict, aa[0:LANES, LANES:2 * LANES], 0.0))
            d['arb'] = _bf(jnp.where(incl, aa[LANES:2 * LANES, 0:LANES], 0.0))
            d['ark'] = _bf(jnp.where(incl, aa[LANES:2 * LANES, LANES:2 * LANES], 0.0))
            d['t'] = eye + (d['nab'] * lvl_ref[0]).astype(F32)
        yield
        for d in inst:
            xv = _mm(jnp.concatenate([d['aak'], d['ark']], axis=0), d['vs'])
            d['x'], d['arkv'] = xv[0:LANES], xv[LANES:2 * LANES]
        for lv in range(1, 6):
            for d in inst:
                d['tb'] = _bf(d['t'])
                d['nt'] = _mm(d['nab'] * lvl_ref[lv], d['tb'])
            yield
            for d in inst:
                d['t'] = d['t'] + _mm(d['tb'], d['nt'])
            yield
        for d in inst:
            d['wu'] = _mm(d['t'], jnp.concatenate([d['la'], _bf(d['x'])], axis=1))
        yield
        for d in inst:
            d['wub'] = _bf(d['wu'])
            qy = _mm(d['arb'], d['wub'])
            d['qt'] = d['lr'].astype(F32) + qy[:, 0:LANES]
            d['y0'] = d['arkv'] + qy[:, LANES:2 * LANES]
        yield
        for d in inst:
            st = st_ref[b, d['p']]
            uy = _mm_nt(jnp.concatenate([d['wub'][:, 0:LANES], _bf(d['qt'])], axis=0), st)
            d['ust'] = uy[0:LANES] + d['wu'][:, LANES:2 * LANES]
            d['st'] = st
            yst = uy[LANES:2 * LANES] + d['y0']
            d['y'] = yst[0:CHUNK] + yst[CHUNK:2 * CHUNK]
        yield
        for d in inst:
            st_ref[b, d['p']] = d['st'] * d['gam'] + _mm_tn(jnp.concatenate([_bf(d['ust']), d['vs']], axis=0),
                                                            jnp.concatenate([d['bh'], d['kh']], axis=0))
        sums = _mm_sel2(jnp.concatenate([t for d in inst
                                         for t in (d['y'], d['r'] * d['k'] * rk_ref[:, d['ls']])], axis=0), bd)
        for i, d in enumerate(inst):
            d['mu'] = sums[2 * i * CHUNK:(2 * i + 1) * CHUNK] * inv
            d['bonus'] = sums[(2 * i + 1) * CHUNK:(2 * i + 2) * CHUNK] * d['v']
        yield
        for d in inst:
            d['yc'] = d['y'] - d['mu']
        sq = _mm_sel2(jnp.concatenate([d['yc'] * d['yc'] for d in inst], axis=0), bd)
        for i, d in enumerate(inst):
            d['var'] = sq[i * CHUNK:(i + 1) * CHUNK] * inv
        yield
        for d in inst:
            ls = d['ls']
            yn = d['yc'] * lax.rsqrt(d['var'] + RWKV_LN_EPS) * lng_ref[:, ls] + lnb_ref[:, ls]
            z_p = slab_ref[b, :, zoff + d['p'] * LANES:zoff + (d['p'] + 1) * LANES]
            o_ref[b, :, ls] = (yn + d['bonus']) * _silu(z_p)

    return [sequence(b) for b in range(nb)], None


def _rwkv(layer, slab, vf, mu, w0, a0, w2a2, v0, v2, kkw, ka, rk, lng, lnb, tri, bd, strict, incl, lvl):
    B, S, W = slab.shape
    nb = BATCH_BLOCK
    ws = _rwkv_shift_cols(layer)
    const = lambda shape: pl.BlockSpec(shape, lambda b, c: (0,) * len(shape))
    tok = lambda w: pl.BlockSpec((nb, CHUNK, w), lambda b, c: (b, c, 0))
    vecw = const((1, GROUP_W))
    in_specs = [tok(W)]
    args = [slab]
    if layer > 0:
        in_specs.append(tok(GROUP_W))
        args.append(vf)
    in_specs += [const((1, ws)), vecw, vecw, const((LANES, 2 * GROUP_W))]
    args += [mu, w0, a0, w2a2]
    if layer > 0:
        in_specs += [vecw, const((LANES, GROUP_W))]
        args += [v0, v2]
    in_specs += [vecw, vecw, vecw, vecw, vecw, const((CHUNK, CHUNK)), const((LANES, LANES)),
                 const((LANES, LANES)), const((LANES, LANES)), const((6, LANES, LANES))]
    args += [kkw, ka, rk, lng, lnb, tri, bd, strict, incl, lvl]
    out_shape = [jax.ShapeDtypeStruct((B, S, GROUP_W), F32)]
    out_specs = [tok(GROUP_W)]
    if layer == 0:
        out_shape.append(jax.ShapeDtypeStruct((B, S, GROUP_W), F32))
        out_specs.append(tok(GROUP_W))
    return dict(
        stages=functools.partial(_rwkv_stages, layer=layer, nb=nb),
        in_specs=in_specs, args=args, out_specs=out_specs, out_shape=out_shape,
        scratch=[pltpu.VMEM((nb, CARRY_ROWS + CHUNK, ws), F32),
                 pltpu.VMEM((nb, GROUP_W // LANES, LANES, LANES), F32)])


def _np_consts():
    i = np.arange(CHUNK)
    tri = (i[None, :] <= i[:, None]).astype(np.float32)
    t = np.arange(LANES)
    same = (t[:, None] // CHUNK) == (t[None, :] // CHUNK)
    strict = (same & (t[None, :] < t[:, None])).astype(np.float32)
    incl = (same & (t[None, :] <= t[:, None])).astype(np.float32)
    bd = same.astype(np.float32)
    lvl = np.stack([(((t[:, None] >> l) == (t[None, :] >> l))
                     & ((t[:, None] >> (l - 1)) != (t[None, :] >> (l - 1)))).astype(np.float32)
                    for l in range(1, 7)])
    ex = np.zeros((LANES, GROUP_W), np.float32)
    for h in range(SSD_HEADS):
        ex[SMALL_DT + h, h * SSD_HEADDIM:(h + 1) * SSD_HEADDIM] = 1.0
    exi = np.zeros((LANES, GROUP_W), np.float32)
    exf = np.zeros((LANES, GROUP_W), np.float32)
    for h in range(MLSTM_HEADS):
        exi[SMALL_I + h, h * MLSTM_HEAD:(h + 1) * MLSTM_HEAD] = 1.0
        exf[SMALL_F + h, h * MLSTM_HEAD:(h + 1) * MLSTM_HEAD] = 1.0
    triu = tri.T.copy()
    zero = np.zeros_like(triu)
    triu2 = np.stack([np.concatenate([triu, zero], axis=1), np.concatenate([zero, triu], axis=1)])
    return dict(tri=tri, triu=triu, triu2=triu2, strict=strict, incl=incl, bd=bd, lvl=lvl, ex=ex,
                exi=exi, exf=exf)


def _pad_rows(w, height):
    return jnp.pad(w, ((0, height - w.shape[0]), (0, 0)))


def _pad_vec(v, width, offset=0):
    v = v.reshape(1, -1)
    return jnp.pad(v, ((0, 0), (offset, width - offset - v.shape[1])))


F32_ROW_TILE = 8


def _layer(x2, v_first, layer, p, B, S, consts, g_final):
    sw = 3 * GROUP_W + RWKV_W_RANK + RWKV_A_RANK + (RWKV_V_RANK if layer > 0 else 0)
    names = ['gla_q', 'gla_k', 'gla_v', 'gla_gk', 'gla_z', 'rwkv_shift', 'rwkv_z', 'ssd_xbc', 'ssd_dt',
             'ssd_z', 'mlstm_qk', 'mlstm_v', 'mlstm_i', 'mlstm_f', 'mlstm_o', 'mlstm_z']
    widths = [256, 256, 512, 16, 512, sw, 512, SSD_XBC, SSD_HEADS, 512, 1024, 512, 4, 4, 512, 512]
    col0 = dict(zip(names, np.concatenate([[0], np.cumsum(widths)[:-1]]).tolist()))
    w_t = p['w_in'].T
    off = lambda name: col0[name]
    assert all(off(n) % F32_ROW_TILE == 0 for n in names if n != 'mlstm_f')
    rows = lambda name, n: w_t[off(name):off(name) + n, :]

    g = p['norm_g'].reshape(1, D_MODEL)
    tri, triu = consts['tri'], consts['triu']
    rows_of = lambda t: t.reshape(t.shape[0], B, S // CHUNK, CHUNK).transpose(1, 2, 0, 3)

    w_small = _pad_rows(jnp.concatenate([rows('gla_gk', GLA_RANK), rows('ssd_dt', SSD_HEADS),
                                         rows('mlstm_i', 2 * MLSTM_HEADS)], axis=0), LANES)
    even_odd = np.concatenate([np.arange(0, SSD_HEADS, 2), np.arange(1, SSD_HEADS, 2)])
    w_rows = jnp.concatenate([rows('ssd_dt', SSD_HEADS)[even_odd], rows('mlstm_i', 2 * MLSTM_HEADS)], axis=0)
    mu = p['rwkv_mu']
    if layer == 0:
        pieces, zero_lanes = [(off('rwkv_shift'), sw + GROUP_W, 0)], ()
        mu_p = mu.reshape(1, -1)
    else:
        pieces = [(off('rwkv_shift'), sw, 0), (off('rwkv_z'), GROUP_W, 1792)]
        zero_lanes = ((1664, 1792),)
        mu_p = _pad_vec(mu, 1792)
    slab, small, gate_rows = _in_proj(x2, g, w_t, pieces, _rwkv_slab_width(layer), zero_lanes,
                                      w_small, w_rows)
    slab = slab.reshape(B, S, _rwkv_slab_width(layer))
    small = small.reshape(B, S, LANES)
    gate_rows = rows_of(gate_rows)
    w2a2 = jnp.zeros((LANES, 2 * GROUP_W), F32)
    w2a2 = w2a2.at[0:RWKV_W_RANK, 0:GROUP_W].set(p['rwkv_w2'])
    w2a2 = w2a2.at[RWKV_W_RANK:, GROUP_W:].set(p['rwkv_a2'])
    vec = lambda t: t.reshape(1, GROUP_W)
    v0 = v2 = None
    if layer > 0:
        v0 = vec(p['rwkv_v0'])
        v2 = _bf(jnp.pad(p['rwkv_v2'], ((0, LANES - RWKV_V_RANK), (0, 0))))
    d_rwkv = _rwkv(layer, slab, v_first, mu_p, vec(p['rwkv_w0']), vec(p['rwkv_a0']), _bf(w2a2),
                            v0, v2, vec(p['rwkv_k_k']), vec(p['rwkv_k_a']), vec(p['rwkv_r_k']),
                            vec(p['rwkv_ln_g']), vec(p['rwkv_ln_b']), _bf(tri), _bf(consts['bd']),
                            consts['strict'], consts['incl'], _bf(consts['lvl']))

    slab = _in_proj(x2, g, w_t, [(off('gla_q'), 1024, 0), (off('gla_z'), GROUP_W, 1024)], GLA_SLAB)
    w2p = _bf(jnp.pad(p['gla_gk_w2'], ((SMALL_GK, LANES - SMALL_GK - GLA_RANK), (0, 0))))
    d_gla = _gla(slab.reshape(B, S, GLA_SLAB), small, w2p, p['gla_gk_b'].reshape(1, -1),
                 p['gla_norm_g'].reshape(1, -1), _bf(tri))

    slab = _in_proj(x2, g, w_t, [(off('ssd_xbc'), SSD_XBC, 0), (off('ssd_z'), GROUP_W, SSD_XBC)], SSD_SLAB)
    a_neg = -jnp.exp(p['ssd_a_log'])
    d_ssd = _ssd(slab.reshape(B, S, SSD_SLAB), small, gate_rows, p['ssd_conv_w'],
                 p['ssd_conv_b'].reshape(1, -1),
                 _pad_vec(p['ssd_dt_bias'], LANES, SMALL_DT), _pad_vec(a_neg, LANES, SMALL_DT),
                 p['ssd_dt_bias'][even_odd].reshape(-1, 1), a_neg[even_odd].reshape(-1, 1),
                 jnp.repeat(p['ssd_d'], SSD_HEADDIM).reshape(1, -1), p['ssd_norm_g'].reshape(1, -1),
                 _bf(tri), _bf(consts['triu2']), _bf(consts['ex']))

    slab = _in_proj(x2, g, w_t, [(off('mlstm_qk'), 3 * GROUP_W, 0), (off('mlstm_o'), 2 * GROUP_W, 3 * GROUP_W)],
                    MLSTM_SLAB)
    gb_col = jnp.concatenate([p['mlstm_ig_b'], p['mlstm_fg_b']]).reshape(-1, 1)
    d_ml = _mlstm(slab.reshape(B, S, MLSTM_SLAB), small, gate_rows, p['mlstm_conv_w'],
                  p['mlstm_conv_b'].reshape(1, -1), _pad_vec(p['mlstm_ig_b'], LANES, SMALL_I),
                  _pad_vec(p['mlstm_fg_b'], LANES, SMALL_F), gb_col, p['mlstm_norm_g'].reshape(1, -1),
                  _bf(tri), _bf(triu), _bf(consts['exi']), _bf(consts['exf']))

    T = B * S
    d_ml['start'], d_ssd['start'], d_gla['start'] = MIXER_STARTS
    o_rwkv, o_ml, o_ssd, o_gla = _run_mixers([d_rwkv, d_ml, d_ssd, d_gla], B, S, "mixers")
    if layer == 0:
        v_first = o_rwkv[1]
    ys = [y.reshape(T, GROUP_W) for y in (o_gla[0], o_rwkv[0], o_ssd[0], o_ml[0])]
    return _out_proj(x2, ys, p['w_out'], g_final), v_first


_PARAM_NAMES_0 = ['norm_g', 'w_in', 'w_out', 'gla_gk_w2', 'gla_gk_b', 'gla_norm_g', 'rwkv_mu', 'rwkv_w0',
                  'rwkv_w2', 'rwkv_a0', 'rwkv_a2', 'rwkv_k_k', 'rwkv_k_a', 'rwkv_r_k', 'rwkv_ln_g',
                  'rwkv_ln_b', 'ssd_conv_w', 'ssd_conv_b', 'ssd_dt_bias', 'ssd_a_log', 'ssd_d',
                  'ssd_norm_g', 'mlstm_conv_w', 'mlstm_conv_b', 'mlstm_ig_b', 'mlstm_fg_b', 'mlstm_norm_g']
_PARAM_NAMES_1 = (_PARAM_NAMES_0[:11] + ['rwkv_v0', 'rwkv_v2'] + _PARAM_NAMES_0[11:])


def kernel(x,
           norm_g_0, w_in_0, w_out_0, gla_gk_w2_0, gla_gk_b_0, gla_norm_g_0,
           rwkv_mu_0, rwkv_w0_0, rwkv_w2_0, rwkv_a0_0, rwkv_a2_0,
           rwkv_k_k_0, rwkv_k_a_0, rwkv_r_k_0, rwkv_ln_g_0, rwkv_ln_b_0,
           ssd_conv_w_0, ssd_conv_b_0, ssd_dt_bias_0, ssd_a_log_0, ssd_d_0, ssd_norm_g_0,
           mlstm_conv_w_0, mlstm_conv_b_0, mlstm_ig_b_0, mlstm_fg_b_0, mlstm_norm_g_0,
           norm_g_1, w_in_1, w_out_1, gla_gk_w2_1, gla_gk_b_1, gla_norm_g_1,
           rwkv_mu_1, rwkv_w0_1, rwkv_w2_1, rwkv_a0_1, rwkv_a2_1, rwkv_v0_1, rwkv_v2_1,
           rwkv_k_k_1, rwkv_k_a_1, rwkv_r_k_1, rwkv_ln_g_1, rwkv_ln_b_1,
           ssd_conv_w_1, ssd_conv_b_1, ssd_dt_bias_1, ssd_a_log_1, ssd_d_1, ssd_norm_g_1,
           mlstm_conv_w_1, mlstm_conv_b_1, mlstm_ig_b_1, mlstm_fg_b_1, mlstm_norm_g_1,
           final_norm_g):
    params = (norm_g_0, w_in_0, w_out_0, gla_gk_w2_0, gla_gk_b_0, gla_norm_g_0,
              rwkv_mu_0, rwkv_w0_0, rwkv_w2_0, rwkv_a0_0, rwkv_a2_0,
              rwkv_k_k_0, rwkv_k_a_0, rwkv_r_k_0, rwkv_ln_g_0, rwkv_ln_b_0,
              ssd_conv_w_0, ssd_conv_b_0, ssd_dt_bias_0, ssd_a_log_0, ssd_d_0, ssd_norm_g_0,
              mlstm_conv_w_0, mlstm_conv_b_0, mlstm_ig_b_0, mlstm_fg_b_0, mlstm_norm_g_0,
              norm_g_1, w_in_1, w_out_1, gla_gk_w2_1, gla_gk_b_1, gla_norm_g_1,
              rwkv_mu_1, rwkv_w0_1, rwkv_w2_1, rwkv_a0_1, rwkv_a2_1, rwkv_v0_1, rwkv_v2_1,
              rwkv_k_k_1, rwkv_k_a_1, rwkv_r_k_1, rwkv_ln_g_1, rwkv_ln_b_1,
              ssd_conv_w_1, ssd_conv_b_1, ssd_dt_bias_1, ssd_a_log_1, ssd_d_1, ssd_norm_g_1,
              mlstm_conv_w_1, mlstm_conv_b_1, mlstm_ig_b_1, mlstm_fg_b_1, mlstm_norm_g_1,
              final_norm_g)
    n0 = len(_PARAM_NAMES_0)
    n1 = len(_PARAM_NAMES_1)
    p0 = dict(zip(_PARAM_NAMES_0, params[:n0]))
    p1 = dict(zip(_PARAM_NAMES_1, params[n0:n0 + n1]))
    final_norm_g = params[n0 + n1]
    B, S, _ = x.shape
    consts = {k: jnp.asarray(v) for k, v in _np_consts().items()}
    x2 = x.reshape(B * S, D_MODEL)
    x2, v_first = _layer(x2, None, 0, p0, B, S, consts, None)
    x2, _ = _layer(x2, v_first, 1, p1, B, S, consts, final_norm_g.reshape(1, D_MODEL))
    return x2.reshape(B, S, D_MODEL)
```

```python
import functools

import numpy as np
import jax
import jax.numpy as jnp
from jax import lax
from jax.experimental import pallas as pl
from jax.experimental.pallas import tpu as pltpu

F32 = jnp.float32
BF16 = jnp.bfloat16

D_MODEL = 2048
CHUNK = 64
GROUP_W = 512
NORM_EPS = 1e-6
LANES = 128
CARRY_ROWS = 8

GLA_HEADS, GLA_DK, GLA_DV, GLA_RANK = 4, 64, 128, 16
GLA_GATE_NORMALIZER = 16.0
RWKV_HEAD, RWKV_W_RANK, RWKV_A_RANK, RWKV_V_RANK = 64, 64, 64, 32
RWKV_LN_EPS = 64e-5
RWKV_DECAY_OFFSET = 0.5
SSD_HEADS, SSD_HEADDIM, SSD_STATE, SSD_CONV = 8, 64, 128, 4
SSD_XBC = 1024
MLSTM_HEADS, MLSTM_HEAD, MLSTM_CONV = 4, 128, 4

ROW_TILES_IN = (512, 1024)
SMALL_GK, SMALL_DT, SMALL_I, SMALL_F = 0, 16, 24, 28
ROWS_DT, ROWS_IF, ROWS_TOTAL = 0, 8, 16
ROW_TILE_OUT = 512
VMEM_LIMIT_PROJ = 48 * 2**20
VMEM_LIMIT_IN_PROJ = 56 * 2**20
VMEM_BUDGET_PROJ = 52 * 2**20
WEIGHT_CAST_ROWS = 256


def _bf(x):
    return x.astype(BF16)


def _mm(a, b):
    return jnp.dot(_bf(a), _bf(b), preferred_element_type=F32)


def _mm_nt(a, b):
    return lax.dot_general(_bf(a), _bf(b), (((1,), (1,)), ((), ())), preferred_element_type=F32)


def _mm_tn(a, b):
    return lax.dot_general(_bf(a), _bf(b), (((0,), (0,)), ((), ())), preferred_element_type=F32)


def _split3(x):
    hi = _bf(x)
    r1 = x - hi.astype(F32)
    mid = _bf(r1)
    lo = _bf(r1 - mid.astype(F32))
    return hi, mid, lo


def _mm_sel_rhs(x, sel):
    hi, mid, lo = _split3(x)
    d = lambda a: jnp.dot(a, sel, preferred_element_type=F32)
    return d(hi) + d(mid) + d(lo)


def _mm_sel2(x, sel):
    hi = _bf(x)
    lo = _bf(x - hi.astype(F32))
    d = lambda a: jnp.dot(a, sel, preferred_element_type=F32)
    return d(hi) + d(lo)


def _mm_sel_lhs(sel, x):
    hi, mid, lo = _split3(x)
    d = lambda a: jnp.dot(sel, a, preferred_element_type=F32)
    return d(hi) + d(mid) + d(lo)


def _rowsum(x):
    ones = jnp.ones((x.shape[-1], LANES), BF16)
    hi = _bf(x)
    lo = _bf(x - hi.astype(F32))
    return (jnp.dot(hi, ones, preferred_element_type=F32)
            + jnp.dot(lo, ones, preferred_element_type=F32))


def _sigmoid(x):
    return 1.0 / (1.0 + jnp.exp(-x))


def _silu(x):
    return x * _sigmoid(x)


def _softplus(x):
    return jnp.maximum(x, 0.0) + jnp.log1p(jnp.exp(-jnp.abs(x)))


def _log_sigmoid(x):
    return -_softplus(-x)


def _lane_half_masks():
    lane = lax.broadcasted_iota(jnp.int32, (1, LANES), 1)
    lo = (lane < LANES // 2).astype(F32)
    return lo, 1.0 - lo


def _causal(n):
    r = lax.broadcasted_iota(jnp.int32, (n, n), 0)
    c = lax.broadcasted_iota(jnp.int32, (n, n), 1)
    return c <= r


def _shifted_rows(buf_ref, cur, offsets):
    prev = buf_ref[0:CARRY_ROWS, :]
    sub = lax.broadcasted_iota(jnp.int32, (CARRY_ROWS, 1), 0)
    outs = []
    for off in offsets:
        if off == 0:
            outs.append(cur)
            continue
        r = pltpu.roll(cur, off, axis=0)
        p = pltpu.roll(prev, off, axis=0)
        head = jnp.where(sub < off, p, r[0:CARRY_ROWS])
        outs.append(jnp.concatenate([head, r[CARRY_ROWS:]], axis=0))
    buf_ref[0:CARRY_ROWS, :] = cur[CHUNK - CARRY_ROWS:CHUNK, :]
    return outs


def _in_proj_kernel(*refs, pieces, zero_lanes, has_small):
    x_ref, g_ref = refs[0], refs[1]
    n = len(pieces)
    wf_refs = refs[2:2 + n]
    w_refs = refs[len(refs) - n:]
    rest = refs[2 + n:len(refs) - n]
    if has_small:
        ws_ref, wr_ref, o_ref, os_ref, or_ref = rest
    else:
        (o_ref,) = rest

    @pl.when(pl.program_id(0) == 0)
    def _():
        for wf_ref, w_ref in zip(wf_refs, w_refs):
            for r0 in range(0, w_ref.shape[0], WEIGHT_CAST_ROWS):
                r1 = min(r0 + WEIGHT_CAST_ROWS, w_ref.shape[0])
                w_ref[r0:r1, :] = _bf(wf_ref[r0:r1, :])

    nt = lambda a, b: lax.dot_general(a, b, (((1,), (1,)), ((), ())), preferred_element_type=F32)
    tm = x_ref.shape[0]
    for lo, hi in zero_lanes:
        o_ref[:, lo:hi] = jnp.zeros((tm, hi - lo), F32)
    for half in range(2):
        rs = slice(half * (tm // 2), (half + 1) * (tm // 2))
        x = x_ref[rs, :]
        h = x * lax.rsqrt(jnp.mean(x * x, axis=-1, keepdims=True) + NORM_EPS) * g_ref[...]
        hb = _bf(h)
        for w_ref, (_, n_rows, lane_off) in zip(w_refs, pieces):
            o_ref[rs, lane_off:lane_off + n_rows] = nt(hb, w_ref[...])
        if has_small:
            os_ref[rs, :] = nt(hb, _bf(ws_ref[...]))
            or_ref[:, rs] = nt(_bf(wr_ref[...]), hb)


def _in_proj(x2, g, w_t, pieces, width, zero_lanes=(), w_small=None, w_rows=None):
    T = x2.shape[0]
    w_rows_total = sum(n_rows for _, n_rows, _ in pieces)
    tm = max(t for t in ROW_TILES_IN
             if (w_rows_total * D_MODEL * (4 + 2)
                 + 2 * (t * D_MODEL * 4 + t * (width + LANES) * 4)) <= VMEM_BUDGET_PROJ)
    const2 = lambda shape: pl.BlockSpec(shape, lambda i: (0, 0))
    in_specs = [pl.BlockSpec((tm, D_MODEL), lambda i: (i, 0)), const2((1, D_MODEL))]
    args = [x2, g]
    for off, n_rows, _ in pieces:
        in_specs.append(pl.BlockSpec((pl.Element(n_rows), pl.Element(D_MODEL)), lambda i, off=off: (off, 0),
                                     pipeline_mode=pl.Buffered(1)))
        args.append(w_t)
    scratch = [pltpu.VMEM((n_rows, D_MODEL), BF16) for _, n_rows, _ in pieces]
    out_shape = [jax.ShapeDtypeStruct((T, width), F32)]
    out_specs = [pl.BlockSpec((tm, width), lambda i: (i, 0))]
    if w_small is not None:
        r = w_rows.shape[0]
        in_specs += [const2((LANES, D_MODEL)), const2((r, D_MODEL))]
        args += [w_small, w_rows]
        out_shape += [jax.ShapeDtypeStruct((T, LANES), F32), jax.ShapeDtypeStruct((r, T), F32)]
        out_specs += [pl.BlockSpec((tm, LANES), lambda i: (i, 0)), pl.BlockSpec((r, tm), lambda i: (0, i))]
    res = pl.pallas_call(
        functools.partial(_in_proj_kernel, pieces=tuple(pieces), zero_lanes=tuple(zero_lanes),
                          has_small=w_small is not None),
        grid=(T // tm,), in_specs=in_specs, out_specs=out_specs, out_shape=out_shape,
        scratch_shapes=scratch,
        compiler_params=pltpu.CompilerParams(dimension_semantics=("arbitrary",),
                                             vmem_limit_bytes=VMEM_LIMIT_IN_PROJ),
        name="in_proj")(*args)
    return res if w_small is not None else res[0]


def _out_proj_kernel(*refs, final):
    if final:
        x_ref, y0, y1, y2, y3, wf_ref, g_ref, o_ref, w_ref = refs
    else:
        x_ref, y0, y1, y2, y3, wf_ref, o_ref, w_ref = refs

    @pl.when(pl.program_id(0) == 0)
    def _():
        for r0 in range(0, D_MODEL, WEIGHT_CAST_ROWS):
            w_ref[r0:r0 + WEIGHT_CAST_ROWS, :] = _bf(wf_ref[r0:r0 + WEIGHT_CAST_ROWS, :])

    acc = x_ref[...]
    for gi, y in enumerate((y0, y1, y2, y3)):
        acc = acc + jnp.dot(_bf(y[...]), w_ref[gi * GROUP_W:(gi + 1) * GROUP_W, :],
                            preferred_element_type=F32)
    if final:
        acc = acc * lax.rsqrt(jnp.mean(acc * acc, axis=-1, keepdims=True) + NORM_EPS) * g_ref[...]
    o_ref[...] = acc


def _out_proj(x2, ys, w, g_final=None):
    T = x2.shape[0]
    tm = ROW_TILE_OUT
    final = g_final is not None
    in_specs = [pl.BlockSpec((tm, D_MODEL), lambda i: (i, 0))]
    in_specs += [pl.BlockSpec((tm, GROUP_W), lambda i: (i, 0)) for _ in range(4)]
    in_specs += [pl.BlockSpec((D_MODEL, D_MODEL), lambda i: (0, 0), pipeline_mode=pl.Buffered(1))]
    args = [x2, *ys, w]
    if final:
        in_specs.append(pl.BlockSpec((1, D_MODEL), lambda i: (0, 0)))
        args.append(g_final)
    return pl.pallas_call(
        functools.partial(_out_proj_kernel, final=final),
        grid=(T // tm,), in_specs=in_specs,
        out_specs=pl.BlockSpec((tm, D_MODEL), lambda i: (i, 0)),
        out_shape=jax.ShapeDtypeStruct((T, D_MODEL), F32),
        scratch_shapes=[pltpu.VMEM((D_MODEL, D_MODEL), BF16)],
        compiler_params=pltpu.CompilerParams(dimension_semantics=("arbitrary",),
                                             vmem_limit_bytes=VMEM_LIMIT_IN_PROJ),
        name="out_proj")(*args)


GLA_SLAB = 1536


BATCH_BLOCK = 4
MIXER_STARTS = (5, 5, 12)


def _lockstep(gens, starts=None):
    gens = list(gens)
    starts = [0] * len(gens) if starts is None else list(starts)
    pending = list(zip(starts, gens))
    tick = 0
    while pending:
        alive = []
        for start, g in pending:
            if start > tick:
                alive.append((start, g))
                continue
            try:
                next(g)
                alive.append((start, g))
            except StopIteration:
                pass
        pending = alive
        tick += 1


def _stepper(gens):
    gens = list(gens)
    while gens:
        alive = []
        for g in gens:
            try:
                next(g)
                alive.append(g)
            except StopIteration:
                pass
        gens = alive
        yield


def _mixers_kernel(*refs, parts):
    n_in = sum(p[1] for p in parts)
    n_out = sum(p[2] for p in parts)
    ins, outs, scs = refs[:n_in], refs[n_in:n_in + n_out], refs[n_in + n_out:]
    gens, starts, posts = [], [], []
    i = o = s = 0
    for stages_fn, ni, no, ns, start in parts:
        g, post = stages_fn(*ins[i:i + ni], *outs[o:o + no], *scs[s:s + ns])
        i, o, s = i + ni, o + no, s + ns
        gens += g
        starts += [start] * len(g)
        posts.append(post)
    _lockstep(gens, starts)
    for post in posts:
        if post is not None:
            post()


def _run_mixers(descs, B, S, name):
    nb = BATCH_BLOCK
    parts = tuple((d['stages'], len(d['args']), len(d['out_shape']), len(d['scratch']), d.get('start', 0))
                  for d in descs)
    res = pl.pallas_call(
        functools.partial(_mixers_kernel, parts=parts), grid=(B // nb, S // CHUNK),
        in_specs=[sp for d in descs for sp in d['in_specs']],
        out_specs=[sp for d in descs for sp in d['out_specs']],
        out_shape=[sh for d in descs for sh in d['out_shape']],
        scratch_shapes=[sc for d in descs for sc in d['scratch']],
        compiler_params=pltpu.CompilerParams(dimension_semantics=("arbitrary", "arbitrary"),
                                             vmem_limit_bytes=VMEM_LIMIT_PROJ),
        name=name)(*[a for d in descs for a in d['args']])
    out, k = [], 0
    for d in descs:
        out.append(res[k:k + len(d['out_shape'])])
        k += len(d['out_shape'])
    return out


def _gla_stages(slab_ref, small_ref, w2_ref, gkb_ref, ng_ref, tri_ref, o_ref, st_ref, *, nb):
    @pl.when(pl.program_id(1) == 0)
    def _():
        st_ref[...] = jnp.zeros_like(st_ref)

    causal = _causal(CHUNK)
    masks = _lane_half_masks()

    def head(b, h, qg, kg, kd, dec):
        p, j = divmod(h, 2)
        ls = slice(p * LANES, (p + 1) * LANES)
        qm = qg[:, ls] * masks[j]
        att = jnp.where(causal, _mm_nt(qm, kg[:, ls]), 0.0)
        yield
        v_h = slab_ref[b, :, 512 + h * GLA_DV:512 + (h + 1) * GLA_DV]
        st = st_ref[b, h]
        o = _mm(att, v_h) + _mm_nt(qm, st)
        st_ref[b, h] = st * dec[:, ls] + _mm_tn(v_h, kd[:, ls] * masks[j])
        yield
        ms = jnp.mean(o * o, axis=-1, keepdims=True)
        yield
        o = o * lax.rsqrt(ms + NORM_EPS)
        o = o * ng_ref[:, h * GLA_DV:(h + 1) * GLA_DV]
        z_h = slab_ref[b, :, 1024 + h * GLA_DV:1024 + (h + 1) * GLA_DV]
        o_ref[b, :, h * GLA_DV:(h + 1) * GLA_DV] = o * _silu(z_h)

    def sequence(b):
        q = slab_ref[b, :, 0:256] * (GLA_DK ** -0.5)
        k = slab_ref[b, :, 256:512]
        gk = _mm(small_ref[b], w2_ref[...]) + gkb_ref[...]
        log_a = _log_sigmoid(gk) / GLA_GATE_NORMALIZER
        cum = _mm_sel_lhs(tri_ref[...], log_a)
        last = cum[CHUNK - 1:CHUNK, :]
        qg = q * jnp.exp(cum)
        kg = k * jnp.exp(-cum)
        kd = k * jnp.exp(last - cum)
        dec = jnp.exp(last)
        yield
        yield from _stepper([head(b, h, qg, kg, kd, dec) for h in range(GLA_HEADS)])

    return [sequence(b) for b in range(nb)], None


def _gla(slab, small, w2p, gkb, ng, tri):
    B, S, _ = slab.shape
    nb = BATCH_BLOCK
    const = lambda shape: pl.BlockSpec(shape, lambda b, c: (0,) * len(shape))
    return dict(
        stages=functools.partial(_gla_stages, nb=nb),
        in_specs=[pl.BlockSpec((nb, CHUNK, GLA_SLAB), lambda b, c: (b, c, 0)),
                  pl.BlockSpec((nb, CHUNK, LANES), lambda b, c: (b, c, 0)),
                  const((LANES, 256)), const((1, 256)), const((1, GROUP_W)), const((CHUNK, CHUNK))],
        args=[slab, small, w2p, gkb, ng, tri],
        out_specs=[pl.BlockSpec((nb, CHUNK, GROUP_W), lambda b, c: (b, c, 0))],
        out_shape=[jax.ShapeDtypeStruct((B, S, GROUP_W), F32)],
        scratch=[pltpu.VMEM((nb, GLA_HEADS, GLA_DV, LANES), F32)])


SSD_SLAB = 1536


def _ssd_stages(slab_ref, small_ref, rows_ref, cw_ref, cb_ref, dtb_ref, a_ref, dtb_col_ref, a_col_ref,
                dskip_ref, ng_ref, tri_ref, triu2_ref, ex_ref, o_ref, buf_ref, st_ref, *, nb):
    @pl.when(pl.program_id(1) == 0)
    def _():
        st_ref[...] = jnp.zeros_like(st_ref)
        for b in range(nb):
            buf_ref[b, 0:CARRY_ROWS, :] = jnp.zeros((CARRY_ROWS, SSD_XBC), F32)

    row_i = lax.broadcasted_iota(jnp.int32, (CHUNK, LANES), 0)
    col_i = lax.broadcasted_iota(jnp.int32, (CHUNK, LANES), 1)
    causal2 = jnp.bitwise_and(col_i, CHUNK - 1) <= row_i
    masks = _lane_half_masks()
    masks_b = (_bf(masks[0]), _bf(masks[1]))

    def pair(b, p, xbc, cum_b, cum_row2, xdt, xw, ecum, dec):
        g = p // 2
        ls = slice(p * LANES, (p + 1) * LANES)
        bm = xbc[:, 512 + g * SSD_STATE:512 + (g + 1) * SSD_STATE]
        cm = xbc[:, 768 + g * SSD_STATE:768 + (g + 1) * SSD_STATE]
        bmb = _bf(bm)
        cbm2 = _mm_nt(cm, jnp.concatenate([bmb, bmb], axis=0))
        st = st_ref[b, p]
        y = _mm(cm, st) * ecum[:, ls]
        st_ref[b, p] = st * dec[:, ls] + _mm_tn(bmb, xw[:, ls])
        lmat = jnp.exp(jnp.where(causal2, cum_b[:, ls] - cum_row2[p:p + 1, :], -jnp.inf))
        yield
        xb = _bf(xdt[:, ls])
        xs = jnp.concatenate([xb * masks_b[0], xb * masks_b[1]], axis=0)
        y = y + _mm(cbm2 * lmat, xs)
        yield
        y = y + dskip_ref[:, ls] * xbc[:, ls]
        o_ref[b, :, ls] = y * _silu(slab_ref[b, :, 1024 + p * LANES:1024 + (p + 1) * LANES])

    def sequence(b):
        taps = _shifted_rows(buf_ref.at[b], slab_ref[b, :, 0:SSD_XBC], (3, 2, 1, 0))
        xbc = cb_ref[...]
        for j in range(SSD_CONV):
            xbc = xbc + taps[j] * cw_ref[j:j + 1, :]
        xbc = _silu(xbc)
        yield
        dt_col = _softplus(small_ref[b] + dtb_ref[...])
        cum_col = _mm_sel_lhs(tri_ref[...], dt_col * a_ref[...])
        dt_row = _softplus(rows_ref[b, 0, ROWS_DT:ROWS_DT + SSD_HEADS, :] + dtb_col_ref[...])
        da_row = dt_row * a_col_ref[...]
        cum_row2 = (_mm_sel_rhs(da_row[0:SSD_HEADS // 2], triu2_ref[0])
                    + _mm_sel_rhs(da_row[SSD_HEADS // 2:SSD_HEADS], triu2_ref[1]))
        dt_b = _mm_sel_rhs(dt_col, ex_ref[...])
        cum_b = _mm_sel_rhs(cum_col, ex_ref[...])
        last_b = cum_b[CHUNK - 1:CHUNK, :]
        xdt = xbc[:, 0:512] * dt_b
        xw = xdt * jnp.exp(last_b - cum_b)
        ecum = jnp.exp(cum_b)
        dec = jnp.exp(last_b)
        yield
        yield from _stepper([pair(b, p, xbc, cum_b, cum_row2, xdt, xw, ecum, dec)
                             for p in range(SSD_HEADS // 2)])

    gens = [sequence(b) for b in range(nb)]

    def group_norm():
        for b in range(nb):
            y = o_ref[b]
            inv = lax.rsqrt(_rowsum(y * y) * (1.0 / GROUP_W) + NORM_EPS)
            for p in range(GROUP_W // LANES):
                ls = slice(p * LANES, (p + 1) * LANES)
                o_ref[b, :, ls] = y[:, ls] * inv * ng_ref[:, ls]

    return gens, group_norm


def _ssd(slab, small, rows, cw, cb, dtb, a, dtb_col, a_col, dskip, ng, tri, triu, ex):
    B, S, _ = slab.shape
    nb = BATCH_BLOCK
    const = lambda shape: pl.BlockSpec(shape, lambda b, c: (0,) * len(shape))
    return dict(
        stages=functools.partial(_ssd_stages, nb=nb),
        in_specs=[pl.BlockSpec((nb, CHUNK, SSD_SLAB), lambda b, c: (b, c, 0)),
                  pl.BlockSpec((nb, CHUNK, LANES), lambda b, c: (b, c, 0)),
                  pl.BlockSpec((nb, 1, ROWS_TOTAL, CHUNK), lambda b, c: (b, c, 0, 0)),
                  const((SSD_CONV, SSD_XBC)), const((1, SSD_XBC)), const((1, LANES)), const((1, LANES)),
                  const((SSD_HEADS, 1)), const((SSD_HEADS, 1)), const((1, GROUP_W)), const((1, GROUP_W)),
                  const((CHUNK, CHUNK)), const((2, CHUNK, LANES)), const((LANES, GROUP_W))],
        args=[slab, small, rows, cw, cb, dtb, a, dtb_col, a_col, dskip, ng, tri, triu, ex],
        out_specs=[pl.BlockSpec((nb, CHUNK, GROUP_W), lambda b, c: (b, c, 0))],
        out_shape=[jax.ShapeDtypeStruct((B, S, GROUP_W), F32)],
        scratch=[pltpu.VMEM((nb, CARRY_ROWS + CHUNK, SSD_XBC), F32),
                 pltpu.VMEM((nb, SSD_HEADS // 2, SSD_STATE, LANES), F32)])


MLSTM_SLAB = 2560


def _mlstm_stages(slab_ref, small_ref, rows_ref, cw_ref, cb_ref, igb_ref, fgb_ref, gb_col_ref, ng_ref,
                  tri_ref, triu_ref, exi_ref, exf_ref, o_ref, buf_ref, c_ref, nm_ref, *, nb):
    @pl.when(pl.program_id(1) == 0)
    def _():
        c_ref[...] = jnp.zeros_like(c_ref)
        nm_ref[...] = jnp.zeros_like(nm_ref)
        for b in range(nb):
            buf_ref[b, 0:CARRY_ROWS, :] = jnp.zeros((CARRY_ROWS, 2 * GROUP_W), F32)

    causal = _causal(CHUNK)

    def head(b, h, qk, li_b, ci_b, logi_row, cum_row):
        ls = slice(h * MLSTM_HEAD, (h + 1) * MLSTM_HEAD)
        q = qk[:, ls]
        k = qk[:, GROUP_W + h * MLSTM_HEAD:GROUP_W + (h + 1) * MLSTM_HEAD] * (MLSTM_HEAD ** -0.5)
        v = slab_ref[b, :, 1024 + h * MLSTM_HEAD:1024 + (h + 1) * MLSTM_HEAD]
        ci = ci_b[:, ls]
        li = li_b[:, ls]
        cr = cum_row[h:h + 1, :]
        lir = logi_row[h:h + 1, :]
        last = ci[CHUNK - 1:CHUNK, :]
        c_prev = c_ref[b, h]
        n_prev = nm_ref[b, h, 0:1, :]
        m_prev = nm_ref[b, h, 1:2, :]

        g = last - ci + li
        g_max = jnp.max(g, axis=0, keepdims=True)
        log_d = jnp.where(causal, ci[:, 0:CHUNK] - cr + lir, -jnp.inf)
        row_max = jnp.max(log_d, axis=-1, keepdims=True)
        qk_h = _mm_nt(q, k)
        qc = _mm(q, c_prev)
        qn = _rowsum(q * n_prev)
        yield
        kw = k * jnp.exp(g - g_max)
        c_loc = _mm_tn(kw, v)
        n_loc = jnp.sum(kw, axis=0, keepdims=True)
        m_new = jnp.maximum(last + m_prev, g_max)
        a_old = jnp.exp(last + m_prev - m_new)
        a_new = jnp.exp(g_max - m_new)
        c_ref[b, h] = a_old * c_prev + a_new * c_loc
        nm_ref[b, h, 0:1, :] = a_old * n_prev + a_new * n_loc
        nm_ref[b, h, 1:2, :] = m_new
        m_inter = ci + m_prev
        m_l = jnp.maximum(m_inter, row_max)
        wqk = qk_h * jnp.exp(log_d - m_l[:, 0:CHUNK])
        w_inter = jnp.exp(m_inter - m_l)
        num = _mm(wqk, v) + w_inter * qc
        den = _rowsum(wqk) + w_inter * qn
        yield
        den = jnp.maximum(jnp.abs(den), jnp.exp(-m_l))
        hh = num / den * _sigmoid(slab_ref[b, :, 1536 + h * MLSTM_HEAD:1536 + (h + 1) * MLSTM_HEAD])
        mu = _rowsum(hh) * (1.0 / MLSTM_HEAD)
        yield
        yc = hh - mu
        var = _rowsum(yc * yc) * (1.0 / MLSTM_HEAD)
        yield
        hh = yc * lax.rsqrt(var + NORM_EPS) * ng_ref[:, ls]
        o_ref[b, :, ls] = hh * _silu(slab_ref[b, :, 2048 + h * MLSTM_HEAD:2048 + (h + 1) * MLSTM_HEAD])

    def sequence(b):
        taps = _shifted_rows(buf_ref.at[b], slab_ref[b, :, 0:2 * GROUP_W], (3, 2, 1, 0))
        qk = cb_ref[...]
        for j in range(MLSTM_CONV):
            qk = qk + taps[j] * cw_ref[j:j + 1, :]
        qk = _silu(qk)
        yield
        logi_col = small_ref[b] + igb_ref[...]
        logf_col = _log_sigmoid(small_ref[b] + fgb_ref[...])
        cum_col = _mm_sel_lhs(tri_ref[...], logf_col)
        li_b = _mm_sel_rhs(logi_col, exi_ref[...])
        ci_b = _mm_sel_rhs(cum_col, exf_ref[...])
        pre_row = rows_ref[b, 0, ROWS_IF:ROWS_IF + 2 * MLSTM_HEADS, :] + gb_col_ref[...]
        logi_row = pre_row[0:MLSTM_HEADS, :]
        logf_row = _log_sigmoid(pre_row[MLSTM_HEADS:2 * MLSTM_HEADS, :])
        cum_row = _mm_sel_rhs(logf_row, triu_ref[...])
        yield
        yield from _stepper([head(b, h, qk, li_b, ci_b, logi_row, cum_row) for h in range(MLSTM_HEADS)])

    return [sequence(b) for b in range(nb)], None


def _mlstm(slab, small, rows, cw, cb, igb, fgb, gb_col, ng, tri, triu, exi, exf):
    B, S, _ = slab.shape
    nb = BATCH_BLOCK
    const = lambda shape: pl.BlockSpec(shape, lambda b, c: (0,) * len(shape))
    return dict(
        stages=functools.partial(_mlstm_stages, nb=nb),
        in_specs=[pl.BlockSpec((nb, CHUNK, MLSTM_SLAB), lambda b, c: (b, c, 0)),
                  pl.BlockSpec((nb, CHUNK, LANES), lambda b, c: (b, c, 0)),
                  pl.BlockSpec((nb, 1, ROWS_TOTAL, CHUNK), lambda b, c: (b, c, 0, 0)),
                  const((MLSTM_CONV, 2 * GROUP_W)), const((1, 2 * GROUP_W)), const((1, LANES)),
                  const((1, LANES)), const((2 * MLSTM_HEADS, 1)), const((1, GROUP_W)),
                  const((CHUNK, CHUNK)), const((CHUNK, CHUNK)),
                  const((LANES, GROUP_W)), const((LANES, GROUP_W))],
        args=[slab, small, rows, cw, cb, igb, fgb, gb_col, ng, tri, triu, exi, exf],
        out_specs=[pl.BlockSpec((nb, CHUNK, GROUP_W), lambda b, c: (b, c, 0))],
        out_shape=[jax.ShapeDtypeStruct((B, S, GROUP_W), F32)],
        scratch=[pltpu.VMEM((nb, CARRY_ROWS + CHUNK, 2 * GROUP_W), F32),
                 pltpu.VMEM((nb, MLSTM_HEADS, MLSTM_HEAD, MLSTM_HEAD), F32),
                 pltpu.VMEM((nb, MLSTM_HEADS, CARRY_ROWS, MLSTM_HEAD), F32)])


def _rwkv_slab_width(layer):
    return 2176 if layer == 0 else 2304


def _rwkv_shift_cols(layer):
    return 1664 if layer == 0 else 1792


def _rwkv_stages(*refs, layer, nb):
    if layer == 0:
        (slab_ref, mu_ref, w0_ref, a0_ref, w2a2_ref, kk_ref, ka_ref, rk_ref, lng_ref, lnb_ref,
         tri_ref, bd_ref, strict_ref, incl_ref, lvl_ref, o_ref, vf_out_ref, buf_ref, st_ref) = refs
    else:
        (slab_ref, vf_ref, mu_ref, w0_ref, a0_ref, w2a2_ref, v0_ref, v2_ref, kk_ref, ka_ref, rk_ref,
         lng_ref, lnb_ref, tri_ref, bd_ref, strict_ref, incl_ref, lvl_ref, o_ref, buf_ref, st_ref) = refs
    ws = _rwkv_shift_cols(layer)
    zoff = _rwkv_slab_width(layer) - GROUP_W
    npair = GROUP_W // LANES

    @pl.when(pl.program_id(1) == 0)
    def _():
        st_ref[...] = jnp.zeros_like(st_ref)
        for b in range(nb):
            buf_ref[b, 0:CARRY_ROWS, :] = jnp.zeros((CARRY_ROWS, ws), F32)

    lane = lax.broadcasted_iota(jnp.int32, (1, LANES), 1)
    masks = _lane_half_masks()
    bd = bd_ref[...]
    strict = strict_ref[...] > 0.5
    incl = incl_ref[...] > 0.5
    masks_b = (_bf(masks[0]), _bf(masks[1]))

    def rows2(t):
        tb = _bf(t)
        return jnp.concatenate([tb * masks_b[0], tb * masks_b[1]], axis=0)

    eye = (lax.broadcasted_iota(jnp.int32, (LANES, LANES), 0)
           == lax.broadcasted_iota(jnp.int32, (LANES, LANES), 1)).astype(F32)

    inv = 1.0 / RWKV_HEAD

    def sequence(b):
        inst = []
        f = slab_ref[b, :, 0:ws]
        (prev,) = _shifted_rows(buf_ref.at[b], f, (1,))
        f = f + mu_ref[...] * (prev - f)
        r = f[:, 0:512]
        k = f[:, 512:1024]
        v = f[:, 1024:1536]
        lora = f[:, 1536:1664]
        lora = jnp.where(lane < RWKV_W_RANK, jnp.tanh(lora), lora)
        wa = _mm(lora, w2a2_ref[...])
        if layer == 0:
            vf_out_ref[b] = v
        else:
            mix = _sigmoid(v0_ref[...] + _mm(f[:, 1664:1792], v2_ref[...]))
            v = v + (vf_ref[b] - v) * mix
        yield
        w_log = -_softplus(-(w0_ref[...] + wa[:, 0:512])) - RWKV_DECAY_OFFSET
        lw = -jnp.exp(w_log)
        a = _sigmoid(a0_ref[...] + wa[:, 512:1024])
        kk = k * kk_ref[...]
        k = k * (1.0 + (a - 1.0) * ka_ref[...])
        cum = _mm_sel_lhs(tri_ref[...], lw)
        ss_all = _mm_sel2(jnp.concatenate([kk[:, p * LANES:(p + 1) * LANES] ** 2 for p in range(npair)],
                                          axis=0), bd)
        ss = [ss_all[p * CHUNK:(p + 1) * CHUNK] for p in range(npair)]
        yield
        last = cum[CHUNK - 1:CHUNK, :]
        e_pos = jnp.exp(cum)
        e_neg = jnp.exp(-cum)
        e_end = jnp.exp(last - cum)
        e_prev = jnp.exp(cum - lw)
        gam = jnp.exp(last)
        for p in range(npair):
            ls = slice(p * LANES, (p + 1) * LANES)
            kk_p = kk[:, ls] / jnp.maximum(jnp.sqrt(ss[p]), 1e-12)
            k_p, r_p, v_p = k[:, ls], r[:, ls], v[:, ls]
            b_p = kk_p * a[:, ls]
            inst.append(dict(
                p=p, ls=ls, r=r_p, k=k_p, v=v_p, gam=gam[:, ls],
                la=rows2(-kk_p * e_prev[:, ls]), lr=rows2(r_p * e_pos[:, ls]),
                rb=rows2(b_p * e_neg[:, ls]), rk=rows2(k_p * e_neg[:, ls]),
                bh=rows2(b_p * e_end[:, ls]), kh=rows2(k_p * e_end[:, ls]), vs=rows2(v_p)))
        yield
        for d in inst:
            aa = _mm_nt(jnp.concatenate([d['la'], d['lr']], axis=0),
                        jnp.concatenate([d['rb'], d['rk']], axis=0))
            d['nab'] = _bf(jnp.where(strict, aa[0:LANES, 0:LANES], 0.0))
            d['aak'] = _bf(jnp.where(strict, aa[0:LANES, LANES:2 * LANES], 0.0))
            d['arb'] = _bf(jnp.where(incl, aa[LANES:2 * LANES, 0:LANES], 0.0))
            d['ark'] = _bf(jnp.where(incl, aa[LANES:2 * LANES, LANES:2 * LANES], 0.0))
            d['t'] = eye + (d['nab'] * lvl_ref[0]).astype(F32)
        yield
        for d in inst:
            xv = _mm(jnp.concatenate([d['aak'], d['ark']], axis=0), d['vs'])
            d['x'], d['arkv'] = xv[0:LANES], xv[LANES:2 * LANES]
        for lv in range(1, 6):
            for d in inst:
                d['tb'] = _bf(d['t'])
                d['nt'] = _mm(d['nab'] * lvl_ref[lv], d['tb'])
            yield
            for d in inst:
                d['t'] = d['t'] + _mm(d['tb'], d['nt'])
            yield
        for d in inst:
            d['wu'] = _mm(d['t'], jnp.concatenate([d['la'], _bf(d['x'])], axis=1))
        yield
        for d in inst:
            d['wub'] = _bf(d['wu'])
            qy = _mm(d['arb'], d['wub'])
            d['qt'] = d['lr'].astype(F32) + qy[:, 0:LANES]
            d['y0'] = d['arkv'] + qy[:, LANES:2 * LANES]
        yield
        for d in inst:
            st = st_ref[b, d['p']]
            uy = _mm_nt(jnp.concatenate([d['wub'][:, 0:LANES], _bf(d['qt'])], axis=0), st)
            d['ust'] = uy[0:LANES] + d['wu'][:, LANES:2 * LANES]
            d['st'] = st
            yst = uy[LANES:2 * LANES] + d['y0']
            d['y'] = yst[0:CHUNK] + yst[CHUNK:2 * CHUNK]
        yield
        for d in inst:
            st_ref[b, d['p']] = d['st'] * d['gam'] + _mm_tn(jnp.concatenate([_bf(d['ust']), d['vs']], axis=0),
                                                            jnp.concatenate([d['bh'], d['kh']], axis=0))
            sums = _mm_sel2(jnp.concatenate([d['y'], d['r'] * d['k'] * rk_ref[:, d['ls']]], axis=0), bd)
            d['mu'] = sums[0:CHUNK] * inv
            d['bonus'] = sums[CHUNK:2 * CHUNK] * d['v']
        yield
        for d in inst:
            d['yc'] = d['y'] - d['mu']
            d['var'] = _mm_sel2(d['yc'] * d['yc'], bd) * inv
        yield
        for d in inst:
            ls = d['ls']
            yn = d['yc'] * lax.rsqrt(d['var'] + RWKV_LN_EPS) * lng_ref[:, ls] + lnb_ref[:, ls]
            z_p = slab_ref[b, :, zoff + d['p'] * LANES:zoff + (d['p'] + 1) * LANES]
            o_ref[b, :, ls] = (yn + d['bonus']) * _silu(z_p)

    return [sequence(b) for b in range(nb)], None


def _rwkv(layer, slab, vf, mu, w0, a0, w2a2, v0, v2, kkw, ka, rk, lng, lnb, tri, bd, strict, incl, lvl):
    B, S, W = slab.shape
    nb = BATCH_BLOCK
    ws = _rwkv_shift_cols(layer)
    const = lambda shape: pl.BlockSpec(shape, lambda b, c: (0,) * len(shape))
    tok = lambda w: pl.BlockSpec((nb, CHUNK, w), lambda b, c: (b, c, 0))
    vecw = const((1, GROUP_W))
    in_specs = [tok(W)]
    args = [slab]
    if layer > 0:
        in_specs.append(tok(GROUP_W))
        args.append(vf)
    in_specs += [const((1, ws)), vecw, vecw, const((LANES, 2 * GROUP_W))]
    args += [mu, w0, a0, w2a2]
    if layer > 0:
        in_specs += [vecw, const((LANES, GROUP_W))]
        args += [v0, v2]
    in_specs += [vecw, vecw, vecw, vecw, vecw, const((CHUNK, CHUNK)), const((LANES, LANES)),
                 const((LANES, LANES)), const((LANES, LANES)), const((6, LANES, LANES))]
    args += [kkw, ka, rk, lng, lnb, tri, bd, strict, incl, lvl]
    out_shape = [jax.ShapeDtypeStruct((B, S, GROUP_W), F32)]
    out_specs = [tok(GROUP_W)]
    if layer == 0:
        out_shape.append(jax.ShapeDtypeStruct((B, S, GROUP_W), F32))
        out_specs.append(tok(GROUP_W))
    return dict(
        stages=functools.partial(_rwkv_stages, layer=layer, nb=nb),
        in_specs=in_specs, args=args, out_specs=out_specs, out_shape=out_shape,
        scratch=[pltpu.VMEM((nb, CARRY_ROWS + CHUNK, ws), F32),
                 pltpu.VMEM((nb, GROUP_W // LANES, LANES, LANES), F32)])


def _np_consts():
    i = np.arange(CHUNK)
    tri = (i[None, :] <= i[:, None]).astype(np.float32)
    t = np.arange(LANES)
    same = (t[:, None] // CHUNK) == (t[None, :] // CHUNK)
    strict = (same & (t[None, :] < t[:, None])).astype(np.float32)
    incl = (same & (t[None, :] <= t[:, None])).astype(np.float32)
    bd = same.astype(np.float32)
    lvl = np.stack([(((t[:, None] >> l) == (t[None, :] >> l))
                     & ((t[:, None] >> (l - 1)) != (t[None, :] >> (l - 1)))).astype(np.float32)
                    for l in range(1, 7)])
    ex = np.zeros((LANES, GROUP_W), np.float32)
    for h in range(SSD_HEADS):
        ex[SMALL_DT + h, h * SSD_HEADDIM:(h + 1) * SSD_HEADDIM] = 1.0
    exi = np.zeros((LANES, GROUP_W), np.float32)
    exf = np.zeros((LANES, GROUP_W), np.float32)
    for h in range(MLSTM_HEADS):
        exi[SMALL_I + h, h * MLSTM_HEAD:(h + 1) * MLSTM_HEAD] = 1.0
        exf[SMALL_F + h, h * MLSTM_HEAD:(h + 1) * MLSTM_HEAD] = 1.0
    triu = tri.T.copy()
    zero = np.zeros_like(triu)
    triu2 = np.stack([np.concatenate([triu, zero], axis=1), np.concatenate([zero, triu], axis=1)])
    return dict(tri=tri, triu=triu, triu2=triu2, strict=strict, incl=incl, bd=bd, lvl=lvl, ex=ex,
                exi=exi, exf=exf)


def _pad_rows(w, height):
    return jnp.pad(w, ((0, height - w.shape[0]), (0, 0)))


def _pad_vec(v, width, offset=0):
    v = v.reshape(1, -1)
    return jnp.pad(v, ((0, 0), (offset, width - offset - v.shape[1])))


F32_ROW_TILE = 8


def _layer(x2, v_first, layer, p, B, S, consts, g_final):
    sw = 3 * GROUP_W + RWKV_W_RANK + RWKV_A_RANK + (RWKV_V_RANK if layer > 0 else 0)
    names = ['gla_q', 'gla_k', 'gla_v', 'gla_gk', 'gla_z', 'rwkv_shift', 'rwkv_z', 'ssd_xbc', 'ssd_dt',
             'ssd_z', 'mlstm_qk', 'mlstm_v', 'mlstm_i', 'mlstm_f', 'mlstm_o', 'mlstm_z']
    widths = [256, 256, 512, 16, 512, sw, 512, SSD_XBC, SSD_HEADS, 512, 1024, 512, 4, 4, 512, 512]
    col0 = dict(zip(names, np.concatenate([[0], np.cumsum(widths)[:-1]]).tolist()))
    w_t = p['w_in'].T
    off = lambda name: col0[name]
    assert all(off(n) % F32_ROW_TILE == 0 for n in names if n != 'mlstm_f')
    rows = lambda name, n: w_t[off(name):off(name) + n, :]

    g = p['norm_g'].reshape(1, D_MODEL)
    tri, triu = consts['tri'], consts['triu']
    rows_of = lambda t: t.reshape(t.shape[0], B, S // CHUNK, CHUNK).transpose(1, 2, 0, 3)

    w_small = _pad_rows(jnp.concatenate([rows('gla_gk', GLA_RANK), rows('ssd_dt', SSD_HEADS),
                                         rows('mlstm_i', 2 * MLSTM_HEADS)], axis=0), LANES)
    even_odd = np.concatenate([np.arange(0, SSD_HEADS, 2), np.arange(1, SSD_HEADS, 2)])
    w_rows = jnp.concatenate([rows('ssd_dt', SSD_HEADS)[even_odd], rows('mlstm_i', 2 * MLSTM_HEADS)], axis=0)
    mu = p['rwkv_mu']
    if layer == 0:
        pieces, zero_lanes = [(off('rwkv_shift'), sw + GROUP_W, 0)], ()
        mu_p = mu.reshape(1, -1)
    else:
        pieces = [(off('rwkv_shift'), sw, 0), (off('rwkv_z'), GROUP_W, 1792)]
        zero_lanes = ((1664, 1792),)
        mu_p = _pad_vec(mu, 1792)
    slab, small, gate_rows = _in_proj(x2, g, w_t, pieces, _rwkv_slab_width(layer), zero_lanes,
                                      w_small, w_rows)
    slab = slab.reshape(B, S, _rwkv_slab_width(layer))
    small = small.reshape(B, S, LANES)
    gate_rows = rows_of(gate_rows)
    w2a2 = jnp.zeros((LANES, 2 * GROUP_W), F32)
    w2a2 = w2a2.at[0:RWKV_W_RANK, 0:GROUP_W].set(p['rwkv_w2'])
    w2a2 = w2a2.at[RWKV_W_RANK:, GROUP_W:].set(p['rwkv_a2'])
    vec = lambda t: t.reshape(1, GROUP_W)
    v0 = v2 = None
    if layer > 0:
        v0 = vec(p['rwkv_v0'])
        v2 = _bf(jnp.pad(p['rwkv_v2'], ((0, LANES - RWKV_V_RANK), (0, 0))))
    d_rwkv = _rwkv(layer, slab, v_first, mu_p, vec(p['rwkv_w0']), vec(p['rwkv_a0']), _bf(w2a2),
                            v0, v2, vec(p['rwkv_k_k']), vec(p['rwkv_k_a']), vec(p['rwkv_r_k']),
                            vec(p['rwkv_ln_g']), vec(p['rwkv_ln_b']), _bf(tri), _bf(consts['bd']),
                            consts['strict'], consts['incl'], _bf(consts['lvl']))

    slab = _in_proj(x2, g, w_t, [(off('gla_q'), 1024, 0), (off('gla_z'), GROUP_W, 1024)], GLA_SLAB)
    w2p = _bf(jnp.pad(p['gla_gk_w2'], ((SMALL_GK, LANES - SMALL_GK - GLA_RANK), (0, 0))))
    d_gla = _gla(slab.reshape(B, S, GLA_SLAB), small, w2p, p['gla_gk_b'].reshape(1, -1),
                 p['gla_norm_g'].reshape(1, -1), _bf(tri))

    slab = _in_proj(x2, g, w_t, [(off('ssd_xbc'), SSD_XBC, 0), (off('ssd_z'), GROUP_W, SSD_XBC)], SSD_SLAB)
    a_neg = -jnp.exp(p['ssd_a_log'])
    d_ssd = _ssd(slab.reshape(B, S, SSD_SLAB), small, gate_rows, p['ssd_conv_w'],
                 p['ssd_conv_b'].reshape(1, -1),
                 _pad_vec(p['ssd_dt_bias'], LANES, SMALL_DT), _pad_vec(a_neg, LANES, SMALL_DT),
                 p['ssd_dt_bias'][even_odd].reshape(-1, 1), a_neg[even_odd].reshape(-1, 1),
                 jnp.repeat(p['ssd_d'], SSD_HEADDIM).reshape(1, -1), p['ssd_norm_g'].reshape(1, -1),
                 _bf(tri), _bf(consts['triu2']), _bf(consts['ex']))

    slab = _in_proj(x2, g, w_t, [(off('mlstm_qk'), 3 * GROUP_W, 0), (off('mlstm_o'), 2 * GROUP_W, 3 * GROUP_W)],
                    MLSTM_SLAB)
    gb_col = jnp.concatenate([p['mlstm_ig_b'], p['mlstm_fg_b']]).reshape(-1, 1)
    d_ml = _mlstm(slab.reshape(B, S, MLSTM_SLAB), small, gate_rows, p['mlstm_conv_w'],
                  p['mlstm_conv_b'].reshape(1, -1), _pad_vec(p['mlstm_ig_b'], LANES, SMALL_I),
                  _pad_vec(p['mlstm_fg_b'], LANES, SMALL_F), gb_col, p['mlstm_norm_g'].reshape(1, -1),
                  _bf(tri), _bf(triu), _bf(consts['exi']), _bf(consts['exf']))

    T = B * S
    d_ml['start'], d_ssd['start'], d_gla['start'] = MIXER_STARTS
    o_rwkv, o_ml, o_ssd, o_gla = _run_mixers([d_rwkv, d_ml, d_ssd, d_gla], B, S, "mixers")
    if layer == 0:
        v_first = o_rwkv[1]
    ys = [y.reshape(T, GROUP_W) for y in (o_gla[0], o_rwkv[0], o_ssd[0], o_ml[0])]
    return _out_proj(x2, ys, p['w_out'], g_final), v_first


_PARAM_NAMES_0 = ['norm_g', 'w_in', 'w_out', 'gla_gk_w2', 'gla_gk_b', 'gla_norm_g', 'rwkv_mu', 'rwkv_w0',
                  'rwkv_w2', 'rwkv_a0', 'rwkv_a2', 'rwkv_k_k', 'rwkv_k_a', 'rwkv_r_k', 'rwkv_ln_g',
                  'rwkv_ln_b', 'ssd_conv_w', 'ssd_conv_b', 'ssd_dt_bias', 'ssd_a_log', 'ssd_d',
                  'ssd_norm_g', 'mlstm_conv_w', 'mlstm_conv_b', 'mlstm_ig_b', 'mlstm_fg_b', 'mlstm_norm_g']
_PARAM_NAMES_1 = (_PARAM_NAMES_0[:11] + ['rwkv_v0', 'rwkv_v2'] + _PARAM_NAMES_0[11:])


def kernel(x,
           norm_g_0, w_in_0, w_out_0, gla_gk_w2_0, gla_gk_b_0, gla_norm_g_0,
           rwkv_mu_0, rwkv_w0_0, rwkv_w2_0, rwkv_a0_0, rwkv_a2_0,
           rwkv_k_k_0, rwkv_k_a_0, rwkv_r_k_0, rwkv_ln_g_0, rwkv_ln_b_0,
           ssd_conv_w_0, ssd_conv_b_0, ssd_dt_bias_0, ssd_a_log_0, ssd_d_0, ssd_norm_g_0,
           mlstm_conv_w_0, mlstm_conv_b_0, mlstm_ig_b_0, mlstm_fg_b_0, mlstm_norm_g_0,
           norm_g_1, w_in_1, w_out_1, gla_gk_w2_1, gla_gk_b_1, gla_norm_g_1,
           rwkv_mu_1, rwkv_w0_1, rwkv_w2_1, rwkv_a0_1, rwkv_a2_1, rwkv_v0_1, rwkv_v2_1,
           rwkv_k_k_1, rwkv_k_a_1, rwkv_r_k_1, rwkv_ln_g_1, rwkv_ln_b_1,
           ssd_conv_w_1, ssd_conv_b_1, ssd_dt_bias_1, ssd_a_log_1, ssd_d_1, ssd_norm_g_1,
           mlstm_conv_w_1, mlstm_conv_b_1, mlstm_ig_b_1, mlstm_fg_b_1, mlstm_norm_g_1,
           final_norm_g):
    params = (norm_g_0, w_in_0, w_out_0, gla_gk_w2_0, gla_gk_b_0, gla_norm_g_0,
              rwkv_mu_0, rwkv_w0_0, rwkv_w2_0, rwkv_a0_0, rwkv_a2_0,
              rwkv_k_k_0, rwkv_k_a_0, rwkv_r_k_0, rwkv_ln_g_0, rwkv_ln_b_0,
              ssd_conv_w_0, ssd_conv_b_0, ssd_dt_bias_0, ssd_a_log_0, ssd_d_0, ssd_norm_g_0,
              mlstm_conv_w_0, mlstm_conv_b_0, mlstm_ig_b_0, mlstm_fg_b_0, mlstm_norm_g_0,
              norm_g_1, w_in_1, w_out_1, gla_gk_w2_1, gla_gk_b_1, gla_norm_g_1,
              rwkv_mu_1, rwkv_w0_1, rwkv_w2_1, rwkv_a0_1, rwkv_a2_1, rwkv_v0_1, rwkv_v2_1,
              rwkv_k_k_1, rwkv_k_a_1, rwkv_r_k_1, rwkv_ln_g_1, rwkv_ln_b_1,
              ssd_conv_w_1, ssd_conv_b_1, ssd_dt_bias_1, ssd_a_log_1, ssd_d_1, ssd_norm_g_1,
              mlstm_conv_w_1, mlstm_conv_b_1, mlstm_ig_b_1, mlstm_fg_b_1, mlstm_norm_g_1,
              final_norm_g)
    n0 = len(_PARAM_NAMES_0)
    n1 = len(_PARAM_NAMES_1)
    p0 = dict(zip(_PARAM_NAMES_0, params[:n0]))
    p1 = dict(zip(_PARAM_NAMES_1, params[n0:n0 + n1]))
    final_norm_g = params[n0 + n1]
    B, S, _ = x.shape
    consts = {k: jnp.asarray(v) for k, v in _np_consts().items()}
    x2 = x.reshape(B * S, D_MODEL)
    x2, v_first = _layer(x2, None, 0, p0, B, S, consts, None)
    x2, _ = _layer(x2, v_first, 1, p1, B, S, consts, final_norm_g.reshape(1, D_MODEL))
    return x2.reshape(B, S, D_MODEL)
```

```python
import functools

import numpy as np
import jax
import jax.numpy as jnp
from jax import lax
from jax.experimental import pallas as pl
from jax.experimental.pallas import tpu as pltpu

F32 = jnp.float32
BF16 = jnp.bfloat16

D_MODEL = 2048
CHUNK = 64
GROUP_W = 512
NORM_EPS = 1e-6
LANES = 128
CARRY_ROWS = 8

GLA_HEADS, GLA_DK, GLA_DV, GLA_RANK = 4, 64, 128, 16
GLA_GATE_NORMALIZER = 16.0
RWKV_HEAD, RWKV_W_RANK, RWKV_A_RANK, RWKV_V_RANK = 64, 64, 64, 32
RWKV_LN_EPS = 64e-5
RWKV_DECAY_OFFSET = 0.5
SSD_HEADS, SSD_HEADDIM, SSD_STATE, SSD_CONV = 8, 64, 128, 4
SSD_XBC = 1024
MLSTM_HEADS, MLSTM_HEAD, MLSTM_CONV = 4, 128, 4

ROW_TILES_IN = (512, 1024)
SMALL_GK, SMALL_DT, SMALL_I, SMALL_F = 0, 16, 24, 28
ROWS_DT, ROWS_IF, ROWS_TOTAL = 0, 8, 16
ROW_TILE_OUT = 512
VMEM_LIMIT_PROJ = 48 * 2**20
VMEM_LIMIT_IN_PROJ = 56 * 2**20
VMEM_BUDGET_PROJ = 52 * 2**20
WEIGHT_CAST_ROWS = 256


def _bf(x):
    return x.astype(BF16)


def _mm(a, b):
    return jnp.dot(_bf(a), _bf(b), preferred_element_type=F32)


def _mm_nt(a, b):
    return lax.dot_general(_bf(a), _bf(b), (((1,), (1,)), ((), ())), preferred_element_type=F32)


def _mm_tn(a, b):
    return lax.dot_general(_bf(a), _bf(b), (((0,), (0,)), ((), ())), preferred_element_type=F32)


def _split3(x):
    hi = _bf(x)
    r1 = x - hi.astype(F32)
    mid = _bf(r1)
    lo = _bf(r1 - mid.astype(F32))
    return hi, mid, lo


def _mm_sel_rhs(x, sel):
    hi, mid, lo = _split3(x)
    d = lambda a: jnp.dot(a, sel, preferred_element_type=F32)
    return d(hi) + d(mid) + d(lo)


def _mm_sel2(x, sel):
    hi = _bf(x)
    lo = _bf(x - hi.astype(F32))
    d = lambda a: jnp.dot(a, sel, preferred_element_type=F32)
    return d(hi) + d(lo)


def _mm_sel_lhs(sel, x):
    hi, mid, lo = _split3(x)
    d = lambda a: jnp.dot(sel, a, preferred_element_type=F32)
    return d(hi) + d(mid) + d(lo)


def _rowsum(x):
    ones = jnp.ones((x.shape[-1], LANES), BF16)
    hi = _bf(x)
    lo = _bf(x - hi.astype(F32))
    return (jnp.dot(hi, ones, preferred_element_type=F32)
            + jnp.dot(lo, ones, preferred_element_type=F32))


def _sigmoid(x):
    return 1.0 / (1.0 + jnp.exp(-x))


def _silu(x):
    return x * _sigmoid(x)


def _softplus(x):
    return jnp.maximum(x, 0.0) + jnp.log1p(jnp.exp(-jnp.abs(x)))


def _log_sigmoid(x):
    return -_softplus(-x)


def _lane_half_masks():
    lane = lax.broadcasted_iota(jnp.int32, (1, LANES), 1)
    lo = (lane < LANES // 2).astype(F32)
    return lo, 1.0 - lo


def _causal(n):
    r = lax.broadcasted_iota(jnp.int32, (n, n), 0)
    c = lax.broadcasted_iota(jnp.int32, (n, n), 1)
    return c <= r


def _shifted_rows(buf_ref, cur, offsets):
    prev = buf_ref[0:CARRY_ROWS, :]
    sub = lax.broadcasted_iota(jnp.int32, (CARRY_ROWS, 1), 0)
    outs = []
    for off in offsets:
        if off == 0:
            outs.append(cur)
            continue
        r = pltpu.roll(cur, off, axis=0)
        p = pltpu.roll(prev, off, axis=0)
        head = jnp.where(sub < off, p, r[0:CARRY_ROWS])
        outs.append(jnp.concatenate([head, r[CARRY_ROWS:]], axis=0))
    buf_ref[0:CARRY_ROWS, :] = cur[CHUNK - CARRY_ROWS:CHUNK, :]
    return outs


def _in_proj_kernel(*refs, pieces, zero_lanes, has_small):
    x_ref, g_ref = refs[0], refs[1]
    n = len(pieces)
    wf_refs = refs[2:2 + n]
    w_refs = refs[len(refs) - n:]
    rest = refs[2 + n:len(refs) - n]
    if has_small:
        ws_ref, wr_ref, o_ref, os_ref, or_ref = rest
    else:
        (o_ref,) = rest

    @pl.when(pl.program_id(0) == 0)
    def _():
        for wf_ref, w_ref in zip(wf_refs, w_refs):
            for r0 in range(0, w_ref.shape[0], WEIGHT_CAST_ROWS):
                r1 = min(r0 + WEIGHT_CAST_ROWS, w_ref.shape[0])
                w_ref[r0:r1, :] = _bf(wf_ref[r0:r1, :])

    nt = lambda a, b: lax.dot_general(a, b, (((1,), (1,)), ((), ())), preferred_element_type=F32)
    tm = x_ref.shape[0]
    for lo, hi in zero_lanes:
        o_ref[:, lo:hi] = jnp.zeros((tm, hi - lo), F32)
    for half in range(2):
        rs = slice(half * (tm // 2), (half + 1) * (tm // 2))
        x = x_ref[rs, :]
        h = x * lax.rsqrt(jnp.mean(x * x, axis=-1, keepdims=True) + NORM_EPS) * g_ref[...]
        hb = _bf(h)
        for w_ref, (_, n_rows, lane_off) in zip(w_refs, pieces):
            o_ref[rs, lane_off:lane_off + n_rows] = nt(hb, w_ref[...])
        if has_small:
            os_ref[rs, :] = nt(hb, _bf(ws_ref[...]))
            or_ref[:, rs] = nt(_bf(wr_ref[...]), hb)


def _in_proj(x2, g, w_t, pieces, width, zero_lanes=(), w_small=None, w_rows=None):
    T = x2.shape[0]
    w_rows_total = sum(n_rows for _, n_rows, _ in pieces)
    tm = max(t for t in ROW_TILES_IN
             if (w_rows_total * D_MODEL * (4 + 2)
                 + 2 * (t * D_MODEL * 4 + t * (width + LANES) * 4)) <= VMEM_BUDGET_PROJ)
    const2 = lambda shape: pl.BlockSpec(shape, lambda i: (0, 0))
    in_specs = [pl.BlockSpec((tm, D_MODEL), lambda i: (i, 0)), const2((1, D_MODEL))]
    args = [x2, g]
    for off, n_rows, _ in pieces:
        in_specs.append(pl.BlockSpec((pl.Element(n_rows), pl.Element(D_MODEL)), lambda i, off=off: (off, 0),
                                     pipeline_mode=pl.Buffered(1)))
        args.append(w_t)
    scratch = [pltpu.VMEM((n_rows, D_MODEL), BF16) for _, n_rows, _ in pieces]
    out_shape = [jax.ShapeDtypeStruct((T, width), F32)]
    out_specs = [pl.BlockSpec((tm, width), lambda i: (i, 0))]
    if w_small is not None:
        r = w_rows.shape[0]
        in_specs += [const2((LANES, D_MODEL)), const2((r, D_MODEL))]
        args += [w_small, w_rows]
        out_shape += [jax.ShapeDtypeStruct((T, LANES), F32), jax.ShapeDtypeStruct((r, T), F32)]
        out_specs += [pl.BlockSpec((tm, LANES), lambda i: (i, 0)), pl.BlockSpec((r, tm), lambda i: (0, i))]
    res = pl.pallas_call(
        functools.partial(_in_proj_kernel, pieces=tuple(pieces), zero_lanes=tuple(zero_lanes),
                          has_small=w_small is not None),
        grid=(T // tm,), in_specs=in_specs, out_specs=out_specs, out_shape=out_shape,
        scratch_shapes=scratch,
        compiler_params=pltpu.CompilerParams(dimension_semantics=("arbitrary",),
                                             vmem_limit_bytes=VMEM_LIMIT_IN_PROJ),
        name="in_proj")(*args)
    return res if w_small is not None else res[0]


def _out_proj_kernel(*refs, final):
    if final:
        x_ref, y0, y1, y2, y3, wf_ref, g_ref, o_ref, w_ref = refs
    else:
        x_ref, y0, y1, y2, y3, wf_ref, o_ref, w_ref = refs

    @pl.when(pl.program_id(0) == 0)
    def _():
        for r0 in range(0, D_MODEL, WEIGHT_CAST_ROWS):
            w_ref[r0:r0 + WEIGHT_CAST_ROWS, :] = _bf(wf_ref[r0:r0 + WEIGHT_CAST_ROWS, :])

    hm = x_ref.shape[0] // 2
    for half in range(2):
        rs = slice(half * hm, (half + 1) * hm)
        acc = x_ref[rs, :]
        for gi, y in enumerate((y0, y1, y2, y3)):
            acc = acc + jnp.dot(_bf(y[rs, :]), w_ref[gi * GROUP_W:(gi + 1) * GROUP_W, :],
                                preferred_element_type=F32)
        if final:
            acc = acc * lax.rsqrt(jnp.mean(acc * acc, axis=-1, keepdims=True) + NORM_EPS) * g_ref[...]
        o_ref[rs, :] = acc


def _out_proj(x2, ys, w, g_final=None):
    T = x2.shape[0]
    tm = ROW_TILE_OUT
    final = g_final is not None
    in_specs = [pl.BlockSpec((tm, D_MODEL), lambda i: (i, 0))]
    in_specs += [pl.BlockSpec((tm, GROUP_W), lambda i: (i, 0)) for _ in range(4)]
    in_specs += [pl.BlockSpec((D_MODEL, D_MODEL), lambda i: (0, 0), pipeline_mode=pl.Buffered(1))]
    args = [x2, *ys, w]
    if final:
        in_specs.append(pl.BlockSpec((1, D_MODEL), lambda i: (0, 0)))
        args.append(g_final)
    return pl.pallas_call(
        functools.partial(_out_proj_kernel, final=final),
        grid=(T // tm,), in_specs=in_specs,
        out_specs=pl.BlockSpec((tm, D_MODEL), lambda i: (i, 0)),
        out_shape=jax.ShapeDtypeStruct((T, D_MODEL), F32),
        scratch_shapes=[pltpu.VMEM((D_MODEL, D_MODEL), BF16)],
        compiler_params=pltpu.CompilerParams(dimension_semantics=("arbitrary",),
                                             vmem_limit_bytes=VMEM_LIMIT_IN_PROJ),
        name="out_proj")(*args)


GLA_SLAB = 1536


BATCH_BLOCK = 4
MIXER_STARTS = (5, 5, 12)


def _lockstep(gens, starts=None):
    gens = list(gens)
    starts = [0] * len(gens) if starts is None else list(starts)
    pending = list(zip(starts, gens))
    tick = 0
    while pending:
        alive = []
        for start, g in pending:
            if start > tick:
                alive.append((start, g))
                continue
            try:
                next(g)
                alive.append((start, g))
            except StopIteration:
                pass
        pending = alive
        tick += 1


def _stepper(gens):
    gens = list(gens)
    while gens:
        alive = []
        for g in gens:
            try:
                next(g)
                alive.append(g)
            except StopIteration:
                pass
        gens = alive
        yield


def _mixers_kernel(*refs, parts):
    n_in = sum(p[1] for p in parts)
    n_out = sum(p[2] for p in parts)
    ins, outs, scs = refs[:n_in], refs[n_in:n_in + n_out], refs[n_in + n_out:]
    gens, starts, posts = [], [], []
    i = o = s = 0
    for stages_fn, ni, no, ns, start in parts:
        g, post = stages_fn(*ins[i:i + ni], *outs[o:o + no], *scs[s:s + ns])
        i, o, s = i + ni, o + no, s + ns
        gens += g
        starts += [start] * len(g)
        posts.append(post)
    _lockstep(gens, starts)
    for post in posts:
        if post is not None:
            post()


def _run_mixers(descs, B, S, name):
    nb = BATCH_BLOCK
    parts = tuple((d['stages'], len(d['args']), len(d['out_shape']), len(d['scratch']), d.get('start', 0))
                  for d in descs)
    res = pl.pallas_call(
        functools.partial(_mixers_kernel, parts=parts), grid=(B // nb, S // CHUNK),
        in_specs=[sp for d in descs for sp in d['in_specs']],
        out_specs=[sp for d in descs for sp in d['out_specs']],
        out_shape=[sh for d in descs for sh in d['out_shape']],
        scratch_shapes=[sc for d in descs for sc in d['scratch']],
        compiler_params=pltpu.CompilerParams(dimension_semantics=("arbitrary", "arbitrary"),
                                             vmem_limit_bytes=VMEM_LIMIT_PROJ),
        name=name)(*[a for d in descs for a in d['args']])
    out, k = [], 0
    for d in descs:
        out.append(res[k:k + len(d['out_shape'])])
        k += len(d['out_shape'])
    return out


def _gla_stages(slab_ref, small_ref, w2_ref, gkb_ref, ng_ref, tri_ref, o_ref, st_ref, *, nb):
    @pl.when(pl.program_id(1) == 0)
    def _():
        st_ref[...] = jnp.zeros_like(st_ref)

    causal = _causal(CHUNK)
    masks = _lane_half_masks()

    def head(b, h, qg, kg, kd, dec):
        p, j = divmod(h, 2)
        ls = slice(p * LANES, (p + 1) * LANES)
        qm = qg[:, ls] * masks[j]
        att = jnp.where(causal, _mm_nt(qm, kg[:, ls]), 0.0)
        yield
        v_h = slab_ref[b, :, 512 + h * GLA_DV:512 + (h + 1) * GLA_DV]
        st = st_ref[b, h]
        o = _mm(att, v_h) + _mm_nt(qm, st)
        st_ref[b, h] = st * dec[:, ls] + _mm_tn(v_h, kd[:, ls] * masks[j])
        yield
        ms = jnp.mean(o * o, axis=-1, keepdims=True)
        yield
        o = o * lax.rsqrt(ms + NORM_EPS)
        o = o * ng_ref[:, h * GLA_DV:(h + 1) * GLA_DV]
        z_h = slab_ref[b, :, 1024 + h * GLA_DV:1024 + (h + 1) * GLA_DV]
        o_ref[b, :, h * GLA_DV:(h + 1) * GLA_DV] = o * _silu(z_h)

    def sequence(b):
        q = slab_ref[b, :, 0:256] * (GLA_DK ** -0.5)
        k = slab_ref[b, :, 256:512]
        gk = _mm(small_ref[b], w2_ref[...]) + gkb_ref[...]
        log_a = _log_sigmoid(gk) / GLA_GATE_NORMALIZER
        cum = _mm_sel_lhs(tri_ref[...], log_a)
        last = cum[CHUNK - 1:CHUNK, :]
        qg = q * jnp.exp(cum)
        kg = k * jnp.exp(-cum)
        kd = k * jnp.exp(last - cum)
        dec = jnp.exp(last)
        yield
        yield from _stepper([head(b, h, qg, kg, kd, dec) for h in range(GLA_HEADS)])

    return [sequence(b) for b in range(nb)], None


def _gla(slab, small, w2p, gkb, ng, tri):
    B, S, _ = slab.shape
    nb = BATCH_BLOCK
    const = lambda shape: pl.BlockSpec(shape, lambda b, c: (0,) * len(shape))
    return dict(
        stages=functools.partial(_gla_stages, nb=nb),
        in_specs=[pl.BlockSpec((nb, CHUNK, GLA_SLAB), lambda b, c: (b, c, 0)),
                  pl.BlockSpec((nb, CHUNK, LANES), lambda b, c: (b, c, 0)),
                  const((LANES, 256)), const((1, 256)), const((1, GROUP_W)), const((CHUNK, CHUNK))],
        args=[slab, small, w2p, gkb, ng, tri],
        out_specs=[pl.BlockSpec((nb, CHUNK, GROUP_W), lambda b, c: (b, c, 0))],
        out_shape=[jax.ShapeDtypeStruct((B, S, GROUP_W), F32)],
        scratch=[pltpu.VMEM((nb, GLA_HEADS, GLA_DV, LANES), F32)])


SSD_SLAB = 1536


def _ssd_stages(slab_ref, small_ref, rows_ref, cw_ref, cb_ref, dtb_ref, a_ref, dtb_col_ref, a_col_ref,
                dskip_ref, ng_ref, tri_ref, triu2_ref, ex_ref, o_ref, buf_ref, st_ref, *, nb):
    @pl.when(pl.program_id(1) == 0)
    def _():
        st_ref[...] = jnp.zeros_like(st_ref)
        for b in range(nb):
            buf_ref[b, 0:CARRY_ROWS, :] = jnp.zeros((CARRY_ROWS, SSD_XBC), F32)

    row_i = lax.broadcasted_iota(jnp.int32, (CHUNK, LANES), 0)
    col_i = lax.broadcasted_iota(jnp.int32, (CHUNK, LANES), 1)
    causal2 = jnp.bitwise_and(col_i, CHUNK - 1) <= row_i
    masks = _lane_half_masks()
    masks_b = (_bf(masks[0]), _bf(masks[1]))

    def pair(b, p, xbc, cum_b, cum_row2, xdt, xw, ecum, dec):
        g = p // 2
        ls = slice(p * LANES, (p + 1) * LANES)
        bm = xbc[:, 512 + g * SSD_STATE:512 + (g + 1) * SSD_STATE]
        cm = xbc[:, 768 + g * SSD_STATE:768 + (g + 1) * SSD_STATE]
        bmb = _bf(bm)
        cbm2 = _mm_nt(cm, jnp.concatenate([bmb, bmb], axis=0))
        st = st_ref[b, p]
        y = _mm(cm, st) * ecum[:, ls]
        st_ref[b, p] = st * dec[:, ls] + _mm_tn(bmb, xw[:, ls])
        lmat = jnp.exp(jnp.where(causal2, cum_b[:, ls] - cum_row2[p:p + 1, :], -jnp.inf))
        yield
        xb = _bf(xdt[:, ls])
        xs = jnp.concatenate([xb * masks_b[0], xb * masks_b[1]], axis=0)
        y = y + _mm(cbm2 * lmat, xs)
        yield
        y = y + dskip_ref[:, ls] * xbc[:, ls]
        o_ref[b, :, ls] = y * _silu(slab_ref[b, :, 1024 + p * LANES:1024 + (p + 1) * LANES])

    def sequence(b):
        taps = _shifted_rows(buf_ref.at[b], slab_ref[b, :, 0:SSD_XBC], (3, 2, 1, 0))
        xbc = cb_ref[...]
        for j in range(SSD_CONV):
            xbc = xbc + taps[j] * cw_ref[j:j + 1, :]
        xbc = _silu(xbc)
        yield
        dt_col = _softplus(small_ref[b] + dtb_ref[...])
        cum_col = _mm_sel_lhs(tri_ref[...], dt_col * a_ref[...])
        dt_row = _softplus(rows_ref[b, 0, ROWS_DT:ROWS_DT + SSD_HEADS, :] + dtb_col_ref[...])
        da_row = dt_row * a_col_ref[...]
        cum_row2 = (_mm_sel_rhs(da_row[0:SSD_HEADS // 2], triu2_ref[0])
                    + _mm_sel_rhs(da_row[SSD_HEADS // 2:SSD_HEADS], triu2_ref[1]))
        dt_b = _mm_sel_rhs(dt_col, ex_ref[...])
        cum_b = _mm_sel_rhs(cum_col, ex_ref[...])
        last_b = cum_b[CHUNK - 1:CHUNK, :]
        xdt = xbc[:, 0:512] * dt_b
        xw = xdt * jnp.exp(last_b - cum_b)
        ecum = jnp.exp(cum_b)
        dec = jnp.exp(last_b)
        yield
        yield from _stepper([pair(b, p, xbc, cum_b, cum_row2, xdt, xw, ecum, dec)
                             for p in range(SSD_HEADS // 2)])

    gens = [sequence(b) for b in range(nb)]

    def group_norm():
        for b in range(nb):
            y = o_ref[b]
            inv = lax.rsqrt(_rowsum(y * y) * (1.0 / GROUP_W) + NORM_EPS)
            for p in range(GROUP_W // LANES):
                ls = slice(p * LANES, (p + 1) * LANES)
                o_ref[b, :, ls] = y[:, ls] * inv * ng_ref[:, ls]

    return gens, group_norm


def _ssd(slab, small, rows, cw, cb, dtb, a, dtb_col, a_col, dskip, ng, tri, triu, ex):
    B, S, _ = slab.shape
    nb = BATCH_BLOCK
    const = lambda shape: pl.BlockSpec(shape, lambda b, c: (0,) * len(shape))
    return dict(
        stages=functools.partial(_ssd_stages, nb=nb),
        in_specs=[pl.BlockSpec((nb, CHUNK, SSD_SLAB), lambda b, c: (b, c, 0)),
                  pl.BlockSpec((nb, CHUNK, LANES), lambda b, c: (b, c, 0)),
                  pl.BlockSpec((nb, 1, ROWS_TOTAL, CHUNK), lambda b, c: (b, c, 0, 0)),
                  const((SSD_CONV, SSD_XBC)), const((1, SSD_XBC)), const((1, LANES)), const((1, LANES)),
                  const((SSD_HEADS, 1)), const((SSD_HEADS, 1)), const((1, GROUP_W)), const((1, GROUP_W)),
                  const((CHUNK, CHUNK)), const((2, CHUNK, LANES)), const((LANES, GROUP_W))],
        args=[slab, small, rows, cw, cb, dtb, a, dtb_col, a_col, dskip, ng, tri, triu, ex],
        out_specs=[pl.BlockSpec((nb, CHUNK, GROUP_W), lambda b, c: (b, c, 0))],
        out_shape=[jax.ShapeDtypeStruct((B, S, GROUP_W), F32)],
        scratch=[pltpu.VMEM((nb, CARRY_ROWS + CHUNK, SSD_XBC), F32),
                 pltpu.VMEM((nb, SSD_HEADS // 2, SSD_STATE, LANES), F32)])


MLSTM_SLAB = 2560


def _mlstm_stages(slab_ref, small_ref, rows_ref, cw_ref, cb_ref, igb_ref, fgb_ref, gb_col_ref, ng_ref,
                  tri_ref, triu_ref, exi_ref, exf_ref, o_ref, buf_ref, c_ref, nm_ref, *, nb):
    @pl.when(pl.program_id(1) == 0)
    def _():
        c_ref[...] = jnp.zeros_like(c_ref)
        nm_ref[...] = jnp.zeros_like(nm_ref)
        for b in range(nb):
            buf_ref[b, 0:CARRY_ROWS, :] = jnp.zeros((CARRY_ROWS, 2 * GROUP_W), F32)

    causal = _causal(CHUNK)

    def head(b, h, qk, li_b, ci_b, logi_row, cum_row):
        ls = slice(h * MLSTM_HEAD, (h + 1) * MLSTM_HEAD)
        q = qk[:, ls]
        k = qk[:, GROUP_W + h * MLSTM_HEAD:GROUP_W + (h + 1) * MLSTM_HEAD] * (MLSTM_HEAD ** -0.5)
        v = slab_ref[b, :, 1024 + h * MLSTM_HEAD:1024 + (h + 1) * MLSTM_HEAD]
        ci = ci_b[:, ls]
        li = li_b[:, ls]
        cr = cum_row[h:h + 1, :]
        lir = logi_row[h:h + 1, :]
        last = ci[CHUNK - 1:CHUNK, :]
        c_prev = c_ref[b, h]
        n_prev = nm_ref[b, h, 0:1, :]
        m_prev = nm_ref[b, h, 1:2, :]

        g = last - ci + li
        g_max = jnp.max(g, axis=0, keepdims=True)
        log_d = jnp.where(causal, ci[:, 0:CHUNK] - cr + lir, -jnp.inf)
        row_max = jnp.max(log_d, axis=-1, keepdims=True)
        qk_h = _mm_nt(q, k)
        qc = _mm(q, c_prev)
        qn = _rowsum(q * n_prev)
        yield
        kw = k * jnp.exp(g - g_max)
        c_loc = _mm_tn(kw, v)
        n_loc = jnp.sum(kw, axis=0, keepdims=True)
        m_new = jnp.maximum(last + m_prev, g_max)
        a_old = jnp.exp(last + m_prev - m_new)
        a_new = jnp.exp(g_max - m_new)
        c_ref[b, h] = a_old * c_prev + a_new * c_loc
        nm_ref[b, h, 0:1, :] = a_old * n_prev + a_new * n_loc
        nm_ref[b, h, 1:2, :] = m_new
        m_inter = ci + m_prev
        m_l = jnp.maximum(m_inter, row_max)
        wqk = qk_h * jnp.exp(log_d - m_l[:, 0:CHUNK])
        w_inter = jnp.exp(m_inter - m_l)
        num = _mm(wqk, v) + w_inter * qc
        den = _rowsum(wqk) + w_inter * qn
        yield
        den = jnp.maximum(jnp.abs(den), jnp.exp(-m_l))
        hh = num / den * _sigmoid(slab_ref[b, :, 1536 + h * MLSTM_HEAD:1536 + (h + 1) * MLSTM_HEAD])
        mu = _rowsum(hh) * (1.0 / MLSTM_HEAD)
        yield
        yc = hh - mu
        var = _rowsum(yc * yc) * (1.0 / MLSTM_HEAD)
        yield
        hh = yc * lax.rsqrt(var + NORM_EPS) * ng_ref[:, ls]
        o_ref[b, :, ls] = hh * _silu(slab_ref[b, :, 2048 + h * MLSTM_HEAD:2048 + (h + 1) * MLSTM_HEAD])

    def sequence(b):
        taps = _shifted_rows(buf_ref.at[b], slab_ref[b, :, 0:2 * GROUP_W], (3, 2, 1, 0))
        qk = cb_ref[...]
        for j in range(MLSTM_CONV):
            qk = qk + taps[j] * cw_ref[j:j + 1, :]
        qk = _silu(qk)
        yield
        logi_col = small_ref[b] + igb_ref[...]
        logf_col = _log_sigmoid(small_ref[b] + fgb_ref[...])
        cum_col = _mm_sel_lhs(tri_ref[...], logf_col)
        li_b = _mm_sel_rhs(logi_col, exi_ref[...])
        ci_b = _mm_sel_rhs(cum_col, exf_ref[...])
        pre_row = rows_ref[b, 0, ROWS_IF:ROWS_IF + 2 * MLSTM_HEADS, :] + gb_col_ref[...]
        logi_row = pre_row[0:MLSTM_HEADS, :]
        logf_row = _log_sigmoid(pre_row[MLSTM_HEADS:2 * MLSTM_HEADS, :])
        cum_row = _mm_sel_rhs(logf_row, triu_ref[...])
        yield
        yield from _stepper([head(b, h, qk, li_b, ci_b, logi_row, cum_row) for h in range(MLSTM_HEADS)])

    return [sequence(b) for b in range(nb)], None


def _mlstm(slab, small, rows, cw, cb, igb, fgb, gb_col, ng, tri, triu, exi, exf):
    B, S, _ = slab.shape
    nb = BATCH_BLOCK
    const = lambda shape: pl.BlockSpec(shape, lambda b, c: (0,) * len(shape))
    return dict(
        stages=functools.partial(_mlstm_stages, nb=nb),
        in_specs=[pl.BlockSpec((nb, CHUNK, MLSTM_SLAB), lambda b, c: (b, c, 0)),
                  pl.BlockSpec((nb, CHUNK, LANES), lambda b, c: (b, c, 0)),
                  pl.BlockSpec((nb, 1, ROWS_TOTAL, CHUNK), lambda b, c: (b, c, 0, 0)),
                  const((MLSTM_CONV, 2 * GROUP_W)), const((1, 2 * GROUP_W)), const((1, LANES)),
                  const((1, LANES)), const((2 * MLSTM_HEADS, 1)), const((1, GROUP_W)),
                  const((CHUNK, CHUNK)), const((CHUNK, CHUNK)),
                  const((LANES, GROUP_W)), const((LANES, GROUP_W))],
        args=[slab, small, rows, cw, cb, igb, fgb, gb_col, ng, tri, triu, exi, exf],
        out_specs=[pl.BlockSpec((nb, CHUNK, GROUP_W), lambda b, c: (b, c, 0))],
        out_shape=[jax.ShapeDtypeStruct((B, S, GROUP_W), F32)],
        scratch=[pltpu.VMEM((nb, CARRY_ROWS + CHUNK, 2 * GROUP_W), F32),
                 pltpu.VMEM((nb, MLSTM_HEADS, MLSTM_HEAD, MLSTM_HEAD), F32),
                 pltpu.VMEM((nb, MLSTM_HEADS, CARRY_ROWS, MLSTM_HEAD), F32)])


def _rwkv_slab_width(layer):
    return 2176 if layer == 0 else 2304


def _rwkv_shift_cols(layer):
    return 1664 if layer == 0 else 1792


def _rwkv_stages(*refs, layer, nb):
    if layer == 0:
        (slab_ref, mu_ref, w0_ref, a0_ref, w2a2_ref, kk_ref, ka_ref, rk_ref, lng_ref, lnb_ref,
         tri_ref, bd_ref, strict_ref, incl_ref, lvl_ref, o_ref, vf_out_ref, buf_ref, st_ref) = refs
    else:
        (slab_ref, vf_ref, mu_ref, w0_ref, a0_ref, w2a2_ref, v0_ref, v2_ref, kk_ref, ka_ref, rk_ref,
         lng_ref, lnb_ref, tri_ref, bd_ref, strict_ref, incl_ref, lvl_ref, o_ref, buf_ref, st_ref) = refs
    ws = _rwkv_shift_cols(layer)
    zoff = _rwkv_slab_width(layer) - GROUP_W
    npair = GROUP_W // LANES

    @pl.when(pl.program_id(1) == 0)
    def _():
        st_ref[...] = jnp.zeros_like(st_ref)
        for b in range(nb):
            buf_ref[b, 0:CARRY_ROWS, :] = jnp.zeros((CARRY_ROWS, ws), F32)

    lane = lax.broadcasted_iota(jnp.int32, (1, LANES), 1)
    masks = _lane_half_masks()
    bd = bd_ref[...]
    strict = strict_ref[...] > 0.5
    incl = incl_ref[...] > 0.5
    masks_b = (_bf(masks[0]), _bf(masks[1]))

    def rows2(t):
        tb = _bf(t)
        return jnp.concatenate([tb * masks_b[0], tb * masks_b[1]], axis=0)

    eye = (lax.broadcasted_iota(jnp.int32, (LANES, LANES), 0)
           == lax.broadcasted_iota(jnp.int32, (LANES, LANES), 1)).astype(F32)

    inv = 1.0 / RWKV_HEAD

    def sequence(b):
        inst = []
        f = slab_ref[b, :, 0:ws]
        (prev,) = _shifted_rows(buf_ref.at[b], f, (1,))
        f = f + mu_ref[...] * (prev - f)
        r = f[:, 0:512]
        k = f[:, 512:1024]
        v = f[:, 1024:1536]
        lora = f[:, 1536:1664]
        lora = jnp.where(lane < RWKV_W_RANK, jnp.tanh(lora), lora)
        wa = _mm(lora, w2a2_ref[...])
        if layer == 0:
            vf_out_ref[b] = v
        else:
            mix = _sigmoid(v0_ref[...] + _mm(f[:, 1664:1792], v2_ref[...]))
            v = v + (vf_ref[b] - v) * mix
        yield
        w_log = -_softplus(-(w0_ref[...] + wa[:, 0:512])) - RWKV_DECAY_OFFSET
        lw = -jnp.exp(w_log)
        a = _sigmoid(a0_ref[...] + wa[:, 512:1024])
        kk = k * kk_ref[...]
        k = k * (1.0 + (a - 1.0) * ka_ref[...])
        cum = _mm_sel_lhs(tri_ref[...], lw)
        ss_all = _mm_sel2(jnp.concatenate([kk[:, p * LANES:(p + 1) * LANES] ** 2 for p in range(npair)],
                                          axis=0), bd)
        ss = [ss_all[p * CHUNK:(p + 1) * CHUNK] for p in range(npair)]
        yield
        last = cum[CHUNK - 1:CHUNK, :]
        e_pos = jnp.exp(cum)
        e_neg = jnp.exp(-cum)
        e_end = jnp.exp(last - cum)
        e_prev = jnp.exp(cum - lw)
        gam = jnp.exp(last)
        for p in range(npair):
            ls = slice(p * LANES, (p + 1) * LANES)
            kk_p = kk[:, ls] / jnp.maximum(jnp.sqrt(ss[p]), 1e-12)
            k_p, r_p, v_p = k[:, ls], r[:, ls], v[:, ls]
            b_p = kk_p * a[:, ls]
            inst.append(dict(
                p=p, ls=ls, r=r_p, k=k_p, v=v_p, gam=gam[:, ls],
                la=rows2(-kk_p * e_prev[:, ls]), lr=rows2(r_p * e_pos[:, ls]),
                rb=rows2(b_p * e_neg[:, ls]), rk=rows2(k_p * e_neg[:, ls]),
                bh=rows2(b_p * e_end[:, ls]), kh=rows2(k_p * e_end[:, ls]), vs=rows2(v_p)))
        yield
        for d in inst:
            aa = _mm_nt(jnp.concatenate([d['la'], d['lr']], axis=0),
                        jnp.concatenate([d['rb'], d['rk']], axis=0))
            d['nab'] = _bf(jnp.where(strict, aa[0:LANES, 0:LANES], 0.0))
            d['aak'] = _bf(jnp.where(strict, aa[0:LANES, LANES:2 * LANES], 0.0))
            d['arb'] = _bf(jnp.where(incl, aa[LANES:2 * LANES, 0:LANES], 0.0))
            d['ark'] = _bf(jnp.where(incl, aa[LANES:2 * LANES, LANES:2 * LANES], 0.0))
            d['t'] = eye + (d['nab'] * lvl_ref[0]).astype(F32)
        yield
        for d in inst:
            xv = _mm(jnp.concatenate([d['aak'], d['ark']], axis=0), d['vs'])
            d['x'], d['arkv'] = xv[0:LANES], xv[LANES:2 * LANES]
        for lv in range(1, 6):
            for d in inst:
                d['tb'] = _bf(d['t'])
                d['nt'] = _mm(d['nab'] * lvl_ref[lv], d['tb'])
            yield
            for d in inst:
                d['t'] = d['t'] + _mm(d['tb'], d['nt'])
            yield
        for d in inst:
            d['wu'] = _mm(d['t'], jnp.concatenate([d['la'], _bf(d['x'])], axis=1))
        yield
        for d in inst:
            d['wub'] = _bf(d['wu'])
            qy = _mm(d['arb'], d['wub'])
            d['qt'] = d['lr'].astype(F32) + qy[:, 0:LANES]
            d['y0'] = d['arkv'] + qy[:, LANES:2 * LANES]
        yield
        for d in inst:
            st = st_ref[b, d['p']]
            uy = _mm_nt(jnp.concatenate([d['wub'][:, 0:LANES], _bf(d['qt'])], axis=0), st)
            d['ust'] = uy[0:LANES] + d['wu'][:, LANES:2 * LANES]
            d['st'] = st
            yst = uy[LANES:2 * LANES] + d['y0']
            d['y'] = yst[0:CHUNK] + yst[CHUNK:2 * CHUNK]
        yield
        for d in inst:
            st_ref[b, d['p']] = d['st'] * d['gam'] + _mm_tn(jnp.concatenate([_bf(d['ust']), d['vs']], axis=0),
                                                            jnp.concatenate([d['bh'], d['kh']], axis=0))
            sums = _mm_sel2(jnp.concatenate([d['y'], d['r'] * d['k'] * rk_ref[:, d['ls']]], axis=0), bd)
            d['mu'] = sums[0:CHUNK] * inv
            d['bonus'] = sums[CHUNK:2 * CHUNK] * d['v']
        yield
        for d in inst:
            d['yc'] = d['y'] - d['mu']
            d['var'] = _mm_sel2(d['yc'] * d['yc'], bd) * inv
        yield
        for d in inst:
            ls = d['ls']
            yn = d['yc'] * lax.rsqrt(d['var'] + RWKV_LN_EPS) * lng_ref[:, ls] + lnb_ref[:, ls]
            z_p = slab_ref[b, :, zoff + d['p'] * LANES:zoff + (d['p'] + 1) * LANES]
            o_ref[b, :, ls] = (yn + d['bonus']) * _silu(z_p)

    return [sequence(b) for b in range(nb)], None


def _rwkv(layer, slab, vf, mu, w0, a0, w2a2, v0, v2, kkw, ka, rk, lng, lnb, tri, bd, strict, incl, lvl):
    B, S, W = slab.shape
    nb = BATCH_BLOCK
    ws = _rwkv_shift_cols(layer)
    const = lambda shape: pl.BlockSpec(shape, lambda b, c: (0,) * len(shape))
    tok = lambda w: pl.BlockSpec((nb, CHUNK, w), lambda b, c: (b, c, 0))
    vecw = const((1, GROUP_W))
    in_specs = [tok(W)]
    args = [slab]
    if layer > 0:
        in_specs.append(tok(GROUP_W))
        args.append(vf)
    in_specs += [const((1, ws)), vecw, vecw, const((LANES, 2 * GROUP_W))]
    args += [mu, w0, a0, w2a2]
    if layer > 0:
        in_specs += [vecw, const((LANES, GROUP_W))]
        args += [v0, v2]
    in_specs += [vecw, vecw, vecw, vecw, vecw, const((CHUNK, CHUNK)), const((LANES, LANES)),
                 const((LANES, LANES)), const((LANES, LANES)), const((6, LANES, LANES))]
    args += [kkw, ka, rk, lng, lnb, tri, bd, strict, incl, lvl]
    out_shape = [jax.ShapeDtypeStruct((B, S, GROUP_W), F32)]
    out_specs = [tok(GROUP_W)]
    if layer == 0:
        out_shape.append(jax.ShapeDtypeStruct((B, S, GROUP_W), F32))
        out_specs.append(tok(GROUP_W))
    return dict(
        stages=functools.partial(_rwkv_stages, layer=layer, nb=nb),
        in_specs=in_specs, args=args, out_specs=out_specs, out_shape=out_shape,
        scratch=[pltpu.VMEM((nb, CARRY_ROWS + CHUNK, ws), F32),
                 pltpu.VMEM((nb, GROUP_W // LANES, LANES, LANES), F32)])


def _np_consts():
    i = np.arange(CHUNK)
    tri = (i[None, :] <= i[:, None]).astype(np.float32)
    t = np.arange(LANES)
    same = (t[:, None] // CHUNK) == (t[None, :] // CHUNK)
    strict = (same & (t[None, :] < t[:, None])).astype(np.float32)
    incl = (same & (t[None, :] <= t[:, None])).astype(np.float32)
    bd = same.astype(np.float32)
    lvl = np.stack([(((t[:, None] >> l) == (t[None, :] >> l))
                     & ((t[:, None] >> (l - 1)) != (t[None, :] >> (l - 1)))).astype(np.float32)
                    for l in range(1, 7)])
    ex = np.zeros((LANES, GROUP_W), np.float32)
    for h in range(SSD_HEADS):
        ex[SMALL_DT + h, h * SSD_HEADDIM:(h + 1) * SSD_HEADDIM] = 1.0
    exi = np.zeros((LANES, GROUP_W), np.float32)
    exf = np.zeros((LANES, GROUP_W), np.float32)
    for h in range(MLSTM_HEADS):
        exi[SMALL_I + h, h * MLSTM_HEAD:(h + 1) * MLSTM_HEAD] = 1.0
        exf[SMALL_F + h, h * MLSTM_HEAD:(h + 1) * MLSTM_HEAD] = 1.0
    triu = tri.T.copy()
    zero = np.zeros_like(triu)
    triu2 = np.stack([np.concatenate([triu, zero], axis=1), np.concatenate([zero, triu], axis=1)])
    return dict(tri=tri, triu=triu, triu2=triu2, strict=strict, incl=incl, bd=bd, lvl=lvl, ex=ex,
                exi=exi, exf=exf)


def _pad_rows(w, height):
    return jnp.pad(w, ((0, height - w.shape[0]), (0, 0)))


def _pad_vec(v, width, offset=0):
    v = v.reshape(1, -1)
    return jnp.pad(v, ((0, 0), (offset, width - offset - v.shape[1])))


F32_ROW_TILE = 8


def _layer(x2, v_first, layer, p, B, S, consts, g_final):
    sw = 3 * GROUP_W + RWKV_W_RANK + RWKV_A_RANK + (RWKV_V_RANK if layer > 0 else 0)
    names = ['gla_q', 'gla_k', 'gla_v', 'gla_gk', 'gla_z', 'rwkv_shift', 'rwkv_z', 'ssd_xbc', 'ssd_dt',
             'ssd_z', 'mlstm_qk', 'mlstm_v', 'mlstm_i', 'mlstm_f', 'mlstm_o', 'mlstm_z']
    widths = [256, 256, 512, 16, 512, sw, 512, SSD_XBC, SSD_HEADS, 512, 1024, 512, 4, 4, 512, 512]
    col0 = dict(zip(names, np.concatenate([[0], np.cumsum(widths)[:-1]]).tolist()))
    w_t = p['w_in'].T
    off = lambda name: col0[name]
    assert all(off(n) % F32_ROW_TILE == 0 for n in names if n != 'mlstm_f')
    rows = lambda name, n: w_t[off(name):off(name) + n, :]

    g = p['norm_g'].reshape(1, D_MODEL)
    tri, triu = consts['tri'], consts['triu']
    rows_of = lambda t: t.reshape(t.shape[0], B, S // CHUNK, CHUNK).transpose(1, 2, 0, 3)

    w_small = _pad_rows(jnp.concatenate([rows('gla_gk', GLA_RANK), rows('ssd_dt', SSD_HEADS),
                                         rows('mlstm_i', 2 * MLSTM_HEADS)], axis=0), LANES)
    even_odd = np.concatenate([np.arange(0, SSD_HEADS, 2), np.arange(1, SSD_HEADS, 2)])
    w_rows = jnp.concatenate([rows('ssd_dt', SSD_HEADS)[even_odd], rows('mlstm_i', 2 * MLSTM_HEADS)], axis=0)
    mu = p['rwkv_mu']
    if layer == 0:
        pieces, zero_lanes = [(off('rwkv_shift'), sw + GROUP_W, 0)], ()
        mu_p = mu.reshape(1, -1)
    else:
        pieces = [(off('rwkv_shift'), sw, 0), (off('rwkv_z'), GROUP_W, 1792)]
        zero_lanes = ((1664, 1792),)
        mu_p = _pad_vec(mu, 1792)
    slab, small, gate_rows = _in_proj(x2, g, w_t, pieces, _rwkv_slab_width(layer), zero_lanes,
                                      w_small, w_rows)
    slab = slab.reshape(B, S, _rwkv_slab_width(layer))
    small = small.reshape(B, S, LANES)
    gate_rows = rows_of(gate_rows)
    w2a2 = jnp.zeros((LANES, 2 * GROUP_W), F32)
    w2a2 = w2a2.at[0:RWKV_W_RANK, 0:GROUP_W].set(p['rwkv_w2'])
    w2a2 = w2a2.at[RWKV_W_RANK:, GROUP_W:].set(p['rwkv_a2'])
    vec = lambda t: t.reshape(1, GROUP_W)
    v0 = v2 = None
    if layer > 0:
        v0 = vec(p['rwkv_v0'])
        v2 = _bf(jnp.pad(p['rwkv_v2'], ((0, LANES - RWKV_V_RANK), (0, 0))))
    d_rwkv = _rwkv(layer, slab, v_first, mu_p, vec(p['rwkv_w0']), vec(p['rwkv_a0']), _bf(w2a2),
                            v0, v2, vec(p['rwkv_k_k']), vec(p['rwkv_k_a']), vec(p['rwkv_r_k']),
                            vec(p['rwkv_ln_g']), vec(p['rwkv_ln_b']), _bf(tri), _bf(consts['bd']),
                            consts['strict'], consts['incl'], _bf(consts['lvl']))

    slab = _in_proj(x2, g, w_t, [(off('gla_q'), 1024, 0), (off('gla_z'), GROUP_W, 1024)], GLA_SLAB)
    w2p = _bf(jnp.pad(p['gla_gk_w2'], ((SMALL_GK, LANES - SMALL_GK - GLA_RANK), (0, 0))))
    d_gla = _gla(slab.reshape(B, S, GLA_SLAB), small, w2p, p['gla_gk_b'].reshape(1, -1),
                 p['gla_norm_g'].reshape(1, -1), _bf(tri))

    slab = _in_proj(x2, g, w_t, [(off('ssd_xbc'), SSD_XBC, 0), (off('ssd_z'), GROUP_W, SSD_XBC)], SSD_SLAB)
    a_neg = -jnp.exp(p['ssd_a_log'])
    d_ssd = _ssd(slab.reshape(B, S, SSD_SLAB), small, gate_rows, p['ssd_conv_w'],
                 p['ssd_conv_b'].reshape(1, -1),
                 _pad_vec(p['ssd_dt_bias'], LANES, SMALL_DT), _pad_vec(a_neg, LANES, SMALL_DT),
                 p['ssd_dt_bias'][even_odd].reshape(-1, 1), a_neg[even_odd].reshape(-1, 1),
                 jnp.repeat(p['ssd_d'], SSD_HEADDIM).reshape(1, -1), p['ssd_norm_g'].reshape(1, -1),
                 _bf(tri), _bf(consts['triu2']), _bf(consts['ex']))

    slab = _in_proj(x2, g, w_t, [(off('mlstm_qk'), 3 * GROUP_W, 0), (off('mlstm_o'), 2 * GROUP_W, 3 * GROUP_W)],
                    MLSTM_SLAB)
    gb_col = jnp.concatenate([p['mlstm_ig_b'], p['mlstm_fg_b']]).reshape(-1, 1)
    d_ml = _mlstm(slab.reshape(B, S, MLSTM_SLAB), small, gate_rows, p['mlstm_conv_w'],
                  p['mlstm_conv_b'].reshape(1, -1), _pad_vec(p['mlstm_ig_b'], LANES, SMALL_I),
                  _pad_vec(p['mlstm_fg_b'], LANES, SMALL_F), gb_col, p['mlstm_norm_g'].reshape(1, -1),
                  _bf(tri), _bf(triu), _bf(consts['exi']), _bf(consts['exf']))

    T = B * S
    d_ml['start'], d_ssd['start'], d_gla['start'] = MIXER_STARTS
    o_rwkv, o_ml, o_ssd, o_gla = _run_mixers([d_rwkv, d_ml, d_ssd, d_gla], B, S, "mixers")
    if layer == 0:
        v_first = o_rwkv[1]
    ys = [y.reshape(T, GROUP_W) for y in (o_gla[0], o_rwkv[0], o_ssd[0], o_ml[0])]
    return _out_proj(x2, ys, p['w_out'], g_final), v_first


_PARAM_NAMES_0 = ['norm_g', 'w_in', 'w_out', 'gla_gk_w2', 'gla_gk_b', 'gla_norm_g', 'rwkv_mu', 'rwkv_w0',
                  'rwkv_w2', 'rwkv_a0', 'rwkv_a2', 'rwkv_k_k', 'rwkv_k_a', 'rwkv_r_k', 'rwkv_ln_g',
                  'rwkv_ln_b', 'ssd_conv_w', 'ssd_conv_b', 'ssd_dt_bias', 'ssd_a_log', 'ssd_d',
                  'ssd_norm_g', 'mlstm_conv_w', 'mlstm_conv_b', 'mlstm_ig_b', 'mlstm_fg_b', 'mlstm_norm_g']
_PARAM_NAMES_1 = (_PARAM_NAMES_0[:11] + ['rwkv_v0', 'rwkv_v2'] + _PARAM_NAMES_0[11:])


def kernel(x,
           norm_g_0, w_in_0, w_out_0, gla_gk_w2_0, gla_gk_b_0, gla_norm_g_0,
           rwkv_mu_0, rwkv_w0_0, rwkv_w2_0, rwkv_a0_0, rwkv_a2_0,
           rwkv_k_k_0, rwkv_k_a_0, rwkv_r_k_0, rwkv_ln_g_0, rwkv_ln_b_0,
           ssd_conv_w_0, ssd_conv_b_0, ssd_dt_bias_0, ssd_a_log_0, ssd_d_0, ssd_norm_g_0,
           mlstm_conv_w_0, mlstm_conv_b_0, mlstm_ig_b_0, mlstm_fg_b_0, mlstm_norm_g_0,
           norm_g_1, w_in_1, w_out_1, gla_gk_w2_1, gla_gk_b_1, gla_norm_g_1,
           rwkv_mu_1, rwkv_w0_1, rwkv_w2_1, rwkv_a0_1, rwkv_a2_1, rwkv_v0_1, rwkv_v2_1,
           rwkv_k_k_1, rwkv_k_a_1, rwkv_r_k_1, rwkv_ln_g_1, rwkv_ln_b_1,
           ssd_conv_w_1, ssd_conv_b_1, ssd_dt_bias_1, ssd_a_log_1, ssd_d_1, ssd_norm_g_1,
           mlstm_conv_w_1, mlstm_conv_b_1, mlstm_ig_b_1, mlstm_fg_b_1, mlstm_norm_g_1,
           final_norm_g):
    params = (norm_g_0, w_in_0, w_out_0, gla_gk_w2_0, gla_gk_b_0, gla_norm_g_0,
              rwkv_mu_0, rwkv_w0_0, rwkv_w2_0, rwkv_a0_0, rwkv_a2_0,
              rwkv_k_k_0, rwkv_k_a_0, rwkv_r_k_0, rwkv_ln_g_0, rwkv_ln_b_0,
              ssd_conv_w_0, ssd_conv_b_0, ssd_dt_bias_0, ssd_a_log_0, ssd_d_0, ssd_norm_g_0,
              mlstm_conv_w_0, mlstm_conv_b_0, mlstm_ig_b_0, mlstm_fg_b_0, mlstm_norm_g_0,
              norm_g_1, w_in_1, w_out_1, gla_gk_w2_1, gla_gk_b_1, gla_norm_g_1,
              rwkv_mu_1, rwkv_w0_1, rwkv_w2_1, rwkv_a0_1, rwkv_a2_1, rwkv_v0_1, rwkv_v2_1,
              rwkv_k_k_1, rwkv_k_a_1, rwkv_r_k_1, rwkv_ln_g_1, rwkv_ln_b_1,
              ssd_conv_w_1, ssd_conv_b_1, ssd_dt_bias_1, ssd_a_log_1, ssd_d_1, ssd_norm_g_1,
              mlstm_conv_w_1, mlstm_conv_b_1, mlstm_ig_b_1, mlstm_fg_b_1, mlstm_norm_g_1,
              final_norm_g)
    n0 = len(_PARAM_NAMES_0)
    n1 = len(_PARAM_NAMES_1)
    p0 = dict(zip(_PARAM_NAMES_0, params[:n0]))
    p1 = dict(zip(_PARAM_NAMES_1, params[n0:n0 + n1]))
    final_norm_g = params[n0 + n1]
    B, S, _ = x.shape
    consts = {k: jnp.asarray(v) for k, v in _np_consts().items()}
    x2 = x.reshape(B * S, D_MODEL)
    x2, v_first = _layer(x2, None, 0, p0, B, S, consts, None)
    x2, _ = _layer(x2, v_first, 1, p1, B, S, consts, final_norm_g.reshape(1, D_MODEL))
    return x2.reshape(B, S, D_MODEL)
```
